```python
import math
import jax, jax.numpy as jnp
from jax import lax
import numpy as np

D_MODEL = 1024
BATCH = 8
SEQ = 8192
DEPTH = 1

D_A = D_MODEL
D_B = D_MODEL
CONV_A_WIDTH = 3
CONV_B_WIDTH = 31
D_FF = 4 * D_MODEL
N_GROUPS = 16
LN_EPS = 1e-5
ALPHA = (2.0 * DEPTH) ** 0.25
BETA = (8.0 * DEPTH) ** -0.25
W_IN_COLS = 3 * D_A + 2 * D_B + 2 * D_MODEL

kernel_name = "hybrid_shortconv_conformer_gated_deepnorm"


def layernorm(x, gamma, beta):
    xf = x.astype(jnp.float32)
    mu = jnp.mean(xf, axis=-1, keepdims=True)
    var = jnp.mean(jnp.square(xf - mu), axis=-1, keepdims=True)
    y = (xf - mu) * lax.rsqrt(var + LN_EPS)
    y = y * gamma.astype(jnp.float32) + beta.astype(jnp.float32)
    return y.astype(x.dtype)


def causal_depthwise_conv(x, w):
    k = w.shape[0]
    c = x.shape[-1]
    return lax.conv_general_dilated(
        x, w[:, None, :].astype(x.dtype),
        window_strides=(1,),
        padding=[(k - 1, 0)],
        dimension_numbers=("NWC", "WIO", "NWC"),
        feature_group_count=c,
    )


def _fwd_setup_inputs(seed: int = 0) -> dict:
    key = jax.random.key(seed)
    ks = jax.random.split(key, 20)
    f32 = jnp.float32
    nrm = lambda k, shape, scale: jax.random.normal(k, shape, f32) * scale
    return {
        "x": jax.random.normal(ks[0], (BATCH, SEQ, D_MODEL), f32),
        "w_in": nrm(ks[1], (D_MODEL, W_IN_COLS), D_MODEL ** -0.5),
        "conv_a_w": nrm(ks[2], (CONV_A_WIDTH, D_A), CONV_A_WIDTH ** -0.5),
        "w_out_a": nrm(ks[3], (D_A, D_MODEL), BETA * D_A ** -0.5),
        "conv_b_w": nrm(ks[4], (CONV_B_WIDTH, D_B), CONV_B_WIDTH ** -0.5),
        "conv_b_bias": nrm(ks[5], (D_B,), 0.02),
        "ln_b_gamma": 1.0 + nrm(ks[6], (D_B,), 0.02),
        "ln_b_beta": nrm(ks[7], (D_B,), 0.02),
        "w_out_b": nrm(ks[8], (D_B, D_MODEL), BETA * D_B ** -0.5),
        "w_o": nrm(ks[9], (D_MODEL, D_MODEL), BETA * D_MODEL ** -0.5),
        "ln1_gamma": 1.0 + nrm(ks[10], (D_MODEL,), 0.02),
        "ln1_beta": nrm(ks[11], (D_MODEL,), 0.02),
        "w_up": nrm(ks[12], (D_MODEL, D_FF), D_MODEL ** -0.5),
        "w_down": nrm(ks[13], (D_FF, D_MODEL), BETA * D_FF ** -0.5),
        "ln2_gamma": 1.0 + nrm(ks[14], (D_MODEL,), 0.02),
        "ln2_beta": nrm(ks[15], (D_MODEL,), 0.02),
    }


def token_mixer(x, w_in, conv_a_w, w_out_a, conv_b_w, conv_b_bias,
                ln_b_gamma, ln_b_beta, w_out_b, w_o):
    p = jnp.einsum("bsd,dc->bsc", x, w_in)
    splits = np.cumsum([D_A, D_A, D_A, D_B, D_B, D_MODEL])
    b_a, c_a, v_a, val_b, gate_b, g_a, g_b = jnp.split(p, splits, axis=-1)

    y_a = b_a * causal_depthwise_conv(c_a * v_a, conv_a_w)
    y_a = jnp.einsum("bsc,cd->bsd", y_a, w_out_a)

    u = val_b * jax.nn.sigmoid(gate_b)
    u = causal_depthwise_conv(u, conv_b_w) + conv_b_bias.astype(u.dtype)
    u = jax.nn.silu(layernorm(u, ln_b_gamma, ln_b_beta))
    y_b = jnp.einsum("bsc,cd->bsd", u, w_out_b)

    merged = jax.nn.sigmoid(g_a) * y_a + jax.nn.sigmoid(g_b) * y_b
    return jnp.einsum("bsd,de->bse", merged, w_o)


def channel_mixer(x, w_up, w_down):
    h = jnp.square(jax.nn.relu(jnp.einsum("bsd,df->bsf", x, w_up)))
    return jnp.einsum("bsf,fd->bsd", h, w_down)


def _fwd_reference(x, w_in, conv_a_w, w_out_a, conv_b_w, conv_b_bias, ln_b_gamma,
              ln_b_beta, w_out_b, w_o, ln1_gamma, ln1_beta, w_up, w_down,
              ln2_gamma, ln2_beta):
    alpha = jnp.asarray(ALPHA, dtype=x.dtype)
    for _ in range(DEPTH):
        mix = token_mixer(x, w_in, conv_a_w, w_out_a, conv_b_w, conv_b_bias,
                          ln_b_gamma, ln_b_beta, w_out_b, w_o)
        x = layernorm(alpha * x + mix, ln1_gamma, ln1_beta)
        ff = channel_mixer(x, w_up, w_down)
        x = layernorm(alpha * x + ff, ln2_gamma, ln2_beta)
    return x


import jax as _jax
import jax.numpy as _jnp

TWIN_FORMAT = 'train_step'
FWD_PARAMS = ['x', 'w_in', 'conv_a_w', 'w_out_a', 'conv_b_w', 'conv_b_bias', 'ln_b_gamma', 'ln_b_beta', 'w_out_b', 'w_o', 'ln1_gamma', 'ln1_beta', 'w_up', 'w_down', 'ln2_gamma', 'ln2_beta']
TWIN_WEIGHTS = ['w_in', 'conv_a_w', 'w_out_a', 'conv_b_w', 'conv_b_bias', 'ln_b_gamma', 'ln_b_beta', 'w_out_b', 'w_o', 'ln1_gamma', 'ln1_beta', 'w_up', 'w_down', 'ln2_gamma', 'ln2_beta']
TWIN_DIFF_INPUT = 'x'
TWIN_INPUTS = ['x', 'w_in', 'conv_a_w', 'w_out_a', 'conv_b_w', 'conv_b_bias', 'ln_b_gamma', 'ln_b_beta', 'w_out_b', 'w_o', 'ln1_gamma', 'ln1_beta', 'w_up', 'w_down', 'ln2_gamma', 'ln2_beta', 'loss_target', 'm_w_in', 'm_conv_a_w', 'm_w_out_a', 'm_conv_b_w', 'm_conv_b_bias', 'm_ln_b_gamma', 'm_ln_b_beta', 'm_w_out_b', 'm_w_o', 'm_ln1_gamma', 'm_ln1_beta', 'm_w_up', 'm_w_down', 'm_ln2_gamma', 'm_ln2_beta', 'v_w_in', 'v_conv_a_w', 'v_w_out_a', 'v_conv_b_w', 'v_conv_b_bias', 'v_ln_b_gamma', 'v_ln_b_beta', 'v_w_out_b', 'v_w_o', 'v_ln1_gamma', 'v_ln1_beta', 'v_w_up', 'v_w_down', 'v_ln2_gamma', 'v_ln2_beta']
TWIN_OUTPUTS = ['loss', 'grad_x', 'grad_w_in', 'grad_conv_a_w', 'grad_w_out_a', 'grad_conv_b_w', 'grad_conv_b_bias', 'grad_ln_b_gamma', 'grad_ln_b_beta', 'grad_w_out_b', 'grad_w_o', 'grad_ln1_gamma', 'grad_ln1_beta', 'grad_w_up', 'grad_w_down', 'grad_ln2_gamma', 'grad_ln2_beta', 'delta_w_in', 'delta_conv_a_w', 'delta_w_out_a', 'delta_conv_b_w', 'delta_conv_b_bias', 'delta_ln_b_gamma', 'delta_ln_b_beta', 'delta_w_out_b', 'delta_w_o', 'delta_ln1_gamma', 'delta_ln1_beta', 'delta_w_up', 'delta_w_down', 'delta_ln2_gamma', 'delta_ln2_beta', 'new_m_w_in', 'new_m_conv_a_w', 'new_m_w_out_a', 'new_m_conv_b_w', 'new_m_conv_b_bias', 'new_m_ln_b_gamma', 'new_m_ln_b_beta', 'new_m_w_out_b', 'new_m_w_o', 'new_m_ln1_gamma', 'new_m_ln1_beta', 'new_m_w_up', 'new_m_w_down', 'new_m_ln2_gamma', 'new_m_ln2_beta', 'new_v_w_in', 'new_v_conv_a_w', 'new_v_w_out_a', 'new_v_conv_b_w', 'new_v_conv_b_bias', 'new_v_ln_b_gamma', 'new_v_ln_b_beta', 'new_v_w_out_b', 'new_v_w_o', 'new_v_ln1_gamma', 'new_v_ln1_beta', 'new_v_w_up', 'new_v_w_down', 'new_v_ln2_gamma', 'new_v_ln2_beta']
TWIN_LEAF_KINDS = {'loss': 'loss', 'grad_x': 'grad_x', 'grad_w_in': 'grad_w', 'grad_conv_a_w': 'grad_w', 'grad_w_out_a': 'grad_w', 'grad_conv_b_w': 'grad_w', 'grad_conv_b_bias': 'grad_w', 'grad_ln_b_gamma': 'grad_w', 'grad_ln_b_beta': 'grad_w', 'grad_w_out_b': 'grad_w', 'grad_w_o': 'grad_w', 'grad_ln1_gamma': 'grad_w', 'grad_ln1_beta': 'grad_w', 'grad_w_up': 'grad_w', 'grad_w_down': 'grad_w', 'grad_ln2_gamma': 'grad_w', 'grad_ln2_beta': 'grad_w', 'delta_w_in': 'delta_w', 'delta_conv_a_w': 'delta_w', 'delta_w_out_a': 'delta_w', 'delta_conv_b_w': 'delta_w', 'delta_conv_b_bias': 'delta_w', 'delta_ln_b_gamma': 'delta_w', 'delta_ln_b_beta': 'delta_w', 'delta_w_out_b': 'delta_w', 'delta_w_o': 'delta_w', 'delta_ln1_gamma': 'delta_w', 'delta_ln1_beta': 'delta_w', 'delta_w_up': 'delta_w', 'delta_w_down': 'delta_w', 'delta_ln2_gamma': 'delta_w', 'delta_ln2_beta': 'delta_w', 'new_m_w_in': 'new_m', 'new_m_conv_a_w': 'new_m', 'new_m_w_out_a': 'new_m', 'new_m_conv_b_w': 'new_m', 'new_m_conv_b_bias': 'new_m', 'new_m_ln_b_gamma': 'new_m', 'new_m_ln_b_beta': 'new_m', 'new_m_w_out_b': 'new_m', 'new_m_w_o': 'new_m', 'new_m_ln1_gamma': 'new_m', 'new_m_ln1_beta': 'new_m', 'new_m_w_up': 'new_m', 'new_m_w_down': 'new_m', 'new_m_ln2_gamma': 'new_m', 'new_m_ln2_beta': 'new_m', 'new_v_w_in': 'new_v', 'new_v_conv_a_w': 'new_v', 'new_v_w_out_a': 'new_v', 'new_v_conv_b_w': 'new_v', 'new_v_conv_b_bias': 'new_v', 'new_v_ln_b_gamma': 'new_v', 'new_v_ln_b_beta': 'new_v', 'new_v_w_out_b': 'new_v', 'new_v_w_o': 'new_v', 'new_v_ln1_gamma': 'new_v', 'new_v_ln1_beta': 'new_v', 'new_v_w_up': 'new_v', 'new_v_w_down': 'new_v', 'new_v_ln2_gamma': 'new_v', 'new_v_ln2_beta': 'new_v'}


def _forward(args):
    return _fwd_reference(*[args[k] for k in FWD_PARAMS])


def _output_shape():
    def fwd():
        inp = _fwd_setup_inputs(0)
        return _fwd_reference(*[inp[k] for k in FWD_PARAMS])
    out = _jax.eval_shape(fwd)
    return out.shape, out.dtype

N_MICROBATCH = 1
ADAM_LR = 0.001
ADAM_B1 = 0.9
ADAM_B2 = 0.999
ADAM_EPS = 1e-08
ADAM_WD = 0.01
ADAM_STEP = 10
PER_EXAMPLE_BATCH_AXIS = {'x': 0, 'loss_target': 0}
SHARED_INPUTS = []
_WEIGHT_DTYPES = {'w_in': _jnp.float32, 'conv_a_w': _jnp.float32, 'w_out_a': _jnp.float32, 'conv_b_w': _jnp.float32, 'conv_b_bias': _jnp.float32, 'ln_b_gamma': _jnp.float32, 'ln_b_beta': _jnp.float32, 'w_out_b': _jnp.float32, 'w_o': _jnp.float32, 'ln1_gamma': _jnp.float32, 'ln1_beta': _jnp.float32, 'w_up': _jnp.float32, 'w_down': _jnp.float32, 'ln2_gamma': _jnp.float32, 'ln2_beta': _jnp.float32}
MOMENT_SCALE = {'w_in': 3.013068e-02, 'conv_a_w': 4.138054e-02, 'w_out_a': 7.025541e-02, 'conv_b_w': 2.627945e-02, 'conv_b_bias': 1.154437e-01, 'ln_b_gamma': 4.788730e-02, 'ln_b_beta': 6.914236e-02, 'w_out_b': 5.601887e-02, 'w_o': 8.808570e-02, 'ln1_gamma': 1.557381e+00, 'ln1_beta': 1.011396e+00, 'w_up': 7.632905e-02, 'w_down': 3.789077e-01, 'ln2_gamma': 6.416246e+01, 'ln2_beta': 1.334841e+01}


def _to_microbatches(a, axis):
    t = _jnp.moveaxis(a, axis, 0)
    t = t.reshape((N_MICROBATCH, t.shape[0] // N_MICROBATCH) + t.shape[1:])
    return _jnp.moveaxis(t, 1, axis + 1)


def setup_inputs(seed: int = 0) -> dict:
    inp = _fwd_setup_inputs(seed)
    key = _jax.random.fold_in(_jax.random.key(seed), 7919)
    shape, _ = _output_shape()
    out = dict(inp)
    out["loss_target"] = _jax.random.normal(_jax.random.fold_in(key, 0), shape, _jnp.float32)
    for i, name in enumerate(TWIN_WEIGHTS):
        w = inp[name].astype(_jnp.float32)
        if MOMENT_SCALE is None:
            s = _jnp.sqrt(_jnp.mean(_jnp.square(w)) + 1e-30)
        else:
            s = MOMENT_SCALE[name]
        km, kv = _jax.random.split(_jax.random.fold_in(key, i + 1))
        out[name] = w
        out["m_" + name] = s * _jax.random.normal(km, w.shape, _jnp.float32)
        out["v_" + name] = (s * s) * _jax.random.uniform(kv, w.shape, _jnp.float32, 0.5, 1.5)
    if N_MICROBATCH > 1:
        for name, axis in PER_EXAMPLE_BATCH_AXIS.items():
            out[name] = _to_microbatches(out[name], axis)
    return {'x': out['x'], 'w_in': out['w_in'], 'conv_a_w': out['conv_a_w'], 'w_out_a': out['w_out_a'], 'conv_b_w': out['conv_b_w'], 'conv_b_bias': out['conv_b_bias'], 'ln_b_gamma': out['ln_b_gamma'], 'ln_b_beta': out['ln_b_beta'], 'w_out_b': out['w_out_b'], 'w_o': out['w_o'], 'ln1_gamma': out['ln1_gamma'], 'ln1_beta': out['ln1_beta'], 'w_up': out['w_up'], 'w_down': out['w_down'], 'ln2_gamma': out['ln2_gamma'], 'ln2_beta': out['ln2_beta'], 'loss_target': out['loss_target'], 'm_w_in': out['m_w_in'], 'm_conv_a_w': out['m_conv_a_w'], 'm_w_out_a': out['m_w_out_a'], 'm_conv_b_w': out['m_conv_b_w'], 'm_conv_b_bias': out['m_conv_b_bias'], 'm_ln_b_gamma': out['m_ln_b_gamma'], 'm_ln_b_beta': out['m_ln_b_beta'], 'm_w_out_b': out['m_w_out_b'], 'm_w_o': out['m_w_o'], 'm_ln1_gamma': out['m_ln1_gamma'], 'm_ln1_beta': out['m_ln1_beta'], 'm_w_up': out['m_w_up'], 'm_w_down': out['m_w_down'], 'm_ln2_gamma': out['m_ln2_gamma'], 'm_ln2_beta': out['m_ln2_beta'], 'v_w_in': out['v_w_in'], 'v_conv_a_w': out['v_conv_a_w'], 'v_w_out_a': out['v_w_out_a'], 'v_conv_b_w': out['v_conv_b_w'], 'v_conv_b_bias': out['v_conv_b_bias'], 'v_ln_b_gamma': out['v_ln_b_gamma'], 'v_ln_b_beta': out['v_ln_b_beta'], 'v_w_out_b': out['v_w_out_b'], 'v_w_o': out['v_w_o'], 'v_ln1_gamma': out['v_ln1_gamma'], 'v_ln1_beta': out['v_ln1_beta'], 'v_w_up': out['v_w_up'], 'v_w_down': out['v_w_down'], 'v_ln2_gamma': out['v_ln2_gamma'], 'v_ln2_beta': out['v_ln2_beta']}


def _loss(weights, diff, rest, loss_target):
    with _jax.named_scope("forward"):
        args = {**rest, TWIN_DIFF_INPUT: diff, **{k: w.astype(_WEIGHT_DTYPES[k]) for k, w in weights.items()}}
        y = _forward(args)
    with _jax.named_scope("loss_head"):
        err = _jnp.square(y.astype(_jnp.float32) - loss_target)
        return 0.5 * _jnp.sum(_jnp.mean(err, axis=-1)) if err.ndim else 0.5 * err


def _adamw(w, g, m, v):
    m = ADAM_B1 * m + (1.0 - ADAM_B1) * g
    v = ADAM_B2 * v + (1.0 - ADAM_B2) * _jnp.square(g)
    m_hat = m / (1.0 - ADAM_B1 ** ADAM_STEP)
    v_hat = v / (1.0 - ADAM_B2 ** ADAM_STEP)
    delta = -ADAM_LR * (m_hat / (_jnp.sqrt(v_hat) + ADAM_EPS) + ADAM_WD * w)
    return delta, m, v


def reference(x, w_in, conv_a_w, w_out_a, conv_b_w, conv_b_bias, ln_b_gamma, ln_b_beta, w_out_b, w_o, ln1_gamma, ln1_beta, w_up, w_down, ln2_gamma, ln2_beta, loss_target, m_w_in, m_conv_a_w, m_w_out_a, m_conv_b_w, m_conv_b_bias, m_ln_b_gamma, m_ln_b_beta, m_w_out_b, m_w_o, m_ln1_gamma, m_ln1_beta, m_w_up, m_w_down, m_ln2_gamma, m_ln2_beta, v_w_in, v_conv_a_w, v_w_out_a, v_conv_b_w, v_conv_b_bias, v_ln_b_gamma, v_ln_b_beta, v_w_out_b, v_w_o, v_ln1_gamma, v_ln1_beta, v_w_up, v_w_down, v_ln2_gamma, v_ln2_beta):
    given = dict(x=x, w_in=w_in, conv_a_w=conv_a_w, w_out_a=w_out_a, conv_b_w=conv_b_w, conv_b_bias=conv_b_bias, ln_b_gamma=ln_b_gamma, ln_b_beta=ln_b_beta, w_out_b=w_out_b, w_o=w_o, ln1_gamma=ln1_gamma, ln1_beta=ln1_beta, w_up=w_up, w_down=w_down, ln2_gamma=ln2_gamma, ln2_beta=ln2_beta, loss_target=loss_target, m_w_in=m_w_in, m_conv_a_w=m_conv_a_w, m_w_out_a=m_w_out_a, m_conv_b_w=m_conv_b_w, m_conv_b_bias=m_conv_b_bias, m_ln_b_gamma=m_ln_b_gamma, m_ln_b_beta=m_ln_b_beta, m_w_out_b=m_w_out_b, m_w_o=m_w_o, m_ln1_gamma=m_ln1_gamma, m_ln1_beta=m_ln1_beta, m_w_up=m_w_up, m_w_down=m_w_down, m_ln2_gamma=m_ln2_gamma, m_ln2_beta=m_ln2_beta, v_w_in=v_w_in, v_conv_a_w=v_conv_a_w, v_w_out_a=v_w_out_a, v_conv_b_w=v_conv_b_w, v_conv_b_bias=v_conv_b_bias, v_ln_b_gamma=v_ln_b_gamma, v_ln_b_beta=v_ln_b_beta, v_w_out_b=v_w_out_b, v_w_o=v_w_o, v_ln1_gamma=v_ln1_gamma, v_ln1_beta=v_ln1_beta, v_w_up=v_w_up, v_w_down=v_w_down, v_ln2_gamma=v_ln2_gamma, v_ln2_beta=v_ln2_beta)
    weights = {n: given[n] for n in TWIN_WEIGHTS}
    shared = {n: given[n] for n in SHARED_INPUTS}
    per_example = {n: given[n] for n in ['x']}
    grad_fn = _jax.value_and_grad(_loss, argnums=(0, 1))

    def one_microbatch(ex, loss_target):
        ex = dict(ex)
        diff = ex.pop(TWIN_DIFF_INPUT)
        return grad_fn(weights, diff, {**shared, **ex}, loss_target)

    if N_MICROBATCH == 1:
        loss, (grad_w, grad_x) = one_microbatch(per_example, given["loss_target"])
    else:
        def body(carry, xs):
            loss_sum, grad_sum = carry
            l_k, (gw_k, gx_k) = one_microbatch(xs[0], xs[1])
            with _jax.named_scope("update"):
                return (loss_sum + l_k, _jax.tree.map(_jnp.add, grad_sum, gw_k)), gx_k

        init = (_jnp.zeros((), _jnp.float32), _jax.tree.map(_jnp.zeros_like, weights))
        (loss, grad_w), grad_x = _jax.lax.scan(body, init, (per_example, given["loss_target"]))
    with _jax.named_scope("update"):
        delta_w, new_m, new_v = {}, {}, {}
        for n in TWIN_WEIGHTS:
            delta_w[n], new_m[n], new_v[n] = _adamw(weights[n], grad_w[n], given["m_" + n], given["v_" + n])
    return (loss, grad_x, *[grad_w[n] for n in TWIN_WEIGHTS], *[delta_w[n] for n in TWIN_WEIGHTS],
            *[new_m[n] for n in TWIN_WEIGHTS], *[new_v[n] for n in TWIN_WEIGHTS])
```

```python
import functools

import jax
import jax.numpy as jnp
from jax import lax
from jax.experimental import pallas as pl
from jax.experimental.pallas import tpu as pltpu

F32 = jnp.float32
BF = jnp.bfloat16
D = 1024
NDEV = 8
ALPHA = 2.0 ** 0.25
LN_EPS = 1e-5
KA, KB = 3, 31
HA, HB = 8, 32
RC = 64
LANES = 128
VMEM_LIMIT = 48 * 1024 * 1024
MESH = pl.DeviceIdType.MESH
ADAM_LR, ADAM_B1, ADAM_B2, ADAM_EPS, ADAM_WD, ADAM_STEP = 0.001, 0.9, 0.999, 1e-08, 0.01, 10

NT_DIMS = (((1,), (1,)), ((), ()))
TN_DIMS = (((0,), (0,)), ((), ()))


def _params(n_axes):
    return pltpu.CompilerParams(dimension_semantics=("arbitrary",) * n_axes, vmem_limit_bytes=VMEM_LIMIT)


def _sigmoid(v):
    return jax.nn.sigmoid(v)


def _ln_fwd(z):
    mu = jnp.mean(z, axis=-1, keepdims=True)
    zc = z - mu
    var = jnp.mean(zc * zc, axis=-1, keepdims=True)
    rstd = lax.rsqrt(var + LN_EPS)
    return zc * rstd, rstd


def _ln_bwd(dy, xhat, rstd, gamma):
    dxhat = dy * gamma
    m1 = jnp.mean(dxhat, axis=-1, keepdims=True)
    m2 = jnp.mean(dxhat * xhat, axis=-1, keepdims=True)
    return rstd * (dxhat - m1 - xhat * m2)


def _colsum(v):
    return jnp.sum(v, axis=0, keepdims=True)


def _allgather(arrs, name):
    n = len(arrs)

    def body(*refs):
        ins, outs = refs[:n], refs[n:2 * n]
        send_sems, recv_sems, local_sems = refs[2 * n:]
        x, y, c = lax.axis_index("x"), lax.axis_index("y"), lax.axis_index("c")
        me, sibling = (x, y, c), (x, y, 1 - c)
        chips = [(1 - x, y), (x, 1 - y), (1 - x, 1 - y)]

        def copy(a, k, block, to, src=None):
            px, py, pc = block
            rows = outs[a].at[4 * px + 2 * py + pc]
            return pltpu.make_async_remote_copy(
                src_ref=rows if src is None else src, dst_ref=rows,
                send_sem=send_sems.at[a, k], recv_sem=recv_sems.at[a, k],
                device_id=to, device_id_type=MESH)

        mine = [pltpu.make_async_copy(ins[a], outs[a].at[4 * x + 2 * y + c], local_sems.at[a]) for a in range(n)]
        for cp in mine:
            cp.start()
        first = []
        for a in range(n):
            first.append(copy(a, 0, me, sibling, src=ins[a]))
            first += [copy(a, 1 + j, me, (*chip, c), src=ins[a]) for j, chip in enumerate(chips)]
        for cp in first:
            cp.start()
        passed = []
        for j, chip in enumerate(chips):
            for a in range(n):
                copy(a, 1 + j, (*chip, c), me).wait_recv()
                fwd = copy(a, 4 + j, (*chip, c), sibling)
                fwd.start()
                passed.append(fwd)
        for a in range(n):
            copy(a, 0, sibling, me).wait_recv()
            for j, chip in enumerate(chips):
                copy(a, 4 + j, (*chip, 1 - c), me).wait_recv()
        for cp in first + passed:
            cp.wait_send()
        for cp in mine:
            cp.wait()

    any_spec = pl.BlockSpec(memory_space=pl.ANY)
    return pl.pallas_call(
        body, name=name,
        in_specs=[any_spec] * n, out_specs=[any_spec] * n,
        out_shape=[jax.ShapeDtypeStruct((NDEV,) + a.shape, a.dtype) for a in arrs],
        scratch_shapes=[pltpu.SemaphoreType.DMA((n, 7)), pltpu.SemaphoreType.DMA((n, 7)), pltpu.SemaphoreType.DMA((n,))],
    )(*arrs)


def _exchange(blocked, name):
    n = len(blocked)

    def body(*refs):
        ins, outs = refs[:n], refs[n:2 * n]
        send_sems, recv_sems = refs[2 * n:]
        x, y, c = lax.axis_index("x"), lax.axis_index("y"), lax.axis_index("c")
        copies = []
        for k in range(1, NDEV):
            px = 1 - x if k & 4 else x
            py = 1 - y if k & 2 else y
            pc = 1 - c if k & 1 else c
            for a in range(n):
                copies.append(pltpu.make_async_remote_copy(
                    src_ref=ins[a].at[4 * px + 2 * py + pc], dst_ref=outs[a].at[k - 1],
                    send_sem=send_sems.at[a, k - 1], recv_sem=recv_sems.at[a, k - 1],
                    device_id=(px, py, pc), device_id_type=MESH))
        for cp in copies:
            cp.start()
        for cp in copies:
            cp.wait()

    any_spec = pl.BlockSpec(memory_space=pl.ANY)
    return pl.pallas_call(
        body, name=name,
        in_specs=[any_spec] * n, out_specs=[any_spec] * n,
        out_shape=[jax.ShapeDtypeStruct((NDEV - 1,) + a.shape[1:], a.dtype) for a in blocked],
        scratch_shapes=[pltpu.SemaphoreType.DMA((n, 7)), pltpu.SemaphoreType.DMA((n, 7))],
    )(*blocked)


def _inproj(x, win_g, tm=512):
    T = x.shape[0]
    nb, _, bw = win_g.shape

    def body(x_ref, w_ref, o_ref):
        o_ref[...] = jnp.dot(x_ref[...].astype(BF), w_ref[0], preferred_element_type=F32)

    return pl.pallas_call(
        body, name="inproj", grid=(T // tm, nb),
        in_specs=[pl.BlockSpec((tm, D), lambda i, j: (i, 0)), pl.BlockSpec((1, D, bw), lambda i, j: (j, 0, 0))],
        out_specs=pl.BlockSpec((tm, bw), lambda i, j: (i, j)),
        out_shape=jax.ShapeDtypeStruct((T, nb * bw), F32),
        compiler_params=_params(2))(x, win_g)


def _seg(tt, s):
    return pl.BlockSpec((tt, D), lambda i: (i, s))


def _prev(tt, h, s):
    return pl.BlockSpec((h, D), lambda i: (jnp.maximum(i * (tt // h) - 1, 0), s))


def _next(tt, h, s, T):
    return pl.BlockSpec((h, D), lambda i: (jnp.minimum((i + 1) * (tt // h), T // h - 1), s))


def _mixer_fwd(p, wa, wb, vecs, tt=256):
    T = p.shape[0]

    def body(ba, ca, va, vb, gb, ca_p, va_p, vb_p, gb_p, wa_ref, wb_ref, vec_ref,
             yapre_ref, conva_ref, xhat_ref, rstd_ref, u3_ref, cabuf, u0buf, u1buf):
        first = pl.program_id(0) == 0
        cabuf[0:HA, :] = jnp.where(first, 0.0, ca_p[...] * va_p[...])
        cabuf[HA:HA + tt, :] = ca[...] * va[...]
        u0buf[0:HB, :] = jnp.where(first, 0.0, vb_p[...] * _sigmoid(gb_p[...]))
        u0buf[HB:HB + tt, :] = vb[...] * _sigmoid(gb[...])

        def lane_body(cidx, carry):
            ls = pl.ds(pl.multiple_of(cidx * LANES, LANES), LANES)
            for r in range(tt // RC):
                acc = jnp.zeros((RC, LANES), F32)
                for k in range(KA):
                    acc = acc + wa_ref[k:k + 1, ls] * cabuf[pl.ds(HA - (KA - 1) + k + r * RC, RC), ls]
                conva_ref[pl.ds(r * RC, RC), ls] = acc
                acc = jnp.zeros((RC, LANES), F32)
                for k in range(KB):
                    acc = acc + wb_ref[k:k + 1, ls] * u0buf[pl.ds(HB - (KB - 1) + k + r * RC, RC), ls]
                u1buf[pl.ds(r * RC, RC), ls] = acc
            return carry

        lax.fori_loop(0, D // LANES, lane_body, 0)
        yapre_ref[...] = (ba[...] * conva_ref[...]).astype(BF)
        xhat, rstd = _ln_fwd(u1buf[...] + vec_ref[0:1, :])
        xhat_ref[...] = xhat
        rstd_ref[...] = rstd
        u2 = xhat * vec_ref[1:2, :] + vec_ref[2:3, :]
        u3_ref[...] = (u2 * _sigmoid(u2)).astype(BF)

    full = lambda r: pl.BlockSpec((r, D), lambda i: (0, 0))
    tok = pl.BlockSpec((tt, D), lambda i: (i, 0))
    return pl.pallas_call(
        body, name="mixer_fwd", grid=(T // tt,),
        in_specs=[_seg(tt, 0), _seg(tt, 1), _seg(tt, 2), _seg(tt, 3), _seg(tt, 4),
                  _prev(tt, HA, 1), _prev(tt, HA, 2), _prev(tt, HB, 3), _prev(tt, HB, 4),
                  full(8), full(32), full(8)],
        out_specs=[tok, tok, tok, pl.BlockSpec((tt, 1), lambda i: (i, 0)), tok],
        out_shape=[jax.ShapeDtypeStruct((T, D), BF), jax.ShapeDtypeStruct((T, D), F32),
                   jax.ShapeDtypeStruct((T, D), F32), jax.ShapeDtypeStruct((T, 1), F32),
                   jax.ShapeDtypeStruct((T, D), BF)],
        scratch_shapes=[pltpu.VMEM((HA + tt, D), F32), pltpu.VMEM((HB + tt, D), F32), pltpu.VMEM((tt, D), F32)],
        compiler_params=_params(1))(p, p, p, p, p, p, p, p, p, wa, wb, vecs)


def _post_mixer(yapre, u3, p, x, woa, wob, wo, vecs, tm=256):
    T = x.shape[0]

    def body(yapre_ref, u3_ref, ga_ref, gb_ref, x_ref, woa_ref, wob_ref, wo_ref, vec_ref,
             ya_ref, yb_ref, merged_ref, xhat_ref, rstd_ref, x1b_ref):
        ya = jnp.dot(yapre_ref[...], woa_ref[...], preferred_element_type=F32)
        yb = jnp.dot(u3_ref[...], wob_ref[...], preferred_element_type=F32)
        ya_ref[...] = ya
        yb_ref[...] = yb
        merged = (_sigmoid(ga_ref[...]) * ya + _sigmoid(gb_ref[...]) * yb).astype(BF)
        merged_ref[...] = merged
        mix = jnp.dot(merged, wo_ref[...], preferred_element_type=F32)
        xhat, rstd = _ln_fwd(ALPHA * x_ref[...] + mix)
        xhat_ref[...] = xhat
        rstd_ref[...] = rstd
        x1b_ref[...] = (xhat * vec_ref[3:4, :] + vec_ref[4:5, :]).astype(BF)

    tok = pl.BlockSpec((tm, D), lambda i: (i, 0))
    wfull = pl.BlockSpec((D, D), lambda i: (0, 0))
    one = pl.BlockSpec((tm, 1), lambda i: (i, 0))
    return pl.pallas_call(
        body, name="post_mixer", grid=(T // tm,),
        in_specs=[tok, tok, _seg(tm, 5), _seg(tm, 6), tok, wfull, wfull, wfull, pl.BlockSpec((8, D), lambda i: (0, 0))],
        out_specs=[tok, tok, tok, tok, one, tok],
        out_shape=[jax.ShapeDtypeStruct((T, D), F32), jax.ShapeDtypeStruct((T, D), F32),
                   jax.ShapeDtypeStruct((T, D), BF), jax.ShapeDtypeStruct((T, D), F32),
                   jax.ShapeDtypeStruct((T, 1), F32), jax.ShapeDtypeStruct((T, D), BF)],
        compiler_params=_params(1))(yapre, u3, p, p, x, woa, wob, wo, vecs)


def _mlp_up(x1b, wup_g, tm=512):
    T = x1b.shape[0]
    nb, _, bw = wup_g.shape

    def body(x_ref, w_ref, hpre_ref, h_ref):
        hpre = jnp.dot(x_ref[...], w_ref[0], preferred_element_type=F32)
        hpre_ref[...] = hpre
        r = jnp.maximum(hpre, 0.0)
        h_ref[...] = (r * r).astype(BF)

    out = pl.BlockSpec((tm, bw), lambda i, j: (i, j))
    return pl.pallas_call(
        body, name="mlp_up", grid=(T // tm, nb),
        in_specs=[pl.BlockSpec((tm, D), lambda i, j: (i, 0)), pl.BlockSpec((1, D, bw), lambda i, j: (j, 0, 0))],
        out_specs=[out, out],
        out_shape=[jax.ShapeDtypeStruct((T, nb * bw), F32), jax.ShapeDtypeStruct((T, nb * bw), BF)],
        compiler_params=_params(2))(x1b, wup_g)


def _mlp_down_loss(h, wdown, xhat1, target, vecs, tm=512, tk=512):
    T, dff = h.shape
    nk = dff // tk

    def body(h_ref, w_ref, xhat1_ref, tgt_ref, vec_ref, dz2_ref, st_ref, acc):
        i, k = pl.program_id(0), pl.program_id(1)

        @pl.when(jnp.logical_and(i == 0, k == 0))
        def _():
            st_ref[...] = jnp.zeros_like(st_ref)

        @pl.when(k == 0)
        def _():
            acc[...] = jnp.zeros_like(acc)

        acc[...] += jnp.dot(h_ref[...], w_ref[...], preferred_element_type=F32)

        @pl.when(k == nk - 1)
        def _():
            x1 = xhat1_ref[...] * vec_ref[3:4, :] + vec_ref[4:5, :]
            xhat2, rstd2 = _ln_fwd(ALPHA * x1 + acc[...])
            g2 = vec_ref[5:6, :]
            diff = xhat2 * g2 + vec_ref[6:7, :] - tgt_ref[...]
            dx2 = diff * (1.0 / D)
            st_ref[0:1, :] += _colsum(dx2 * xhat2)
            st_ref[1:2, :] += _colsum(dx2)
            st_ref[2:3, :] += _colsum(diff * diff)
            dz2_ref[...] = _ln_bwd(dx2, xhat2, rstd2, g2)

    tok = pl.BlockSpec((tm, D), lambda i, k: (i, 0))
    vec = pl.BlockSpec((8, D), lambda i, k: (0, 0))
    return pl.pallas_call(
        body, name="mlp_down_loss", grid=(T // tm, nk),
        in_specs=[pl.BlockSpec((tm, tk), lambda i, k: (i, k)), pl.BlockSpec((tk, D), lambda i, k: (k, 0)), tok, tok, vec],
        out_specs=[tok, vec],
        out_shape=[jax.ShapeDtypeStruct((T, D), F32), jax.ShapeDtypeStruct((8, D), F32)],
        scratch_shapes=[pltpu.VMEM((tm, D), F32)],
        compiler_params=_params(2))(h, wdown, xhat1, target, vecs)


def _mlp_down_bwd(dz2, wdown, hpre, tm=512, tk=512):
    T, dff = hpre.shape

    def body(dz_ref, w_ref, hpre_ref, o_ref):
        dh = lax.dot_general(dz_ref[...].astype(BF), w_ref[...], NT_DIMS, preferred_element_type=F32)
        o_ref[...] = (dh * (2.0 * jnp.maximum(hpre_ref[...], 0.0))).astype(BF)

    blk = pl.BlockSpec((tm, tk), lambda i, j: (i, j))
    return pl.pallas_call(
        body, name="mlp_down_bwd", grid=(T // tm, dff // tk),
        in_specs=[pl.BlockSpec((tm, D), lambda i, j: (i, 0)), pl.BlockSpec((tk, D), lambda i, j: (j, 0)), blk],
        out_specs=blk,
        out_shape=jax.ShapeDtypeStruct((T, dff), BF),
        compiler_params=_params(2))(dz2, wdown, hpre)


def _tn_matmul(a, b, nblk, a_bw, b_bw, a_blocked, b_blocked, name, tt=512):
    T = a.shape[0]
    nt = T // tt

    def body(a_ref, b_ref, o32_ref, o16_ref):
        t = pl.program_id(1)

        @pl.when(t == 0)
        def _():
            o32_ref[...] = jnp.zeros_like(o32_ref)

        o32_ref[0] += lax.dot_general(a_ref[...].astype(BF), b_ref[...].astype(BF), TN_DIMS, preferred_element_type=F32)

        @pl.when(t == nt - 1)
        def _():
            o16_ref[...] = o32_ref[...].astype(BF)

    a_spec = pl.BlockSpec((tt, a_bw), (lambda j, t: (t, j)) if a_blocked else (lambda j, t: (t, 0)))
    b_spec = pl.BlockSpec((tt, b_bw), (lambda j, t: (t, j)) if b_blocked else (lambda j, t: (t, 0)))
    out = pl.BlockSpec((1, a_bw, b_bw), lambda j, t: (j, 0, 0))
    return pl.pallas_call(
        body, name=name, grid=(nblk, nt),
        in_specs=[a_spec, b_spec], out_specs=[out, out],
        out_shape=[jax.ShapeDtypeStruct((nblk, a_bw, b_bw), F32), jax.ShapeDtypeStruct((nblk, a_bw, b_bw), BF)],
        compiler_params=_params(2))(a, b)


def _mlp_up_bwd(dhpre, wup_g, dz2, xhat1, rstd1, vecs, tm=512):
    T = dz2.shape[0]
    nb, _, bw = wup_g.shape

    def body(dh_ref, w_ref, dz2_ref, xhat_ref, rstd_ref, vec_ref, dz1_ref, st_ref, acc):
        i, j = pl.program_id(0), pl.program_id(1)

        @pl.when(jnp.logical_and(i == 0, j == 0))
        def _():
            st_ref[...] = jnp.zeros_like(st_ref)

        @pl.when(j == 0)
        def _():
            acc[...] = jnp.zeros_like(acc)

        acc[...] += lax.dot_general(dh_ref[...], w_ref[0], NT_DIMS, preferred_element_type=F32)

        @pl.when(j == nb - 1)
        def _():
            dx1 = acc[...] + ALPHA * dz2_ref[...]
            xhat = xhat_ref[...]
            st_ref[0:1, :] += _colsum(dx1 * xhat)
            st_ref[1:2, :] += _colsum(dx1)
            dz1_ref[...] = _ln_bwd(dx1, xhat, rstd_ref[...], vec_ref[3:4, :])

    tok = pl.BlockSpec((tm, D), lambda i, j: (i, 0))
    vec = pl.BlockSpec((8, D), lambda i, j: (0, 0))
    return pl.pallas_call(
        body, name="mlp_up_bwd", grid=(T // tm, nb),
        in_specs=[pl.BlockSpec((tm, bw), lambda i, j: (i, j)), pl.BlockSpec((1, D, bw), lambda i, j: (j, 0, 0)),
                  tok, tok, pl.BlockSpec((tm, 1), lambda i, j: (i, 0)), vec],
        out_specs=[tok, vec],
        out_shape=[jax.ShapeDtypeStruct((T, D), F32), jax.ShapeDtypeStruct((8, D), F32)],
        scratch_shapes=[pltpu.VMEM((tm, D), F32)],
        compiler_params=_params(2))(dhpre, wup_g, dz2, xhat1, rstd1, vecs)


def _merge_bwd(dz1, p, ya, yb, conva, xhatb, rstdb, woa, wob, wo, vecs, tm=256):
    T = dz1.shape[0]

    def body(dz1_ref, ga_ref, gb_ref, ba_ref, ya_ref, yb_ref, conva_ref, xhat_ref, rstd_ref,
             woa_ref, wob_ref, wo_ref, vec_ref,
             dya_ref, dyb_ref, dg_ref, dba_ref, dconva_ref, du1_ref, st_ref):
        @pl.when(pl.program_id(0) == 0)
        def _():
            st_ref[...] = jnp.zeros_like(st_ref)

        dmerged = lax.dot_general(dz1_ref[...].astype(BF), wo_ref[...], NT_DIMS, preferred_element_type=F32)
        sa, sb = _sigmoid(ga_ref[...]), _sigmoid(gb_ref[...])
        dya = (dmerged * sa).astype(BF)
        dyb = (dmerged * sb).astype(BF)
        dya_ref[...] = dya
        dyb_ref[...] = dyb
        dg_ref[:, 0:D] = (dmerged * ya_ref[...] * (sa * (1.0 - sa))).astype(BF)
        dg_ref[:, D:2 * D] = (dmerged * yb_ref[...] * (sb * (1.0 - sb))).astype(BF)

        dyapre = lax.dot_general(dya, woa_ref[...], NT_DIMS, preferred_element_type=F32)
        dba_ref[...] = (dyapre * conva_ref[...]).astype(BF)
        dconva_ref[...] = dyapre * ba_ref[...]

        du3 = lax.dot_general(dyb, wob_ref[...], NT_DIMS, preferred_element_type=F32)
        xhat = xhat_ref[...]
        gamma = vec_ref[1:2, :]
        u2 = xhat * gamma + vec_ref[2:3, :]
        s = _sigmoid(u2)
        du2 = du3 * (s * (1.0 + u2 * (1.0 - s)))
        st_ref[0:1, :] += _colsum(du2 * xhat)
        st_ref[1:2, :] += _colsum(du2)
        du1 = _ln_bwd(du2, xhat, rstd_ref[...], gamma)
        st_ref[2:3, :] += _colsum(du1)
        du1_ref[...] = du1

    tok = pl.BlockSpec((tm, D), lambda i: (i, 0))
    wfull = pl.BlockSpec((D, D), lambda i: (0, 0))
    vec = pl.BlockSpec((8, D), lambda i: (0, 0))
    return pl.pallas_call(
        body, name="merge_bwd", grid=(T // tm,),
        in_specs=[tok, _seg(tm, 5), _seg(tm, 6), _seg(tm, 0), tok, tok, tok, tok, pl.BlockSpec((tm, 1), lambda i: (i, 0)),
                  wfull, wfull, wfull, vec],
        out_specs=[tok, tok, pl.BlockSpec((tm, 2 * D), lambda i: (i, 0)), tok, tok, tok, vec],
        out_shape=[jax.ShapeDtypeStruct((T, D), BF), jax.ShapeDtypeStruct((T, D), BF),
                   jax.ShapeDtypeStruct((T, 2 * D), BF), jax.ShapeDtypeStruct((T, D), BF),
                   jax.ShapeDtypeStruct((T, D), F32), jax.ShapeDtypeStruct((T, D), F32),
                   jax.ShapeDtypeStruct((8, D), F32)],
        compiler_params=_params(1))(dz1, p, p, p, ya, yb, conva, xhatb, rstdb, woa, wob, wo, vecs)


def _rows8(v):
    out = v[0:8]
    for q in range(1, RC // 8):
        out = out + v[8 * q:8 * q + 8]
    return out


def _conv_bwd(dconva, du1, p, dba, dg, wa, wb, tt=256):
    T = p.shape[0]
    nsteps = T // tt

    def body(dca_ref, dca_n, du1_ref, du1_n, ca, va, vb, gb, ca_p, va_p, vb_p, gb_p, dba_ref, dg_ref, wa_ref, wb_ref,
             dp_ref, gw_ref, cabuf, u0buf, dcabuf, du1buf, dcain, du0, gwa, gwb):
        i = pl.program_id(0)
        first, last = i == 0, i == nsteps - 1

        @pl.when(first)
        def _():
            gwa[...] = jnp.zeros_like(gwa)
            gwb[...] = jnp.zeros_like(gwb)

        cabuf[0:HA, :] = jnp.where(first, 0.0, ca_p[...] * va_p[...])
        cabuf[HA:HA + tt, :] = ca[...] * va[...]
        sg = _sigmoid(gb[...])
        u0buf[0:HB, :] = jnp.where(first, 0.0, vb_p[...] * _sigmoid(gb_p[...]))
        u0buf[HB:HB + tt, :] = vb[...] * sg
        dcabuf[0:tt, :] = dca_ref[...]
        dcabuf[tt:tt + HA, :] = jnp.where(last, 0.0, dca_n[...])
        du1buf[0:tt, :] = du1_ref[...]
        du1buf[tt:tt + HB, :] = jnp.where(last, 0.0, du1_n[...])

        def lane_body(cidx, carry):
            ls = pl.ds(pl.multiple_of(cidx * LANES, LANES), LANES)
            for r in range(tt // RC):
                dout = dcabuf[pl.ds(r * RC, RC), ls]
                acc = jnp.zeros((RC, LANES), F32)
                for k in range(KA):
                    acc = acc + wa_ref[k:k + 1, ls] * dcabuf[pl.ds(r * RC + KA - 1 - k, RC), ls]
                    gwa[8 * k:8 * k + 8, ls] += _rows8(dout * cabuf[pl.ds(HA - (KA - 1) + k + r * RC, RC), ls])
                dcain[pl.ds(r * RC, RC), ls] = acc
                dout = du1buf[pl.ds(r * RC, RC), ls]
                acc = jnp.zeros((RC, LANES), F32)
                for k in range(KB):
                    acc = acc + wb_ref[k:k + 1, ls] * du1buf[pl.ds(r * RC + KB - 1 - k, RC), ls]
                    gwb[8 * k:8 * k + 8, ls] += _rows8(dout * u0buf[pl.ds(HB - (KB - 1) + k + r * RC, RC), ls])
                du0[pl.ds(r * RC, RC), ls] = acc
            return carry

        lax.fori_loop(0, D // LANES, lane_body, 0)
        dca_in = dcain[...]
        du0v = du0[...]
        dp_ref[:, 0:D] = dba_ref[...]
        dp_ref[:, D:2 * D] = (dca_in * va[...]).astype(BF)
        dp_ref[:, 2 * D:3 * D] = (dca_in * ca[...]).astype(BF)
        dp_ref[:, 3 * D:4 * D] = (du0v * sg).astype(BF)
        dp_ref[:, 4 * D:5 * D] = (du0v * vb[...] * (sg * (1.0 - sg))).astype(BF)
        dp_ref[:, 5 * D:7 * D] = dg_ref[...]

        @pl.when(last)
        def _():
            gw_ref[...] = jnp.zeros_like(gw_ref)
            for k in range(KA):
                gw_ref[k:k + 1, :] = _colsum(gwa[8 * k:8 * k + 8, :])
            for k in range(KB):
                gw_ref[8 + k:9 + k, :] = _colsum(gwb[8 * k:8 * k + 8, :])

    full = lambda r: pl.BlockSpec((r, D), lambda i: (0, 0))
    tok = pl.BlockSpec((tt, D), lambda i: (i, 0))
    nxt = lambda h: pl.BlockSpec((h, D), lambda i: (jnp.minimum((i + 1) * (tt // h), T // h - 1), 0))
    return pl.pallas_call(
        body, name="conv_bwd", grid=(nsteps,),
        in_specs=[tok, nxt(HA), tok, nxt(HB),
                  _seg(tt, 1), _seg(tt, 2), _seg(tt, 3), _seg(tt, 4),
                  _prev(tt, HA, 1), _prev(tt, HA, 2), _prev(tt, HB, 3), _prev(tt, HB, 4),
                  tok, pl.BlockSpec((tt, 2 * D), lambda i: (i, 0)), full(8), full(32)],
        out_specs=[pl.BlockSpec((tt, 7 * D), lambda i: (i, 0)), full(40)],
        out_shape=[jax.ShapeDtypeStruct((T, 7 * D), BF), jax.ShapeDtypeStruct((40, D), F32)],
        scratch_shapes=[pltpu.VMEM((HA + tt, D), F32), pltpu.VMEM((HB + tt, D), F32),
                        pltpu.VMEM((tt + HA, D), F32), pltpu.VMEM((tt + HB, D), F32),
                        pltpu.VMEM((tt, D), F32), pltpu.VMEM((tt, D), F32),
                        pltpu.VMEM((8 * KA, D), F32), pltpu.VMEM((8 * KB, D), F32)],
        compiler_params=_params(1))(dconva, dconva, du1, du1, p, p, p, p, p, p, p, p, dba, dg, wa, wb)


def _inproj_bwd(dp, win_g, dz1, tm=512):
    T = dz1.shape[0]
    nb, _, bw = win_g.shape

    def body(dp_ref, w_ref, dz1_ref, o_ref):
        j = pl.program_id(1)

        @pl.when(j == 0)
        def _():
            o_ref[...] = ALPHA * dz1_ref[...]

        o_ref[...] += lax.dot_general(dp_ref[...], w_ref[0], NT_DIMS, preferred_element_type=F32)

    tok = pl.BlockSpec((tm, D), lambda i, j: (i, 0))
    return pl.pallas_call(
        body, name="inproj_bwd", grid=(T // tm, nb),
        in_specs=[pl.BlockSpec((tm, bw), lambda i, j: (i, j)), pl.BlockSpec((1, D, bw), lambda i, j: (j, 0, 0)), tok],
        out_specs=tok,
        out_shape=jax.ShapeDtypeStruct((T, D), F32),
        compiler_params=_params(2))(dp, win_g, dz1)


def _sum_blocks(g8):
    def body(g_ref, o_ref):
        acc = g_ref[0]
        for j in range(1, NDEV):
            acc = acc + g_ref[j]
        o_ref[...] = acc

    return pl.pallas_call(body, name="sum_small_grads",
                          out_shape=jax.ShapeDtypeStruct(g8.shape[1:], g8.dtype))(g8)


def _adamw(w, m, v, own, landing, name, rb):
    R, C = w.shape
    nl = 0 if landing is None else landing.shape[0]

    def body(*refs):
        w_ref, m_ref, v_ref, own_ref = refs[:4]
        l_ref = refs[4] if nl else None
        g_ref, d_ref, nm_ref, nv_ref = refs[4 + (1 if nl else 0):]
        g = own_ref[...]
        for k in range(nl):
            g = g + l_ref[k].astype(F32)
        nm = ADAM_B1 * m_ref[...] + (1.0 - ADAM_B1) * g
        nv = ADAM_B2 * v_ref[...] + (1.0 - ADAM_B2) * (g * g)
        m_hat = nm / (1.0 - ADAM_B1 ** ADAM_STEP)
        v_hat = nv / (1.0 - ADAM_B2 ** ADAM_STEP)
        g_ref[...] = g
        d_ref[...] = -ADAM_LR * (m_hat / (jnp.sqrt(v_hat) + ADAM_EPS) + ADAM_WD * w_ref[...])
        nm_ref[...] = nm
        nv_ref[...] = nv

    blk = pl.BlockSpec((rb, C), lambda i: (i, 0))
    in_specs = [blk] * 4 + ([pl.BlockSpec((nl, rb, C), lambda i: (0, i, 0))] if nl else [])
    args = (w, m, v, own) + ((landing,) if nl else ())
    return pl.pallas_call(
        body, name=name, grid=(R // rb,), in_specs=in_specs, out_specs=[blk] * 4,
        out_shape=[jax.ShapeDtypeStruct((R, C), F32)] * 4,
        compiler_params=_params(1))(*args)


def _pad_rows(a, rows):
    return jnp.pad(a, ((0, rows - a.shape[0]), (0, 0)))


def _local_step(x, target, win_g, wup_g, wdown, woa, wob, wo, wa, wb, vecs):
    p = _inproj(x, win_g)
    yapre, conva, xhatb, rstdb, u3 = _mixer_fwd(p, wa, wb, vecs)
    ya, yb, merged, xhat1, rstd1, x1b = _post_mixer(yapre, u3, p, x, woa, wob, wo, vecs)
    hpre, h = _mlp_up(x1b, wup_g)
    dz2, st2 = _mlp_down_loss(h, wdown, xhat1, target, vecs)

    dhpre = _mlp_down_bwd(dz2, wdown, hpre)
    g_wdown = _tn_matmul(h, dz2, NDEV, 512, D, True, False, "grad_w_down")
    dz1, st1 = _mlp_up_bwd(dhpre, wup_g, dz2, xhat1, rstd1, vecs)
    g_wup = _tn_matmul(x1b, dhpre, NDEV, D, 512, False, True, "grad_w_up")
    dya, dyb, dg, dba, dconva, du1, stb = _merge_bwd(dz1, p, ya, yb, conva, xhatb, rstdb, woa, wob, wo, vecs)
    g_wo = _tn_matmul(merged, dz1, 1, D, D, False, False, "grad_w_o")
    g_woa = _tn_matmul(yapre, dya, 1, D, D, False, False, "grad_w_out_a")
    g_wob = _tn_matmul(u3, dyb, 1, D, D, False, False, "grad_w_out_b")
    dp, gw = _conv_bwd(dconva, du1, p, dba, dg, wa, wb)
    g_win = _tn_matmul(x, dp, NDEV, D, 896, False, True, "grad_w_in")
    grad_x = _inproj_bwd(dp, win_g, dz1)

    small = jnp.concatenate([stb[2:3], stb[0:2], st1[0:2], st2[0:3], gw], axis=0)
    return grad_x, (g_win, g_wup, g_wdown, g_woa, g_wob, g_wo), small


def kernel(x, w_in, conv_a_w, w_out_a, conv_b_w, conv_b_bias, ln_b_gamma, ln_b_beta, w_out_b, w_o, ln1_gamma, ln1_beta, w_up, w_down, ln2_gamma, ln2_beta, loss_target, m_w_in, m_conv_a_w, m_w_out_a, m_conv_b_w, m_conv_b_bias, m_ln_b_gamma, m_ln_b_beta, m_w_out_b, m_w_o, m_ln1_gamma, m_ln1_beta, m_w_up, m_w_down, m_ln2_gamma, m_ln2_beta, v_w_in, v_conv_a_w, v_w_out_a, v_conv_b_w, v_conv_b_bias, v_ln_b_gamma, v_ln_b_beta, v_w_out_b, v_w_o, v_ln1_gamma, v_ln1_beta, v_w_up, v_w_down, v_ln2_gamma, v_ln2_beta):
    T = x.shape[1]
    me = 4 * lax.axis_index("x") + 2 * lax.axis_index("y") + lax.axis_index("c")

    conv_shard = jnp.concatenate([_pad_rows(conv_a_w, 8), _pad_rows(conv_b_w, 32)], axis=0)
    win_g, wup_g, wdown_g, woa_g, wob_g, wo_g, conv_g = _allgather(
        [w_in.astype(BF), w_up.astype(BF), w_down.astype(BF), w_out_a.astype(BF), w_out_b.astype(BF), w_o.astype(BF),
         conv_shard], "allgather_weights")
    conv_full = jnp.transpose(conv_g, (1, 0, 2)).reshape(40, D)
    vecs = jnp.stack([conv_b_bias, ln_b_gamma, ln_b_beta, ln1_gamma, ln1_beta, ln2_gamma, ln2_beta,
                      jnp.zeros_like(ln2_beta)])

    grad_x, big, small = _local_step(
        x[0], loss_target[0], win_g, wup_g, wdown_g.reshape(NDEV * 512, D), woa_g.reshape(D, D), wob_g.reshape(D, D),
        wo_g.reshape(D, D), conv_full[0:8], conv_full[8:40], vecs)

    (g_win, g_wup, g_wdown, g_woa, g_wob, g_wo) = big
    blocked16 = [g_win[1], g_wup[1], g_wdown[1], g_woa[1].reshape(NDEV, 128, D), g_wob[1].reshape(NDEV, 128, D),
                 g_wo[1].reshape(NDEV, 128, D)]
    landing = _exchange(blocked16, "exchange_grads")
    (small_g,) = _allgather([small], "allgather_small_grads")
    small_sum = _sum_blocks(small_g)

    loss = lax.psum(0.5 / D * jnp.sum(small[7]), ("x", "y", "c"))

    def own(g32):
        blocked = g32 if g32.shape[0] == NDEV else g32.reshape(NDEV, 128, D)
        return lax.dynamic_index_in_dim(blocked, me, 0, keepdims=False)

    r_win = _adamw(w_in, m_w_in, v_w_in, own(g_win[0]), landing[0], "adamw_w_in", 256)
    r_wup = _adamw(w_up, m_w_up, v_w_up, own(g_wup[0]), landing[1], "adamw_w_up", 256)
    r_wdown = _adamw(w_down, m_w_down, v_w_down, own(g_wdown[0]), landing[2], "adamw_w_down", 256)
    r_woa = _adamw(w_out_a, m_w_out_a, v_w_out_a, own(g_woa[0]), landing[3], "adamw_w_out_a", 128)
    r_wob = _adamw(w_out_b, m_w_out_b, v_w_out_b, own(g_wob[0]), landing[4], "adamw_w_out_b", 128)
    r_wo = _adamw(w_o, m_w_o, v_w_o, own(g_wo[0]), landing[5], "adamw_w_o", 128)

    vec_m = jnp.stack([m_conv_b_bias, m_ln_b_gamma, m_ln_b_beta, m_ln1_gamma, m_ln1_beta, m_ln2_gamma, m_ln2_beta,
                       jnp.zeros_like(ln2_beta)])
    vec_v = jnp.stack([v_conv_b_bias, v_ln_b_gamma, v_ln_b_beta, v_ln1_gamma, v_ln1_beta, v_ln2_gamma, v_ln2_beta,
                       jnp.zeros_like(ln2_beta)])
    vec_g = jnp.concatenate([small_sum[0:7], jnp.zeros((1, D), F32)], axis=0)
    r_vec = _adamw(vecs, vec_m, vec_v, vec_g, None, "adamw_vectors", 8)

    conv_cols = lax.dynamic_slice_in_dim(small_sum[8:48], me * 128, 128, axis=1)
    conv_m = jnp.concatenate([_pad_rows(m_conv_a_w, 8), _pad_rows(m_conv_b_w, 32)], axis=0)
    conv_v = jnp.concatenate([_pad_rows(v_conv_a_w, 8), _pad_rows(v_conv_b_w, 32)], axis=0)
    r_conv = _adamw(conv_shard, conv_m, conv_v, conv_cols, None, "adamw_conv", 40)

    per_weight = []
    for q in range(4):
        per_weight.append([
            r_win[q], r_conv[q][0:KA], r_woa[q], r_conv[q][8:8 + KB],
            r_vec[q][0], r_vec[q][1], r_vec[q][2], r_wob[q], r_wo[q], r_vec[q][3], r_vec[q][4],
            r_wup[q], r_wdown[q], r_vec[q][5], r_vec[q][6]])
    return (loss, grad_x[None], *per_weight[0], *per_weight[1], *per_weight[2], *per_weight[3])
```

```python
import functools

import jax
import jax.numpy as jnp
from jax import lax
from jax.experimental import pallas as pl
from jax.experimental.pallas import tpu as pltpu

F32 = jnp.float32
BF = jnp.bfloat16
D = 1024
NDEV = 8
ALPHA = 2.0 ** 0.25
LN_EPS = 1e-5
KA, KB = 3, 31
HA, HB = 16, 32
HN = 8
RC = 64
LANES = 128
VMEM_LIMIT = 48 * 1024 * 1024
MESH = pl.DeviceIdType.MESH
ADAM_LR, ADAM_B1, ADAM_B2, ADAM_EPS, ADAM_WD, ADAM_STEP = 0.001, 0.9, 0.999, 1e-08, 0.01, 10

NT_DIMS = (((1,), (1,)), ((), ()))
TN_DIMS = (((0,), (0,)), ((), ()))


def _params(n_axes):
    return pltpu.CompilerParams(dimension_semantics=("arbitrary",) * n_axes, vmem_limit_bytes=VMEM_LIMIT)


def _sigmoid(v):
    return jax.nn.sigmoid(v)


def _ln_fwd(z):
    mu = jnp.mean(z, axis=-1, keepdims=True)
    zc = z - mu
    var = jnp.mean(zc * zc, axis=-1, keepdims=True)
    rstd = lax.rsqrt(var + LN_EPS)
    return zc * rstd, rstd


def _ln_bwd(dy, xhat, rstd, gamma):
    dxhat = dy * gamma
    m1 = jnp.mean(dxhat, axis=-1, keepdims=True)
    m2 = jnp.mean(dxhat * xhat, axis=-1, keepdims=True)
    return rstd * (dxhat - m1 - xhat * m2)


def _colsum(v):
    return jnp.sum(v, axis=0, keepdims=True)


def _allgather(arrs, name):
    n = len(arrs)

    def body(*refs):
        ins, outs = refs[:n], refs[n:2 * n]
        send_sems, recv_sems, local_sems = refs[2 * n:]
        x, y, c = lax.axis_index("x"), lax.axis_index("y"), lax.axis_index("c")
        me, sibling = (x, y, c), (x, y, 1 - c)
        chips = [(1 - x, y), (x, 1 - y), (1 - x, 1 - y)]

        def copy(a, k, block, to, src=None):
            px, py, pc = block
            rows = outs[a].at[4 * px + 2 * py + pc]
            return pltpu.make_async_remote_copy(
                src_ref=rows if src is None else src, dst_ref=rows,
                send_sem=send_sems.at[a, k], recv_sem=recv_sems.at[a, k],
                device_id=to, device_id_type=MESH)

        mine = [pltpu.make_async_copy(ins[a], outs[a].at[4 * x + 2 * y + c], local_sems.at[a]) for a in range(n)]
        for cp in mine:
            cp.start()
        first = []
        for a in range(n):
            first.append(copy(a, 0, me, sibling, src=ins[a]))
            first += [copy(a, 1 + j, me, (*chip, c), src=ins[a]) for j, chip in enumerate(chips)]
        for cp in first:
            cp.start()
        passed = []
        for j, chip in enumerate(chips):
            for a in range(n):
                copy(a, 1 + j, (*chip, c), me).wait_recv()
                fwd = copy(a, 4 + j, (*chip, c), sibling)
                fwd.start()
                passed.append(fwd)
        for a in range(n):
            copy(a, 0, sibling, me).wait_recv()
            for j, chip in enumerate(chips):
                copy(a, 4 + j, (*chip, 1 - c), me).wait_recv()
        for cp in first + passed:
            cp.wait_send()
        for cp in mine:
            cp.wait()

    any_spec = pl.BlockSpec(memory_space=pl.ANY)
    return pl.pallas_call(
        body, name=name,
        in_specs=[any_spec] * n, out_specs=[any_spec] * n,
        out_shape=[jax.ShapeDtypeStruct((NDEV,) + a.shape, a.dtype) for a in arrs],
        scratch_shapes=[pltpu.SemaphoreType.DMA((n, 7)), pltpu.SemaphoreType.DMA((n, 7)), pltpu.SemaphoreType.DMA((n,))],
    )(*arrs)


def _exchange(blocked, name):
    n = len(blocked)

    def body(*refs):
        ins, outs = refs[:n], refs[n:2 * n]
        send_sems, recv_sems = refs[2 * n:]
        x, y, c = lax.axis_index("x"), lax.axis_index("y"), lax.axis_index("c")
        copies = []
        for k in range(1, NDEV):
            px = 1 - x if k & 4 else x
            py = 1 - y if k & 2 else y
            pc = 1 - c if k & 1 else c
            for a in range(n):
                copies.append(pltpu.make_async_remote_copy(
                    src_ref=ins[a].at[4 * px + 2 * py + pc], dst_ref=outs[a].at[k - 1],
                    send_sem=send_sems.at[a, k - 1], recv_sem=recv_sems.at[a, k - 1],
                    device_id=(px, py, pc), device_id_type=MESH))
        for cp in copies:
            cp.start()
        for cp in copies:
            cp.wait()

    any_spec = pl.BlockSpec(memory_space=pl.ANY)
    return pl.pallas_call(
        body, name=name,
        in_specs=[any_spec] * n, out_specs=[any_spec] * n,
        out_shape=[jax.ShapeDtypeStruct((NDEV - 1,) + a.shape[1:], a.dtype) for a in blocked],
        scratch_shapes=[pltpu.SemaphoreType.DMA((n, 7)), pltpu.SemaphoreType.DMA((n, 7))],
    )(*blocked)


def _inproj(x, win_g, tm=1024):
    T = x.shape[0]
    nb, _, bw = win_g.shape

    def body(x_ref, w_ref, o_ref, xb_ref):
        @pl.when(pl.program_id(1) == 0)
        def _():
            xb_ref[...] = x_ref[...].astype(BF)

        o_ref[...] = jnp.dot(xb_ref[...], w_ref[0], preferred_element_type=F32).astype(BF)

    return pl.pallas_call(
        body, name="inproj", grid=(T // tm, nb),
        in_specs=[pl.BlockSpec((tm, D), lambda i, j: (i, 0)), pl.BlockSpec((1, D, bw), lambda i, j: (j, 0, 0))],
        out_specs=[pl.BlockSpec((tm, bw), lambda i, j: (i, j)), pl.BlockSpec((tm, D), lambda i, j: (i, 0))],
        out_shape=[jax.ShapeDtypeStruct((T, nb * bw), BF), jax.ShapeDtypeStruct((T, D), BF)],
        compiler_params=_params(2))(x, win_g)


def _seg(tt, s):
    return pl.BlockSpec((tt, D), lambda i: (i, s))


def _prev(tt, h, s):
    return pl.BlockSpec((h, D), lambda i: (jnp.maximum(i * (tt // h) - 1, 0), s))


def _shifted_copies(shbuf, src, ls, n):
    for s in range(1, 8):
        shbuf[s, 0:n, :] = src[pl.ds(s, n), ls]


def _tap(shbuf, src, ls, off, rows):
    s, q = off % 8, off // 8
    if s == 0:
        return src[pl.ds(off, rows), ls]
    return shbuf[s, pl.ds(8 * q, rows), :]


def _mixer_fwd(p, wa, wb, vecs, tt=256):
    T = p.shape[0]

    def body(ba, ca, va, vb, gb, ca_p, va_p, vb_p, gb_p, wa_ref, wb_ref, vec_ref,
             yapre_ref, conva_ref, xhat_ref, rstd_ref, u3_ref, cabuf, u0buf, u1buf, shu):
        first = pl.program_id(0) == 0
        f = lambda ref: ref[...].astype(F32)
        cabuf[0:HA, :] = jnp.where(first, 0.0, f(ca_p) * f(va_p))
        cabuf[HA:HA + tt, :] = f(ca) * f(va)
        u0buf[0:HB, :] = jnp.where(first, 0.0, f(vb_p) * _sigmoid(f(gb_p)))
        u0buf[HB:HB + tt, :] = f(vb) * _sigmoid(f(gb))

        def lane_body(cidx, carry):
            ls = pl.ds(pl.multiple_of(cidx * LANES, LANES), LANES)
            _shifted_copies(shu, u0buf, ls, tt + HB - 8)
            for r in range(tt // RC):
                acc = jnp.zeros((RC, LANES), F32)
                for k in range(KA):
                    acc = acc + wa_ref[k:k + 1, ls] * cabuf[pl.ds(HA - (KA - 1) + k + r * RC, RC), ls]
                conva_ref[pl.ds(r * RC, RC), ls] = acc
                acc = jnp.zeros((RC, LANES), F32)
                for k in range(KB):
                    acc = acc + wb_ref[k:k + 1, ls] * _tap(shu, u0buf, ls, HB - (KB - 1) + k + r * RC, RC)
                u1buf[pl.ds(r * RC, RC), ls] = acc
            return carry

        lax.fori_loop(0, D // LANES, lane_body, 0)
        yapre_ref[...] = (f(ba) * conva_ref[...]).astype(BF)
        xhat, rstd = _ln_fwd(u1buf[...] + vec_ref[0:1, :])
        xhat_ref[...] = xhat
        rstd_ref[...] = rstd
        u2 = xhat * vec_ref[1:2, :] + vec_ref[2:3, :]
        u3_ref[...] = (u2 * _sigmoid(u2)).astype(BF)

    full = lambda r: pl.BlockSpec((r, D), lambda i: (0, 0))
    tok = pl.BlockSpec((tt, D), lambda i: (i, 0))
    return pl.pallas_call(
        body, name="mixer_fwd", grid=(T // tt,),
        in_specs=[_seg(tt, 0), _seg(tt, 1), _seg(tt, 2), _seg(tt, 3), _seg(tt, 4),
                  _prev(tt, HA, 1), _prev(tt, HA, 2), _prev(tt, HB, 3), _prev(tt, HB, 4),
                  full(8), full(32), full(8)],
        out_specs=[tok, tok, tok, pl.BlockSpec((tt, 1), lambda i: (i, 0)), tok],
        out_shape=[jax.ShapeDtypeStruct((T, D), BF), jax.ShapeDtypeStruct((T, D), F32),
                   jax.ShapeDtypeStruct((T, D), F32), jax.ShapeDtypeStruct((T, 1), F32),
                   jax.ShapeDtypeStruct((T, D), BF)],
        scratch_shapes=[pltpu.VMEM((HA + tt, D), F32), pltpu.VMEM((HB + tt, D), F32), pltpu.VMEM((tt, D), F32),
                        pltpu.VMEM((8, HB + tt, LANES), F32)],
        compiler_params=_params(1))(p, p, p, p, p, p, p, p, p, wa, wb, vecs)


def _post_mixer(yapre, u3, p, x, woa, wob, wo, vecs, tm=256):
    T = x.shape[0]

    def body(yapre_ref, u3_ref, ga_ref, gb_ref, x_ref, woa_ref, wob_ref, wo_ref, vec_ref,
             ya_ref, yb_ref, merged_ref, xhat_ref, rstd_ref, x1b_ref):
        ya = jnp.dot(yapre_ref[...], woa_ref[...], preferred_element_type=F32)
        yb = jnp.dot(u3_ref[...], wob_ref[...], preferred_element_type=F32)
        ya_ref[...] = ya.astype(BF)
        yb_ref[...] = yb.astype(BF)
        merged = (_sigmoid(ga_ref[...].astype(F32)) * ya + _sigmoid(gb_ref[...].astype(F32)) * yb).astype(BF)
        merged_ref[...] = merged
        mix = jnp.dot(merged, wo_ref[...], preferred_element_type=F32)
        xhat, rstd = _ln_fwd(ALPHA * x_ref[...] + mix)
        xhat_ref[...] = xhat
        rstd_ref[...] = rstd
        x1b_ref[...] = (xhat * vec_ref[3:4, :] + vec_ref[4:5, :]).astype(BF)

    tok = pl.BlockSpec((tm, D), lambda i: (i, 0))
    wfull = pl.BlockSpec((D, D), lambda i: (0, 0))
    one = pl.BlockSpec((tm, 1), lambda i: (i, 0))
    return pl.pallas_call(
        body, name="post_mixer", grid=(T // tm,),
        in_specs=[tok, tok, _seg(tm, 5), _seg(tm, 6), tok, wfull, wfull, wfull, pl.BlockSpec((8, D), lambda i: (0, 0))],
        out_specs=[tok, tok, tok, tok, one, tok],
        out_shape=[jax.ShapeDtypeStruct((T, D), BF), jax.ShapeDtypeStruct((T, D), BF),
                   jax.ShapeDtypeStruct((T, D), BF), jax.ShapeDtypeStruct((T, D), F32),
                   jax.ShapeDtypeStruct((T, 1), F32), jax.ShapeDtypeStruct((T, D), BF)],
        compiler_params=_params(1))(yapre, u3, p, p, x, woa, wob, wo, vecs)


def _mlp_up(x1b, wup_g, tm=1024):
    T = x1b.shape[0]
    nb, _, bw = wup_g.shape

    def body(x_ref, w_ref, r_ref, h_ref):
        r = jnp.maximum(jnp.dot(x_ref[...], w_ref[0], preferred_element_type=F32), 0.0)
        r_ref[...] = r.astype(BF)
        h_ref[...] = (r * r).astype(BF)

    out = pl.BlockSpec((tm, bw), lambda i, j: (i, j))
    return pl.pallas_call(
        body, name="mlp_up", grid=(T // tm, nb),
        in_specs=[pl.BlockSpec((tm, D), lambda i, j: (i, 0)), pl.BlockSpec((1, D, bw), lambda i, j: (j, 0, 0))],
        out_specs=[out, out],
        out_shape=[jax.ShapeDtypeStruct((T, nb * bw), BF), jax.ShapeDtypeStruct((T, nb * bw), BF)],
        compiler_params=_params(2))(x1b, wup_g)


def _mlp_down_loss(h, wdown, xhat1, target, vecs, tm=512, tk=512):
    T, dff = h.shape
    nk = dff // tk

    def body(h_ref, w_ref, xhat1_ref, tgt_ref, vec_ref, dz2_ref, dz2b_ref, st_ref, acc):
        i, k = pl.program_id(0), pl.program_id(1)

        @pl.when(jnp.logical_and(i == 0, k == 0))
        def _():
            st_ref[...] = jnp.zeros_like(st_ref)

        @pl.when(k == 0)
        def _():
            acc[...] = jnp.zeros_like(acc)

        acc[...] += jnp.dot(h_ref[...], w_ref[...], preferred_element_type=F32)

        @pl.when(k == nk - 1)
        def _():
            x1 = xhat1_ref[...] * vec_ref[3:4, :] + vec_ref[4:5, :]
            xhat2, rstd2 = _ln_fwd(ALPHA * x1 + acc[...])
            g2 = vec_ref[5:6, :]
            diff = xhat2 * g2 + vec_ref[6:7, :] - tgt_ref[...]
            dx2 = diff * (1.0 / D)
            st_ref[0:1, :] += _colsum(dx2 * xhat2)
            st_ref[1:2, :] += _colsum(dx2)
            st_ref[2:3, :] += _colsum(diff * diff)
            dz2 = _ln_bwd(dx2, xhat2, rstd2, g2)
            dz2_ref[...] = dz2
            dz2b_ref[...] = dz2.astype(BF)

    tok = pl.BlockSpec((tm, D), lambda i, k: (i, 0))
    vec = pl.BlockSpec((8, D), lambda i, k: (0, 0))
    return pl.pallas_call(
        body, name="mlp_down_loss", grid=(T // tm, nk),
        in_specs=[pl.BlockSpec((tm, tk), lambda i, k: (i, k)), pl.BlockSpec((tk, D), lambda i, k: (k, 0)), tok, tok, vec],
        out_specs=[tok, tok, vec],
        out_shape=[jax.ShapeDtypeStruct((T, D), F32), jax.ShapeDtypeStruct((T, D), BF), jax.ShapeDtypeStruct((8, D), F32)],
        scratch_shapes=[pltpu.VMEM((tm, D), F32)],
        compiler_params=_params(2))(h, wdown, xhat1, target, vecs)


def _mlp_down_bwd(dz2b, wdown, r, tm=1024, tk=512):
    T, dff = r.shape

    def body(dz_ref, w_ref, r_ref, o_ref):
        dh = lax.dot_general(dz_ref[...], w_ref[...], NT_DIMS, preferred_element_type=F32)
        o_ref[...] = (dh * (2.0 * r_ref[...].astype(F32))).astype(BF)

    blk = pl.BlockSpec((tm, tk), lambda i, j: (i, j))
    return pl.pallas_call(
        body, name="mlp_down_bwd", grid=(T // tm, dff // tk),
        in_specs=[pl.BlockSpec((tm, D), lambda i, j: (i, 0)), pl.BlockSpec((tk, D), lambda i, j: (j, 0)), blk],
        out_specs=blk,
        out_shape=jax.ShapeDtypeStruct((T, dff), BF),
        compiler_params=_params(2))(dz2b, wdown, r)


def _tn_matmul(a, b, nblk, a_bw, b_bw, a_blocked, b_blocked, name, tt=512):
    T = a.shape[0]
    nt = T // tt

    def body(a_ref, b_ref, o32_ref, o16_ref):
        t = pl.program_id(1)

        @pl.when(t == 0)
        def _():
            o32_ref[...] = jnp.zeros_like(o32_ref)

        o32_ref[0] += lax.dot_general(a_ref[...], b_ref[...], TN_DIMS, preferred_element_type=F32)

        @pl.when(t == nt - 1)
        def _():
            o16_ref[...] = o32_ref[...].astype(BF)

    a_spec = pl.BlockSpec((tt, a_bw), (lambda j, t: (t, j)) if a_blocked else (lambda j, t: (t, 0)))
    b_spec = pl.BlockSpec((tt, b_bw), (lambda j, t: (t, j)) if b_blocked else (lambda j, t: (t, 0)))
    out = pl.BlockSpec((1, a_bw, b_bw), lambda j, t: (j, 0, 0))
    return pl.pallas_call(
        body, name=name, grid=(nblk, nt),
        in_specs=[a_spec, b_spec], out_specs=[out, out],
        out_shape=[jax.ShapeDtypeStruct((nblk, a_bw, b_bw), F32), jax.ShapeDtypeStruct((nblk, a_bw, b_bw), BF)],
        compiler_params=_params(2))(a, b)


def _mlp_up_bwd(dhpre, wup_g, dz2, xhat1, rstd1, vecs, tm=512):
    T = dz2.shape[0]
    nb, _, bw = wup_g.shape

    def body(dh_ref, w_ref, dz2_ref, xhat_ref, rstd_ref, vec_ref, dz1_ref, dz1b_ref, st_ref, acc):
        i, j = pl.program_id(0), pl.program_id(1)

        @pl.when(jnp.logical_and(i == 0, j == 0))
        def _():
            st_ref[...] = jnp.zeros_like(st_ref)

        @pl.when(j == 0)
        def _():
            acc[...] = jnp.zeros_like(acc)

        acc[...] += lax.dot_general(dh_ref[...], w_ref[0], NT_DIMS, preferred_element_type=F32)

        @pl.when(j == nb - 1)
        def _():
            dx1 = acc[...] + ALPHA * dz2_ref[...]
            xhat = xhat_ref[...]
            st_ref[0:1, :] += _colsum(dx1 * xhat)
            st_ref[1:2, :] += _colsum(dx1)
            dz1 = _ln_bwd(dx1, xhat, rstd_ref[...], vec_ref[3:4, :])
            dz1_ref[...] = dz1
            dz1b_ref[...] = dz1.astype(BF)

    tok = pl.BlockSpec((tm, D), lambda i, j: (i, 0))
    vec = pl.BlockSpec((8, D), lambda i, j: (0, 0))
    return pl.pallas_call(
        body, name="mlp_up_bwd", grid=(T // tm, nb),
        in_specs=[pl.BlockSpec((tm, bw), lambda i, j: (i, j)), pl.BlockSpec((1, D, bw), lambda i, j: (j, 0, 0)),
                  tok, tok, pl.BlockSpec((tm, 1), lambda i, j: (i, 0)), vec],
        out_specs=[tok, tok, vec],
        out_shape=[jax.ShapeDtypeStruct((T, D), F32), jax.ShapeDtypeStruct((T, D), BF), jax.ShapeDtypeStruct((8, D), F32)],
        scratch_shapes=[pltpu.VMEM((tm, D), F32)],
        compiler_params=_params(2))(dhpre, wup_g, dz2, xhat1, rstd1, vecs)


def _merge_bwd(dz1, p, ya, yb, conva, xhatb, rstdb, woa, wob, wo, vecs, tm=256):
    T = dz1.shape[0]

    def body(dz1_ref, ga_ref, gb_ref, ba_ref, ya_ref, yb_ref, conva_ref, xhat_ref, rstd_ref,
             woa_ref, wob_ref, wo_ref, vec_ref,
             dya_ref, dyb_ref, dg_ref, dba_ref, dconva_ref, du1_ref, st_ref):
        @pl.when(pl.program_id(0) == 0)
        def _():
            st_ref[...] = jnp.zeros_like(st_ref)

        dmerged = lax.dot_general(dz1_ref[...], wo_ref[...], NT_DIMS, preferred_element_type=F32)
        sa, sb = _sigmoid(ga_ref[...].astype(F32)), _sigmoid(gb_ref[...].astype(F32))
        dya = (dmerged * sa).astype(BF)
        dyb = (dmerged * sb).astype(BF)
        dya_ref[...] = dya
        dyb_ref[...] = dyb
        dg_ref[:, 0:D] = (dmerged * ya_ref[...].astype(F32) * (sa * (1.0 - sa))).astype(BF)
        dg_ref[:, D:2 * D] = (dmerged * yb_ref[...].astype(F32) * (sb * (1.0 - sb))).astype(BF)

        dyapre = lax.dot_general(dya, woa_ref[...], NT_DIMS, preferred_element_type=F32)
        dba_ref[...] = (dyapre * conva_ref[...]).astype(BF)
        dconva_ref[...] = dyapre * ba_ref[...].astype(F32)

        du3 = lax.dot_general(dyb, wob_ref[...], NT_DIMS, preferred_element_type=F32)
        xhat = xhat_ref[...]
        gamma = vec_ref[1:2, :]
        u2 = xhat * gamma + vec_ref[2:3, :]
        s = _sigmoid(u2)
        du2 = du3 * (s * (1.0 + u2 * (1.0 - s)))
        st_ref[0:1, :] += _colsum(du2 * xhat)
        st_ref[1:2, :] += _colsum(du2)
        du1 = _ln_bwd(du2, xhat, rstd_ref[...], gamma)
        st_ref[2:3, :] += _colsum(du1)
        du1_ref[...] = du1

    tok = pl.BlockSpec((tm, D), lambda i: (i, 0))
    wfull = pl.BlockSpec((D, D), lambda i: (0, 0))
    vec = pl.BlockSpec((8, D), lambda i: (0, 0))
    return pl.pallas_call(
        body, name="merge_bwd", grid=(T // tm,),
        in_specs=[tok, _seg(tm, 5), _seg(tm, 6), _seg(tm, 0), tok, tok, tok, tok, pl.BlockSpec((tm, 1), lambda i: (i, 0)),
                  wfull, wfull, wfull, vec],
        out_specs=[tok, tok, pl.BlockSpec((tm, 2 * D), lambda i: (i, 0)), tok, tok, tok, vec],
        out_shape=[jax.ShapeDtypeStruct((T, D), BF), jax.ShapeDtypeStruct((T, D), BF),
                   jax.ShapeDtypeStruct((T, 2 * D), BF), jax.ShapeDtypeStruct((T, D), BF),
                   jax.ShapeDtypeStruct((T, D), F32), jax.ShapeDtypeStruct((T, D), F32),
                   jax.ShapeDtypeStruct((8, D), F32)],
        compiler_params=_params(1))(dz1, p, p, p, ya, yb, conva, xhatb, rstdb, woa, wob, wo, vecs)


def _rows8(v):
    out = v[0:8]
    for q in range(1, RC // 8):
        out = out + v[8 * q:8 * q + 8]
    return out


def _conv_bwd(dconva, du1, p, dba, dg, wa, wb, tt=256):
    T = p.shape[0]
    nsteps = T // tt

    def body(dca_ref, dca_n, du1_ref, du1_n, ca, va, vb, gb, ca_p, va_p, vb_p, gb_p, dba_ref, dg_ref, wa_ref, wb_ref,
             dp_ref, gw_ref, cabuf, u0buf, dcabuf, du1buf, dcain, du0, gwa, gwb, shu, shd):
        i = pl.program_id(0)
        first, last = i == 0, i == nsteps - 1

        @pl.when(first)
        def _():
            gwa[...] = jnp.zeros_like(gwa)
            gwb[...] = jnp.zeros_like(gwb)

        f = lambda ref: ref[...].astype(F32)
        cav, vav, vbv = f(ca), f(va), f(vb)
        cabuf[0:HA, :] = jnp.where(first, 0.0, f(ca_p) * f(va_p))
        cabuf[HA:HA + tt, :] = cav * vav
        sg = _sigmoid(f(gb))
        u0buf[0:HB, :] = jnp.where(first, 0.0, f(vb_p) * _sigmoid(f(gb_p)))
        u0buf[HB:HB + tt, :] = vbv * sg
        dcabuf[0:tt, :] = dca_ref[...]
        dcabuf[tt:tt + HN, :] = jnp.where(last, 0.0, dca_n[...])
        du1buf[0:tt, :] = du1_ref[...]
        du1buf[tt:tt + HB, :] = jnp.where(last, 0.0, du1_n[...])

        def lane_body(cidx, carry):
            ls = pl.ds(pl.multiple_of(cidx * LANES, LANES), LANES)
            _shifted_copies(shu, u0buf, ls, tt + HB - 8)
            _shifted_copies(shd, du1buf, ls, tt + HB - 8)
            for r in range(tt // RC):
                dout = dcabuf[pl.ds(r * RC, RC), ls]
                acc = jnp.zeros((RC, LANES), F32)
                for k in range(KA):
                    acc = acc + wa_ref[k:k + 1, ls] * dcabuf[pl.ds(r * RC + KA - 1 - k, RC), ls]
                    gwa[8 * k:8 * k + 8, ls] += _rows8(dout * cabuf[pl.ds(HA - (KA - 1) + k + r * RC, RC), ls])
                dcain[pl.ds(r * RC, RC), ls] = acc
                dout = du1buf[pl.ds(r * RC, RC), ls]
                acc = jnp.zeros((RC, LANES), F32)
                for k in range(KB):
                    acc = acc + wb_ref[k:k + 1, ls] * _tap(shd, du1buf, ls, r * RC + KB - 1 - k, RC)
                    gwb[8 * k:8 * k + 8, ls] += _rows8(dout * _tap(shu, u0buf, ls, HB - (KB - 1) + k + r * RC, RC))
                du0[pl.ds(r * RC, RC), ls] = acc
            return carry

        lax.fori_loop(0, D // LANES, lane_body, 0)
        dca_in = dcain[...]
        du0v = du0[...]
        dp_ref[:, 0:D] = dba_ref[...]
        dp_ref[:, D:2 * D] = (dca_in * vav).astype(BF)
        dp_ref[:, 2 * D:3 * D] = (dca_in * cav).astype(BF)
        dp_ref[:, 3 * D:4 * D] = (du0v * sg).astype(BF)
        dp_ref[:, 4 * D:5 * D] = (du0v * vbv * (sg * (1.0 - sg))).astype(BF)
        dp_ref[:, 5 * D:7 * D] = dg_ref[...]

        @pl.when(last)
        def _():
            gw_ref[...] = jnp.zeros_like(gw_ref)
            for k in range(KA):
                gw_ref[k:k + 1, :] = _colsum(gwa[8 * k:8 * k + 8, :])
            for k in range(KB):
                gw_ref[8 + k:9 + k, :] = _colsum(gwb[8 * k:8 * k + 8, :])

    full = lambda r: pl.BlockSpec((r, D), lambda i: (0, 0))
    tok = pl.BlockSpec((tt, D), lambda i: (i, 0))
    nxt = lambda h: pl.BlockSpec((h, D), lambda i: (jnp.minimum((i + 1) * (tt // h), T // h - 1), 0))
    return pl.pallas_call(
        body, name="conv_bwd", grid=(nsteps,),
        in_specs=[tok, nxt(HN), tok, nxt(HB),
                  _seg(tt, 1), _seg(tt, 2), _seg(tt, 3), _seg(tt, 4),
                  _prev(tt, HA, 1), _prev(tt, HA, 2), _prev(tt, HB, 3), _prev(tt, HB, 4),
                  tok, pl.BlockSpec((tt, 2 * D), lambda i: (i, 0)), full(8), full(32)],
        out_specs=[pl.BlockSpec((tt, 7 * D), lambda i: (i, 0)), full(40)],
        out_shape=[jax.ShapeDtypeStruct((T, 7 * D), BF), jax.ShapeDtypeStruct((40, D), F32)],
        scratch_shapes=[pltpu.VMEM((HA + tt, D), F32), pltpu.VMEM((HB + tt, D), F32),
                        pltpu.VMEM((tt + HN, D), F32), pltpu.VMEM((tt + HB, D), F32),
                        pltpu.VMEM((tt, D), F32), pltpu.VMEM((tt, D), F32),
                        pltpu.VMEM((8 * KA, D), F32), pltpu.VMEM((8 * KB, D), F32),
                        pltpu.VMEM((8, HB + tt, LANES), F32), pltpu.VMEM((8, HB + tt, LANES), F32)],
        compiler_params=_params(1))(dconva, dconva, du1, du1, p, p, p, p, p, p, p, p, dba, dg, wa, wb)


def _inproj_bwd(dp, win_g, dz1, tm=1024):
    T = dz1.shape[0]
    nb, _, bw = win_g.shape

    def body(dp_ref, w_ref, dz1_ref, o_ref):
        j = pl.program_id(1)

        @pl.when(j == 0)
        def _():
            o_ref[...] = ALPHA * dz1_ref[...]

        o_ref[...] += lax.dot_general(dp_ref[...], w_ref[0], NT_DIMS, preferred_element_type=F32)

    tok = pl.BlockSpec((tm, D), lambda i, j: (i, 0))
    return pl.pallas_call(
        body, name="inproj_bwd", grid=(T // tm, nb),
        in_specs=[pl.BlockSpec((tm, bw), lambda i, j: (i, j)), pl.BlockSpec((1, D, bw), lambda i, j: (j, 0, 0)), tok],
        out_specs=tok,
        out_shape=jax.ShapeDtypeStruct((T, D), F32),
        compiler_params=_params(2))(dp, win_g, dz1)


def _sum_blocks(g8):
    def body(g_ref, o_ref):
        acc = g_ref[0]
        for j in range(1, NDEV):
            acc = acc + g_ref[j]
        o_ref[...] = acc

    return pl.pallas_call(body, name="sum_small_grads",
                          out_shape=jax.ShapeDtypeStruct(g8.shape[1:], g8.dtype))(g8)


def _adamw(w, m, v, own, landing, name, rb):
    R, C = w.shape
    nl = 0 if landing is None else landing.shape[0]

    def body(*refs):
        w_ref, m_ref, v_ref, own_ref = refs[:4]
        l_ref = refs[4] if nl else None
        g_ref, d_ref, nm_ref, nv_ref = refs[4 + (1 if nl else 0):]
        g = own_ref[...]
        for k in range(nl):
            g = g + l_ref[k].astype(F32)
        nm = ADAM_B1 * m_ref[...] + (1.0 - ADAM_B1) * g
        nv = ADAM_B2 * v_ref[...] + (1.0 - ADAM_B2) * (g * g)
        m_hat = nm / (1.0 - ADAM_B1 ** ADAM_STEP)
        v_hat = nv / (1.0 - ADAM_B2 ** ADAM_STEP)
        g_ref[...] = g
        d_ref[...] = -ADAM_LR * (m_hat / (jnp.sqrt(v_hat) + ADAM_EPS) + ADAM_WD * w_ref[...])
        nm_ref[...] = nm
        nv_ref[...] = nv

    blk = pl.BlockSpec((rb, C), lambda i: (i, 0))
    in_specs = [blk] * 4 + ([pl.BlockSpec((nl, rb, C), lambda i: (0, i, 0))] if nl else [])
    args = (w, m, v, own) + ((landing,) if nl else ())
    return pl.pallas_call(
        body, name=name, grid=(R // rb,), in_specs=in_specs, out_specs=[blk] * 4,
        out_shape=[jax.ShapeDtypeStruct((R, C), F32)] * 4,
        compiler_params=_params(1))(*args)


def _pad_rows(a, rows):
    return jnp.pad(a, ((0, rows - a.shape[0]), (0, 0)))


def _local_step(x, target, win_g, wup_g, wdown, woa, wob, wo, wa, wb, vecs):
    p, xb = _inproj(x, win_g)
    yapre, conva, xhatb, rstdb, u3 = _mixer_fwd(p, wa, wb, vecs)
    ya, yb, merged, xhat1, rstd1, x1b = _post_mixer(yapre, u3, p, x, woa, wob, wo, vecs)
    r, h = _mlp_up(x1b, wup_g)
    dz2, dz2b, st2 = _mlp_down_loss(h, wdown, xhat1, target, vecs)

    dhpre = _mlp_down_bwd(dz2b, wdown, r)
    g_wdown = _tn_matmul(h, dz2b, NDEV, 512, D, True, False, "grad_w_down")
    dz1, dz1b, st1 = _mlp_up_bwd(dhpre, wup_g, dz2, xhat1, rstd1, vecs)
    g_wup = _tn_matmul(x1b, dhpre, NDEV, D, 512, False, True, "grad_w_up")
    dya, dyb, dg, dba, dconva, du1, stb = _merge_bwd(dz1b, p, ya, yb, conva, xhatb, rstdb, woa, wob, wo, vecs)
    g_wo = _tn_matmul(merged, dz1b, 1, D, D, False, False, "grad_w_o")
    g_woa = _tn_matmul(yapre, dya, 1, D, D, False, False, "grad_w_out_a")
    g_wob = _tn_matmul(u3, dyb, 1, D, D, False, False, "grad_w_out_b")
    dp, gw = _conv_bwd(dconva, du1, p, dba, dg, wa, wb)
    g_win = _tn_matmul(xb, dp, NDEV, D, 896, False, True, "grad_w_in")
    grad_x = _inproj_bwd(dp, win_g, dz1)

    small = jnp.concatenate([stb[2:3], stb[0:2], st1[0:2], st2[0:3], gw], axis=0)
    return grad_x, (g_win, g_wup, g_wdown, g_woa, g_wob, g_wo), small


def kernel(x, w_in, conv_a_w, w_out_a, conv_b_w, conv_b_bias, ln_b_gamma, ln_b_beta, w_out_b, w_o, ln1_gamma, ln1_beta, w_up, w_down, ln2_gamma, ln2_beta, loss_target, m_w_in, m_conv_a_w, m_w_out_a, m_conv_b_w, m_conv_b_bias, m_ln_b_gamma, m_ln_b_beta, m_w_out_b, m_w_o, m_ln1_gamma, m_ln1_beta, m_w_up, m_w_down, m_ln2_gamma, m_ln2_beta, v_w_in, v_conv_a_w, v_w_out_a, v_conv_b_w, v_conv_b_bias, v_ln_b_gamma, v_ln_b_beta, v_w_out_b, v_w_o, v_ln1_gamma, v_ln1_beta, v_w_up, v_w_down, v_ln2_gamma, v_ln2_beta):
    T = x.shape[1]
    me = 4 * lax.axis_index("x") + 2 * lax.axis_index("y") + lax.axis_index("c")

    conv_shard = jnp.concatenate([_pad_rows(conv_a_w, 8), _pad_rows(conv_b_w, 32)], axis=0)
    win_g, wup_g, wdown_g, woa_g, wob_g, wo_g, conv_g = _allgather(
        [w_in.astype(BF), w_up.astype(BF), w_down.astype(BF), w_out_a.astype(BF), w_out_b.astype(BF), w_o.astype(BF),
         conv_shard], "allgather_weights")
    conv_full = jnp.transpose(conv_g, (1, 0, 2)).reshape(40, D)
    vecs = jnp.stack([conv_b_bias, ln_b_gamma, ln_b_beta, ln1_gamma, ln1_beta, ln2_gamma, ln2_beta,
                      jnp.zeros_like(ln2_beta)])

    grad_x, big, small = _local_step(
        x[0], loss_target[0], win_g, wup_g, wdown_g.reshape(NDEV * 512, D), woa_g.reshape(D, D), wob_g.reshape(D, D),
        wo_g.reshape(D, D), conv_full[0:8], conv_full[8:40], vecs)

    (g_win, g_wup, g_wdown, g_woa, g_wob, g_wo) = big
    blocked16 = [g_win[1], g_wup[1], g_wdown[1], g_woa[1].reshape(NDEV, 128, D), g_wob[1].reshape(NDEV, 128, D),
                 g_wo[1].reshape(NDEV, 128, D)]
    landing = _exchange(blocked16, "exchange_grads")
    (small_g,) = _allgather([small], "allgather_small_grads")
    small_sum = _sum_blocks(small_g)

    loss = lax.psum(0.5 / D * jnp.sum(small[7]), ("x", "y", "c"))

    def own(g32):
        blocked = g32 if g32.shape[0] == NDEV else g32.reshape(NDEV, 128, D)
        return lax.dynamic_index_in_dim(blocked, me, 0, keepdims=False)

    r_win = _adamw(w_in, m_w_in, v_w_in, own(g_win[0]), landing[0], "adamw_w_in", 256)
    r_wup = _adamw(w_up, m_w_up, v_w_up, own(g_wup[0]), landing[1], "adamw_w_up", 256)
    r_wdown = _adamw(w_down, m_w_down, v_w_down, own(g_wdown[0]), landing[2], "adamw_w_down", 256)
    r_woa = _adamw(w_out_a, m_w_out_a, v_w_out_a, own(g_woa[0]), landing[3], "adamw_w_out_a", 128)
    r_wob = _adamw(w_out_b, m_w_out_b, v_w_out_b, own(g_wob[0]), landing[4], "adamw_w_out_b", 128)
    r_wo = _adamw(w_o, m_w_o, v_w_o, own(g_wo[0]), landing[5], "adamw_w_o", 128)

    vec_m = jnp.stack([m_conv_b_bias, m_ln_b_gamma, m_ln_b_beta, m_ln1_gamma, m_ln1_beta, m_ln2_gamma, m_ln2_beta,
                       jnp.zeros_like(ln2_beta)])
    vec_v = jnp.stack([v_conv_b_bias, v_ln_b_gamma, v_ln_b_beta, v_ln1_gamma, v_ln1_beta, v_ln2_gamma, v_ln2_beta,
                       jnp.zeros_like(ln2_beta)])
    vec_g = jnp.concatenate([small_sum[0:7], jnp.zeros((1, D), F32)], axis=0)
    r_vec = _adamw(vecs, vec_m, vec_v, vec_g, None, "adamw_vectors", 8)

    conv_cols = lax.dynamic_slice_in_dim(small_sum[8:48], me * 128, 128, axis=1)
    conv_m = jnp.concatenate([_pad_rows(m_conv_a_w, 8), _pad_rows(m_conv_b_w, 32)], axis=0)
    conv_v = jnp.concatenate([_pad_rows(v_conv_a_w, 8), _pad_rows(v_conv_b_w, 32)], axis=0)
    r_conv = _adamw(conv_shard, conv_m, conv_v, conv_cols, None, "adamw_conv", 40)

    per_weight = []
    for q in range(4):
        per_weight.append([
            r_win[q], r_conv[q][0:KA], r_woa[q], r_conv[q][8:8 + KB],
            r_vec[q][0], r_vec[q][1], r_vec[q][2], r_wob[q], r_wo[q], r_vec[q][3], r_vec[q][4],
            r_wup[q], r_wdown[q], r_vec[q][5], r_vec[q][6]])
    return (loss, grad_x[None], *per_weight[0], *per_weight[1], *per_weight[2], *per_weight[3])
```

```python
import functools

import jax
import jax.numpy as jnp
from jax import lax
from jax.experimental import pallas as pl
from jax.experimental.pallas import tpu as pltpu

F32 = jnp.float32
BF = jnp.bfloat16
D = 1024
NDEV = 8
ALPHA = 2.0 ** 0.25
LN_EPS = 1e-5
KA, KB = 3, 31
HA, HB = 16, 32
HN = 8
RC = 64
LANES = 128
VMEM_LIMIT = 48 * 1024 * 1024
MESH = pl.DeviceIdType.MESH
ADAM_LR, ADAM_B1, ADAM_B2, ADAM_EPS, ADAM_WD, ADAM_STEP = 0.001, 0.9, 0.999, 1e-08, 0.01, 10

ANY_SPEC = pl.BlockSpec(memory_space=pl.ANY)
NT_DIMS = (((1,), (1,)), ((), ()))
TN_DIMS = (((0,), (0,)), ((), ()))


def _params(n_axes):
    return pltpu.CompilerParams(dimension_semantics=("arbitrary",) * n_axes, vmem_limit_bytes=VMEM_LIMIT)


def _sigmoid(v):
    return jax.nn.sigmoid(v)


def _ln_fwd(z):
    mu = jnp.mean(z, axis=-1, keepdims=True)
    zc = z - mu
    var = jnp.mean(zc * zc, axis=-1, keepdims=True)
    rstd = lax.rsqrt(var + LN_EPS)
    return zc * rstd, rstd


def _ln_bwd(dy, xhat, rstd, gamma):
    dxhat = dy * gamma
    m1 = jnp.mean(dxhat, axis=-1, keepdims=True)
    m2 = jnp.mean(dxhat * xhat, axis=-1, keepdims=True)
    return rstd * (dxhat - m1 - xhat * m2)


def _colsum(v):
    return jnp.sum(v, axis=0, keepdims=True)


def _allgather(arrs, name):
    n = len(arrs)

    def body(*refs):
        g = _TwoLevelGather(refs[:n], refs[n:2 * n], *refs[2 * n:])
        g.start()
        g.forward()
        g.finish()

    return pl.pallas_call(
        body, name=name,
        in_specs=[ANY_SPEC] * n, out_specs=[ANY_SPEC] * n,
        out_shape=_TwoLevelGather.out_shape(arrs), scratch_shapes=_TwoLevelGather.scratch(n),
    )(*arrs)


class _TwoLevelGather:
    def __init__(self, ins, outs, send_sems, recv_sems, local_sems):
        self.ins, self.outs = ins, outs
        self.send_sems, self.recv_sems, self.local_sems = send_sems, recv_sems, local_sems
        x, y, c = lax.axis_index("x"), lax.axis_index("y"), lax.axis_index("c")
        self.me, self.sibling, self.c = (x, y, c), (x, y, 1 - c), c
        self.chips = [(1 - x, y), (x, 1 - y), (1 - x, 1 - y)]
        self.n = len(ins)

    @staticmethod
    def out_shape(arrs):
        return [jax.ShapeDtypeStruct((NDEV,) + a.shape, a.dtype) for a in arrs]

    @staticmethod
    def scratch(n):
        return [pltpu.SemaphoreType.DMA((n, 7)), pltpu.SemaphoreType.DMA((n, 7)), pltpu.SemaphoreType.DMA((n,))]

    def _copy(self, a, k, block, to, src=None):
        px, py, pc = block
        rows = self.outs[a].at[4 * px + 2 * py + pc]
        return pltpu.make_async_remote_copy(
            src_ref=rows if src is None else src, dst_ref=rows,
            send_sem=self.send_sems.at[a, k], recv_sem=self.recv_sems.at[a, k],
            device_id=to, device_id_type=MESH)

    def _mine(self, a):
        x, y, c = self.me
        return pltpu.make_async_copy(self.ins[a], self.outs[a].at[4 * x + 2 * y + c], self.local_sems.at[a])

    def _first(self, a):
        cps = [self._copy(a, 0, self.me, self.sibling, src=self.ins[a])]
        return cps + [self._copy(a, 1 + j, self.me, (*chip, self.c), src=self.ins[a]) for j, chip in enumerate(self.chips)]

    def _passed(self, a, j):
        return self._copy(a, 4 + j, (*self.chips[j], self.c), self.sibling)

    def start(self):
        for a in range(self.n):
            self._mine(a).start()
        for a in range(self.n):
            for cp in self._first(a):
                cp.start()

    def forward(self):
        for j, chip in enumerate(self.chips):
            for a in range(self.n):
                self._copy(a, 1 + j, (*chip, self.c), self.me).wait_recv()
                self._passed(a, j).start()

    def finish(self):
        for a in range(self.n):
            self._copy(a, 0, self.sibling, self.me).wait_recv()
            for j, chip in enumerate(self.chips):
                self._copy(a, 4 + j, (*chip, 1 - self.c), self.me).wait_recv()
        for a in range(self.n):
            for cp in self._first(a) + [self._passed(a, j) for j in range(3)]:
                cp.wait_send()
            self._mine(a).wait()


class _Push:
    def __init__(self, exch=(), gath=()):
        self.exch, self.gath = list(exch), list(gath)
        self.n = len(self.exch) + len(self.gath)

    def operands(self):
        return self.exch + self.gath

    def out_shape(self):
        return ([jax.ShapeDtypeStruct((NDEV - 1,) + a.shape[1:], a.dtype) for a in self.exch]
                + [jax.ShapeDtypeStruct((NDEV,) + a.shape, a.dtype) for a in self.gath])

    def scratch(self):
        return [pltpu.SemaphoreType.DMA((self.n, 7)), pltpu.SemaphoreType.DMA((self.n, 7)),
                pltpu.SemaphoreType.DMA((max(len(self.gath), 1),))]

    def copies(self, ins, outs, send_sems, recv_sems, local_sems):
        x, y, c = lax.axis_index("x"), lax.axis_index("y"), lax.axis_index("c")
        me = 4 * x + 2 * y + c
        ne = len(self.exch)
        remote = []
        for k in range(1, NDEV):
            px = 1 - x if k & 4 else x
            py = 1 - y if k & 2 else y
            pc = 1 - c if k & 1 else c
            for a in range(self.n):
                src = ins[a].at[4 * px + 2 * py + pc] if a < ne else ins[a]
                dst = outs[a].at[k - 1] if a < ne else outs[a].at[me]
                remote.append(pltpu.make_async_remote_copy(
                    src_ref=src, dst_ref=dst, send_sem=send_sems.at[a, k - 1], recv_sem=recv_sems.at[a, k - 1],
                    device_id=(px, py, pc), device_id_type=MESH))
        local = [pltpu.make_async_copy(ins[a], outs[a].at[me], local_sems.at[a - ne]) for a in range(ne, self.n)]
        return remote, local


def _push(exch=(), gath=()):
    return _Push(exch, gath)


def _pallas(body, *, name, grid, in_specs, out_specs, out_shape, args, scratch_shapes=(), push=None):
    ni, no, ns = len(in_specs), len(out_specs), len(scratch_shapes)
    if push is None:
        outs = pl.pallas_call(
            body, name=name, grid=grid, in_specs=in_specs, out_specs=out_specs, out_shape=out_shape,
            scratch_shapes=list(scratch_shapes), compiler_params=_params(len(grid)))(*args)
        return list(outs), []
    npush = push.n

    def wrapped(*refs):
        ins, pins = refs[:ni], refs[ni:ni + npush]
        outs, pouts = refs[ni + npush:ni + npush + no], refs[ni + npush + no:ni + 2 * npush + no]
        scr, sems = refs[ni + 2 * npush + no:ni + 2 * npush + no + ns], refs[ni + 2 * npush + no + ns:]
        first = functools.reduce(jnp.logical_and, [pl.program_id(d) == 0 for d in range(len(grid))])
        last = functools.reduce(jnp.logical_and, [pl.program_id(d) == grid[d] - 1 for d in range(len(grid))])
        remote, local = push.copies(pins, pouts, *sems)

        @pl.when(first)
        def _():
            for cp in local + remote:
                cp.start()

        body(*ins, *outs, *scr)

        @pl.when(last)
        def _():
            for cp in remote + local:
                cp.wait()

    outs = pl.pallas_call(
        wrapped, name=name, grid=grid,
        in_specs=list(in_specs) + [ANY_SPEC] * npush, out_specs=list(out_specs) + [ANY_SPEC] * npush,
        out_shape=list(out_shape) + push.out_shape(), scratch_shapes=list(scratch_shapes) + push.scratch(),
        compiler_params=_params(len(grid)))(*args, *push.operands())
    return list(outs[:no]), list(outs[no:])


def _inproj(x, win_g, late, tm=1024):
    T = x.shape[0]
    nb, _, bw = win_g.shape
    n = len(late)
    ni = T // tm

    def body(*refs):
        x_ref, w_ref = refs[:2]
        o_ref, xb_ref = refs[2 + n:4 + n]
        gather = _TwoLevelGather(refs[2:2 + n], refs[4 + n:4 + 2 * n], *refs[4 + 2 * n:])
        i, j = pl.program_id(0), pl.program_id(1)

        @pl.when(jnp.logical_and(i == 0, j == 0))
        def _():
            gather.start()

        @pl.when(j == 0)
        def _():
            xb_ref[...] = x_ref[...].astype(BF)

        o_ref[...] = jnp.dot(xb_ref[...], w_ref[0], preferred_element_type=F32).astype(BF)

        @pl.when(jnp.logical_and(i == (3 * ni) // 4, j == 0))
        def _():
            gather.forward()

        @pl.when(jnp.logical_and(i == ni - 1, j == nb - 1))
        def _():
            gather.finish()

    outs = pl.pallas_call(
        body, name="inproj", grid=(ni, nb),
        in_specs=[pl.BlockSpec((tm, D), lambda i, j: (i, 0)), pl.BlockSpec((1, D, bw), lambda i, j: (j, 0, 0))]
        + [ANY_SPEC] * n,
        out_specs=[pl.BlockSpec((tm, bw), lambda i, j: (i, j)), pl.BlockSpec((tm, D), lambda i, j: (i, 0))]
        + [ANY_SPEC] * n,
        out_shape=[jax.ShapeDtypeStruct((T, nb * bw), BF), jax.ShapeDtypeStruct((T, D), BF)]
        + _TwoLevelGather.out_shape(late),
        scratch_shapes=_TwoLevelGather.scratch(n),
        compiler_params=_params(2))(x, win_g, *late)
    return outs[0], outs[1], list(outs[2:])


def _seg(tt, s):
    return pl.BlockSpec((tt, D), lambda i: (i, s))


def _prev(tt, h, s):
    return pl.BlockSpec((h, D), lambda i: (jnp.maximum(i * (tt // h) - 1, 0), s))


def _shifted_copies(shbuf, src, ls, n):
    for s in range(1, 8):
        shbuf[s, 0:n, :] = src[pl.ds(s, n), ls]


def _tap(shbuf, src, ls, off, rows):
    s, q = off % 8, off // 8
    if s == 0:
        return src[pl.ds(off, rows), ls]
    return shbuf[s, pl.ds(8 * q, rows), :]


def _mixer_fwd(p, wa, wb, vecs, tt=256):
    T = p.shape[0]

    def body(ba, ca, va, vb, gb, ca_p, va_p, vb_p, gb_p, wa_ref, wb_ref, vec_ref,
             yapre_ref, conva_ref, xhat_ref, rstd_ref, u3_ref, cabuf, u0buf, u1buf, shu):
        first = pl.program_id(0) == 0
        f = lambda ref: ref[...].astype(F32)
        cabuf[0:HA, :] = jnp.where(first, 0.0, f(ca_p) * f(va_p))
        cabuf[HA:HA + tt, :] = f(ca) * f(va)
        u0buf[0:HB, :] = jnp.where(first, 0.0, f(vb_p) * _sigmoid(f(gb_p)))
        u0buf[HB:HB + tt, :] = f(vb) * _sigmoid(f(gb))

        def lane_body(cidx, carry):
            ls = pl.ds(pl.multiple_of(cidx * LANES, LANES), LANES)
            _shifted_copies(shu, u0buf, ls, tt + HB - 8)
            for r in range(tt // RC):
                acc = jnp.zeros((RC, LANES), F32)
                for k in range(KA):
                    acc = acc + wa_ref[k:k + 1, ls] * cabuf[pl.ds(HA - (KA - 1) + k + r * RC, RC), ls]
                conva_ref[pl.ds(r * RC, RC), ls] = acc
                acc = jnp.zeros((RC, LANES), F32)
                for k in range(KB):
                    acc = acc + wb_ref[k:k + 1, ls] * _tap(shu, u0buf, ls, HB - (KB - 1) + k + r * RC, RC)
                u1buf[pl.ds(r * RC, RC), ls] = acc
            return carry

        lax.fori_loop(0, D // LANES, lane_body, 0)
        yapre_ref[...] = (f(ba) * conva_ref[...]).astype(BF)
        xhat, rstd = _ln_fwd(u1buf[...] + vec_ref[0:1, :])
        xhat_ref[...] = xhat
        rstd_ref[...] = rstd
        u2 = xhat * vec_ref[1:2, :] + vec_ref[2:3, :]
        u3_ref[...] = (u2 * _sigmoid(u2)).astype(BF)

    full = lambda r: pl.BlockSpec((r, D), lambda i: (0, 0))
    tok = pl.BlockSpec((tt, D), lambda i: (i, 0))
    return pl.pallas_call(
        body, name="mixer_fwd", grid=(T // tt,),
        in_specs=[_seg(tt, 0), _seg(tt, 1), _seg(tt, 2), _seg(tt, 3), _seg(tt, 4),
                  _prev(tt, HA, 1), _prev(tt, HA, 2), _prev(tt, HB, 3), _prev(tt, HB, 4),
                  full(8), full(32), full(8)],
        out_specs=[tok, tok, tok, pl.BlockSpec((tt, 1), lambda i: (i, 0)), tok],
        out_shape=[jax.ShapeDtypeStruct((T, D), BF), jax.ShapeDtypeStruct((T, D), F32),
                   jax.ShapeDtypeStruct((T, D), F32), jax.ShapeDtypeStruct((T, 1), F32),
                   jax.ShapeDtypeStruct((T, D), BF)],
        scratch_shapes=[pltpu.VMEM((HA + tt, D), F32), pltpu.VMEM((HB + tt, D), F32), pltpu.VMEM((tt, D), F32),
                        pltpu.VMEM((8, HB + tt, LANES), F32)],
        compiler_params=_params(1))(p, p, p, p, p, p, p, p, p, wa, wb, vecs)


def _post_mixer(yapre, u3, p, x, woa, wob, wo, vecs, tm=256):
    T = x.shape[0]

    def body(yapre_ref, u3_ref, ga_ref, gb_ref, x_ref, woa_ref, wob_ref, wo_ref, vec_ref,
             ya_ref, yb_ref, merged_ref, xhat_ref, rstd_ref, x1b_ref):
        ya = jnp.dot(yapre_ref[...], woa_ref[...], preferred_element_type=F32)
        yb = jnp.dot(u3_ref[...], wob_ref[...], preferred_element_type=F32)
        ya_ref[...] = ya.astype(BF)
        yb_ref[...] = yb.astype(BF)
        merged = (_sigmoid(ga_ref[...].astype(F32)) * ya + _sigmoid(gb_ref[...].astype(F32)) * yb).astype(BF)
        merged_ref[...] = merged
        mix = jnp.dot(merged, wo_ref[...], preferred_element_type=F32)
        xhat, rstd = _ln_fwd(ALPHA * x_ref[...] + mix)
        xhat_ref[...] = xhat
        rstd_ref[...] = rstd
        x1b_ref[...] = (xhat * vec_ref[3:4, :] + vec_ref[4:5, :]).astype(BF)

    tok = pl.BlockSpec((tm, D), lambda i: (i, 0))
    wfull = pl.BlockSpec((D, D), lambda i: (0, 0))
    one = pl.BlockSpec((tm, 1), lambda i: (i, 0))
    return pl.pallas_call(
        body, name="post_mixer", grid=(T // tm,),
        in_specs=[tok, tok, _seg(tm, 5), _seg(tm, 6), tok, wfull, wfull, wfull, pl.BlockSpec((8, D), lambda i: (0, 0))],
        out_specs=[tok, tok, tok, tok, one, tok],
        out_shape=[jax.ShapeDtypeStruct((T, D), BF), jax.ShapeDtypeStruct((T, D), BF),
                   jax.ShapeDtypeStruct((T, D), BF), jax.ShapeDtypeStruct((T, D), F32),
                   jax.ShapeDtypeStruct((T, 1), F32), jax.ShapeDtypeStruct((T, D), BF)],
        compiler_params=_params(1))(yapre, u3, p, p, x, woa, wob, wo, vecs)


def _mlp_up(x1b, wup_g, tm=1024):
    T = x1b.shape[0]
    nb, _, bw = wup_g.shape

    def body(x_ref, w_ref, r_ref, h_ref):
        r = jnp.maximum(jnp.dot(x_ref[...], w_ref[0], preferred_element_type=F32), 0.0)
        r_ref[...] = r.astype(BF)
        h_ref[...] = (r * r).astype(BF)

    out = pl.BlockSpec((tm, bw), lambda i, j: (i, j))
    return pl.pallas_call(
        body, name="mlp_up", grid=(T // tm, nb),
        in_specs=[pl.BlockSpec((tm, D), lambda i, j: (i, 0)), pl.BlockSpec((1, D, bw), lambda i, j: (j, 0, 0))],
        out_specs=[out, out],
        out_shape=[jax.ShapeDtypeStruct((T, nb * bw), BF), jax.ShapeDtypeStruct((T, nb * bw), BF)],
        compiler_params=_params(2))(x1b, wup_g)


def _mlp_down_loss(h, wdown, xhat1, target, vecs, tm=512, tk=512):
    T, dff = h.shape
    nk = dff // tk

    def body(h_ref, w_ref, xhat1_ref, tgt_ref, vec_ref, dz2_ref, dz2b_ref, st_ref, acc):
        i, k = pl.program_id(0), pl.program_id(1)

        @pl.when(jnp.logical_and(i == 0, k == 0))
        def _():
            st_ref[...] = jnp.zeros_like(st_ref)

        @pl.when(k == 0)
        def _():
            acc[...] = jnp.zeros_like(acc)

        acc[...] += jnp.dot(h_ref[...], w_ref[...], preferred_element_type=F32)

        @pl.when(k == nk - 1)
        def _():
            x1 = xhat1_ref[...] * vec_ref[3:4, :] + vec_ref[4:5, :]
            xhat2, rstd2 = _ln_fwd(ALPHA * x1 + acc[...])
            g2 = vec_ref[5:6, :]
            diff = xhat2 * g2 + vec_ref[6:7, :] - tgt_ref[...]
            dx2 = diff * (1.0 / D)
            st_ref[0:1, :] += _colsum(dx2 * xhat2)
            st_ref[1:2, :] += _colsum(dx2)
            st_ref[2:3, :] += _colsum(diff * diff)
            dz2 = _ln_bwd(dx2, xhat2, rstd2, g2)
            dz2_ref[...] = dz2
            dz2b_ref[...] = dz2.astype(BF)

    tok = pl.BlockSpec((tm, D), lambda i, k: (i, 0))
    vec = pl.BlockSpec((8, D), lambda i, k: (0, 0))
    return pl.pallas_call(
        body, name="mlp_down_loss", grid=(T // tm, nk),
        in_specs=[pl.BlockSpec((tm, tk), lambda i, k: (i, k)), pl.BlockSpec((tk, D), lambda i, k: (k, 0)), tok, tok, vec],
        out_specs=[tok, tok, vec],
        out_shape=[jax.ShapeDtypeStruct((T, D), F32), jax.ShapeDtypeStruct((T, D), BF), jax.ShapeDtypeStruct((8, D), F32)],
        scratch_shapes=[pltpu.VMEM((tm, D), F32)],
        compiler_params=_params(2))(h, wdown, xhat1, target, vecs)


def _mlp_down_bwd(dz2b, wdown, r, tm=1024, tk=512):
    T, dff = r.shape

    def body(dz_ref, w_ref, r_ref, o_ref):
        dh = lax.dot_general(dz_ref[...], w_ref[...], NT_DIMS, preferred_element_type=F32)
        o_ref[...] = (dh * (2.0 * r_ref[...].astype(F32))).astype(BF)

    blk = pl.BlockSpec((tm, tk), lambda i, j: (i, j))
    return pl.pallas_call(
        body, name="mlp_down_bwd", grid=(T // tm, dff // tk),
        in_specs=[pl.BlockSpec((tm, D), lambda i, j: (i, 0)), pl.BlockSpec((tk, D), lambda i, j: (j, 0)), blk],
        out_specs=blk,
        out_shape=jax.ShapeDtypeStruct((T, dff), BF),
        compiler_params=_params(2))(dz2b, wdown, r)


def _tn_matmul(a, b, nblk, a_bw, b_bw, a_blocked, b_blocked, name, tt=512):
    T = a.shape[0]
    nt = T // tt

    def body(a_ref, b_ref, o32_ref, o16_ref):
        t = pl.program_id(1)

        @pl.when(t == 0)
        def _():
            o32_ref[...] = jnp.zeros_like(o32_ref)

        o32_ref[0] += lax.dot_general(a_ref[...], b_ref[...], TN_DIMS, preferred_element_type=F32)

        @pl.when(t == nt - 1)
        def _():
            o16_ref[...] = o32_ref[...].astype(BF)

    a_spec = pl.BlockSpec((tt, a_bw), (lambda j, t: (t, j)) if a_blocked else (lambda j, t: (t, 0)))
    b_spec = pl.BlockSpec((tt, b_bw), (lambda j, t: (t, j)) if b_blocked else (lambda j, t: (t, 0)))
    out = pl.BlockSpec((1, a_bw, b_bw), lambda j, t: (j, 0, 0))
    return pl.pallas_call(
        body, name=name, grid=(nblk, nt),
        in_specs=[a_spec, b_spec], out_specs=[out, out],
        out_shape=[jax.ShapeDtypeStruct((nblk, a_bw, b_bw), F32), jax.ShapeDtypeStruct((nblk, a_bw, b_bw), BF)],
        compiler_params=_params(2))(a, b)


def _mlp_up_bwd(dhpre, wup_g, dz2, xhat1, rstd1, vecs, push, tm=512):
    T = dz2.shape[0]
    nb, _, bw = wup_g.shape

    def body(dh_ref, w_ref, dz2_ref, xhat_ref, rstd_ref, vec_ref, dz1_ref, dz1b_ref, st_ref, acc):
        i, j = pl.program_id(0), pl.program_id(1)

        @pl.when(jnp.logical_and(i == 0, j == 0))
        def _():
            st_ref[...] = jnp.zeros_like(st_ref)

        @pl.when(j == 0)
        def _():
            acc[...] = jnp.zeros_like(acc)

        acc[...] += lax.dot_general(dh_ref[...], w_ref[0], NT_DIMS, preferred_element_type=F32)

        @pl.when(j == nb - 1)
        def _():
            dx1 = acc[...] + ALPHA * dz2_ref[...]
            xhat = xhat_ref[...]
            st_ref[0:1, :] += _colsum(dx1 * xhat)
            st_ref[1:2, :] += _colsum(dx1)
            dz1 = _ln_bwd(dx1, xhat, rstd_ref[...], vec_ref[3:4, :])
            dz1_ref[...] = dz1
            dz1b_ref[...] = dz1.astype(BF)

    tok = pl.BlockSpec((tm, D), lambda i, j: (i, 0))
    vec = pl.BlockSpec((8, D), lambda i, j: (0, 0))
    return _pallas(
        body, name="mlp_up_bwd", grid=(T // tm, nb),
        in_specs=[pl.BlockSpec((tm, bw), lambda i, j: (i, j)), pl.BlockSpec((1, D, bw), lambda i, j: (j, 0, 0)),
                  tok, tok, pl.BlockSpec((tm, 1), lambda i, j: (i, 0)), vec],
        out_specs=[tok, tok, vec],
        out_shape=[jax.ShapeDtypeStruct((T, D), F32), jax.ShapeDtypeStruct((T, D), BF), jax.ShapeDtypeStruct((8, D), F32)],
        scratch_shapes=[pltpu.VMEM((tm, D), F32)],
        args=(dhpre, wup_g, dz2, xhat1, rstd1, vecs), push=push)


def _merge_bwd(dz1, p, ya, yb, conva, xhatb, rstdb, woa, wob, wo, vecs, push, tm=256):
    T = dz1.shape[0]

    def body(dz1_ref, ga_ref, gb_ref, ba_ref, ya_ref, yb_ref, conva_ref, xhat_ref, rstd_ref,
             woa_ref, wob_ref, wo_ref, vec_ref,
             dya_ref, dyb_ref, dg_ref, dba_ref, dconva_ref, du1_ref, st_ref):
        @pl.when(pl.program_id(0) == 0)
        def _():
            st_ref[...] = jnp.zeros_like(st_ref)

        dmerged = lax.dot_general(dz1_ref[...], wo_ref[...], NT_DIMS, preferred_element_type=F32)
        sa, sb = _sigmoid(ga_ref[...].astype(F32)), _sigmoid(gb_ref[...].astype(F32))
        dya = (dmerged * sa).astype(BF)
        dyb = (dmerged * sb).astype(BF)
        dya_ref[...] = dya
        dyb_ref[...] = dyb
        dg_ref[:, 0:D] = (dmerged * ya_ref[...].astype(F32) * (sa * (1.0 - sa))).astype(BF)
        dg_ref[:, D:2 * D] = (dmerged * yb_ref[...].astype(F32) * (sb * (1.0 - sb))).astype(BF)

        dyapre = lax.dot_general(dya, woa_ref[...], NT_DIMS, preferred_element_type=F32)
        dba_ref[...] = (dyapre * conva_ref[...]).astype(BF)
        dconva_ref[...] = dyapre * ba_ref[...].astype(F32)

        du3 = lax.dot_general(dyb, wob_ref[...], NT_DIMS, preferred_element_type=F32)
        xhat = xhat_ref[...]
        gamma = vec_ref[1:2, :]
        u2 = xhat * gamma + vec_ref[2:3, :]
        s = _sigmoid(u2)
        du2 = du3 * (s * (1.0 + u2 * (1.0 - s)))
        st_ref[0:1, :] += _colsum(du2 * xhat)
        st_ref[1:2, :] += _colsum(du2)
        du1 = _ln_bwd(du2, xhat, rstd_ref[...], gamma)
        st_ref[2:3, :] += _colsum(du1)
        du1_ref[...] = du1

    tok = pl.BlockSpec((tm, D), lambda i: (i, 0))
    wfull = pl.BlockSpec((D, D), lambda i: (0, 0))
    vec = pl.BlockSpec((8, D), lambda i: (0, 0))
    return _pallas(
        body, name="merge_bwd", grid=(T // tm,),
        in_specs=[tok, _seg(tm, 5), _seg(tm, 6), _seg(tm, 0), tok, tok, tok, tok, pl.BlockSpec((tm, 1), lambda i: (i, 0)),
                  wfull, wfull, wfull, vec],
        out_specs=[tok, tok, pl.BlockSpec((tm, 2 * D), lambda i: (i, 0)), tok, tok, tok, vec],
        out_shape=[jax.ShapeDtypeStruct((T, D), BF), jax.ShapeDtypeStruct((T, D), BF),
                   jax.ShapeDtypeStruct((T, 2 * D), BF), jax.ShapeDtypeStruct((T, D), BF),
                   jax.ShapeDtypeStruct((T, D), F32), jax.ShapeDtypeStruct((T, D), F32),
                   jax.ShapeDtypeStruct((8, D), F32)],
        args=(dz1, p, p, p, ya, yb, conva, xhatb, rstdb, woa, wob, wo, vecs), push=push)


def _rows8(v):
    out = v[0:8]
    for q in range(1, RC // 8):
        out = out + v[8 * q:8 * q + 8]
    return out


def _conv_bwd(dconva, du1, p, dba, dg, wa, wb, push, tt=256):
    T = p.shape[0]
    nsteps = T // tt

    def body(dca_ref, dca_n, du1_ref, du1_n, ca, va, vb, gb, ca_p, va_p, vb_p, gb_p, dba_ref, dg_ref, wa_ref, wb_ref,
             dp_ref, gw_ref, cabuf, u0buf, dcabuf, du1buf, dcain, du0, gwa, gwb, shu, shd):
        i = pl.program_id(0)
        first, last = i == 0, i == nsteps - 1

        @pl.when(first)
        def _():
            gwa[...] = jnp.zeros_like(gwa)
            gwb[...] = jnp.zeros_like(gwb)

        f = lambda ref: ref[...].astype(F32)
        cav, vav, vbv = f(ca), f(va), f(vb)
        cabuf[0:HA, :] = jnp.where(first, 0.0, f(ca_p) * f(va_p))
        cabuf[HA:HA + tt, :] = cav * vav
        sg = _sigmoid(f(gb))
        u0buf[0:HB, :] = jnp.where(first, 0.0, f(vb_p) * _sigmoid(f(gb_p)))
        u0buf[HB:HB + tt, :] = vbv * sg
        dcabuf[0:tt, :] = dca_ref[...]
        dcabuf[tt:tt + HN, :] = jnp.where(last, 0.0, dca_n[...])
        du1buf[0:tt, :] = du1_ref[...]
        du1buf[tt:tt + HB, :] = jnp.where(last, 0.0, du1_n[...])

        def lane_body(cidx, carry):
            ls = pl.ds(pl.multiple_of(cidx * LANES, LANES), LANES)
            _shifted_copies(shu, u0buf, ls, tt + HB - 8)
            _shifted_copies(shd, du1buf, ls, tt + HB - 8)
            for r in range(tt // RC):
                dout = dcabuf[pl.ds(r * RC, RC), ls]
                acc = jnp.zeros((RC, LANES), F32)
                for k in range(KA):
                    acc = acc + wa_ref[k:k + 1, ls] * dcabuf[pl.ds(r * RC + KA - 1 - k, RC), ls]
                    gwa[8 * k:8 * k + 8, ls] += _rows8(dout * cabuf[pl.ds(HA - (KA - 1) + k + r * RC, RC), ls])
                dcain[pl.ds(r * RC, RC), ls] = acc
                dout = du1buf[pl.ds(r * RC, RC), ls]
                acc = jnp.zeros((RC, LANES), F32)
                for k in range(KB):
                    acc = acc + wb_ref[k:k + 1, ls] * _tap(shd, du1buf, ls, r * RC + KB - 1 - k, RC)
                    gwb[8 * k:8 * k + 8, ls] += _rows8(dout * _tap(shu, u0buf, ls, HB - (KB - 1) + k + r * RC, RC))
                du0[pl.ds(r * RC, RC), ls] = acc
            return carry

        lax.fori_loop(0, D // LANES, lane_body, 0)
        dca_in = dcain[...]
        du0v = du0[...]
        dp_ref[:, 0:D] = dba_ref[...]
        dp_ref[:, D:2 * D] = (dca_in * vav).astype(BF)
        dp_ref[:, 2 * D:3 * D] = (dca_in * cav).astype(BF)
        dp_ref[:, 3 * D:4 * D] = (du0v * sg).astype(BF)
        dp_ref[:, 4 * D:5 * D] = (du0v * vbv * (sg * (1.0 - sg))).astype(BF)
        dp_ref[:, 5 * D:7 * D] = dg_ref[...]

        @pl.when(last)
        def _():
            gw_ref[...] = jnp.zeros_like(gw_ref)
            for k in range(KA):
                gw_ref[k:k + 1, :] = _colsum(gwa[8 * k:8 * k + 8, :])
            for k in range(KB):
                gw_ref[8 + k:9 + k, :] = _colsum(gwb[8 * k:8 * k + 8, :])

    full = lambda r: pl.BlockSpec((r, D), lambda i: (0, 0))
    tok = pl.BlockSpec((tt, D), lambda i: (i, 0))
    nxt = lambda h: pl.BlockSpec((h, D), lambda i: (jnp.minimum((i + 1) * (tt // h), T // h - 1), 0))
    return _pallas(
        body, name="conv_bwd", grid=(nsteps,),
        in_specs=[tok, nxt(HN), tok, nxt(HB),
                  _seg(tt, 1), _seg(tt, 2), _seg(tt, 3), _seg(tt, 4),
                  _prev(tt, HA, 1), _prev(tt, HA, 2), _prev(tt, HB, 3), _prev(tt, HB, 4),
                  tok, pl.BlockSpec((tt, 2 * D), lambda i: (i, 0)), full(8), full(32)],
        out_specs=[pl.BlockSpec((tt, 7 * D), lambda i: (i, 0)), full(40)],
        out_shape=[jax.ShapeDtypeStruct((T, 7 * D), BF), jax.ShapeDtypeStruct((40, D), F32)],
        scratch_shapes=[pltpu.VMEM((HA + tt, D), F32), pltpu.VMEM((HB + tt, D), F32),
                        pltpu.VMEM((tt + HN, D), F32), pltpu.VMEM((tt + HB, D), F32),
                        pltpu.VMEM((tt, D), F32), pltpu.VMEM((tt, D), F32),
                        pltpu.VMEM((8 * KA, D), F32), pltpu.VMEM((8 * KB, D), F32),
                        pltpu.VMEM((8, HB + tt, LANES), F32), pltpu.VMEM((8, HB + tt, LANES), F32)],
        args=(dconva, dconva, du1, du1, p, p, p, p, p, p, p, p, dba, dg, wa, wb), push=push)


def _inproj_bwd(dp, win_g, dz1, push, tm=1024):
    T = dz1.shape[0]
    nb, _, bw = win_g.shape

    def body(dp_ref, w_ref, dz1_ref, o_ref):
        j = pl.program_id(1)

        @pl.when(j == 0)
        def _():
            o_ref[...] = ALPHA * dz1_ref[...]

        o_ref[...] += lax.dot_general(dp_ref[...], w_ref[0], NT_DIMS, preferred_element_type=F32)

    tok = pl.BlockSpec((tm, D), lambda i, j: (i, 0))
    return _pallas(
        body, name="inproj_bwd", grid=(T // tm, nb),
        in_specs=[pl.BlockSpec((tm, bw), lambda i, j: (i, j)), pl.BlockSpec((1, D, bw), lambda i, j: (j, 0, 0)), tok],
        out_specs=[tok],
        out_shape=[jax.ShapeDtypeStruct((T, D), F32)],
        args=(dp, win_g, dz1), push=push)


def _sum_blocks(g8):
    def body(g_ref, o_ref):
        acc = g_ref[0]
        for j in range(1, NDEV):
            acc = acc + g_ref[j]
        o_ref[...] = acc

    return pl.pallas_call(body, name="sum_small_grads",
                          out_shape=jax.ShapeDtypeStruct(g8.shape[1:], g8.dtype))(g8)


def _adamw(w, m, v, own, landing, name, rb):
    R, C = w.shape
    nl = 0 if landing is None else landing.shape[0]

    def body(*refs):
        w_ref, m_ref, v_ref, own_ref = refs[:4]
        l_ref = refs[4] if nl else None
        g_ref, d_ref, nm_ref, nv_ref = refs[4 + (1 if nl else 0):]
        g = own_ref[...]
        for k in range(nl):
            g = g + l_ref[k].astype(F32)
        nm = ADAM_B1 * m_ref[...] + (1.0 - ADAM_B1) * g
        nv = ADAM_B2 * v_ref[...] + (1.0 - ADAM_B2) * (g * g)
        m_hat = nm / (1.0 - ADAM_B1 ** ADAM_STEP)
        v_hat = nv / (1.0 - ADAM_B2 ** ADAM_STEP)
        g_ref[...] = g
        d_ref[...] = -ADAM_LR * (m_hat / (jnp.sqrt(v_hat) + ADAM_EPS) + ADAM_WD * w_ref[...])
        nm_ref[...] = nm
        nv_ref[...] = nv

    blk = pl.BlockSpec((rb, C), lambda i: (i, 0))
    in_specs = [blk] * 4 + ([pl.BlockSpec((nl, rb, C), lambda i: (0, i, 0))] if nl else [])
    args = (w, m, v, own) + ((landing,) if nl else ())
    return pl.pallas_call(
        body, name=name, grid=(R // rb,), in_specs=in_specs, out_specs=[blk] * 4,
        out_shape=[jax.ShapeDtypeStruct((R, C), F32)] * 4,
        compiler_params=_params(1))(*args)


def _pad_rows(a, rows):
    return jnp.pad(a, ((0, rows - a.shape[0]), (0, 0)))


def _local_step(p, xb, x, target, win_g, wup_g, wdown, woa, wob, wo, wa, wb, vecs):
    yapre, conva, xhatb, rstdb, u3 = _mixer_fwd(p, wa, wb, vecs)
    ya, yb, merged, xhat1, rstd1, x1b = _post_mixer(yapre, u3, p, x, woa, wob, wo, vecs)
    r, h = _mlp_up(x1b, wup_g)
    dz2, dz2b, st2 = _mlp_down_loss(h, wdown, xhat1, target, vecs)

    by_owner = lambda g16: g16.reshape(NDEV, D // NDEV, D)
    dhpre = _mlp_down_bwd(dz2b, wdown, r)
    g_wdown = _tn_matmul(h, dz2b, NDEV, 512, D, True, False, "grad_w_down")
    (dz1, dz1b, st1), land_wdown = _mlp_up_bwd(dhpre, wup_g, dz2, xhat1, rstd1, vecs, _push(exch=[g_wdown[1]]))
    g_wup = _tn_matmul(x1b, dhpre, NDEV, D, 512, False, True, "grad_w_up")
    (dya, dyb, dg, dba, dconva, du1, stb), land_wup = _merge_bwd(
        dz1b, p, ya, yb, conva, xhatb, rstdb, woa, wob, wo, vecs, _push(exch=[g_wup[1]]))
    g_wo = _tn_matmul(merged, dz1b, 1, D, D, False, False, "grad_w_o")
    g_woa = _tn_matmul(yapre, dya, 1, D, D, False, False, "grad_w_out_a")
    g_wob = _tn_matmul(u3, dyb, 1, D, D, False, False, "grad_w_out_b")
    (dp, gw), land_sq = _conv_bwd(dconva, du1, p, dba, dg, wa, wb,
                                  _push(exch=[by_owner(g_woa[1]), by_owner(g_wob[1]), by_owner(g_wo[1])]))
    g_win = _tn_matmul(xb, dp, NDEV, D, 896, False, True, "grad_w_in")

    small = jnp.concatenate([stb[2:3], stb[0:2], st1[0:2], st2[0:3], gw], axis=0)
    (grad_x,), land_last = _inproj_bwd(dp, win_g, dz1, _push(exch=[g_win[1]], gath=[small]))
    grads = (g_win[0], g_wup[0], g_wdown[0], g_woa[0], g_wob[0], g_wo[0])
    return grad_x, grads, small, land_wdown + land_wup + land_sq + land_last


def kernel(x, w_in, conv_a_w, w_out_a, conv_b_w, conv_b_bias, ln_b_gamma, ln_b_beta, w_out_b, w_o, ln1_gamma, ln1_beta, w_up, w_down, ln2_gamma, ln2_beta, loss_target, m_w_in, m_conv_a_w, m_w_out_a, m_conv_b_w, m_conv_b_bias, m_ln_b_gamma, m_ln_b_beta, m_w_out_b, m_w_o, m_ln1_gamma, m_ln1_beta, m_w_up, m_w_down, m_ln2_gamma, m_ln2_beta, v_w_in, v_conv_a_w, v_w_out_a, v_conv_b_w, v_conv_b_bias, v_ln_b_gamma, v_ln_b_beta, v_w_out_b, v_w_o, v_ln1_gamma, v_ln1_beta, v_w_up, v_w_down, v_ln2_gamma, v_ln2_beta):
    T = x.shape[1]
    me = 4 * lax.axis_index("x") + 2 * lax.axis_index("y") + lax.axis_index("c")

    conv_shard = jnp.concatenate([_pad_rows(conv_a_w, 8), _pad_rows(conv_b_w, 32)], axis=0)
    win_g, conv_g = _allgather([w_in.astype(BF), conv_shard], "allgather_w_in")
    conv_full = jnp.transpose(conv_g, (1, 0, 2)).reshape(40, D)
    vecs = jnp.stack([conv_b_bias, ln_b_gamma, ln_b_beta, ln1_gamma, ln1_beta, ln2_gamma, ln2_beta,
                      jnp.zeros_like(ln2_beta)])
    p, xb, (wup_g, wdown_g, woa_g, wob_g, wo_g) = _inproj(
        x[0], win_g, [w_up.astype(BF), w_down.astype(BF), w_out_a.astype(BF), w_out_b.astype(BF), w_o.astype(BF)])

    grad_x, grads, small, landing = _local_step(
        p, xb, x[0], loss_target[0], win_g, wup_g, wdown_g.reshape(NDEV * 512, D), woa_g.reshape(D, D),
        wob_g.reshape(D, D), wo_g.reshape(D, D), conv_full[0:8], conv_full[8:40], vecs)
    g_win, g_wup, g_wdown, g_woa, g_wob, g_wo = grads
    l_wdown, l_wup, l_woa, l_wob, l_wo, l_win, small_g = landing
    small_sum = _sum_blocks(small_g)

    loss = lax.psum(0.5 / D * jnp.sum(small[7]), ("x", "y", "c"))

    def own(g32):
        blocked = g32 if g32.shape[0] == NDEV else g32.reshape(NDEV, 128, D)
        return lax.dynamic_index_in_dim(blocked, me, 0, keepdims=False)

    r_win = _adamw(w_in, m_w_in, v_w_in, own(g_win), l_win, "adamw_w_in", 256)
    r_wup = _adamw(w_up, m_w_up, v_w_up, own(g_wup), l_wup, "adamw_w_up", 256)
    r_wdown = _adamw(w_down, m_w_down, v_w_down, own(g_wdown), l_wdown, "adamw_w_down", 256)
    r_woa = _adamw(w_out_a, m_w_out_a, v_w_out_a, own(g_woa), l_woa, "adamw_w_out_a", 128)
    r_wob = _adamw(w_out_b, m_w_out_b, v_w_out_b, own(g_wob), l_wob, "adamw_w_out_b", 128)
    r_wo = _adamw(w_o, m_w_o, v_w_o, own(g_wo), l_wo, "adamw_w_o", 128)

    vec_m = jnp.stack([m_conv_b_bias, m_ln_b_gamma, m_ln_b_beta, m_ln1_gamma, m_ln1_beta, m_ln2_gamma, m_ln2_beta,
                       jnp.zeros_like(ln2_beta)])
    vec_v = jnp.stack([v_conv_b_bias, v_ln_b_gamma, v_ln_b_beta, v_ln1_gamma, v_ln1_beta, v_ln2_gamma, v_ln2_beta,
                       jnp.zeros_like(ln2_beta)])
    vec_g = jnp.concatenate([small_sum[0:7], jnp.zeros((1, D), F32)], axis=0)
    r_vec = _adamw(vecs, vec_m, vec_v, vec_g, None, "adamw_vectors", 8)

    conv_cols = lax.dynamic_slice_in_dim(small_sum[8:48], me * 128, 128, axis=1)
    conv_m = jnp.concatenate([_pad_rows(m_conv_a_w, 8), _pad_rows(m_conv_b_w, 32)], axis=0)
    conv_v = jnp.concatenate([_pad_rows(v_conv_a_w, 8), _pad_rows(v_conv_b_w, 32)], axis=0)
    r_conv = _adamw(conv_shard, conv_m, conv_v, conv_cols, None, "adamw_conv", 40)

    per_weight = []
    for q in range(4):
        per_weight.append([
            r_win[q], r_conv[q][0:KA], r_woa[q], r_conv[q][8:8 + KB],
            r_vec[q][0], r_vec[q][1], r_vec[q][2], r_wob[q], r_wo[q], r_vec[q][3], r_vec[q][4],
            r_wup[q], r_wdown[q], r_vec[q][5], r_vec[q][6]])
    return (loss, grad_x[None], *per_weight[0], *per_weight[1], *per_weight[2], *per_weight[3])
```

```python
import functools

import jax
import jax.numpy as jnp
from jax import lax
from jax.experimental import pallas as pl
from jax.experimental.pallas import tpu as pltpu

F32 = jnp.float32
BF = jnp.bfloat16
D = 1024
NDEV = 8
ALPHA = 2.0 ** 0.25
LN_EPS = 1e-5
KA, KB = 3, 31
HA, HB = 16, 32
HN = 8
RC = 64
LANES = 128
VMEM_LIMIT = 56 * 1024 * 1024
MESH = pl.DeviceIdType.MESH
ADAM_LR, ADAM_B1, ADAM_B2, ADAM_EPS, ADAM_WD, ADAM_STEP = 0.001, 0.9, 0.999, 1e-08, 0.01, 10

ANY_SPEC = pl.BlockSpec(memory_space=pl.ANY)
NT_DIMS = (((1,), (1,)), ((), ()))
TN_DIMS = (((0,), (0,)), ((), ()))


def _params(n_axes):
    return pltpu.CompilerParams(dimension_semantics=("arbitrary",) * n_axes, vmem_limit_bytes=VMEM_LIMIT)


def _sigmoid(v):
    return jax.nn.sigmoid(v)


def _ln_fwd(z):
    mu = jnp.mean(z, axis=-1, keepdims=True)
    zc = z - mu
    var = jnp.mean(zc * zc, axis=-1, keepdims=True)
    rstd = lax.rsqrt(var + LN_EPS)
    return zc * rstd, rstd


def _ln_bwd(dy, xhat, rstd, gamma):
    dxhat = dy * gamma
    m1 = jnp.mean(dxhat, axis=-1, keepdims=True)
    m2 = jnp.mean(dxhat * xhat, axis=-1, keepdims=True)
    return rstd * (dxhat - m1 - xhat * m2)


def _colsum(v):
    return jnp.sum(v, axis=0, keepdims=True)


def _allgather(arrs, name):
    n = len(arrs)

    def body(*refs):
        g = _TwoLevelGather(refs[:n], refs[n:2 * n], *refs[2 * n:])
        g.start()
        g.forward()
        g.finish()

    return pl.pallas_call(
        body, name=name,
        in_specs=[ANY_SPEC] * n, out_specs=[ANY_SPEC] * n,
        out_shape=_TwoLevelGather.out_shape(arrs), scratch_shapes=_TwoLevelGather.scratch(n),
    )(*arrs)


class _TwoLevelGather:
    def __init__(self, ins, outs, send_sems, recv_sems, local_sems):
        self.ins, self.outs = ins, outs
        self.send_sems, self.recv_sems, self.local_sems = send_sems, recv_sems, local_sems
        x, y, c = lax.axis_index("x"), lax.axis_index("y"), lax.axis_index("c")
        self.me, self.sibling, self.c = (x, y, c), (x, y, 1 - c), c
        self.chips = [(1 - x, y), (x, 1 - y), (1 - x, 1 - y)]
        self.n = len(ins)

    @staticmethod
    def out_shape(arrs):
        return [jax.ShapeDtypeStruct((NDEV,) + a.shape, a.dtype) for a in arrs]

    @staticmethod
    def scratch(n):
        return [pltpu.SemaphoreType.DMA((n, 7)), pltpu.SemaphoreType.DMA((n, 7)), pltpu.SemaphoreType.DMA((n,))]

    def _copy(self, a, k, block, to, src=None):
        px, py, pc = block
        rows = self.outs[a].at[4 * px + 2 * py + pc]
        return pltpu.make_async_remote_copy(
            src_ref=rows if src is None else src, dst_ref=rows,
            send_sem=self.send_sems.at[a, k], recv_sem=self.recv_sems.at[a, k],
            device_id=to, device_id_type=MESH)

    def _mine(self, a):
        x, y, c = self.me
        return pltpu.make_async_copy(self.ins[a], self.outs[a].at[4 * x + 2 * y + c], self.local_sems.at[a])

    def _first(self, a):
        cps = [self._copy(a, 0, self.me, self.sibling, src=self.ins[a])]
        return cps + [self._copy(a, 1 + j, self.me, (*chip, self.c), src=self.ins[a]) for j, chip in enumerate(self.chips)]

    def _passed(self, a, j):
        return self._copy(a, 4 + j, (*self.chips[j], self.c), self.sibling)

    def start(self):
        for a in range(self.n):
            self._mine(a).start()
        for a in range(self.n):
            for cp in self._first(a):
                cp.start()

    def forward(self):
        for j, chip in enumerate(self.chips):
            for a in range(self.n):
                self._copy(a, 1 + j, (*chip, self.c), self.me).wait_recv()
                self._passed(a, j).start()

    def finish(self):
        for a in range(self.n):
            self._copy(a, 0, self.sibling, self.me).wait_recv()
            for j, chip in enumerate(self.chips):
                self._copy(a, 4 + j, (*chip, 1 - self.c), self.me).wait_recv()
        for a in range(self.n):
            for cp in self._first(a) + [self._passed(a, j) for j in range(3)]:
                cp.wait_send()
            self._mine(a).wait()


class _Push:
    def __init__(self, exch=(), gath=()):
        self.exch, self.gath = list(exch), list(gath)
        self.n = len(self.exch) + len(self.gath)

    def operands(self):
        return self.exch + self.gath

    def out_shape(self):
        return ([jax.ShapeDtypeStruct((NDEV - 1,) + a.shape[1:], a.dtype) for a in self.exch]
                + [jax.ShapeDtypeStruct((NDEV,) + a.shape, a.dtype) for a in self.gath])

    def scratch(self):
        return [pltpu.SemaphoreType.DMA((self.n, 7)), pltpu.SemaphoreType.DMA((self.n, 7)),
                pltpu.SemaphoreType.DMA((max(len(self.gath), 1),))]

    def copies(self, ins, outs, send_sems, recv_sems, local_sems):
        x, y, c = lax.axis_index("x"), lax.axis_index("y"), lax.axis_index("c")
        me = 4 * x + 2 * y + c
        ne = len(self.exch)
        remote = []
        for k in range(1, NDEV):
            px = 1 - x if k & 4 else x
            py = 1 - y if k & 2 else y
            pc = 1 - c if k & 1 else c
            for a in range(self.n):
                src = ins[a].at[4 * px + 2 * py + pc] if a < ne else ins[a]
                dst = outs[a].at[k - 1] if a < ne else outs[a].at[me]
                remote.append(pltpu.make_async_remote_copy(
                    src_ref=src, dst_ref=dst, send_sem=send_sems.at[a, k - 1], recv_sem=recv_sems.at[a, k - 1],
                    device_id=(px, py, pc), device_id_type=MESH))
        local = [pltpu.make_async_copy(ins[a], outs[a].at[me], local_sems.at[a - ne]) for a in range(ne, self.n)]
        return remote, local


def _push(exch=(), gath=()):
    return _Push(exch, gath)


def _pallas(body, *, name, grid, in_specs, out_specs, out_shape, args, scratch_shapes=(), push=None):
    ni, no, ns = len(in_specs), len(out_specs), len(scratch_shapes)
    if push is None:
        outs = pl.pallas_call(
            body, name=name, grid=grid, in_specs=in_specs, out_specs=out_specs, out_shape=out_shape,
            scratch_shapes=list(scratch_shapes), compiler_params=_params(len(grid)))(*args)
        return list(outs), []
    npush = push.n

    def wrapped(*refs):
        ins, pins = refs[:ni], refs[ni:ni + npush]
        outs, pouts = refs[ni + npush:ni + npush + no], refs[ni + npush + no:ni + 2 * npush + no]
        scr, sems = refs[ni + 2 * npush + no:ni + 2 * npush + no + ns], refs[ni + 2 * npush + no + ns:]
        first = functools.reduce(jnp.logical_and, [pl.program_id(d) == 0 for d in range(len(grid))])
        last = functools.reduce(jnp.logical_and, [pl.program_id(d) == grid[d] - 1 for d in range(len(grid))])
        remote, local = push.copies(pins, pouts, *sems)

        @pl.when(first)
        def _():
            for cp in local + remote:
                cp.start()

        body(*ins, *outs, *scr)

        @pl.when(last)
        def _():
            for cp in remote + local:
                cp.wait()

    outs = pl.pallas_call(
        wrapped, name=name, grid=grid,
        in_specs=list(in_specs) + [ANY_SPEC] * npush, out_specs=list(out_specs) + [ANY_SPEC] * npush,
        out_shape=list(out_shape) + push.out_shape(), scratch_shapes=list(scratch_shapes) + push.scratch(),
        compiler_params=_params(len(grid)))(*args, *push.operands())
    return list(outs[:no]), list(outs[no:])


def _inproj(x, win, late, tm=1024, bw=1024):
    T = x.shape[0]
    nb = win.shape[1] // bw
    n = len(late)
    ni = T // tm

    def body(*refs):
        x_ref, w_ref = refs[:2]
        o_ref, xb_ref = refs[2 + n:4 + n]
        gather = _TwoLevelGather(refs[2:2 + n], refs[4 + n:4 + 2 * n], *refs[4 + 2 * n:])
        i, j = pl.program_id(0), pl.program_id(1)

        @pl.when(jnp.logical_and(i == 0, j == 0))
        def _():
            gather.start()

        @pl.when(j == 0)
        def _():
            xb_ref[...] = x_ref[...].astype(BF)

        o_ref[...] = jnp.dot(xb_ref[...], w_ref[...], preferred_element_type=F32).astype(BF)

        @pl.when(jnp.logical_and(i == (3 * ni) // 4, j == 0))
        def _():
            gather.forward()

        @pl.when(jnp.logical_and(i == ni - 1, j == nb - 1))
        def _():
            gather.finish()

    outs = pl.pallas_call(
        body, name="inproj", grid=(ni, nb),
        in_specs=[pl.BlockSpec((tm, D), lambda i, j: (i, 0)), pl.BlockSpec((D, bw), lambda i, j: (0, j))]
        + [ANY_SPEC] * n,
        out_specs=[pl.BlockSpec((tm, bw), lambda i, j: (i, j)), pl.BlockSpec((tm, D), lambda i, j: (i, 0))]
        + [ANY_SPEC] * n,
        out_shape=[jax.ShapeDtypeStruct((T, nb * bw), BF), jax.ShapeDtypeStruct((T, D), BF)]
        + _TwoLevelGather.out_shape(late),
        scratch_shapes=_TwoLevelGather.scratch(n),
        compiler_params=_params(2))(x, win, *late)
    return outs[0], outs[1], list(outs[2:])


def _seg(tt, s):
    return pl.BlockSpec((tt, D), lambda i: (i, s))


def _prev(tt, h, s):
    return pl.BlockSpec((h, D), lambda i: (jnp.maximum(i * (tt // h) - 1, 0), s))


def _shifted_copies(shbuf, src, ls, n):
    for s in range(1, 8):
        shbuf[s, 0:n, :] = src[pl.ds(s, n), ls]


def _tap(shbuf, src, ls, off, rows):
    s, q = off % 8, off // 8
    if s == 0:
        return src[pl.ds(off, rows), ls]
    return shbuf[s, pl.ds(8 * q, rows), :]


def _mixer_fwd(p, wa, wb, vecs, tt=256):
    T = p.shape[0]

    def body(ba, ca, va, vb, gb, ca_p, va_p, vb_p, gb_p, wa_ref, wb_ref, vec_ref,
             yapre_ref, conva_ref, xhat_ref, rstd_ref, u3_ref, cabuf, u0buf, u1buf, shu):
        first = pl.program_id(0) == 0
        f = lambda ref: ref[...].astype(F32)
        cabuf[0:HA, :] = jnp.where(first, 0.0, f(ca_p) * f(va_p))
        cabuf[HA:HA + tt, :] = f(ca) * f(va)
        u0buf[0:HB, :] = jnp.where(first, 0.0, f(vb_p) * _sigmoid(f(gb_p)))
        u0buf[HB:HB + tt, :] = f(vb) * _sigmoid(f(gb))

        def lane_body(cidx, carry):
            ls = pl.ds(pl.multiple_of(cidx * LANES, LANES), LANES)
            _shifted_copies(shu, u0buf, ls, tt + HB - 8)
            for r in range(tt // RC):
                acc = jnp.zeros((RC, LANES), F32)
                for k in range(KA):
                    acc = acc + wa_ref[k:k + 1, ls] * cabuf[pl.ds(HA - (KA - 1) + k + r * RC, RC), ls]
                conva_ref[pl.ds(r * RC, RC), ls] = acc
                acc = jnp.zeros((RC, LANES), F32)
                for k in range(KB):
                    acc = acc + wb_ref[k:k + 1, ls] * _tap(shu, u0buf, ls, HB - (KB - 1) + k + r * RC, RC)
                u1buf[pl.ds(r * RC, RC), ls] = acc
            return carry

        lax.fori_loop(0, D // LANES, lane_body, 0)
        yapre_ref[...] = (f(ba) * conva_ref[...]).astype(BF)
        xhat, rstd = _ln_fwd(u1buf[...] + vec_ref[0:1, :])
        xhat_ref[...] = xhat
        rstd_ref[...] = rstd
        u2 = xhat * vec_ref[1:2, :] + vec_ref[2:3, :]
        u3_ref[...] = (u2 * _sigmoid(u2)).astype(BF)

    full = lambda r: pl.BlockSpec((r, D), lambda i: (0, 0))
    tok = pl.BlockSpec((tt, D), lambda i: (i, 0))
    return pl.pallas_call(
        body, name="mixer_fwd", grid=(T // tt,),
        in_specs=[_seg(tt, 0), _seg(tt, 1), _seg(tt, 2), _seg(tt, 3), _seg(tt, 4),
                  _prev(tt, HA, 1), _prev(tt, HA, 2), _prev(tt, HB, 3), _prev(tt, HB, 4),
                  full(8), full(32), full(8)],
        out_specs=[tok, tok, tok, pl.BlockSpec((tt, 1), lambda i: (i, 0)), tok],
        out_shape=[jax.ShapeDtypeStruct((T, D), BF), jax.ShapeDtypeStruct((T, D), F32),
                   jax.ShapeDtypeStruct((T, D), F32), jax.ShapeDtypeStruct((T, 1), F32),
                   jax.ShapeDtypeStruct((T, D), BF)],
        scratch_shapes=[pltpu.VMEM((HA + tt, D), F32), pltpu.VMEM((HB + tt, D), F32), pltpu.VMEM((tt, D), F32),
                        pltpu.VMEM((8, HB + tt, LANES), F32)],
        compiler_params=_params(1))(p, p, p, p, p, p, p, p, p, wa, wb, vecs)


def _post_mixer(yapre, u3, p, x, woa, wob, wo, vecs, tm=256):
    T = x.shape[0]

    def body(yapre_ref, u3_ref, ga_ref, gb_ref, x_ref, woa_ref, wob_ref, wo_ref, vec_ref,
             ya_ref, yb_ref, merged_ref, xhat_ref, rstd_ref, x1b_ref):
        ya = jnp.dot(yapre_ref[...], woa_ref[...], preferred_element_type=F32)
        yb = jnp.dot(u3_ref[...], wob_ref[...], preferred_element_type=F32)
        ya_ref[...] = ya.astype(BF)
        yb_ref[...] = yb.astype(BF)
        merged = (_sigmoid(ga_ref[...].astype(F32)) * ya + _sigmoid(gb_ref[...].astype(F32)) * yb).astype(BF)
        merged_ref[...] = merged
        mix = jnp.dot(merged, wo_ref[...], preferred_element_type=F32)
        xhat, rstd = _ln_fwd(ALPHA * x_ref[...] + mix)
        xhat_ref[...] = xhat
        rstd_ref[...] = rstd
        x1b_ref[...] = (xhat * vec_ref[3:4, :] + vec_ref[4:5, :]).astype(BF)

    tok = pl.BlockSpec((tm, D), lambda i: (i, 0))
    wfull = pl.BlockSpec((D, D), lambda i: (0, 0))
    one = pl.BlockSpec((tm, 1), lambda i: (i, 0))
    return pl.pallas_call(
        body, name="post_mixer", grid=(T // tm,),
        in_specs=[tok, tok, _seg(tm, 5), _seg(tm, 6), tok, wfull, wfull, wfull, pl.BlockSpec((8, D), lambda i: (0, 0))],
        out_specs=[tok, tok, tok, tok, one, tok],
        out_shape=[jax.ShapeDtypeStruct((T, D), BF), jax.ShapeDtypeStruct((T, D), BF),
                   jax.ShapeDtypeStruct((T, D), BF), jax.ShapeDtypeStruct((T, D), F32),
                   jax.ShapeDtypeStruct((T, 1), F32), jax.ShapeDtypeStruct((T, D), BF)],
        compiler_params=_params(1))(yapre, u3, p, p, x, woa, wob, wo, vecs)


def _mlp_up(x1b, wup, tm=512, tn=2048):
    T = x1b.shape[0]
    dff = wup.shape[1]

    def body(x_ref, w_ref, r_ref, h_ref):
        r = jnp.maximum(jnp.dot(x_ref[...], w_ref[...], preferred_element_type=F32), 0.0)
        r_ref[...] = r.astype(BF)
        h_ref[...] = (r * r).astype(BF)

    out = pl.BlockSpec((tm, tn), lambda j, i: (i, j))
    return pl.pallas_call(
        body, name="mlp_up", grid=(dff // tn, T // tm),
        in_specs=[pl.BlockSpec((tm, D), lambda j, i: (i, 0)), pl.BlockSpec((D, tn), lambda j, i: (0, j))],
        out_specs=[out, out],
        out_shape=[jax.ShapeDtypeStruct((T, dff), BF), jax.ShapeDtypeStruct((T, dff), BF)],
        compiler_params=_params(2))(x1b, wup)


def _resident(shape):
    return pl.BlockSpec(shape, lambda *_: (0,) * len(shape), pipeline_mode=pl.Buffered(1))


def _mlp_down_loss(h, wdown, xhat1, target, vecs, tm=512):
    T, dff = h.shape

    def body(h_ref, w_ref, xhat1_ref, tgt_ref, vec_ref, dz2_ref, dz2b_ref, st_ref):
        @pl.when(pl.program_id(0) == 0)
        def _():
            st_ref[...] = jnp.zeros_like(st_ref)

        ff = jnp.dot(h_ref[...], w_ref[...], preferred_element_type=F32)
        x1 = xhat1_ref[...] * vec_ref[3:4, :] + vec_ref[4:5, :]
        xhat2, rstd2 = _ln_fwd(ALPHA * x1 + ff)
        g2 = vec_ref[5:6, :]
        diff = xhat2 * g2 + vec_ref[6:7, :] - tgt_ref[...]
        dx2 = diff * (1.0 / D)
        st_ref[0:1, :] += _colsum(dx2 * xhat2)
        st_ref[1:2, :] += _colsum(dx2)
        st_ref[2:3, :] += _colsum(diff * diff)
        dz2 = _ln_bwd(dx2, xhat2, rstd2, g2)
        dz2_ref[...] = dz2
        dz2b_ref[...] = dz2.astype(BF)

    tok = pl.BlockSpec((tm, D), lambda i: (i, 0))
    vec = pl.BlockSpec((8, D), lambda i: (0, 0))
    return pl.pallas_call(
        body, name="mlp_down_loss", grid=(T // tm,),
        in_specs=[pl.BlockSpec((tm, dff), lambda i: (i, 0)), _resident((dff, D)), tok, tok, vec],
        out_specs=[tok, tok, vec],
        out_shape=[jax.ShapeDtypeStruct((T, D), F32), jax.ShapeDtypeStruct((T, D), BF), jax.ShapeDtypeStruct((8, D), F32)],
        compiler_params=_params(1))(h, wdown, xhat1, target, vecs)


def _mlp_down_bwd(dz2b, wdown, r, tm=512, tk=2048):
    T, dff = r.shape

    def body(dz_ref, w_ref, r_ref, o_ref):
        dh = lax.dot_general(dz_ref[...], w_ref[...], NT_DIMS, preferred_element_type=F32)
        o_ref[...] = (dh * (2.0 * r_ref[...].astype(F32))).astype(BF)

    blk = pl.BlockSpec((tm, tk), lambda j, i: (i, j))
    return pl.pallas_call(
        body, name="mlp_down_bwd", grid=(dff // tk, T // tm),
        in_specs=[pl.BlockSpec((tm, D), lambda j, i: (i, 0)), pl.BlockSpec((tk, D), lambda j, i: (j, 0)), blk],
        out_specs=blk,
        out_shape=jax.ShapeDtypeStruct((T, dff), BF),
        compiler_params=_params(2))(dz2b, wdown, r)


def _tn_matmul(a, b, nblk, a_bw, b_bw, a_blocked, b_blocked, name, tt=2048):
    T = a.shape[0]
    nt = T // tt

    def body(a_ref, b_ref, o32_ref, o16_ref):
        t = pl.program_id(1)

        @pl.when(t == 0)
        def _():
            o32_ref[...] = jnp.zeros_like(o32_ref)

        o32_ref[0] += lax.dot_general(a_ref[...], b_ref[...], TN_DIMS, preferred_element_type=F32)

        @pl.when(t == nt - 1)
        def _():
            o16_ref[...] = o32_ref[...].astype(BF)

    a_spec = pl.BlockSpec((tt, a_bw), (lambda j, t: (t, j)) if a_blocked else (lambda j, t: (t, 0)))
    b_spec = pl.BlockSpec((tt, b_bw), (lambda j, t: (t, j)) if b_blocked else (lambda j, t: (t, 0)))
    out = pl.BlockSpec((1, a_bw, b_bw), lambda j, t: (j, 0, 0))
    return pl.pallas_call(
        body, name=name, grid=(nblk, nt),
        in_specs=[a_spec, b_spec], out_specs=[out, out],
        out_shape=[jax.ShapeDtypeStruct((nblk, a_bw, b_bw), F32), jax.ShapeDtypeStruct((nblk, a_bw, b_bw), BF)],
        compiler_params=_params(2))(a, b)


def _mlp_up_bwd(dhpre, wup, dz2, xhat1, rstd1, vecs, push, tm=512):
    T, dff = dhpre.shape

    def body(dh_ref, w_ref, dz2_ref, xhat_ref, rstd_ref, vec_ref, dz1_ref, dz1b_ref, st_ref):
        @pl.when(pl.program_id(0) == 0)
        def _():
            st_ref[...] = jnp.zeros_like(st_ref)

        dx1 = lax.dot_general(dh_ref[...], w_ref[...], NT_DIMS, preferred_element_type=F32) + ALPHA * dz2_ref[...]
        xhat = xhat_ref[...]
        st_ref[0:1, :] += _colsum(dx1 * xhat)
        st_ref[1:2, :] += _colsum(dx1)
        dz1 = _ln_bwd(dx1, xhat, rstd_ref[...], vec_ref[3:4, :])
        dz1_ref[...] = dz1
        dz1b_ref[...] = dz1.astype(BF)

    tok = pl.BlockSpec((tm, D), lambda i: (i, 0))
    vec = pl.BlockSpec((8, D), lambda i: (0, 0))
    return _pallas(
        body, name="mlp_up_bwd", grid=(T // tm,),
        in_specs=[pl.BlockSpec((tm, dff), lambda i: (i, 0)), _resident((D, dff)),
                  tok, tok, pl.BlockSpec((tm, 1), lambda i: (i, 0)), vec],
        out_specs=[tok, tok, vec],
        out_shape=[jax.ShapeDtypeStruct((T, D), F32), jax.ShapeDtypeStruct((T, D), BF), jax.ShapeDtypeStruct((8, D), F32)],
        args=(dhpre, wup, dz2, xhat1, rstd1, vecs), push=push)


def _merge_bwd(dz1, p, ya, yb, conva, xhatb, rstdb, woa, wob, wo, vecs, push, tm=256):
    T = dz1.shape[0]

    def body(dz1_ref, ga_ref, gb_ref, ba_ref, ya_ref, yb_ref, conva_ref, xhat_ref, rstd_ref,
             woa_ref, wob_ref, wo_ref, vec_ref,
             dya_ref, dyb_ref, dg_ref, dba_ref, dconva_ref, du1_ref, st_ref):
        @pl.when(pl.program_id(0) == 0)
        def _():
            st_ref[...] = jnp.zeros_like(st_ref)

        dmerged = lax.dot_general(dz1_ref[...], wo_ref[...], NT_DIMS, preferred_element_type=F32)
        sa, sb = _sigmoid(ga_ref[...].astype(F32)), _sigmoid(gb_ref[...].astype(F32))
        dya = (dmerged * sa).astype(BF)
        dyb = (dmerged * sb).astype(BF)
        dya_ref[...] = dya
        dyb_ref[...] = dyb
        dg_ref[:, 0:D] = (dmerged * ya_ref[...].astype(F32) * (sa * (1.0 - sa))).astype(BF)
        dg_ref[:, D:2 * D] = (dmerged * yb_ref[...].astype(F32) * (sb * (1.0 - sb))).astype(BF)

        dyapre = lax.dot_general(dya, woa_ref[...], NT_DIMS, preferred_element_type=F32)
        dba_ref[...] = (dyapre * conva_ref[...]).astype(BF)
        dconva_ref[...] = dyapre * ba_ref[...].astype(F32)

        du3 = lax.dot_general(dyb, wob_ref[...], NT_DIMS, preferred_element_type=F32)
        xhat = xhat_ref[...]
        gamma = vec_ref[1:2, :]
        u2 = xhat * gamma + vec_ref[2:3, :]
        s = _sigmoid(u2)
        du2 = du3 * (s * (1.0 + u2 * (1.0 - s)))
        st_ref[0:1, :] += _colsum(du2 * xhat)
        st_ref[1:2, :] += _colsum(du2)
        du1 = _ln_bwd(du2, xhat, rstd_ref[...], gamma)
        st_ref[2:3, :] += _colsum(du1)
        du1_ref[...] = du1

    tok = pl.BlockSpec((tm, D), lambda i: (i, 0))
    wfull = pl.BlockSpec((D, D), lambda i: (0, 0))
    vec = pl.BlockSpec((8, D), lambda i: (0, 0))
    return _pallas(
        body, name="merge_bwd", grid=(T // tm,),
        in_specs=[tok, _seg(tm, 5), _seg(tm, 6), _seg(tm, 0), tok, tok, tok, tok, pl.BlockSpec((tm, 1), lambda i: (i, 0)),
                  wfull, wfull, wfull, vec],
        out_specs=[tok, tok, pl.BlockSpec((tm, 2 * D), lambda i: (i, 0)), tok, tok, tok, vec],
        out_shape=[jax.ShapeDtypeStruct((T, D), BF), jax.ShapeDtypeStruct((T, D), BF),
                   jax.ShapeDtypeStruct((T, 2 * D), BF), jax.ShapeDtypeStruct((T, D), BF),
                   jax.ShapeDtypeStruct((T, D), F32), jax.ShapeDtypeStruct((T, D), F32),
                   jax.ShapeDtypeStruct((8, D), F32)],
        args=(dz1, p, p, p, ya, yb, conva, xhatb, rstdb, woa, wob, wo, vecs), push=push)


def _rows8(v):
    out = v[0:8]
    for q in range(1, RC // 8):
        out = out + v[8 * q:8 * q + 8]
    return out


def _conv_bwd(dconva, du1, p, dba, dg, wa, wb, push, tt=256):
    T = p.shape[0]
    nsteps = T // tt

    def body(dca_ref, dca_n, du1_ref, du1_n, ca, va, vb, gb, ca_p, va_p, vb_p, gb_p, dba_ref, dg_ref, wa_ref, wb_ref,
             dp_ref, gw_ref, cabuf, u0buf, dcabuf, du1buf, dcain, du0, gwa, gwb, shu, shd):
        i = pl.program_id(0)
        first, last = i == 0, i == nsteps - 1

        @pl.when(first)
        def _():
            gwa[...] = jnp.zeros_like(gwa)
            gwb[...] = jnp.zeros_like(gwb)

        f = lambda ref: ref[...].astype(F32)
        cav, vav, vbv = f(ca), f(va), f(vb)
        cabuf[0:HA, :] = jnp.where(first, 0.0, f(ca_p) * f(va_p))
        cabuf[HA:HA + tt, :] = cav * vav
        sg = _sigmoid(f(gb))
        u0buf[0:HB, :] = jnp.where(first, 0.0, f(vb_p) * _sigmoid(f(gb_p)))
        u0buf[HB:HB + tt, :] = vbv * sg
        dcabuf[0:tt, :] = dca_ref[...]
        dcabuf[tt:tt + HN, :] = jnp.where(last, 0.0, dca_n[...])
        du1buf[0:tt, :] = du1_ref[...]
        du1buf[tt:tt + HB, :] = jnp.where(last, 0.0, du1_n[...])

        def lane_body(cidx, carry):
            ls = pl.ds(pl.multiple_of(cidx * LANES, LANES), LANES)
            _shifted_copies(shu, u0buf, ls, tt + HB - 8)
            _shifted_copies(shd, du1buf, ls, tt + HB - 8)
            for r in range(tt // RC):
                dout = dcabuf[pl.ds(r * RC, RC), ls]
                acc = jnp.zeros((RC, LANES), F32)
                for k in range(KA):
                    acc = acc + wa_ref[k:k + 1, ls] * dcabuf[pl.ds(r * RC + KA - 1 - k, RC), ls]
                    gwa[8 * k:8 * k + 8, ls] += _rows8(dout * cabuf[pl.ds(HA - (KA - 1) + k + r * RC, RC), ls])
                dcain[pl.ds(r * RC, RC), ls] = acc
                dout = du1buf[pl.ds(r * RC, RC), ls]
                acc = jnp.zeros((RC, LANES), F32)
                for k in range(KB):
                    acc = acc + wb_ref[k:k + 1, ls] * _tap(shd, du1buf, ls, r * RC + KB - 1 - k, RC)
                    gwb[8 * k:8 * k + 8, ls] += _rows8(dout * _tap(shu, u0buf, ls, HB - (KB - 1) + k + r * RC, RC))
                du0[pl.ds(r * RC, RC), ls] = acc
            return carry

        lax.fori_loop(0, D // LANES, lane_body, 0)
        dca_in = dcain[...]
        du0v = du0[...]
        dp_ref[:, 0:D] = dba_ref[...]
        dp_ref[:, D:2 * D] = (dca_in * vav).astype(BF)
        dp_ref[:, 2 * D:3 * D] = (dca_in * cav).astype(BF)
        dp_ref[:, 3 * D:4 * D] = (du0v * sg).astype(BF)
        dp_ref[:, 4 * D:5 * D] = (du0v * vbv * (sg * (1.0 - sg))).astype(BF)
        dp_ref[:, 5 * D:7 * D] = dg_ref[...]

        @pl.when(last)
        def _():
            gw_ref[...] = jnp.zeros_like(gw_ref)
            for k in range(KA):
                gw_ref[k:k + 1, :] = _colsum(gwa[8 * k:8 * k + 8, :])
            for k in range(KB):
                gw_ref[8 + k:9 + k, :] = _colsum(gwb[8 * k:8 * k + 8, :])

    full = lambda r: pl.BlockSpec((r, D), lambda i: (0, 0))
    tok = pl.BlockSpec((tt, D), lambda i: (i, 0))
    nxt = lambda h: pl.BlockSpec((h, D), lambda i: (jnp.minimum((i + 1) * (tt // h), T // h - 1), 0))
    return _pallas(
        body, name="conv_bwd", grid=(nsteps,),
        in_specs=[tok, nxt(HN), tok, nxt(HB),
                  _seg(tt, 1), _seg(tt, 2), _seg(tt, 3), _seg(tt, 4),
                  _prev(tt, HA, 1), _prev(tt, HA, 2), _prev(tt, HB, 3), _prev(tt, HB, 4),
                  tok, pl.BlockSpec((tt, 2 * D), lambda i: (i, 0)), full(8), full(32)],
        out_specs=[pl.BlockSpec((tt, 7 * D), lambda i: (i, 0)), full(40)],
        out_shape=[jax.ShapeDtypeStruct((T, 7 * D), BF), jax.ShapeDtypeStruct((40, D), F32)],
        scratch_shapes=[pltpu.VMEM((HA + tt, D), F32), pltpu.VMEM((HB + tt, D), F32),
                        pltpu.VMEM((tt + HN, D), F32), pltpu.VMEM((tt + HB, D), F32),
                        pltpu.VMEM((tt, D), F32), pltpu.VMEM((tt, D), F32),
                        pltpu.VMEM((8 * KA, D), F32), pltpu.VMEM((8 * KB, D), F32),
                        pltpu.VMEM((8, HB + tt, LANES), F32), pltpu.VMEM((8, HB + tt, LANES), F32)],
        args=(dconva, dconva, du1, du1, p, p, p, p, p, p, p, p, dba, dg, wa, wb), push=push)


def _inproj_bwd(dp, win, dz1, push, tm=512):
    T, cols = dp.shape

    def body(dp_ref, w_ref, dz1_ref, o_ref):
        o_ref[...] = ALPHA * dz1_ref[...] + lax.dot_general(dp_ref[...], w_ref[...], NT_DIMS,
                                                            preferred_element_type=F32)

    tok = pl.BlockSpec((tm, D), lambda i: (i, 0))
    return _pallas(
        body, name="inproj_bwd", grid=(T // tm,),
        in_specs=[pl.BlockSpec((tm, cols), lambda i: (i, 0)), _resident((D, cols)), tok],
        out_specs=[tok],
        out_shape=[jax.ShapeDtypeStruct((T, D), F32)],
        args=(dp, win, dz1), push=push)


def _sum_blocks(g8):
    def body(g_ref, o_ref):
        acc = g_ref[0]
        for j in range(1, NDEV):
            acc = acc + g_ref[j]
        o_ref[...] = acc

    return pl.pallas_call(body, name="sum_small_grads",
                          out_shape=jax.ShapeDtypeStruct(g8.shape[1:], g8.dtype))(g8)


def _adamw(w, m, v, own, landing, name, rb):
    R, C = w.shape
    nl = 0 if landing is None else landing.shape[0]

    def body(*refs):
        w_ref, m_ref, v_ref, own_ref = refs[:4]
        l_ref = refs[4] if nl else None
        g_ref, d_ref, nm_ref, nv_ref = refs[4 + (1 if nl else 0):]
        g = own_ref[...]
        for k in range(nl):
            g = g + l_ref[k].astype(F32)
        nm = ADAM_B1 * m_ref[...] + (1.0 - ADAM_B1) * g
        nv = ADAM_B2 * v_ref[...] + (1.0 - ADAM_B2) * (g * g)
        m_hat = nm / (1.0 - ADAM_B1 ** ADAM_STEP)
        v_hat = nv / (1.0 - ADAM_B2 ** ADAM_STEP)
        g_ref[...] = g
        d_ref[...] = -ADAM_LR * (m_hat / (jnp.sqrt(v_hat) + ADAM_EPS) + ADAM_WD * w_ref[...])
        nm_ref[...] = nm
        nv_ref[...] = nv

    blk = pl.BlockSpec((rb, C), lambda i: (i, 0))
    in_specs = [blk] * 4 + ([pl.BlockSpec((nl, rb, C), lambda i: (0, i, 0))] if nl else [])
    args = (w, m, v, own) + ((landing,) if nl else ())
    return pl.pallas_call(
        body, name=name, grid=(R // rb,), in_specs=in_specs, out_specs=[blk] * 4,
        out_shape=[jax.ShapeDtypeStruct((R, C), F32)] * 4,
        compiler_params=_params(1))(*args)


def _pad_rows(a, rows):
    return jnp.pad(a, ((0, rows - a.shape[0]), (0, 0)))


def _local_step(p, xb, x, target, win, wup, wdown, woa, wob, wo, wa, wb, vecs):
    yapre, conva, xhatb, rstdb, u3 = _mixer_fwd(p, wa, wb, vecs)
    ya, yb, merged, xhat1, rstd1, x1b = _post_mixer(yapre, u3, p, x, woa, wob, wo, vecs)
    r, h = _mlp_up(x1b, wup)
    dz2, dz2b, st2 = _mlp_down_loss(h, wdown, xhat1, target, vecs)

    by_owner = lambda g16: g16.reshape(NDEV, D // NDEV, D)
    dhpre = _mlp_down_bwd(dz2b, wdown, r)
    g_wdown = _tn_matmul(h, dz2b, NDEV, 512, D, True, False, "grad_w_down")
    (dz1, dz1b, st1), land_wdown = _mlp_up_bwd(dhpre, wup, dz2, xhat1, rstd1, vecs, _push(exch=[g_wdown[1]]))
    g_wup = _tn_matmul(x1b, dhpre, NDEV, D, 512, False, True, "grad_w_up")
    (dya, dyb, dg, dba, dconva, du1, stb), land_wup = _merge_bwd(
        dz1b, p, ya, yb, conva, xhatb, rstdb, woa, wob, wo, vecs, _push(exch=[g_wup[1]]))
    g_wo = _tn_matmul(merged, dz1b, 1, D, D, False, False, "grad_w_o")
    g_woa = _tn_matmul(yapre, dya, 1, D, D, False, False, "grad_w_out_a")
    g_wob = _tn_matmul(u3, dyb, 1, D, D, False, False, "grad_w_out_b")
    (dp, gw), land_sq = _conv_bwd(dconva, du1, p, dba, dg, wa, wb,
                                  _push(exch=[by_owner(g_woa[1]), by_owner(g_wob[1]), by_owner(g_wo[1])]))
    g_win = _tn_matmul(xb, dp, NDEV, D, 896, False, True, "grad_w_in")

    small = jnp.concatenate([stb[2:3], stb[0:2], st1[0:2], st2[0:3], gw], axis=0)
    (grad_x,), land_last = _inproj_bwd(dp, win, dz1, _push(exch=[g_win[1]], gath=[small]))
    grads = (g_win[0], g_wup[0], g_wdown[0], g_woa[0], g_wob[0], g_wo[0])
    return grad_x, grads, small, land_wdown + land_wup + land_sq + land_last


def kernel(x, w_in, conv_a_w, w_out_a, conv_b_w, conv_b_bias, ln_b_gamma, ln_b_beta, w_out_b, w_o, ln1_gamma, ln1_beta, w_up, w_down, ln2_gamma, ln2_beta, loss_target, m_w_in, m_conv_a_w, m_w_out_a, m_conv_b_w, m_conv_b_bias, m_ln_b_gamma, m_ln_b_beta, m_w_out_b, m_w_o, m_ln1_gamma, m_ln1_beta, m_w_up, m_w_down, m_ln2_gamma, m_ln2_beta, v_w_in, v_conv_a_w, v_w_out_a, v_conv_b_w, v_conv_b_bias, v_ln_b_gamma, v_ln_b_beta, v_w_out_b, v_w_o, v_ln1_gamma, v_ln1_beta, v_w_up, v_w_down, v_ln2_gamma, v_ln2_beta):
    T = x.shape[1]
    me = 4 * lax.axis_index("x") + 2 * lax.axis_index("y") + lax.axis_index("c")

    conv_shard = jnp.concatenate([_pad_rows(conv_a_w, 8), _pad_rows(conv_b_w, 32)], axis=0)
    win_g, conv_g = _allgather([w_in.astype(BF), conv_shard], "allgather_w_in")
    conv_full = jnp.transpose(conv_g, (1, 0, 2)).reshape(40, D)
    vecs = jnp.stack([conv_b_bias, ln_b_gamma, ln_b_beta, ln1_gamma, ln1_beta, ln2_gamma, ln2_beta,
                      jnp.zeros_like(ln2_beta)])
    whole = lambda g: jnp.transpose(g, (1, 0, 2)).reshape(D, -1)
    win = whole(win_g)
    p, xb, (wup_g, wdown_g, woa_g, wob_g, wo_g) = _inproj(
        x[0], win, [w_up.astype(BF), w_down.astype(BF), w_out_a.astype(BF), w_out_b.astype(BF), w_o.astype(BF)])

    grad_x, grads, small, landing = _local_step(
        p, xb, x[0], loss_target[0], win, whole(wup_g), wdown_g.reshape(NDEV * 512, D), woa_g.reshape(D, D),
        wob_g.reshape(D, D), wo_g.reshape(D, D), conv_full[0:8], conv_full[8:40], vecs)
    g_win, g_wup, g_wdown, g_woa, g_wob, g_wo = grads
    l_wdown, l_wup, l_woa, l_wob, l_wo, l_win, small_g = landing
    small_sum = _sum_blocks(small_g)

    loss = lax.psum(0.5 / D * jnp.sum(small[7]), ("x", "y", "c"))

    def own(g32):
        blocked = g32 if g32.shape[0] == NDEV else g32.reshape(NDEV, 128, D)
        return lax.dynamic_index_in_dim(blocked, me, 0, keepdims=False)

    r_win = _adamw(w_in, m_w_in, v_w_in, own(g_win), l_win, "adamw_w_in", 256)
    r_wup = _adamw(w_up, m_w_up, v_w_up, own(g_wup), l_wup, "adamw_w_up", 256)
    r_wdown = _adamw(w_down, m_w_down, v_w_down, own(g_wdown), l_wdown, "adamw_w_down", 256)
    r_woa = _adamw(w_out_a, m_w_out_a, v_w_out_a, own(g_woa), l_woa, "adamw_w_out_a", 128)
    r_wob = _adamw(w_out_b, m_w_out_b, v_w_out_b, own(g_wob), l_wob, "adamw_w_out_b", 128)
    r_wo = _adamw(w_o, m_w_o, v_w_o, own(g_wo), l_wo, "adamw_w_o", 128)

    vec_m = jnp.stack([m_conv_b_bias, m_ln_b_gamma, m_ln_b_beta, m_ln1_gamma, m_ln1_beta, m_ln2_gamma, m_ln2_beta,
                       jnp.zeros_like(ln2_beta)])
    vec_v = jnp.stack([v_conv_b_bias, v_ln_b_gamma, v_ln_b_beta, v_ln1_gamma, v_ln1_beta, v_ln2_gamma, v_ln2_beta,
                       jnp.zeros_like(ln2_beta)])
    vec_g = jnp.concatenate([small_sum[0:7], jnp.zeros((1, D), F32)], axis=0)
    r_vec = _adamw(vecs, vec_m, vec_v, vec_g, None, "adamw_vectors", 8)

    conv_cols = lax.dynamic_slice_in_dim(small_sum[8:48], me * 128, 128, axis=1)
    conv_m = jnp.concatenate([_pad_rows(m_conv_a_w, 8), _pad_rows(m_conv_b_w, 32)], axis=0)
    conv_v = jnp.concatenate([_pad_rows(v_conv_a_w, 8), _pad_rows(v_conv_b_w, 32)], axis=0)
    r_conv = _adamw(conv_shard, conv_m, conv_v, conv_cols, None, "adamw_conv", 40)

    per_weight = []
    for q in range(4):
        per_weight.append([
            r_win[q], r_conv[q][0:KA], r_woa[q], r_conv[q][8:8 + KB],
            r_vec[q][0], r_vec[q][1], r_vec[q][2], r_wob[q], r_wo[q], r_vec[q][3], r_vec[q][4],
            r_wup[q], r_wdown[q], r_vec[q][5], r_vec[q][6]])
    return (loss, grad_x[None], *per_weight[0], *per_weight[1], *per_weight[2], *per_weight[3])
```

```python
import functools

import jax
import jax.numpy as jnp
from jax import lax
from jax.experimental import pallas as pl
from jax.experimental.pallas import tpu as pltpu

F32 = jnp.float32
BF = jnp.bfloat16
D = 1024
NDEV = 8
ALPHA = 2.0 ** 0.25
LN_EPS = 1e-5
KA, KB = 3, 31
HA, HB = 16, 32
HN = 8
RC = 64
LANES = 128
VMEM_LIMIT = 56 * 1024 * 1024
MESH = pl.DeviceIdType.MESH
ADAM_LR, ADAM_B1, ADAM_B2, ADAM_EPS, ADAM_WD, ADAM_STEP = 0.001, 0.9, 0.999, 1e-08, 0.01, 10

ANY_SPEC = pl.BlockSpec(memory_space=pl.ANY)
NT_DIMS = (((1,), (1,)), ((), ()))
TN_DIMS = (((0,), (0,)), ((), ()))


def _params(n_axes):
    return pltpu.CompilerParams(dimension_semantics=("arbitrary",) * n_axes, vmem_limit_bytes=VMEM_LIMIT)


def _sigmoid(v):
    return 0.5 * jnp.tanh(0.5 * v) + 0.5


def _ln_fwd(z):
    mu = jnp.mean(z, axis=-1, keepdims=True)
    zc = z - mu
    var = jnp.mean(zc * zc, axis=-1, keepdims=True)
    rstd = lax.rsqrt(var + LN_EPS)
    return zc * rstd, rstd


def _ln_bwd(dy, xhat, rstd, gamma):
    dxhat = dy * gamma
    m1 = jnp.mean(dxhat, axis=-1, keepdims=True)
    m2 = jnp.mean(dxhat * xhat, axis=-1, keepdims=True)
    return rstd * (dxhat - m1 - xhat * m2)


def _colsum(v):
    return jnp.sum(v, axis=0, keepdims=True)


def _allgather(arrs, name):
    n = len(arrs)

    def body(*refs):
        g = _TwoLevelGather(refs[:n], refs[n:2 * n], *refs[2 * n:])
        g.start()
        g.forward()
        g.finish()

    return pl.pallas_call(
        body, name=name,
        in_specs=[ANY_SPEC] * n, out_specs=[ANY_SPEC] * n,
        out_shape=_TwoLevelGather.out_shape(arrs), scratch_shapes=_TwoLevelGather.scratch(n),
    )(*arrs)


class _TwoLevelGather:
    def __init__(self, ins, outs, send_sems, recv_sems, local_sems):
        self.ins, self.outs = ins, outs
        self.send_sems, self.recv_sems, self.local_sems = send_sems, recv_sems, local_sems
        x, y, c = lax.axis_index("x"), lax.axis_index("y"), lax.axis_index("c")
        self.me, self.sibling, self.c = (x, y, c), (x, y, 1 - c), c
        self.chips = [(1 - x, y), (x, 1 - y), (1 - x, 1 - y)]
        self.n = len(ins)

    @staticmethod
    def out_shape(arrs):
        return [jax.ShapeDtypeStruct((NDEV,) + a.shape, a.dtype) for a in arrs]

    @staticmethod
    def scratch(n):
        return [pltpu.SemaphoreType.DMA((n, 7)), pltpu.SemaphoreType.DMA((n, 7)), pltpu.SemaphoreType.DMA((n,))]

    def _copy(self, a, k, block, to, src=None):
        px, py, pc = block
        rows = self.outs[a].at[4 * px + 2 * py + pc]
        return pltpu.make_async_remote_copy(
            src_ref=rows if src is None else src, dst_ref=rows,
            send_sem=self.send_sems.at[a, k], recv_sem=self.recv_sems.at[a, k],
            device_id=to, device_id_type=MESH)

    def _mine(self, a):
        x, y, c = self.me
        return pltpu.make_async_copy(self.ins[a], self.outs[a].at[4 * x + 2 * y + c], self.local_sems.at[a])

    def _first(self, a):
        cps = [self._copy(a, 0, self.me, self.sibling, src=self.ins[a])]
        return cps + [self._copy(a, 1 + j, self.me, (*chip, self.c), src=self.ins[a]) for j, chip in enumerate(self.chips)]

    def _passed(self, a, j):
        return self._copy(a, 4 + j, (*self.chips[j], self.c), self.sibling)

    def start(self):
        for a in range(self.n):
            self._mine(a).start()
        for a in range(self.n):
            for cp in self._first(a):
                cp.start()

    def forward(self):
        for j, chip in enumerate(self.chips):
            for a in range(self.n):
                self._copy(a, 1 + j, (*chip, self.c), self.me).wait_recv()
                self._passed(a, j).start()

    def finish(self):
        for a in range(self.n):
            self._copy(a, 0, self.sibling, self.me).wait_recv()
            for j, chip in enumerate(self.chips):
                self._copy(a, 4 + j, (*chip, 1 - self.c), self.me).wait_recv()
        for a in range(self.n):
            for cp in self._first(a) + [self._passed(a, j) for j in range(3)]:
                cp.wait_send()
            self._mine(a).wait()


class _Push:
    def __init__(self, exch=(), gath=()):
        self.exch, self.gath = list(exch), list(gath)
        self.n = len(self.exch) + len(self.gath)

    def operands(self):
        return self.exch + self.gath

    def out_shape(self):
        return ([jax.ShapeDtypeStruct((NDEV - 1,) + a.shape[1:], a.dtype) for a in self.exch]
                + [jax.ShapeDtypeStruct((NDEV,) + a.shape, a.dtype) for a in self.gath])

    def scratch(self):
        return [pltpu.SemaphoreType.DMA((self.n, 7)), pltpu.SemaphoreType.DMA((self.n, 7)),
                pltpu.SemaphoreType.DMA((max(len(self.gath), 1),))]

    def copies(self, ins, outs, send_sems, recv_sems, local_sems):
        x, y, c = lax.axis_index("x"), lax.axis_index("y"), lax.axis_index("c")
        me = 4 * x + 2 * y + c
        ne = len(self.exch)
        remote = []
        for k in range(1, NDEV):
            px = 1 - x if k & 4 else x
            py = 1 - y if k & 2 else y
            pc = 1 - c if k & 1 else c
            for a in range(self.n):
                src = ins[a].at[4 * px + 2 * py + pc] if a < ne else ins[a]
                dst = outs[a].at[k - 1] if a < ne else outs[a].at[me]
                remote.append(pltpu.make_async_remote_copy(
                    src_ref=src, dst_ref=dst, send_sem=send_sems.at[a, k - 1], recv_sem=recv_sems.at[a, k - 1],
                    device_id=(px, py, pc), device_id_type=MESH))
        local = [pltpu.make_async_copy(ins[a], outs[a].at[me], local_sems.at[a - ne]) for a in range(ne, self.n)]
        return remote, local


def _push(exch=(), gath=()):
    return _Push(exch, gath)


def _pallas(body, *, name, grid, in_specs, out_specs, out_shape, args, scratch_shapes=(), push=None):
    ni, no, ns = len(in_specs), len(out_specs), len(scratch_shapes)
    if push is None:
        outs = pl.pallas_call(
            body, name=name, grid=grid, in_specs=in_specs, out_specs=out_specs, out_shape=out_shape,
            scratch_shapes=list(scratch_shapes), compiler_params=_params(len(grid)))(*args)
        return list(outs), []
    npush = push.n

    def wrapped(*refs):
        ins, pins = refs[:ni], refs[ni:ni + npush]
        outs, pouts = refs[ni + npush:ni + npush + no], refs[ni + npush + no:ni + 2 * npush + no]
        scr, sems = refs[ni + 2 * npush + no:ni + 2 * npush + no + ns], refs[ni + 2 * npush + no + ns:]
        first = functools.reduce(jnp.logical_and, [pl.program_id(d) == 0 for d in range(len(grid))])
        last = functools.reduce(jnp.logical_and, [pl.program_id(d) == grid[d] - 1 for d in range(len(grid))])
        remote, local = push.copies(pins, pouts, *sems)

        @pl.when(first)
        def _():
            for cp in local + remote:
                cp.start()

        body(*ins, *outs, *scr)

        @pl.when(last)
        def _():
            for cp in remote + local:
                cp.wait()

    outs = pl.pallas_call(
        wrapped, name=name, grid=grid,
        in_specs=list(in_specs) + [ANY_SPEC] * npush, out_specs=list(out_specs) + [ANY_SPEC] * npush,
        out_shape=list(out_shape) + push.out_shape(), scratch_shapes=list(scratch_shapes) + push.scratch(),
        compiler_params=_params(len(grid)))(*args, *push.operands())
    return list(outs[:no]), list(outs[no:])


def _inproj(x, win, late, tm=1024, bw=1024):
    T = x.shape[0]
    nb = win.shape[1] // bw
    n = len(late)
    ni = T // tm

    def body(*refs):
        x_ref, w_ref = refs[:2]
        o_ref, xb_ref = refs[2 + n:4 + n]
        gather = _TwoLevelGather(refs[2:2 + n], refs[4 + n:4 + 2 * n], *refs[4 + 2 * n:])
        i, j = pl.program_id(0), pl.program_id(1)

        @pl.when(jnp.logical_and(i == 0, j == 0))
        def _():
            gather.start()

        @pl.when(j == 0)
        def _():
            xb_ref[...] = x_ref[...].astype(BF)

        o_ref[...] = jnp.dot(xb_ref[...], w_ref[...], preferred_element_type=F32).astype(BF)

        @pl.when(jnp.logical_and(i == (3 * ni) // 4, j == 0))
        def _():
            gather.forward()

        @pl.when(jnp.logical_and(i == ni - 1, j == nb - 1))
        def _():
            gather.finish()

    outs = pl.pallas_call(
        body, name="inproj", grid=(ni, nb),
        in_specs=[pl.BlockSpec((tm, D), lambda i, j: (i, 0)), pl.BlockSpec((D, bw), lambda i, j: (0, j))]
        + [ANY_SPEC] * n,
        out_specs=[pl.BlockSpec((tm, bw), lambda i, j: (i, j)), pl.BlockSpec((tm, D), lambda i, j: (i, 0))]
        + [ANY_SPEC] * n,
        out_shape=[jax.ShapeDtypeStruct((T, nb * bw), BF), jax.ShapeDtypeStruct((T, D), BF)]
        + _TwoLevelGather.out_shape(late),
        scratch_shapes=_TwoLevelGather.scratch(n),
        compiler_params=_params(2))(x, win, *late)
    return outs[0], outs[1], list(outs[2:])


def _mixer_fwd(p, wa, wb, vecs, tt=256):
    T = p.shape[0]

    def body(ba, ca, va, vb, gb, ca_p, va_p, vb_p, gb_p, wa_ref, wb_ref, vec_ref,
             yapre_ref, conva_ref, xhat_ref, rstd_ref, u3_ref, cabuf, u0buf, u1buf, shu):
        first = pl.program_id(0) == 0
        f = lambda ref: ref[...].astype(F32)
        cabuf[0:HA, :] = jnp.where(first, 0.0, f(ca_p) * f(va_p))
        cabuf[HA:HA + tt, :] = f(ca) * f(va)
        u0buf[0:HB, :] = jnp.where(first, 0.0, f(vb_p) * _sigmoid(f(gb_p)))
        u0buf[HB:HB + tt, :] = f(vb) * _sigmoid(f(gb))

        def lane_body(cidx, carry):
            ls = pl.ds(pl.multiple_of(cidx * LANES, LANES), LANES)
            _shifted_copies(shu, u0buf, ls, tt + HB - 8)
            for r in range(tt // RC):
                acc = jnp.zeros((RC, LANES), F32)
                for k in range(KA):
                    acc = acc + wa_ref[k:k + 1, ls] * cabuf[pl.ds(HA - (KA - 1) + k + r * RC, RC), ls]
                conva_ref[pl.ds(r * RC, RC), ls] = acc
                acc = jnp.zeros((RC, LANES), F32)
                for k in range(KB):
                    acc = acc + wb_ref[k:k + 1, ls] * _tap(shu, u0buf, ls, HB - (KB - 1) + k + r * RC, RC)
                u1buf[pl.ds(r * RC, RC), ls] = acc
            return carry

        lax.fori_loop(0, D // LANES, lane_body, 0)
        yapre_ref[...] = (f(ba) * conva_ref[...]).astype(BF)
        xhat, rstd = _ln_fwd(u1buf[...] + vec_ref[0:1, :])
        xhat_ref[...] = xhat
        rstd_ref[...] = rstd
        u2 = xhat * vec_ref[1:2, :] + vec_ref[2:3, :]
        u3_ref[...] = (u2 * _sigmoid(u2)).astype(BF)

    full = lambda r: pl.BlockSpec((r, D), lambda i: (0, 0))
    tok = pl.BlockSpec((tt, D), lambda i: (i, 0))
    return pl.pallas_call(
        body, name="mixer_fwd", grid=(T // tt,),
        in_specs=[_seg(tt, 0), _seg(tt, 1), _seg(tt, 2), _seg(tt, 3), _seg(tt, 4),
                  _prev(tt, HA, 1), _prev(tt, HA, 2), _prev(tt, HB, 3), _prev(tt, HB, 4),
                  full(8), full(32), full(8)],
        out_specs=[tok, tok, tok, pl.BlockSpec((tt, 1), lambda i: (i, 0)), tok],
        out_shape=[jax.ShapeDtypeStruct((T, D), BF), jax.ShapeDtypeStruct((T, D), F32),
                   jax.ShapeDtypeStruct((T, D), F32), jax.ShapeDtypeStruct((T, 1), F32),
                   jax.ShapeDtypeStruct((T, D), BF)],
        scratch_shapes=[pltpu.VMEM((HA + tt, D), F32), pltpu.VMEM((HB + tt, D), F32), pltpu.VMEM((tt, D), F32),
                        pltpu.VMEM((8, HB + tt, LANES), F32)],
        compiler_params=_params(1))(p, p, p, p, p, p, p, p, p, wa, wb, vecs)


def _seg(tt, s):
    return pl.BlockSpec((tt, D), lambda i: (i, s))


def _prev(tt, h, s):
    return pl.BlockSpec((h, D), lambda i: (jnp.maximum(i * (tt // h) - 1, 0), s))


def _shifted_copies(shbuf, src, ls, n):
    for s in range(1, 8):
        shbuf[s, 0:n, :] = src[pl.ds(s, n), ls]


def _tap(shbuf, src, ls, off, rows):
    s, q = off % 8, off // 8
    if s == 0:
        return src[pl.ds(off, rows), ls]
    return shbuf[s, pl.ds(8 * q, rows), :]


def _post_mixer(yapre, u3, p, x, woa, wob, wo, vecs, tm=256):
    T = x.shape[0]

    def body(yapre_ref, u3_ref, ga_ref, gb_ref, x_ref, woa_ref, wob_ref, wo_ref, vec_ref,
             ya_ref, yb_ref, merged_ref, xhat_ref, rstd_ref, x1b_ref):
        ya = jnp.dot(yapre_ref[...], woa_ref[...], preferred_element_type=F32)
        yb = jnp.dot(u3_ref[...], wob_ref[...], preferred_element_type=F32)
        ya_ref[...] = ya.astype(BF)
        yb_ref[...] = yb.astype(BF)
        merged = (_sigmoid(ga_ref[...].astype(F32)) * ya + _sigmoid(gb_ref[...].astype(F32)) * yb).astype(BF)
        merged_ref[...] = merged
        mix = jnp.dot(merged, wo_ref[...], preferred_element_type=F32)
        xhat, rstd = _ln_fwd(ALPHA * x_ref[...] + mix)
        xhat_ref[...] = xhat
        rstd_ref[...] = rstd
        x1b_ref[...] = (xhat * vec_ref[3:4, :] + vec_ref[4:5, :]).astype(BF)

    tok = pl.BlockSpec((tm, D), lambda i: (i, 0))
    wfull = pl.BlockSpec((D, D), lambda i: (0, 0))
    one = pl.BlockSpec((tm, 1), lambda i: (i, 0))
    return pl.pallas_call(
        body, name="post_mixer", grid=(T // tm,),
        in_specs=[tok, tok, _seg(tm, 5), _seg(tm, 6), tok, wfull, wfull, wfull, pl.BlockSpec((8, D), lambda i: (0, 0))],
        out_specs=[tok, tok, tok, tok, one, tok],
        out_shape=[jax.ShapeDtypeStruct((T, D), BF), jax.ShapeDtypeStruct((T, D), BF),
                   jax.ShapeDtypeStruct((T, D), BF), jax.ShapeDtypeStruct((T, D), F32),
                   jax.ShapeDtypeStruct((T, 1), F32), jax.ShapeDtypeStruct((T, D), BF)],
        compiler_params=_params(1))(yapre, u3, p, p, x, woa, wob, wo, vecs)


def _mlp_up(x1b, wup, tm=512, tn=2048):
    T = x1b.shape[0]
    dff = wup.shape[1]

    def body(x_ref, w_ref, r_ref, h_ref):
        r = jnp.maximum(jnp.dot(x_ref[...], w_ref[...], preferred_element_type=F32), 0.0)
        r_ref[...] = r.astype(BF)
        h_ref[...] = (r * r).astype(BF)

    out = pl.BlockSpec((tm, tn), lambda j, i: (i, j))
    return pl.pallas_call(
        body, name="mlp_up", grid=(dff // tn, T // tm),
        in_specs=[pl.BlockSpec((tm, D), lambda j, i: (i, 0)), pl.BlockSpec((D, tn), lambda j, i: (0, j))],
        out_specs=[out, out],
        out_shape=[jax.ShapeDtypeStruct((T, dff), BF), jax.ShapeDtypeStruct((T, dff), BF)],
        compiler_params=_params(2))(x1b, wup)


def _resident(shape):
    return pl.BlockSpec(shape, lambda *_: (0,) * len(shape), pipeline_mode=pl.Buffered(1))


def _mlp_down_loss(h, wdown, xhat1, target, vecs, tm=512):
    T, dff = h.shape

    def body(h_ref, w_ref, xhat1_ref, tgt_ref, vec_ref, dz2_ref, dz2b_ref, st_ref):
        @pl.when(pl.program_id(0) == 0)
        def _():
            st_ref[...] = jnp.zeros_like(st_ref)

        ff = jnp.dot(h_ref[...], w_ref[...], preferred_element_type=F32)
        x1 = xhat1_ref[...] * vec_ref[3:4, :] + vec_ref[4:5, :]
        xhat2, rstd2 = _ln_fwd(ALPHA * x1 + ff)
        g2 = vec_ref[5:6, :]
        diff = xhat2 * g2 + vec_ref[6:7, :] - tgt_ref[...]
        dx2 = diff * (1.0 / D)
        st_ref[0:1, :] += _colsum(dx2 * xhat2)
        st_ref[1:2, :] += _colsum(dx2)
        st_ref[2:3, :] += _colsum(diff * diff)
        dz2 = _ln_bwd(dx2, xhat2, rstd2, g2)
        dz2_ref[...] = dz2
        dz2b_ref[...] = dz2.astype(BF)

    tok = pl.BlockSpec((tm, D), lambda i: (i, 0))
    vec = pl.BlockSpec((8, D), lambda i: (0, 0))
    return pl.pallas_call(
        body, name="mlp_down_loss", grid=(T // tm,),
        in_specs=[pl.BlockSpec((tm, dff), lambda i: (i, 0)), _resident((dff, D)), tok, tok, vec],
        out_specs=[tok, tok, vec],
        out_shape=[jax.ShapeDtypeStruct((T, D), F32), jax.ShapeDtypeStruct((T, D), BF), jax.ShapeDtypeStruct((8, D), F32)],
        compiler_params=_params(1))(h, wdown, xhat1, target, vecs)


def _mlp_down_bwd(dz2b, wdown, r, tm=512, tk=2048):
    T, dff = r.shape

    def body(dz_ref, w_ref, r_ref, o_ref):
        dh = lax.dot_general(dz_ref[...], w_ref[...], NT_DIMS, preferred_element_type=F32)
        o_ref[...] = (dh * (2.0 * r_ref[...].astype(F32))).astype(BF)

    blk = pl.BlockSpec((tm, tk), lambda j, i: (i, j))
    return pl.pallas_call(
        body, name="mlp_down_bwd", grid=(dff // tk, T // tm),
        in_specs=[pl.BlockSpec((tm, D), lambda j, i: (i, 0)), pl.BlockSpec((tk, D), lambda j, i: (j, 0)), blk],
        out_specs=blk,
        out_shape=jax.ShapeDtypeStruct((T, dff), BF),
        compiler_params=_params(2))(dz2b, wdown, r)


def _tn_matmul(a, b, nblk, a_bw, b_bw, a_blocked, b_blocked, name, tt=2048):
    T = a.shape[0]
    nt = T // tt

    def body(a_ref, b_ref, o32_ref, o16_ref):
        t = pl.program_id(1)

        @pl.when(t == 0)
        def _():
            o32_ref[...] = jnp.zeros_like(o32_ref)

        o32_ref[0] += lax.dot_general(a_ref[...], b_ref[...], TN_DIMS, preferred_element_type=F32)

        @pl.when(t == nt - 1)
        def _():
            o16_ref[...] = o32_ref[...].astype(BF)

    a_spec = pl.BlockSpec((tt, a_bw), (lambda j, t: (t, j)) if a_blocked else (lambda j, t: (t, 0)))
    b_spec = pl.BlockSpec((tt, b_bw), (lambda j, t: (t, j)) if b_blocked else (lambda j, t: (t, 0)))
    out = pl.BlockSpec((1, a_bw, b_bw), lambda j, t: (j, 0, 0))
    return pl.pallas_call(
        body, name=name, grid=(nblk, nt),
        in_specs=[a_spec, b_spec], out_specs=[out, out],
        out_shape=[jax.ShapeDtypeStruct((nblk, a_bw, b_bw), F32), jax.ShapeDtypeStruct((nblk, a_bw, b_bw), BF)],
        compiler_params=_params(2))(a, b)


def _mlp_up_bwd(dhpre, wup, dz2, xhat1, rstd1, vecs, push, tm=512):
    T, dff = dhpre.shape

    def body(dh_ref, w_ref, dz2_ref, xhat_ref, rstd_ref, vec_ref, dz1_ref, dz1b_ref, st_ref):
        @pl.when(pl.program_id(0) == 0)
        def _():
            st_ref[...] = jnp.zeros_like(st_ref)

        dx1 = lax.dot_general(dh_ref[...], w_ref[...], NT_DIMS, preferred_element_type=F32) + ALPHA * dz2_ref[...]
        xhat = xhat_ref[...]
        st_ref[0:1, :] += _colsum(dx1 * xhat)
        st_ref[1:2, :] += _colsum(dx1)
        dz1 = _ln_bwd(dx1, xhat, rstd_ref[...], vec_ref[3:4, :])
        dz1_ref[...] = dz1
        dz1b_ref[...] = dz1.astype(BF)

    tok = pl.BlockSpec((tm, D), lambda i: (i, 0))
    vec = pl.BlockSpec((8, D), lambda i: (0, 0))
    return _pallas(
        body, name="mlp_up_bwd", grid=(T // tm,),
        in_specs=[pl.BlockSpec((tm, dff), lambda i: (i, 0)), _resident((D, dff)),
                  tok, tok, pl.BlockSpec((tm, 1), lambda i: (i, 0)), vec],
        out_specs=[tok, tok, vec],
        out_shape=[jax.ShapeDtypeStruct((T, D), F32), jax.ShapeDtypeStruct((T, D), BF), jax.ShapeDtypeStruct((8, D), F32)],
        args=(dhpre, wup, dz2, xhat1, rstd1, vecs), push=push)


def _merge_bwd(dz1, p, ya, yb, conva, xhatb, rstdb, woa, wob, wo, vecs, push, tm=256):
    T = dz1.shape[0]

    def body(dz1_ref, ga_ref, gb_ref, ba_ref, ya_ref, yb_ref, conva_ref, xhat_ref, rstd_ref,
             woa_ref, wob_ref, wo_ref, vec_ref,
             dya_ref, dyb_ref, dg_ref, dba_ref, dconva_ref, du1_ref, st_ref):
        @pl.when(pl.program_id(0) == 0)
        def _():
            st_ref[...] = jnp.zeros_like(st_ref)

        dmerged = lax.dot_general(dz1_ref[...], wo_ref[...], NT_DIMS, preferred_element_type=F32)
        sa, sb = _sigmoid(ga_ref[...].astype(F32)), _sigmoid(gb_ref[...].astype(F32))
        dya = (dmerged * sa).astype(BF)
        dyb = (dmerged * sb).astype(BF)
        dya_ref[...] = dya
        dyb_ref[...] = dyb
        dg_ref[:, 0:D] = (dmerged * ya_ref[...].astype(F32) * (sa * (1.0 - sa))).astype(BF)
        dg_ref[:, D:2 * D] = (dmerged * yb_ref[...].astype(F32) * (sb * (1.0 - sb))).astype(BF)

        dyapre = lax.dot_general(dya, woa_ref[...], NT_DIMS, preferred_element_type=F32)
        dba_ref[...] = (dyapre * conva_ref[...]).astype(BF)
        dconva_ref[...] = dyapre * ba_ref[...].astype(F32)

        du3 = lax.dot_general(dyb, wob_ref[...], NT_DIMS, preferred_element_type=F32)
        xhat = xhat_ref[...]
        gamma = vec_ref[1:2, :]
        u2 = xhat * gamma + vec_ref[2:3, :]
        s = _sigmoid(u2)
        du2 = du3 * (s * (1.0 + u2 * (1.0 - s)))
        st_ref[0:1, :] += _colsum(du2 * xhat)
        st_ref[1:2, :] += _colsum(du2)
        du1 = _ln_bwd(du2, xhat, rstd_ref[...], gamma)
        st_ref[2:3, :] += _colsum(du1)
        du1_ref[...] = du1

    tok = pl.BlockSpec((tm, D), lambda i: (i, 0))
    wfull = pl.BlockSpec((D, D), lambda i: (0, 0))
    vec = pl.BlockSpec((8, D), lambda i: (0, 0))
    return _pallas(
        body, name="merge_bwd", grid=(T // tm,),
        in_specs=[tok, _seg(tm, 5), _seg(tm, 6), _seg(tm, 0), tok, tok, tok, tok, pl.BlockSpec((tm, 1), lambda i: (i, 0)),
                  wfull, wfull, wfull, vec],
        out_specs=[tok, tok, pl.BlockSpec((tm, 2 * D), lambda i: (i, 0)), tok, tok, tok, vec],
        out_shape=[jax.ShapeDtypeStruct((T, D), BF), jax.ShapeDtypeStruct((T, D), BF),
                   jax.ShapeDtypeStruct((T, 2 * D), BF), jax.ShapeDtypeStruct((T, D), BF),
                   jax.ShapeDtypeStruct((T, D), F32), jax.ShapeDtypeStruct((T, D), F32),
                   jax.ShapeDtypeStruct((8, D), F32)],
        args=(dz1, p, p, p, ya, yb, conva, xhatb, rstdb, woa, wob, wo, vecs), push=push)


def _rows8(v):
    out = v[0:8]
    for q in range(1, RC // 8):
        out = out + v[8 * q:8 * q + 8]
    return out


def _conv_bwd(dconva, du1, p, dba, dg, wa, wb, push, tt=256):
    T = p.shape[0]
    nsteps = T // tt

    def body(dca_ref, dca_n, du1_ref, du1_n, ca, va, vb, gb, ca_p, va_p, vb_p, gb_p, dba_ref, dg_ref, wa_ref, wb_ref,
             dp_ref, gw_ref, cabuf, u0buf, dcabuf, du1buf, dcain, du0, gwa, gwb, shu, shd):
        i = pl.program_id(0)
        first, last = i == 0, i == nsteps - 1

        @pl.when(first)
        def _():
            gwa[...] = jnp.zeros_like(gwa)
            gwb[...] = jnp.zeros_like(gwb)

        f = lambda ref: ref[...].astype(F32)
        cav, vav, vbv = f(ca), f(va), f(vb)
        cabuf[0:HA, :] = jnp.where(first, 0.0, f(ca_p) * f(va_p))
        cabuf[HA:HA + tt, :] = cav * vav
        sg = _sigmoid(f(gb))
        u0buf[0:HB, :] = jnp.where(first, 0.0, f(vb_p) * _sigmoid(f(gb_p)))
        u0buf[HB:HB + tt, :] = vbv * sg
        dcabuf[0:tt, :] = dca_ref[...]
        dcabuf[tt:tt + HN, :] = jnp.where(last, 0.0, dca_n[...])
        du1buf[0:tt, :] = du1_ref[...]
        du1buf[tt:tt + HB, :] = jnp.where(last, 0.0, du1_n[...])

        def lane_body(cidx, carry):
            ls = pl.ds(pl.multiple_of(cidx * LANES, LANES), LANES)
            _shifted_copies(shu, u0buf, ls, tt + HB - 8)
            _shifted_copies(shd, du1buf, ls, tt + HB - 8)
            for r in range(tt // RC):
                dout = dcabuf[pl.ds(r * RC, RC), ls]
                acc = jnp.zeros((RC, LANES), F32)
                for k in range(KA):
                    acc = acc + wa_ref[k:k + 1, ls] * dcabuf[pl.ds(r * RC + KA - 1 - k, RC), ls]
                    gwa[8 * k:8 * k + 8, ls] += _rows8(dout * cabuf[pl.ds(HA - (KA - 1) + k + r * RC, RC), ls])
                dcain[pl.ds(r * RC, RC), ls] = acc
                dout = du1buf[pl.ds(r * RC, RC), ls]
                acc = jnp.zeros((RC, LANES), F32)
                for k in range(KB):
                    acc = acc + wb_ref[k:k + 1, ls] * _tap(shd, du1buf, ls, r * RC + KB - 1 - k, RC)
                    gwb[8 * k:8 * k + 8, ls] += _rows8(dout * _tap(shu, u0buf, ls, HB - (KB - 1) + k + r * RC, RC))
                du0[pl.ds(r * RC, RC), ls] = acc
            return carry

        lax.fori_loop(0, D // LANES, lane_body, 0)
        dca_in = dcain[...]
        du0v = du0[...]
        dp_ref[:, 0:D] = dba_ref[...]
        dp_ref[:, D:2 * D] = (dca_in * vav).astype(BF)
        dp_ref[:, 2 * D:3 * D] = (dca_in * cav).astype(BF)
        dp_ref[:, 3 * D:4 * D] = (du0v * sg).astype(BF)
        dp_ref[:, 4 * D:5 * D] = (du0v * vbv * (sg * (1.0 - sg))).astype(BF)
        dp_ref[:, 5 * D:7 * D] = dg_ref[...]

        @pl.when(last)
        def _():
            gw_ref[...] = jnp.zeros_like(gw_ref)
            for k in range(KA):
                gw_ref[k:k + 1, :] = _colsum(gwa[8 * k:8 * k + 8, :])
            for k in range(KB):
                gw_ref[8 + k:9 + k, :] = _colsum(gwb[8 * k:8 * k + 8, :])

    full = lambda r: pl.BlockSpec((r, D), lambda i: (0, 0))
    tok = pl.BlockSpec((tt, D), lambda i: (i, 0))
    nxt = lambda h: pl.BlockSpec((h, D), lambda i: (jnp.minimum((i + 1) * (tt // h), T // h - 1), 0))
    return _pallas(
        body, name="conv_bwd", grid=(nsteps,),
        in_specs=[tok, nxt(HN), tok, nxt(HB),
                  _seg(tt, 1), _seg(tt, 2), _seg(tt, 3), _seg(tt, 4),
                  _prev(tt, HA, 1), _prev(tt, HA, 2), _prev(tt, HB, 3), _prev(tt, HB, 4),
                  tok, pl.BlockSpec((tt, 2 * D), lambda i: (i, 0)), full(8), full(32)],
        out_specs=[pl.BlockSpec((tt, 7 * D), lambda i: (i, 0)), full(40)],
        out_shape=[jax.ShapeDtypeStruct((T, 7 * D), BF), jax.ShapeDtypeStruct((40, D), F32)],
        scratch_shapes=[pltpu.VMEM((HA + tt, D), F32), pltpu.VMEM((HB + tt, D), F32),
                        pltpu.VMEM((tt + HN, D), F32), pltpu.VMEM((tt + HB, D), F32),
                        pltpu.VMEM((tt, D), F32), pltpu.VMEM((tt, D), F32),
                        pltpu.VMEM((8 * KA, D), F32), pltpu.VMEM((8 * KB, D), F32),
                        pltpu.VMEM((8, HB + tt, LANES), F32), pltpu.VMEM((8, HB + tt, LANES), F32)],
        args=(dconva, dconva, du1, du1, p, p, p, p, p, p, p, p, dba, dg, wa, wb), push=push)


def _inproj_bwd(dp, win, dz1, push, tm=512):
    T, cols = dp.shape

    def body(dp_ref, w_ref, dz1_ref, o_ref):
        o_ref[...] = ALPHA * dz1_ref[...] + lax.dot_general(dp_ref[...], w_ref[...], NT_DIMS,
                                                            preferred_element_type=F32)

    tok = pl.BlockSpec((tm, D), lambda i: (i, 0))
    return _pallas(
        body, name="inproj_bwd", grid=(T // tm,),
        in_specs=[pl.BlockSpec((tm, cols), lambda i: (i, 0)), _resident((D, cols)), tok],
        out_specs=[tok],
        out_shape=[jax.ShapeDtypeStruct((T, D), F32)],
        args=(dp, win, dz1), push=push)


def _adam_math(w, m, v, g):
    nm = ADAM_B1 * m + (1.0 - ADAM_B1) * g
    nv = ADAM_B2 * v + (1.0 - ADAM_B2) * (g * g)
    m_hat = nm / (1.0 - ADAM_B1 ** ADAM_STEP)
    v_hat = nv / (1.0 - ADAM_B2 ** ADAM_STEP)
    return -ADAM_LR * (m_hat / (jnp.sqrt(v_hat) + ADAM_EPS) + ADAM_WD * w), nm, nv


def _adamw(w, m, v, g32, landing, me, name, rb):
    R, C = w.shape
    nl = landing.shape[0]

    def body(me_ref, w_ref, m_ref, v_ref, own_ref, l_ref, g_ref, d_ref, nm_ref, nv_ref):
        g = own_ref[0]
        for k in range(nl):
            g = g + l_ref[k].astype(F32)
        g_ref[...] = g
        d_ref[...], nm_ref[...], nv_ref[...] = _adam_math(w_ref[...], m_ref[...], v_ref[...], g)

    blk = pl.BlockSpec((rb, C), lambda i, me_ref: (i, 0))
    grid_spec = pltpu.PrefetchScalarGridSpec(
        num_scalar_prefetch=1, grid=(R // rb,),
        in_specs=[blk, blk, blk, pl.BlockSpec((1, rb, C), lambda i, me_ref: (me_ref[0], i, 0)),
                  pl.BlockSpec((nl, rb, C), lambda i, me_ref: (0, i, 0))],
        out_specs=[blk] * 4)
    return pl.pallas_call(
        body, name=name, grid_spec=grid_spec, out_shape=[jax.ShapeDtypeStruct((R, C), F32)] * 4,
        compiler_params=_params(1))(me, w, m, v, g32, landing)


def _adamw_small(small_g, vec_w, vec_m, vec_v, conv_w, conv_m, conv_v):
    nv_ = len(vec_w)
    conv_rows = [(8, KA), (16, KB)]

    def body(*refs):
        g_ref = refs[0]
        w_refs, m_refs, v_refs = refs[1:10], refs[10:19], refs[19:28]
        out_refs, gsum = refs[28:64], refs[64]
        acc = g_ref[0]
        for j in range(1, NDEV):
            acc = acc + g_ref[j]
        gsum[...] = acc
        me = 4 * lax.axis_index("x") + 2 * lax.axis_index("y") + lax.axis_index("c")
        cols = pl.ds(pl.multiple_of(me * LANES, LANES), LANES)
        for i in range(nv_ + 2):
            if i < nv_:
                g = gsum[i:i + 1, :]
            else:
                r0, k = conv_rows[i - nv_]
                g = gsum[r0:r0 + k, cols]
            o = out_refs[4 * i:4 * i + 4]
            o[0][...] = g
            o[1][...], o[2][...], o[3][...] = _adam_math(w_refs[i][...], m_refs[i][...], v_refs[i][...], g)

    ws, ms, vs = list(vec_w) + list(conv_w), list(vec_m) + list(conv_m), list(vec_v) + list(conv_v)
    out_shape = [jax.ShapeDtypeStruct(w.shape, F32) for w in ws for _ in range(4)]
    return pl.pallas_call(
        body, name="adamw_small", out_shape=out_shape,
        scratch_shapes=[pltpu.VMEM(small_g.shape[1:], F32)])(small_g, *ws, *ms, *vs)


def _pad_rows(a, rows):
    return jnp.pad(a, ((0, rows - a.shape[0]), (0, 0)))


def _local_step(p, xb, x, target, win, wup, wdown, woa, wob, wo, wa, wb, vecs):
    yapre, conva, xhatb, rstdb, u3 = _mixer_fwd(p, wa, wb, vecs)
    ya, yb, merged, xhat1, rstd1, x1b = _post_mixer(yapre, u3, p, x, woa, wob, wo, vecs)
    r, h = _mlp_up(x1b, wup)
    dz2, dz2b, st2 = _mlp_down_loss(h, wdown, xhat1, target, vecs)

    by_owner = lambda g16: g16.reshape(NDEV, D // NDEV, D)
    dhpre = _mlp_down_bwd(dz2b, wdown, r)
    g_wdown = _tn_matmul(h, dz2b, NDEV, 512, D, True, False, "grad_w_down")
    (dz1, dz1b, st1), land_wdown = _mlp_up_bwd(dhpre, wup, dz2, xhat1, rstd1, vecs, _push(exch=[g_wdown[1]]))
    g_wup = _tn_matmul(x1b, dhpre, NDEV, D, 512, False, True, "grad_w_up")
    (dya, dyb, dg, dba, dconva, du1, stb), land_wup = _merge_bwd(
        dz1b, p, ya, yb, conva, xhatb, rstdb, woa, wob, wo, vecs, _push(exch=[g_wup[1]]))
    g_wo = _tn_matmul(merged, dz1b, 1, D, D, False, False, "grad_w_o")
    g_woa = _tn_matmul(yapre, dya, 1, D, D, False, False, "grad_w_out_a")
    g_wob = _tn_matmul(u3, dyb, 1, D, D, False, False, "grad_w_out_b")
    (dp, gw), land_sq = _conv_bwd(dconva, du1, p, dba, dg, wa, wb,
                                  _push(exch=[by_owner(g_woa[1]), by_owner(g_wob[1]), by_owner(g_wo[1])]))
    g_win = _tn_matmul(xb, dp, NDEV, D, 896, False, True, "grad_w_in")

    small = jnp.concatenate([stb[2:3], stb[0:2], st1[0:2], st2[0:3], gw], axis=0)
    (grad_x,), land_last = _inproj_bwd(dp, win, dz1, _push(exch=[g_win[1]], gath=[small]))
    grads = (g_win[0], g_wup[0], g_wdown[0], g_woa[0], g_wob[0], g_wo[0])
    return grad_x, grads, small, land_wdown + land_wup + land_sq + land_last


def kernel(x, w_in, conv_a_w, w_out_a, conv_b_w, conv_b_bias, ln_b_gamma, ln_b_beta, w_out_b, w_o, ln1_gamma, ln1_beta, w_up, w_down, ln2_gamma, ln2_beta, loss_target, m_w_in, m_conv_a_w, m_w_out_a, m_conv_b_w, m_conv_b_bias, m_ln_b_gamma, m_ln_b_beta, m_w_out_b, m_w_o, m_ln1_gamma, m_ln1_beta, m_w_up, m_w_down, m_ln2_gamma, m_ln2_beta, v_w_in, v_conv_a_w, v_w_out_a, v_conv_b_w, v_conv_b_bias, v_ln_b_gamma, v_ln_b_beta, v_w_out_b, v_w_o, v_ln1_gamma, v_ln1_beta, v_w_up, v_w_down, v_ln2_gamma, v_ln2_beta):
    T = x.shape[1]
    me = 4 * lax.axis_index("x") + 2 * lax.axis_index("y") + lax.axis_index("c")

    conv_shard = jnp.concatenate([_pad_rows(conv_a_w, 8), _pad_rows(conv_b_w, 32)], axis=0)
    win_g, conv_g = _allgather([w_in.astype(BF), conv_shard], "allgather_w_in")
    conv_full = jnp.transpose(conv_g, (1, 0, 2)).reshape(40, D)
    vecs = jnp.stack([conv_b_bias, ln_b_gamma, ln_b_beta, ln1_gamma, ln1_beta, ln2_gamma, ln2_beta,
                      jnp.zeros_like(ln2_beta)])
    whole = lambda g: jnp.transpose(g, (1, 0, 2)).reshape(D, -1)
    win = whole(win_g)
    p, xb, (wup_g, wdown_g, woa_g, wob_g, wo_g) = _inproj(
        x[0], win, [w_up.astype(BF), w_down.astype(BF), w_out_a.astype(BF), w_out_b.astype(BF), w_o.astype(BF)])

    grad_x, grads, small, landing = _local_step(
        p, xb, x[0], loss_target[0], win, whole(wup_g), wdown_g.reshape(NDEV * 512, D), woa_g.reshape(D, D),
        wob_g.reshape(D, D), wo_g.reshape(D, D), conv_full[0:8], conv_full[8:40], vecs)
    g_win, g_wup, g_wdown, g_woa, g_wob, g_wo = grads
    l_wdown, l_wup, l_woa, l_wob, l_wo, l_win, small_g = landing

    loss = lax.psum(0.5 / D * jnp.sum(small[7]), ("x", "y", "c"))

    me1 = me.astype(jnp.int32).reshape(1)
    by_owner = lambda g32: g32.reshape(NDEV, D // NDEV, D)
    r_win = _adamw(w_in, m_w_in, v_w_in, g_win, l_win, me1, "adamw_w_in", 256)
    r_wup = _adamw(w_up, m_w_up, v_w_up, g_wup, l_wup, me1, "adamw_w_up", 256)
    r_wdown = _adamw(w_down, m_w_down, v_w_down, g_wdown, l_wdown, me1, "adamw_w_down", 256)
    r_woa = _adamw(w_out_a, m_w_out_a, v_w_out_a, by_owner(g_woa), l_woa, me1, "adamw_w_out_a", 128)
    r_wob = _adamw(w_out_b, m_w_out_b, v_w_out_b, by_owner(g_wob), l_wob, me1, "adamw_w_out_b", 128)
    r_wo = _adamw(w_o, m_w_o, v_w_o, by_owner(g_wo), l_wo, me1, "adamw_w_o", 128)

    row = lambda vec: vec.reshape(1, D)
    small_out = _adamw_small(
        small_g,
        [row(a) for a in (conv_b_bias, ln_b_gamma, ln_b_beta, ln1_gamma, ln1_beta, ln2_gamma, ln2_beta)],
        [row(a) for a in (m_conv_b_bias, m_ln_b_gamma, m_ln_b_beta, m_ln1_gamma, m_ln1_beta, m_ln2_gamma, m_ln2_beta)],
        [row(a) for a in (v_conv_b_bias, v_ln_b_gamma, v_ln_b_beta, v_ln1_gamma, v_ln1_beta, v_ln2_gamma, v_ln2_beta)],
        [conv_a_w, conv_b_w], [m_conv_a_w, m_conv_b_w], [v_conv_a_w, v_conv_b_w])
    r_vec = [[small_out[4 * i + q].reshape(D) for q in range(4)] for i in range(7)]
    r_conva, r_convb = small_out[28:32], small_out[32:36]

    per_weight = []
    for q in range(4):
        per_weight.append([
            r_win[q], r_conva[q], r_woa[q], r_convb[q],
            r_vec[0][q], r_vec[1][q], r_vec[2][q], r_wob[q], r_wo[q], r_vec[3][q], r_vec[4][q],
            r_wup[q], r_wdown[q], r_vec[5][q], r_vec[6][q]])
    return (loss, grad_x[None], *per_weight[0], *per_weight[1], *per_weight[2], *per_weight[3])
```

```python
import functools

import jax
import jax.numpy as jnp
from jax import lax
from jax.experimental import pallas as pl
from jax.experimental.pallas import tpu as pltpu

F32 = jnp.float32
BF = jnp.bfloat16
D = 1024
NDEV = 8
ALPHA = 2.0 ** 0.25
LN_EPS = 1e-5
KA, KB = 3, 31
HA, HB = 16, 32
HN = 8
RC = 64
LANES = 128
VMEM_LIMIT = 56 * 1024 * 1024
MESH = pl.DeviceIdType.MESH
ADAM_LR, ADAM_B1, ADAM_B2, ADAM_EPS, ADAM_WD, ADAM_STEP = 0.001, 0.9, 0.999, 1e-08, 0.01, 10

ANY_SPEC = pl.BlockSpec(memory_space=pl.ANY)
NT_DIMS = (((1,), (1,)), ((), ()))
TN_DIMS = (((0,), (0,)), ((), ()))


def _params(n_axes):
    return pltpu.CompilerParams(dimension_semantics=("arbitrary",) * n_axes, vmem_limit_bytes=VMEM_LIMIT)


def _sigmoid(v):
    return 0.5 * jnp.tanh(0.5 * v) + 0.5


def _ln_fwd(z):
    mu = jnp.mean(z, axis=-1, keepdims=True)
    zc = z - mu
    var = jnp.mean(zc * zc, axis=-1, keepdims=True)
    rstd = lax.rsqrt(var + LN_EPS)
    return zc * rstd, rstd


def _ln_bwd(dy, xhat, rstd, gamma):
    dxhat = dy * gamma
    m1 = jnp.mean(dxhat, axis=-1, keepdims=True)
    m2 = jnp.mean(dxhat * xhat, axis=-1, keepdims=True)
    return rstd * (dxhat - m1 - xhat * m2)


def _colsum(v):
    return jnp.sum(v, axis=0, keepdims=True)


def _allgather(arrs, name):
    n = len(arrs)

    def body(*refs):
        g = _TwoLevelGather(refs[:n], refs[n:2 * n], *refs[2 * n:])
        g.start()
        g.forward()
        g.finish()

    return pl.pallas_call(
        body, name=name,
        in_specs=[ANY_SPEC] * n, out_specs=[ANY_SPEC] * n,
        out_shape=_TwoLevelGather.out_shape(arrs), scratch_shapes=_TwoLevelGather.scratch(n),
    )(*arrs)


class _TwoLevelGather:
    def __init__(self, ins, outs, send_sems, recv_sems, local_sems):
        self.ins, self.outs = ins, outs
        self.send_sems, self.recv_sems, self.local_sems = send_sems, recv_sems, local_sems
        x, y, c = lax.axis_index("x"), lax.axis_index("y"), lax.axis_index("c")
        self.me, self.sibling, self.c = (x, y, c), (x, y, 1 - c), c
        self.chips = [(1 - x, y), (x, 1 - y), (1 - x, 1 - y)]
        self.n = len(ins)

    @staticmethod
    def out_shape(arrs):
        return [jax.ShapeDtypeStruct((NDEV,) + a.shape, a.dtype) for a in arrs]

    @staticmethod
    def scratch(n):
        return [pltpu.SemaphoreType.DMA((n, 7)), pltpu.SemaphoreType.DMA((n, 7)), pltpu.SemaphoreType.DMA((n,))]

    def _copy(self, a, k, block, to, src=None):
        px, py, pc = block
        rows = self.outs[a].at[4 * px + 2 * py + pc]
        return pltpu.make_async_remote_copy(
            src_ref=rows if src is None else src, dst_ref=rows,
            send_sem=self.send_sems.at[a, k], recv_sem=self.recv_sems.at[a, k],
            device_id=to, device_id_type=MESH)

    def _mine(self, a):
        x, y, c = self.me
        return pltpu.make_async_copy(self.ins[a], self.outs[a].at[4 * x + 2 * y + c], self.local_sems.at[a])

    def _first(self, a):
        cps = [self._copy(a, 0, self.me, self.sibling, src=self.ins[a])]
        return cps + [self._copy(a, 1 + j, self.me, (*chip, self.c), src=self.ins[a]) for j, chip in enumerate(self.chips)]

    def _passed(self, a, j):
        return self._copy(a, 4 + j, (*self.chips[j], self.c), self.sibling)

    def start(self):
        for a in range(self.n):
            self._mine(a).start()
        for a in range(self.n):
            for cp in self._first(a):
                cp.start()

    def forward(self):
        for j, chip in enumerate(self.chips):
            for a in range(self.n):
                self._copy(a, 1 + j, (*chip, self.c), self.me).wait_recv()
                self._passed(a, j).start()

    def finish(self):
        for a in range(self.n):
            self._copy(a, 0, self.sibling, self.me).wait_recv()
            for j, chip in enumerate(self.chips):
                self._copy(a, 4 + j, (*chip, 1 - self.c), self.me).wait_recv()
        for a in range(self.n):
            for cp in self._first(a) + [self._passed(a, j) for j in range(3)]:
                cp.wait_send()
            self._mine(a).wait()


class _Push:
    def __init__(self, exch=(), gath=()):
        self.exch, self.gath = list(exch), list(gath)
        self.n = len(self.exch) + len(self.gath)

    def operands(self):
        return self.exch + self.gath

    def out_shape(self):
        return ([jax.ShapeDtypeStruct((NDEV - 1,) + a.shape[1:], a.dtype) for a in self.exch]
                + [jax.ShapeDtypeStruct((NDEV,) + a.shape, a.dtype) for a in self.gath])

    def scratch(self):
        return [pltpu.SemaphoreType.DMA((self.n, 7)), pltpu.SemaphoreType.DMA((self.n, 7)),
                pltpu.SemaphoreType.DMA((max(len(self.gath), 1),))]

    def copies(self, ins, outs, send_sems, recv_sems, local_sems):
        x, y, c = lax.axis_index("x"), lax.axis_index("y"), lax.axis_index("c")
        me = 4 * x + 2 * y + c
        ne = len(self.exch)
        remote = []
        for k in range(1, NDEV):
            px = 1 - x if k & 4 else x
            py = 1 - y if k & 2 else y
            pc = 1 - c if k & 1 else c
            for a in range(self.n):
                src = ins[a].at[4 * px + 2 * py + pc] if a < ne else ins[a]
                dst = outs[a].at[k - 1] if a < ne else outs[a].at[me]
                remote.append(pltpu.make_async_remote_copy(
                    src_ref=src, dst_ref=dst, send_sem=send_sems.at[a, k - 1], recv_sem=recv_sems.at[a, k - 1],
                    device_id=(px, py, pc), device_id_type=MESH))
        local = [pltpu.make_async_copy(ins[a], outs[a].at[me], local_sems.at[a - ne]) for a in range(ne, self.n)]
        return remote, local


def _push(exch=(), gath=()):
    return _Push(exch, gath)


def _pallas(body, *, name, grid, in_specs, out_specs, out_shape, args, scratch_shapes=(), push=None):
    ni, no, ns = len(in_specs), len(out_specs), len(scratch_shapes)
    if push is None:
        outs = pl.pallas_call(
            body, name=name, grid=grid, in_specs=in_specs, out_specs=out_specs, out_shape=out_shape,
            scratch_shapes=list(scratch_shapes), compiler_params=_params(len(grid)))(*args)
        return list(outs), []
    npush = push.n

    def wrapped(*refs):
        ins, pins = refs[:ni], refs[ni:ni + npush]
        outs, pouts = refs[ni + npush:ni + npush + no], refs[ni + npush + no:ni + 2 * npush + no]
        scr, sems = refs[ni + 2 * npush + no:ni + 2 * npush + no + ns], refs[ni + 2 * npush + no + ns:]
        first = functools.reduce(jnp.logical_and, [pl.program_id(d) == 0 for d in range(len(grid))])
        last = functools.reduce(jnp.logical_and, [pl.program_id(d) == grid[d] - 1 for d in range(len(grid))])
        remote, local = push.copies(pins, pouts, *sems)

        @pl.when(first)
        def _():
            for cp in local + remote:
                cp.start()

        body(*ins, *outs, *scr)

        @pl.when(last)
        def _():
            for cp in remote + local:
                cp.wait()

    outs = pl.pallas_call(
        wrapped, name=name, grid=grid,
        in_specs=list(in_specs) + [ANY_SPEC] * npush, out_specs=list(out_specs) + [ANY_SPEC] * npush,
        out_shape=list(out_shape) + push.out_shape(), scratch_shapes=list(scratch_shapes) + push.scratch(),
        compiler_params=_params(len(grid)))(*args, *push.operands())
    return list(outs[:no]), list(outs[no:])


def _inproj(x, win, late, tm=512, bw=1024):
    T = x.shape[0]
    cols = win.shape[1]
    n = len(late)
    ni = T // tm

    def body(*refs):
        x_ref, w_ref = refs[:2]
        o_ref, xb_ref = refs[2 + n:4 + n]
        gather = _TwoLevelGather(refs[2:2 + n], refs[4 + n:4 + 2 * n], *refs[4 + 2 * n:])
        i = pl.program_id(0)

        @pl.when(i == 0)
        def _():
            gather.start()

        xb = x_ref[...].astype(BF)
        xb_ref[...] = xb
        for j in range(cols // bw):
            cs = slice(j * bw, (j + 1) * bw)
            o_ref[:, cs] = jnp.dot(xb, w_ref[:, cs], preferred_element_type=F32).astype(BF)

        @pl.when(i == (3 * ni) // 4)
        def _():
            gather.forward()

        @pl.when(i == ni - 1)
        def _():
            gather.finish()

    outs = pl.pallas_call(
        body, name="inproj", grid=(ni,),
        in_specs=[pl.BlockSpec((tm, D), lambda i: (i, 0)), _resident((D, cols))] + [ANY_SPEC] * n,
        out_specs=[pl.BlockSpec((tm, cols), lambda i: (i, 0)), pl.BlockSpec((tm, D), lambda i: (i, 0))]
        + [ANY_SPEC] * n,
        out_shape=[jax.ShapeDtypeStruct((T, cols), BF), jax.ShapeDtypeStruct((T, D), BF)]
        + _TwoLevelGather.out_shape(late),
        scratch_shapes=_TwoLevelGather.scratch(n),
        compiler_params=_params(1))(x, win, *late)
    return outs[0], outs[1], list(outs[2:])


def _mixer_fwd(p, wa, wb, vecs, tt=256):
    T = p.shape[0]

    def body(ba, ca, va, vb, gb, ca_p, va_p, vb_p, gb_p, wa_ref, wb_ref, vec_ref,
             yapre_ref, conva_ref, xhat_ref, rstd_ref, u3_ref, cabuf, u0buf, u1buf, shu):
        first = pl.program_id(0) == 0
        f = lambda ref: ref[...].astype(F32)
        cabuf[0:HA, :] = jnp.where(first, 0.0, f(ca_p) * f(va_p))
        cabuf[HA:HA + tt, :] = f(ca) * f(va)
        u0buf[0:HB, :] = jnp.where(first, 0.0, f(vb_p) * _sigmoid(f(gb_p)))
        u0buf[HB:HB + tt, :] = f(vb) * _sigmoid(f(gb))

        def lane_body(cidx, carry):
            ls = pl.ds(pl.multiple_of(cidx * LANES, LANES), LANES)
            _shifted_copies(shu, u0buf, ls, tt + HB - 8)
            for r in range(tt // RC):
                acc = jnp.zeros((RC, LANES), F32)
                for k in range(KA):
                    acc = acc + wa_ref[k:k + 1, ls] * cabuf[pl.ds(HA - (KA - 1) + k + r * RC, RC), ls]
                conva_ref[pl.ds(r * RC, RC), ls] = acc
                acc = jnp.zeros((RC, LANES), F32)
                for k in range(KB):
                    acc = acc + wb_ref[k:k + 1, ls] * _tap(shu, u0buf, ls, HB - (KB - 1) + k + r * RC, RC)
                u1buf[pl.ds(r * RC, RC), ls] = acc
            return carry

        lax.fori_loop(0, D // LANES, lane_body, 0)
        yapre_ref[...] = (f(ba) * conva_ref[...]).astype(BF)
        xhat, rstd = _ln_fwd(u1buf[...] + vec_ref[0:1, :])
        xhat_ref[...] = xhat
        rstd_ref[...] = rstd
        u2 = xhat * vec_ref[1:2, :] + vec_ref[2:3, :]
        u3_ref[...] = (u2 * _sigmoid(u2)).astype(BF)

    full = lambda r: pl.BlockSpec((r, D), lambda i: (0, 0))
    tok = pl.BlockSpec((tt, D), lambda i: (i, 0))
    return pl.pallas_call(
        body, name="mixer_fwd", grid=(T // tt,),
        in_specs=[_seg(tt, 0), _seg(tt, 1), _seg(tt, 2), _seg(tt, 3), _seg(tt, 4),
                  _prev(tt, HA, 1), _prev(tt, HA, 2), _prev(tt, HB, 3), _prev(tt, HB, 4),
                  full(8), full(32), full(8)],
        out_specs=[tok, tok, tok, pl.BlockSpec((tt, 1), lambda i: (i, 0)), tok],
        out_shape=[jax.ShapeDtypeStruct((T, D), BF), jax.ShapeDtypeStruct((T, D), F32),
                   jax.ShapeDtypeStruct((T, D), F32), jax.ShapeDtypeStruct((T, 1), F32),
                   jax.ShapeDtypeStruct((T, D), BF)],
        scratch_shapes=[pltpu.VMEM((HA + tt, D), F32), pltpu.VMEM((HB + tt, D), F32), pltpu.VMEM((tt, D), F32),
                        pltpu.VMEM((8, HB + tt, LANES), F32)],
        compiler_params=_params(1))(p, p, p, p, p, p, p, p, p, wa, wb, vecs)


def _seg(tt, s):
    return pl.BlockSpec((tt, D), lambda i: (i, s))


def _prev(tt, h, s):
    return pl.BlockSpec((h, D), lambda i: (jnp.maximum(i * (tt // h) - 1, 0), s))


def _shifted_copies(shbuf, src, ls, n):
    for s in range(1, 8):
        shbuf[s, 0:n, :] = src[pl.ds(s, n), ls]


def _tap(shbuf, src, ls, off, rows):
    s, q = off % 8, off // 8
    if s == 0:
        return src[pl.ds(off, rows), ls]
    return shbuf[s, pl.ds(8 * q, rows), :]


def _post_mixer(yapre, u3, p, x, woa, wob, wo, vecs, tm=256):
    T = x.shape[0]

    def body(yapre_ref, u3_ref, ga_ref, gb_ref, x_ref, woa_ref, wob_ref, wo_ref, vec_ref,
             ya_ref, yb_ref, merged_ref, xhat_ref, rstd_ref, x1b_ref):
        ya = jnp.dot(yapre_ref[...], woa_ref[...], preferred_element_type=F32)
        yb = jnp.dot(u3_ref[...], wob_ref[...], preferred_element_type=F32)
        ya_ref[...] = ya.astype(BF)
        yb_ref[...] = yb.astype(BF)
        merged = (_sigmoid(ga_ref[...].astype(F32)) * ya + _sigmoid(gb_ref[...].astype(F32)) * yb).astype(BF)
        merged_ref[...] = merged
        mix = jnp.dot(merged, wo_ref[...], preferred_element_type=F32)
        xhat, rstd = _ln_fwd(ALPHA * x_ref[...] + mix)
        xhat_ref[...] = xhat
        rstd_ref[...] = rstd
        x1b_ref[...] = (xhat * vec_ref[3:4, :] + vec_ref[4:5, :]).astype(BF)

    tok = pl.BlockSpec((tm, D), lambda i: (i, 0))
    wfull = pl.BlockSpec((D, D), lambda i: (0, 0))
    one = pl.BlockSpec((tm, 1), lambda i: (i, 0))
    return pl.pallas_call(
        body, name="post_mixer", grid=(T // tm,),
        in_specs=[tok, tok, _seg(tm, 5), _seg(tm, 6), tok, wfull, wfull, wfull, pl.BlockSpec((8, D), lambda i: (0, 0))],
        out_specs=[tok, tok, tok, tok, one, tok],
        out_shape=[jax.ShapeDtypeStruct((T, D), BF), jax.ShapeDtypeStruct((T, D), BF),
                   jax.ShapeDtypeStruct((T, D), BF), jax.ShapeDtypeStruct((T, D), F32),
                   jax.ShapeDtypeStruct((T, 1), F32), jax.ShapeDtypeStruct((T, D), BF)],
        compiler_params=_params(1))(yapre, u3, p, p, x, woa, wob, wo, vecs)


def _mlp_up(x1b, wup, tm=512, tn=2048):
    T = x1b.shape[0]
    dff = wup.shape[1]

    def body(x_ref, w_ref, r_ref, h_ref):
        r = jnp.maximum(jnp.dot(x_ref[...], w_ref[...], preferred_element_type=F32), 0.0)
        r_ref[...] = r.astype(BF)
        h_ref[...] = (r * r).astype(BF)

    out = pl.BlockSpec((tm, tn), lambda j, i: (i, j))
    return pl.pallas_call(
        body, name="mlp_up", grid=(dff // tn, T // tm),
        in_specs=[pl.BlockSpec((tm, D), lambda j, i: (i, 0)), pl.BlockSpec((D, tn), lambda j, i: (0, j))],
        out_specs=[out, out],
        out_shape=[jax.ShapeDtypeStruct((T, dff), BF), jax.ShapeDtypeStruct((T, dff), BF)],
        compiler_params=_params(2))(x1b, wup)


def _resident(shape):
    return pl.BlockSpec(shape, lambda *_: (0,) * len(shape), pipeline_mode=pl.Buffered(1))


def _mlp_down_loss(h, wdown, xhat1, target, vecs, tm=512):
    T, dff = h.shape

    def body(h_ref, w_ref, xhat1_ref, tgt_ref, vec_ref, dz2_ref, dz2b_ref, st_ref):
        @pl.when(pl.program_id(0) == 0)
        def _():
            st_ref[...] = jnp.zeros_like(st_ref)

        ff = jnp.dot(h_ref[...], w_ref[...], preferred_element_type=F32)
        x1 = xhat1_ref[...] * vec_ref[3:4, :] + vec_ref[4:5, :]
        xhat2, rstd2 = _ln_fwd(ALPHA * x1 + ff)
        g2 = vec_ref[5:6, :]
        diff = xhat2 * g2 + vec_ref[6:7, :] - tgt_ref[...]
        dx2 = diff * (1.0 / D)
        st_ref[0:1, :] += _colsum(dx2 * xhat2)
        st_ref[1:2, :] += _colsum(dx2)
        st_ref[2:3, :] += _colsum(diff * diff)
        dz2 = _ln_bwd(dx2, xhat2, rstd2, g2)
        dz2_ref[...] = dz2
        dz2b_ref[...] = dz2.astype(BF)

    tok = pl.BlockSpec((tm, D), lambda i: (i, 0))
    vec = pl.BlockSpec((8, D), lambda i: (0, 0))
    return pl.pallas_call(
        body, name="mlp_down_loss", grid=(T // tm,),
        in_specs=[pl.BlockSpec((tm, dff), lambda i: (i, 0)), _resident((dff, D)), tok, tok, vec],
        out_specs=[tok, tok, vec],
        out_shape=[jax.ShapeDtypeStruct((T, D), F32), jax.ShapeDtypeStruct((T, D), BF), jax.ShapeDtypeStruct((8, D), F32)],
        compiler_params=_params(1))(h, wdown, xhat1, target, vecs)


def _mlp_down_bwd(dz2b, wdown, r, tm=512, tk=2048):
    T, dff = r.shape

    def body(dz_ref, w_ref, r_ref, o_ref):
        dh = lax.dot_general(dz_ref[...], w_ref[...], NT_DIMS, preferred_element_type=F32)
        o_ref[...] = (dh * (2.0 * r_ref[...].astype(F32))).astype(BF)

    blk = pl.BlockSpec((tm, tk), lambda j, i: (i, j))
    return pl.pallas_call(
        body, name="mlp_down_bwd", grid=(dff // tk, T // tm),
        in_specs=[pl.BlockSpec((tm, D), lambda j, i: (i, 0)), pl.BlockSpec((tk, D), lambda j, i: (j, 0)), blk],
        out_specs=blk,
        out_shape=jax.ShapeDtypeStruct((T, dff), BF),
        compiler_params=_params(2))(dz2b, wdown, r)


def _tn_matmul(a, b, nblk, a_bw, b_bw, a_blocked, b_blocked, name, tt=2048):
    T = a.shape[0]
    nt = T // tt

    def body(a_ref, b_ref, o32_ref, o16_ref):
        t = pl.program_id(1)

        @pl.when(t == 0)
        def _():
            o32_ref[...] = jnp.zeros_like(o32_ref)

        o32_ref[0] += lax.dot_general(a_ref[...], b_ref[...], TN_DIMS, preferred_element_type=F32)

        @pl.when(t == nt - 1)
        def _():
            o16_ref[...] = o32_ref[...].astype(BF)

    a_spec = pl.BlockSpec((tt, a_bw), (lambda j, t: (t, j)) if a_blocked else (lambda j, t: (t, 0)))
    b_spec = pl.BlockSpec((tt, b_bw), (lambda j, t: (t, j)) if b_blocked else (lambda j, t: (t, 0)))
    out = pl.BlockSpec((1, a_bw, b_bw), lambda j, t: (j, 0, 0))
    return pl.pallas_call(
        body, name=name, grid=(nblk, nt),
        in_specs=[a_spec, b_spec], out_specs=[out, out],
        out_shape=[jax.ShapeDtypeStruct((nblk, a_bw, b_bw), F32), jax.ShapeDtypeStruct((nblk, a_bw, b_bw), BF)],
        compiler_params=_params(2))(a, b)


def _mlp_up_bwd(dhpre, wup, dz2, xhat1, rstd1, vecs, push, tm=512):
    T, dff = dhpre.shape

    def body(dh_ref, w_ref, dz2_ref, xhat_ref, rstd_ref, vec_ref, dz1_ref, dz1b_ref, st_ref):
        @pl.when(pl.program_id(0) == 0)
        def _():
            st_ref[...] = jnp.zeros_like(st_ref)

        dx1 = lax.dot_general(dh_ref[...], w_ref[...], NT_DIMS, preferred_element_type=F32) + ALPHA * dz2_ref[...]
        xhat = xhat_ref[...]
        st_ref[0:1, :] += _colsum(dx1 * xhat)
        st_ref[1:2, :] += _colsum(dx1)
        dz1 = _ln_bwd(dx1, xhat, rstd_ref[...], vec_ref[3:4, :])
        dz1_ref[...] = dz1
        dz1b_ref[...] = dz1.astype(BF)

    tok = pl.BlockSpec((tm, D), lambda i: (i, 0))
    vec = pl.BlockSpec((8, D), lambda i: (0, 0))
    return _pallas(
        body, name="mlp_up_bwd", grid=(T // tm,),
        in_specs=[pl.BlockSpec((tm, dff), lambda i: (i, 0)), _resident((D, dff)),
                  tok, tok, pl.BlockSpec((tm, 1), lambda i: (i, 0)), vec],
        out_specs=[tok, tok, vec],
        out_shape=[jax.ShapeDtypeStruct((T, D), F32), jax.ShapeDtypeStruct((T, D), BF), jax.ShapeDtypeStruct((8, D), F32)],
        args=(dhpre, wup, dz2, xhat1, rstd1, vecs), push=push)


def _merge_bwd(dz1, p, ya, yb, conva, xhatb, rstdb, woa, wob, wo, vecs, push, tm=256):
    T = dz1.shape[0]

    def body(dz1_ref, ga_ref, gb_ref, ba_ref, ya_ref, yb_ref, conva_ref, xhat_ref, rstd_ref,
             woa_ref, wob_ref, wo_ref, vec_ref,
             dya_ref, dyb_ref, dg_ref, dba_ref, dconva_ref, du1_ref, st_ref):
        @pl.when(pl.program_id(0) == 0)
        def _():
            st_ref[...] = jnp.zeros_like(st_ref)

        dmerged = lax.dot_general(dz1_ref[...], wo_ref[...], NT_DIMS, preferred_element_type=F32)
        sa, sb = _sigmoid(ga_ref[...].astype(F32)), _sigmoid(gb_ref[...].astype(F32))
        dya = (dmerged * sa).astype(BF)
        dyb = (dmerged * sb).astype(BF)
        dya_ref[...] = dya
        dyb_ref[...] = dyb
        dg_ref[:, 0:D] = (dmerged * ya_ref[...].astype(F32) * (sa * (1.0 - sa))).astype(BF)
        dg_ref[:, D:2 * D] = (dmerged * yb_ref[...].astype(F32) * (sb * (1.0 - sb))).astype(BF)

        dyapre = lax.dot_general(dya, woa_ref[...], NT_DIMS, preferred_element_type=F32)
        dba_ref[...] = (dyapre * conva_ref[...]).astype(BF)
        dconva_ref[...] = dyapre * ba_ref[...].astype(F32)

        du3 = lax.dot_general(dyb, wob_ref[...], NT_DIMS, preferred_element_type=F32)
        xhat = xhat_ref[...]
        gamma = vec_ref[1:2, :]
        u2 = xhat * gamma + vec_ref[2:3, :]
        s = _sigmoid(u2)
        du2 = du3 * (s * (1.0 + u2 * (1.0 - s)))
        st_ref[0:1, :] += _colsum(du2 * xhat)
        st_ref[1:2, :] += _colsum(du2)
        du1 = _ln_bwd(du2, xhat, rstd_ref[...], gamma)
        st_ref[2:3, :] += _colsum(du1)
        du1_ref[...] = du1

    tok = pl.BlockSpec((tm, D), lambda i: (i, 0))
    wfull = pl.BlockSpec((D, D), lambda i: (0, 0))
    vec = pl.BlockSpec((8, D), lambda i: (0, 0))
    return _pallas(
        body, name="merge_bwd", grid=(T // tm,),
        in_specs=[tok, _seg(tm, 5), _seg(tm, 6), _seg(tm, 0), tok, tok, tok, tok, pl.BlockSpec((tm, 1), lambda i: (i, 0)),
                  wfull, wfull, wfull, vec],
        out_specs=[tok, tok, pl.BlockSpec((tm, 2 * D), lambda i: (i, 0)), tok, tok, tok, vec],
        out_shape=[jax.ShapeDtypeStruct((T, D), BF), jax.ShapeDtypeStruct((T, D), BF),
                   jax.ShapeDtypeStruct((T, 2 * D), BF), jax.ShapeDtypeStruct((T, D), BF),
                   jax.ShapeDtypeStruct((T, D), F32), jax.ShapeDtypeStruct((T, D), F32),
                   jax.ShapeDtypeStruct((8, D), F32)],
        args=(dz1, p, p, p, ya, yb, conva, xhatb, rstdb, woa, wob, wo, vecs), push=push)


def _rows8(v):
    out = v[0:8]
    for q in range(1, RC // 8):
        out = out + v[8 * q:8 * q + 8]
    return out


def _conv_bwd(dconva, du1, p, dba, dg, wa, wb, push, tt=256):
    T = p.shape[0]
    nsteps = T // tt

    def body(dca_ref, dca_n, du1_ref, du1_n, ca, va, vb, gb, dba_ref, dg_ref, wa_ref, wb_ref,
             dp_ref, gw_ref, cabuf, u0buf, dcabuf, du1buf, dcain, du0, gwa, gwb, shd):
        i = pl.program_id(0)
        first, last = i == 0, i == nsteps - 1

        @pl.when(first)
        def _():
            gwa[...] = jnp.zeros_like(gwa)
            gwb[...] = jnp.zeros_like(gwb)

        f = lambda ref: ref[...].astype(F32)
        cav, vav, vbv = f(ca), f(va), f(vb)
        cabuf[...] = cav * vav
        sg = _sigmoid(f(gb))
        u0buf[...] = vbv * sg
        dcabuf[0:tt, :] = dca_ref[...]
        dcabuf[tt:tt + HN, :] = jnp.where(last, 0.0, dca_n[...])
        du1buf[0:tt, :] = du1_ref[...]
        du1buf[tt:tt + HB, :] = jnp.where(last, 0.0, du1_n[...])

        def lane_body(cidx, carry):
            ls = pl.ds(pl.multiple_of(cidx * LANES, LANES), LANES)
            _shifted_copies(shd, du1buf, ls, tt + HB - 8)
            for r in range(tt // RC):
                rows = pl.ds(r * RC, RC)
                cin = cabuf[rows, ls]
                acc = jnp.zeros((RC, LANES), F32)
                for k in range(KA):
                    dout = dcabuf[pl.ds(r * RC + KA - 1 - k, RC), ls]
                    acc = acc + wa_ref[k:k + 1, ls] * dout
                    gwa[8 * k:8 * k + 8, ls] += _rows8(cin * dout)
                dcain[rows, ls] = acc
                uin = u0buf[rows, ls]
                acc = jnp.zeros((RC, LANES), F32)
                for k in range(KB):
                    dout = _tap(shd, du1buf, ls, r * RC + KB - 1 - k, RC)
                    acc = acc + wb_ref[k:k + 1, ls] * dout
                    gwb[8 * k:8 * k + 8, ls] += _rows8(uin * dout)
                du0[rows, ls] = acc
            return carry

        lax.fori_loop(0, D // LANES, lane_body, 0)
        dca_in = dcain[...]
        du0v = du0[...]
        dp_ref[:, 0:D] = dba_ref[...]
        dp_ref[:, D:2 * D] = (dca_in * vav).astype(BF)
        dp_ref[:, 2 * D:3 * D] = (dca_in * cav).astype(BF)
        dp_ref[:, 3 * D:4 * D] = (du0v * sg).astype(BF)
        dp_ref[:, 4 * D:5 * D] = (du0v * vbv * (sg * (1.0 - sg))).astype(BF)
        dp_ref[:, 5 * D:7 * D] = dg_ref[...]

        @pl.when(last)
        def _():
            gw_ref[...] = jnp.zeros_like(gw_ref)
            for k in range(KA):
                gw_ref[k:k + 1, :] = _colsum(gwa[8 * k:8 * k + 8, :])
            for k in range(KB):
                gw_ref[8 + k:9 + k, :] = _colsum(gwb[8 * k:8 * k + 8, :])

    full = lambda r: pl.BlockSpec((r, D), lambda i: (0, 0))
    tok = pl.BlockSpec((tt, D), lambda i: (i, 0))
    nxt = lambda h: pl.BlockSpec((h, D), lambda i: (jnp.minimum((i + 1) * (tt // h), T // h - 1), 0))
    return _pallas(
        body, name="conv_bwd", grid=(nsteps,),
        in_specs=[tok, nxt(HN), tok, nxt(HB),
                  _seg(tt, 1), _seg(tt, 2), _seg(tt, 3), _seg(tt, 4),
                  tok, pl.BlockSpec((tt, 2 * D), lambda i: (i, 0)), full(8), full(32)],
        out_specs=[pl.BlockSpec((tt, 7 * D), lambda i: (i, 0)), full(40)],
        out_shape=[jax.ShapeDtypeStruct((T, 7 * D), BF), jax.ShapeDtypeStruct((40, D), F32)],
        scratch_shapes=[pltpu.VMEM((tt, D), F32), pltpu.VMEM((tt, D), F32),
                        pltpu.VMEM((tt + HN, D), F32), pltpu.VMEM((tt + HB, D), F32),
                        pltpu.VMEM((tt, D), F32), pltpu.VMEM((tt, D), F32),
                        pltpu.VMEM((8 * KA, D), F32), pltpu.VMEM((8 * KB, D), F32),
                        pltpu.VMEM((8, HB + tt, LANES), F32)],
        args=(dconva, dconva, du1, du1, p, p, p, p, dba, dg, wa, wb), push=push)


def _inproj_bwd(dp, win, dz1, push, tm=512):
    T, cols = dp.shape

    def body(dp_ref, w_ref, dz1_ref, o_ref):
        o_ref[...] = ALPHA * dz1_ref[...] + lax.dot_general(dp_ref[...], w_ref[...], NT_DIMS,
                                                            preferred_element_type=F32)

    tok = pl.BlockSpec((tm, D), lambda i: (i, 0))
    return _pallas(
        body, name="inproj_bwd", grid=(T // tm,),
        in_specs=[pl.BlockSpec((tm, cols), lambda i: (i, 0)), _resident((D, cols)), tok],
        out_specs=[tok],
        out_shape=[jax.ShapeDtypeStruct((T, D), F32)],
        args=(dp, win, dz1), push=push)


def _adam_math(w, m, v, g):
    nm = ADAM_B1 * m + (1.0 - ADAM_B1) * g
    nv = ADAM_B2 * v + (1.0 - ADAM_B2) * (g * g)
    m_hat = nm / (1.0 - ADAM_B1 ** ADAM_STEP)
    v_hat = nv / (1.0 - ADAM_B2 ** ADAM_STEP)
    return -ADAM_LR * (m_hat / (jnp.sqrt(v_hat) + ADAM_EPS) + ADAM_WD * w), nm, nv


def _adamw(w, m, v, g32, landing, me, name, rb):
    R, C = w.shape
    nl = landing.shape[0]

    def body(me_ref, w_ref, m_ref, v_ref, own_ref, l_ref, g_ref, d_ref, nm_ref, nv_ref):
        g = own_ref[0]
        for k in range(nl):
            g = g + l_ref[k].astype(F32)
        g_ref[...] = g
        d_ref[...], nm_ref[...], nv_ref[...] = _adam_math(w_ref[...], m_ref[...], v_ref[...], g)

    blk = pl.BlockSpec((rb, C), lambda i, me_ref: (i, 0))
    grid_spec = pltpu.PrefetchScalarGridSpec(
        num_scalar_prefetch=1, grid=(R // rb,),
        in_specs=[blk, blk, blk, pl.BlockSpec((1, rb, C), lambda i, me_ref: (me_ref[0], i, 0)),
                  pl.BlockSpec((nl, rb, C), lambda i, me_ref: (0, i, 0))],
        out_specs=[blk] * 4)
    return pl.pallas_call(
        body, name=name, grid_spec=grid_spec, out_shape=[jax.ShapeDtypeStruct((R, C), F32)] * 4,
        compiler_params=_params(1))(me, w, m, v, g32, landing)


def _adamw_small(small_g, vec_w, vec_m, vec_v, conv_w, conv_m, conv_v):
    nv_ = len(vec_w)
    conv_rows = [(8, KA), (16, KB)]

    def body(*refs):
        g_ref = refs[0]
        w_refs, m_refs, v_refs = refs[1:10], refs[10:19], refs[19:28]
        out_refs, gsum = refs[28:64], refs[64]
        acc = g_ref[0]
        for j in range(1, NDEV):
            acc = acc + g_ref[j]
        gsum[...] = acc
        me = 4 * lax.axis_index("x") + 2 * lax.axis_index("y") + lax.axis_index("c")
        cols = pl.ds(pl.multiple_of(me * LANES, LANES), LANES)
        for i in range(nv_ + 2):
            if i < nv_:
                g = gsum[i:i + 1, :]
            else:
                r0, k = conv_rows[i - nv_]
                g = gsum[r0:r0 + k, cols]
            o = out_refs[4 * i:4 * i + 4]
            o[0][...] = g
            o[1][...], o[2][...], o[3][...] = _adam_math(w_refs[i][...], m_refs[i][...], v_refs[i][...], g)

    ws, ms, vs = list(vec_w) + list(conv_w), list(vec_m) + list(conv_m), list(vec_v) + list(conv_v)
    out_shape = [jax.ShapeDtypeStruct(w.shape, F32) for w in ws for _ in range(4)]
    return pl.pallas_call(
        body, name="adamw_small", out_shape=out_shape,
        scratch_shapes=[pltpu.VMEM(small_g.shape[1:], F32)])(small_g, *ws, *ms, *vs)


def _pad_rows(a, rows):
    return jnp.pad(a, ((0, rows - a.shape[0]), (0, 0)))


def _local_step(p, xb, x, target, win, wup, wdown, woa, wob, wo, wa, wb, vecs):
    yapre, conva, xhatb, rstdb, u3 = _mixer_fwd(p, wa, wb, vecs)
    ya, yb, merged, xhat1, rstd1, x1b = _post_mixer(yapre, u3, p, x, woa, wob, wo, vecs)
    r, h = _mlp_up(x1b, wup)
    dz2, dz2b, st2 = _mlp_down_loss(h, wdown, xhat1, target, vecs)

    by_owner = lambda g16: g16.reshape(NDEV, D // NDEV, D)
    dhpre = _mlp_down_bwd(dz2b, wdown, r)
    g_wdown = _tn_matmul(h, dz2b, NDEV, 512, D, True, False, "grad_w_down")
    (dz1, dz1b, st1), land_wdown = _mlp_up_bwd(dhpre, wup, dz2, xhat1, rstd1, vecs, _push(exch=[g_wdown[1]]))
    g_wup = _tn_matmul(x1b, dhpre, NDEV, D, 512, False, True, "grad_w_up")
    (dya, dyb, dg, dba, dconva, du1, stb), land_wup = _merge_bwd(
        dz1b, p, ya, yb, conva, xhatb, rstdb, woa, wob, wo, vecs, _push(exch=[g_wup[1]]))
    g_wo = _tn_matmul(merged, dz1b, 1, D, D, False, False, "grad_w_o")
    g_woa = _tn_matmul(yapre, dya, 1, D, D, False, False, "grad_w_out_a")
    g_wob = _tn_matmul(u3, dyb, 1, D, D, False, False, "grad_w_out_b")
    (dp, gw), land_sq = _conv_bwd(dconva, du1, p, dba, dg, wa, wb,
                                  _push(exch=[by_owner(g_woa[1]), by_owner(g_wob[1]), by_owner(g_wo[1])]))
    g_win = _tn_matmul(xb, dp, NDEV, D, 896, False, True, "grad_w_in")

    small = jnp.concatenate([stb[2:3], stb[0:2], st1[0:2], st2[0:3], gw], axis=0)
    (grad_x,), land_last = _inproj_bwd(dp, win, dz1, _push(exch=[g_win[1]], gath=[small]))
    grads = (g_win[0], g_wup[0], g_wdown[0], g_woa[0], g_wob[0], g_wo[0])
    return grad_x, grads, small, land_wdown + land_wup + land_sq + land_last


def kernel(x, w_in, conv_a_w, w_out_a, conv_b_w, conv_b_bias, ln_b_gamma, ln_b_beta, w_out_b, w_o, ln1_gamma, ln1_beta, w_up, w_down, ln2_gamma, ln2_beta, loss_target, m_w_in, m_conv_a_w, m_w_out_a, m_conv_b_w, m_conv_b_bias, m_ln_b_gamma, m_ln_b_beta, m_w_out_b, m_w_o, m_ln1_gamma, m_ln1_beta, m_w_up, m_w_down, m_ln2_gamma, m_ln2_beta, v_w_in, v_conv_a_w, v_w_out_a, v_conv_b_w, v_conv_b_bias, v_ln_b_gamma, v_ln_b_beta, v_w_out_b, v_w_o, v_ln1_gamma, v_ln1_beta, v_w_up, v_w_down, v_ln2_gamma, v_ln2_beta):
    T = x.shape[1]
    me = 4 * lax.axis_index("x") + 2 * lax.axis_index("y") + lax.axis_index("c")

    conv_shard = jnp.concatenate([_pad_rows(conv_a_w, 8), _pad_rows(conv_b_w, 32)], axis=0)
    win_g, conv_g = _allgather([w_in.astype(BF), conv_shard], "allgather_w_in")
    conv_full = jnp.transpose(conv_g, (1, 0, 2)).reshape(40, D)
    vecs = jnp.stack([conv_b_bias, ln_b_gamma, ln_b_beta, ln1_gamma, ln1_beta, ln2_gamma, ln2_beta,
                      jnp.zeros_like(ln2_beta)])
    whole = lambda g: jnp.transpose(g, (1, 0, 2)).reshape(D, -1)
    win = whole(win_g)
    p, xb, (wup_g, wdown_g, woa_g, wob_g, wo_g) = _inproj(
        x[0], win, [w_up.astype(BF), w_down.astype(BF), w_out_a.astype(BF), w_out_b.astype(BF), w_o.astype(BF)])

    grad_x, grads, small, landing = _local_step(
        p, xb, x[0], loss_target[0], win, whole(wup_g), wdown_g.reshape(NDEV * 512, D), woa_g.reshape(D, D),
        wob_g.reshape(D, D), wo_g.reshape(D, D), conv_full[0:8], conv_full[8:40], vecs)
    g_win, g_wup, g_wdown, g_woa, g_wob, g_wo = grads
    l_wdown, l_wup, l_woa, l_wob, l_wo, l_win, small_g = landing

    loss = lax.psum(0.5 / D * jnp.sum(small[7]), ("x", "y", "c"))

    me1 = me.astype(jnp.int32).reshape(1)
    by_owner = lambda g32: g32.reshape(NDEV, D // NDEV, D)
    r_win = _adamw(w_in, m_w_in, v_w_in, g_win, l_win, me1, "adamw_w_in", 256)
    r_wup = _adamw(w_up, m_w_up, v_w_up, g_wup, l_wup, me1, "adamw_w_up", 256)
    r_wdown = _adamw(w_down, m_w_down, v_w_down, g_wdown, l_wdown, me1, "adamw_w_down", 256)
    r_woa = _adamw(w_out_a, m_w_out_a, v_w_out_a, by_owner(g_woa), l_woa, me1, "adamw_w_out_a", 128)
    r_wob = _adamw(w_out_b, m_w_out_b, v_w_out_b, by_owner(g_wob), l_wob, me1, "adamw_w_out_b", 128)
    r_wo = _adamw(w_o, m_w_o, v_w_o, by_owner(g_wo), l_wo, me1, "adamw_w_o", 128)

    row = lambda vec: vec.reshape(1, D)
    small_out = _adamw_small(
        small_g,
        [row(a) for a in (conv_b_bias, ln_b_gamma, ln_b_beta, ln1_gamma, ln1_beta, ln2_gamma, ln2_beta)],
        [row(a) for a in (m_conv_b_bias, m_ln_b_gamma, m_ln_b_beta, m_ln1_gamma, m_ln1_beta, m_ln2_gamma, m_ln2_beta)],
        [row(a) for a in (v_conv_b_bias, v_ln_b_gamma, v_ln_b_beta, v_ln1_gamma, v_ln1_beta, v_ln2_gamma, v_ln2_beta)],
        [conv_a_w, conv_b_w], [m_conv_a_w, m_conv_b_w], [v_conv_a_w, v_conv_b_w])
    r_vec = [[small_out[4 * i + q].reshape(D) for q in range(4)] for i in range(7)]
    r_conva, r_convb = small_out[28:32], small_out[32:36]

    per_weight = []
    for q in range(4):
        per_weight.append([
            r_win[q], r_conva[q], r_woa[q], r_convb[q],
            r_vec[0][q], r_vec[1][q], r_vec[2][q], r_wob[q], r_wo[q], r_vec[3][q], r_vec[4][q],
            r_wup[q], r_wdown[q], r_vec[5][q], r_vec[6][q]])
    return (loss, grad_x[None], *per_weight[0], *per_weight[1], *per_weight[2], *per_weight[3])
```

```python
import functools

import jax
import jax.numpy as jnp
from jax import lax
from jax.experimental import pallas as pl
from jax.experimental.pallas import tpu as pltpu

F32 = jnp.float32
BF = jnp.bfloat16
D = 1024
NDEV = 8
ALPHA = 2.0 ** 0.25
LN_EPS = 1e-5
KA, KB = 3, 31
HA, HB = 16, 32
HN = 8
RC = 64
LANES = 128
VMEM_LIMIT = 56 * 1024 * 1024
MESH = pl.DeviceIdType.MESH
ADAM_LR, ADAM_B1, ADAM_B2, ADAM_EPS, ADAM_WD, ADAM_STEP = 0.001, 0.9, 0.999, 1e-08, 0.01, 10

ANY_SPEC = pl.BlockSpec(memory_space=pl.ANY)
NT_DIMS = (((1,), (1,)), ((), ()))
TN_DIMS = (((0,), (0,)), ((), ()))


def _params(n_axes):
    return pltpu.CompilerParams(dimension_semantics=("arbitrary",) * n_axes, vmem_limit_bytes=VMEM_LIMIT)


def _sigmoid(v):
    return 0.5 * jnp.tanh(0.5 * v) + 0.5


def _ln_fwd(z):
    mu = jnp.mean(z, axis=-1, keepdims=True)
    zc = z - mu
    var = jnp.mean(zc * zc, axis=-1, keepdims=True)
    rstd = lax.rsqrt(var + LN_EPS)
    return zc * rstd, rstd


def _ln_bwd(dy, xhat, rstd, gamma):
    dxhat = dy * gamma
    m1 = jnp.mean(dxhat, axis=-1, keepdims=True)
    m2 = jnp.mean(dxhat * xhat, axis=-1, keepdims=True)
    return rstd * (dxhat - m1 - xhat * m2)


def _colsum(v):
    return jnp.sum(v, axis=0, keepdims=True)


class _TwoLevelGather:
    def __init__(self, ins, outs, send_sems, recv_sems, local_sems):
        self.ins, self.outs = ins, outs
        self.send_sems, self.recv_sems, self.local_sems = send_sems, recv_sems, local_sems
        x, y, c = lax.axis_index("x"), lax.axis_index("y"), lax.axis_index("c")
        self.me, self.sibling, self.c = (x, y, c), (x, y, 1 - c), c
        self.chips = [(1 - x, y), (x, 1 - y), (1 - x, 1 - y)]
        self.n = len(ins)

    @staticmethod
    def out_shape(arrs):
        return [jax.ShapeDtypeStruct((NDEV,) + a.shape, a.dtype) for a in arrs]

    @staticmethod
    def scratch(n):
        return [pltpu.SemaphoreType.DMA((n, 7)), pltpu.SemaphoreType.DMA((n, 7)), pltpu.SemaphoreType.DMA((n,))]

    def _copy(self, a, k, block, to, src=None):
        px, py, pc = block
        rows = self.outs[a].at[4 * px + 2 * py + pc]
        return pltpu.make_async_remote_copy(
            src_ref=rows if src is None else src, dst_ref=rows,
            send_sem=self.send_sems.at[a, k], recv_sem=self.recv_sems.at[a, k],
            device_id=to, device_id_type=MESH)

    def _mine(self, a):
        x, y, c = self.me
        return pltpu.make_async_copy(self.ins[a], self.outs[a].at[4 * x + 2 * y + c], self.local_sems.at[a])

    def _first(self, a):
        cps = [self._copy(a, 0, self.me, self.sibling, src=self.ins[a])]
        return cps + [self._copy(a, 1 + j, self.me, (*chip, self.c), src=self.ins[a]) for j, chip in enumerate(self.chips)]

    def _passed(self, a, j):
        return self._copy(a, 4 + j, (*self.chips[j], self.c), self.sibling)

    def start(self):
        for a in range(self.n):
            self._mine(a).start()
        for a in range(self.n):
            for cp in self._first(a):
                cp.start()

    def wait_ici(self, j):
        for a in range(self.n):
            self._copy(a, 1 + j, (*self.chips[j], self.c), self.me).wait_recv()

    def pass_on(self, j):
        for a in range(self.n):
            self._passed(a, j).start()

    def wait_sibling(self):
        for a in range(self.n):
            self._copy(a, 0, self.sibling, self.me).wait_recv()

    def wait_passed(self, j):
        for a in range(self.n):
            self._copy(a, 4 + j, (*self.chips[j], 1 - self.c), self.me).wait_recv()

    def drain(self):
        for a in range(self.n):
            for cp in self._first(a) + [self._passed(a, j) for j in range(3)]:
                cp.wait_send()
            self._mine(a).wait()

    def forward(self):
        for j in range(3):
            self.wait_ici(j)
            self.pass_on(j)

    def finish(self):
        self.wait_sibling()
        for j in range(3):
            self.wait_passed(j)
        self.drain()


class _Push:
    def __init__(self, exch=(), gath=()):
        self.exch, self.gath = list(exch), list(gath)
        self.n = len(self.exch) + len(self.gath)

    def operands(self):
        return self.exch + self.gath

    def out_shape(self):
        return ([jax.ShapeDtypeStruct((NDEV - 1,) + a.shape[1:], a.dtype) for a in self.exch]
                + [jax.ShapeDtypeStruct((NDEV,) + a.shape, a.dtype) for a in self.gath])

    def scratch(self):
        return [pltpu.SemaphoreType.DMA((self.n, 7)), pltpu.SemaphoreType.DMA((self.n, 7)),
                pltpu.SemaphoreType.DMA((max(len(self.gath), 1),))]

    def copies(self, ins, outs, send_sems, recv_sems, local_sems):
        x, y, c = lax.axis_index("x"), lax.axis_index("y"), lax.axis_index("c")
        me = 4 * x + 2 * y + c
        ne = len(self.exch)
        remote = []
        for k in range(1, NDEV):
            px = 1 - x if k & 4 else x
            py = 1 - y if k & 2 else y
            pc = 1 - c if k & 1 else c
            for a in range(self.n):
                src = ins[a].at[4 * px + 2 * py + pc] if a < ne else ins[a]
                dst = outs[a].at[k - 1] if a < ne else outs[a].at[me]
                remote.append(pltpu.make_async_remote_copy(
                    src_ref=src, dst_ref=dst, send_sem=send_sems.at[a, k - 1], recv_sem=recv_sems.at[a, k - 1],
                    device_id=(px, py, pc), device_id_type=MESH))
        local = [pltpu.make_async_copy(ins[a], outs[a].at[me], local_sems.at[a - ne]) for a in range(ne, self.n)]
        return remote, local


def _push(exch=(), gath=()):
    return _Push(exch, gath)


def _pallas(body, *, name, grid, in_specs, out_specs, out_shape, args, scratch_shapes=(), push=None):
    ni, no, ns = len(in_specs), len(out_specs), len(scratch_shapes)
    if push is None:
        outs = pl.pallas_call(
            body, name=name, grid=grid, in_specs=in_specs, out_specs=out_specs, out_shape=out_shape,
            scratch_shapes=list(scratch_shapes), compiler_params=_params(len(grid)))(*args)
        return list(outs), []
    npush = push.n

    def wrapped(*refs):
        ins, pins = refs[:ni], refs[ni:ni + npush]
        outs, pouts = refs[ni + npush:ni + npush + no], refs[ni + npush + no:ni + 2 * npush + no]
        scr, sems = refs[ni + 2 * npush + no:ni + 2 * npush + no + ns], refs[ni + 2 * npush + no + ns:]
        first = functools.reduce(jnp.logical_and, [pl.program_id(d) == 0 for d in range(len(grid))])
        last = functools.reduce(jnp.logical_and, [pl.program_id(d) == grid[d] - 1 for d in range(len(grid))])
        remote, local = push.copies(pins, pouts, *sems)

        @pl.when(first)
        def _():
            for cp in local + remote:
                cp.start()

        body(*ins, *outs, *scr)

        @pl.when(last)
        def _():
            for cp in remote + local:
                cp.wait()

    outs = pl.pallas_call(
        wrapped, name=name, grid=grid,
        in_specs=list(in_specs) + [ANY_SPEC] * npush, out_specs=list(out_specs) + [ANY_SPEC] * npush,
        out_shape=list(out_shape) + push.out_shape(), scratch_shapes=list(scratch_shapes) + push.scratch(),
        compiler_params=_params(len(grid)))(*args, *push.operands())
    return list(outs[:no]), list(outs[no:])


def _inproj_gather(x, w_shard, conv_shard, tm=512):
    T = x.shape[0]
    ni = T // tm
    bw = w_shard.shape[1]
    cx, cy, cc = lax.axis_index("x"), lax.axis_index("y"), lax.axis_index("c")
    blk = lambda px, py, pc: 4 * px + 2 * py + pc
    order = [blk(cx, cy, cc), blk(cx, cy, 1 - cc)]
    for chip in [(1 - cx, cy), (cx, 1 - cy), (1 - cx, 1 - cy)]:
        order += [blk(*chip, cc), blk(*chip, 1 - cc)]
    order = jnp.stack(order).astype(jnp.int32)

    def body(order_ref, x_ref, w_ref, conv_ref, p_ref, xb_ref, wing_ref, convg_ref, xbs, wbuf, wsem, *sems):
        gather = _TwoLevelGather([w_ref, conv_ref], [wing_ref, convg_ref], *sems)
        j, i = pl.program_id(0), pl.program_id(1)

        def load(src):
            cp = pltpu.make_async_copy(src, wbuf, wsem)
            cp.start()
            cp.wait()

        def arrival(jj):
            if jj == 0:
                gather.start()
                load(w_ref)
                return
            if jj == 1:
                gather.wait_sibling()
            elif jj % 2 == 0:
                gather.wait_ici(jj // 2 - 1)
                gather.pass_on(jj // 2 - 1)
            else:
                gather.wait_passed(jj // 2 - 1)
            load(wing_ref.at[order_ref[jj]])

        for jj in range(NDEV):
            pl.when(jnp.logical_and(j == jj, i == 0))(functools.partial(arrival, jj))

        @pl.when(j == 0)
        def _():
            xb = x_ref[...].astype(BF)
            xbs[i] = xb
            xb_ref[...] = xb

        p_ref[...] = jnp.dot(xbs[i], wbuf[...], preferred_element_type=F32).astype(BF)

        @pl.when(jnp.logical_and(j == NDEV - 1, i == ni - 1))
        def _():
            gather.drain()

    rows_once = lambda j, i, o: (jnp.where(j == 0, i, ni - 1), 0)
    grid_spec = pltpu.PrefetchScalarGridSpec(
        num_scalar_prefetch=1, grid=(NDEV, ni),
        in_specs=[pl.BlockSpec((tm, D), rows_once), ANY_SPEC, ANY_SPEC],
        out_specs=[pl.BlockSpec((tm, bw), lambda j, i, o: (i, o[j])), pl.BlockSpec((tm, D), rows_once),
                   ANY_SPEC, ANY_SPEC],
        scratch_shapes=[pltpu.VMEM((ni, tm, D), BF), pltpu.VMEM((D, bw), BF), pltpu.SemaphoreType.DMA(())]
        + _TwoLevelGather.scratch(2))
    p, xb, win_g, conv_g = pl.pallas_call(
        body, name="inproj_gather", grid_spec=grid_spec,
        out_shape=[jax.ShapeDtypeStruct((T, NDEV * bw), BF), jax.ShapeDtypeStruct((T, D), BF)]
        + _TwoLevelGather.out_shape([w_shard, conv_shard]),
        compiler_params=_params(2))(order, x, w_shard, conv_shard)
    return p, xb, win_g, conv_g


def _mixer_fwd(p, wa, wb, vecs, late, tt=256):
    T = p.shape[0]
    n = len(late)
    nt = T // tt

    def body(*refs):
        ba, ca, va, vb, gb, ca_p, va_p, vb_p, gb_p, wa_ref, wb_ref, vec_ref = refs[:12]
        yapre_ref, conva_ref, xhat_ref, rstd_ref, u3_ref = refs[12 + n:17 + n]
        cabuf, u0buf, u1buf, shu = refs[17 + 2 * n:21 + 2 * n]
        gather = _TwoLevelGather(refs[12:12 + n], refs[17 + n:17 + 2 * n], *refs[21 + 2 * n:])
        step = pl.program_id(0)
        first = step == 0

        @pl.when(first)
        def _():
            gather.start()

        @pl.when(step == (3 * nt) // 4)
        def _():
            gather.forward()

        @pl.when(step == nt - 1)
        def _():
            gather.finish()

        f = lambda ref: ref[...].astype(F32)
        cabuf[0:HA, :] = jnp.where(first, 0.0, f(ca_p) * f(va_p))
        cabuf[HA:HA + tt, :] = f(ca) * f(va)
        u0buf[0:HB, :] = jnp.where(first, 0.0, f(vb_p) * _sigmoid(f(gb_p)))
        u0buf[HB:HB + tt, :] = f(vb) * _sigmoid(f(gb))

        def lane_body(cidx, carry):
            ls = pl.ds(pl.multiple_of(cidx * LANES, LANES), LANES)
            _shifted_copies(shu, u0buf, ls, tt + HB - 8)
            for r in range(tt // RC):
                acc = jnp.zeros((RC, LANES), F32)
                for k in range(KA):
                    acc = acc + wa_ref[k:k + 1, ls] * cabuf[pl.ds(HA - (KA - 1) + k + r * RC, RC), ls]
                conva_ref[pl.ds(r * RC, RC), ls] = acc
                acc = jnp.zeros((RC, LANES), F32)
                for k in range(KB):
                    acc = acc + wb_ref[k:k + 1, ls] * _tap(shu, u0buf, ls, HB - (KB - 1) + k + r * RC, RC)
                u1buf[pl.ds(r * RC, RC), ls] = acc
            return carry

        lax.fori_loop(0, D // LANES, lane_body, 0)
        yapre_ref[...] = (f(ba) * conva_ref[...]).astype(BF)
        xhat, rstd = _ln_fwd(u1buf[...] + vec_ref[0:1, :])
        xhat_ref[...] = xhat
        rstd_ref[...] = rstd
        u2 = xhat * vec_ref[1:2, :] + vec_ref[2:3, :]
        u3_ref[...] = (u2 * _sigmoid(u2)).astype(BF)

    full = lambda r: pl.BlockSpec((r, D), lambda i: (0, 0))
    tok = pl.BlockSpec((tt, D), lambda i: (i, 0))
    outs = pl.pallas_call(
        body, name="mixer_fwd", grid=(nt,),
        in_specs=[_seg(tt, 0), _seg(tt, 1), _seg(tt, 2), _seg(tt, 3), _seg(tt, 4),
                  _prev(tt, HA, 1), _prev(tt, HA, 2), _prev(tt, HB, 3), _prev(tt, HB, 4),
                  full(8), full(32), full(8)] + [ANY_SPEC] * n,
        out_specs=[tok, tok, tok, pl.BlockSpec((tt, 1), lambda i: (i, 0)), tok] + [ANY_SPEC] * n,
        out_shape=[jax.ShapeDtypeStruct((T, D), BF), jax.ShapeDtypeStruct((T, D), F32),
                   jax.ShapeDtypeStruct((T, D), F32), jax.ShapeDtypeStruct((T, 1), F32),
                   jax.ShapeDtypeStruct((T, D), BF)] + _TwoLevelGather.out_shape(late),
        scratch_shapes=[pltpu.VMEM((HA + tt, D), F32), pltpu.VMEM((HB + tt, D), F32), pltpu.VMEM((tt, D), F32),
                        pltpu.VMEM((8, HB + tt, LANES), F32)] + _TwoLevelGather.scratch(n),
        compiler_params=_params(1))(p, p, p, p, p, p, p, p, p, wa, wb, vecs, *late)
    return list(outs[:5]), list(outs[5:])


def _seg(tt, s):
    return pl.BlockSpec((tt, D), lambda i: (i, s))


def _prev(tt, h, s):
    return pl.BlockSpec((h, D), lambda i: (jnp.maximum(i * (tt // h) - 1, 0), s))


def _shifted_copies(shbuf, src, ls, n):
    for s in range(1, 8):
        shbuf[s, 0:n, :] = src[pl.ds(s, n), ls]


def _tap(shbuf, src, ls, off, rows):
    s, q = off % 8, off // 8
    if s == 0:
        return src[pl.ds(off, rows), ls]
    return shbuf[s, pl.ds(8 * q, rows), :]


def _post_mixer(yapre, u3, p, x, woa, wob, wo, vecs, tm=512):
    T = x.shape[0]

    def body(yapre_ref, u3_ref, ga_ref, gb_ref, x_ref, woa_ref, wob_ref, wo_ref, vec_ref,
             ya_ref, yb_ref, merged_ref, xhat_ref, rstd_ref, x1b_ref):
        ya = jnp.dot(yapre_ref[...], woa_ref[...], preferred_element_type=F32)
        yb = jnp.dot(u3_ref[...], wob_ref[...], preferred_element_type=F32)
        ya_ref[...] = ya.astype(BF)
        yb_ref[...] = yb.astype(BF)
        merged = (_sigmoid(ga_ref[...].astype(F32)) * ya + _sigmoid(gb_ref[...].astype(F32)) * yb).astype(BF)
        merged_ref[...] = merged
        mix = jnp.dot(merged, wo_ref[...], preferred_element_type=F32)
        xhat, rstd = _ln_fwd(ALPHA * x_ref[...] + mix)
        xhat_ref[...] = xhat
        rstd_ref[...] = rstd
        x1b_ref[...] = (xhat * vec_ref[3:4, :] + vec_ref[4:5, :]).astype(BF)

    tok = pl.BlockSpec((tm, D), lambda i: (i, 0))
    wfull = _resident((D, D))
    one = pl.BlockSpec((tm, 1), lambda i: (i, 0))
    return pl.pallas_call(
        body, name="post_mixer", grid=(T // tm,),
        in_specs=[tok, tok, _seg(tm, 5), _seg(tm, 6), tok, wfull, wfull, wfull, pl.BlockSpec((8, D), lambda i: (0, 0))],
        out_specs=[tok, tok, tok, tok, one, tok],
        out_shape=[jax.ShapeDtypeStruct((T, D), BF), jax.ShapeDtypeStruct((T, D), BF),
                   jax.ShapeDtypeStruct((T, D), BF), jax.ShapeDtypeStruct((T, D), F32),
                   jax.ShapeDtypeStruct((T, 1), F32), jax.ShapeDtypeStruct((T, D), BF)],
        compiler_params=_params(1))(yapre, u3, p, p, x, woa, wob, wo, vecs)


def _mlp_up(x1b, wup, tm=512, tn=2048):
    T = x1b.shape[0]
    dff = wup.shape[1]

    def body(x_ref, w_ref, r_ref, h_ref):
        r = jnp.maximum(jnp.dot(x_ref[...], w_ref[...], preferred_element_type=F32), 0.0)
        r_ref[...] = r.astype(BF)
        h_ref[...] = (r * r).astype(BF)

    out = pl.BlockSpec((tm, tn), lambda j, i: (i, j))
    return pl.pallas_call(
        body, name="mlp_up", grid=(dff // tn, T // tm),
        in_specs=[pl.BlockSpec((tm, D), lambda j, i: (i, 0)), pl.BlockSpec((D, tn), lambda j, i: (0, j))],
        out_specs=[out, out],
        out_shape=[jax.ShapeDtypeStruct((T, dff), BF), jax.ShapeDtypeStruct((T, dff), BF)],
        compiler_params=_params(2))(x1b, wup)


def _resident(shape):
    return pl.BlockSpec(shape, lambda *_: (0,) * len(shape), pipeline_mode=pl.Buffered(1))


def _mlp_down_loss(h, wdown, xhat1, target, vecs, tm=512):
    T, dff = h.shape

    def body(h_ref, w_ref, xhat1_ref, tgt_ref, vec_ref, dz2_ref, dz2b_ref, st_ref):
        @pl.when(pl.program_id(0) == 0)
        def _():
            st_ref[...] = jnp.zeros_like(st_ref)

        ff = jnp.dot(h_ref[...], w_ref[...], preferred_element_type=F32)
        x1 = xhat1_ref[...] * vec_ref[3:4, :] + vec_ref[4:5, :]
        xhat2, rstd2 = _ln_fwd(ALPHA * x1 + ff)
        g2 = vec_ref[5:6, :]
        diff = xhat2 * g2 + vec_ref[6:7, :] - tgt_ref[...]
        dx2 = diff * (1.0 / D)
        st_ref[0:1, :] += _colsum(dx2 * xhat2)
        st_ref[1:2, :] += _colsum(dx2)
        st_ref[2:3, :] += _colsum(diff * diff)
        dz2 = _ln_bwd(dx2, xhat2, rstd2, g2)
        dz2_ref[...] = dz2
        dz2b_ref[...] = dz2.astype(BF)

    tok = pl.BlockSpec((tm, D), lambda i: (i, 0))
    vec = pl.BlockSpec((8, D), lambda i: (0, 0))
    return pl.pallas_call(
        body, name="mlp_down_loss", grid=(T // tm,),
        in_specs=[pl.BlockSpec((tm, dff), lambda i: (i, 0)), _resident((dff, D)), tok, tok, vec],
        out_specs=[tok, tok, vec],
        out_shape=[jax.ShapeDtypeStruct((T, D), F32), jax.ShapeDtypeStruct((T, D), BF), jax.ShapeDtypeStruct((8, D), F32)],
        compiler_params=_params(1))(h, wdown, xhat1, target, vecs)


def _mlp_down_bwd(dz2b, wdown, r, tm=512, tk=2048):
    T, dff = r.shape

    def body(dz_ref, w_ref, r_ref, o_ref):
        dh = lax.dot_general(dz_ref[...], w_ref[...], NT_DIMS, preferred_element_type=F32)
        o_ref[...] = (dh * (2.0 * r_ref[...].astype(F32))).astype(BF)

    blk = pl.BlockSpec((tm, tk), lambda j, i: (i, j))
    return pl.pallas_call(
        body, name="mlp_down_bwd", grid=(dff // tk, T // tm),
        in_specs=[pl.BlockSpec((tm, D), lambda j, i: (i, 0)), pl.BlockSpec((tk, D), lambda j, i: (j, 0)), blk],
        out_specs=blk,
        out_shape=jax.ShapeDtypeStruct((T, dff), BF),
        compiler_params=_params(2))(dz2b, wdown, r)


def _tn_matmul(a, b, nblk, a_bw, b_bw, a_blocked, b_blocked, name, tt=2048):
    T = a.shape[0]
    nt = T // tt

    def body(a_ref, b_ref, o32_ref, o16_ref):
        t = pl.program_id(1)

        @pl.when(t == 0)
        def _():
            o32_ref[...] = jnp.zeros_like(o32_ref)

        o32_ref[0] += lax.dot_general(a_ref[...], b_ref[...], TN_DIMS, preferred_element_type=F32)

        @pl.when(t == nt - 1)
        def _():
            o16_ref[...] = o32_ref[...].astype(BF)

    a_spec = pl.BlockSpec((tt, a_bw), (lambda j, t: (t, j)) if a_blocked else (lambda j, t: (t, 0)))
    b_spec = pl.BlockSpec((tt, b_bw), (lambda j, t: (t, j)) if b_blocked else (lambda j, t: (t, 0)))
    out = pl.BlockSpec((1, a_bw, b_bw), lambda j, t: (j, 0, 0))
    return pl.pallas_call(
        body, name=name, grid=(nblk, nt),
        in_specs=[a_spec, b_spec], out_specs=[out, out],
        out_shape=[jax.ShapeDtypeStruct((nblk, a_bw, b_bw), F32), jax.ShapeDtypeStruct((nblk, a_bw, b_bw), BF)],
        compiler_params=_params(2))(a, b)


def _mlp_up_bwd(dhpre, wup, dz2, xhat1, rstd1, vecs, push, tm=512):
    T, dff = dhpre.shape

    def body(dh_ref, w_ref, dz2_ref, xhat_ref, rstd_ref, vec_ref, dz1_ref, dz1b_ref, st_ref):
        @pl.when(pl.program_id(0) == 0)
        def _():
            st_ref[...] = jnp.zeros_like(st_ref)

        dx1 = lax.dot_general(dh_ref[...], w_ref[...], NT_DIMS, preferred_element_type=F32) + ALPHA * dz2_ref[...]
        xhat = xhat_ref[...]
        st_ref[0:1, :] += _colsum(dx1 * xhat)
        st_ref[1:2, :] += _colsum(dx1)
        dz1 = _ln_bwd(dx1, xhat, rstd_ref[...], vec_ref[3:4, :])
        dz1_ref[...] = dz1
        dz1b_ref[...] = dz1.astype(BF)

    tok = pl.BlockSpec((tm, D), lambda i: (i, 0))
    vec = pl.BlockSpec((8, D), lambda i: (0, 0))
    return _pallas(
        body, name="mlp_up_bwd", grid=(T // tm,),
        in_specs=[pl.BlockSpec((tm, dff), lambda i: (i, 0)), _resident((D, dff)),
                  tok, tok, pl.BlockSpec((tm, 1), lambda i: (i, 0)), vec],
        out_specs=[tok, tok, vec],
        out_shape=[jax.ShapeDtypeStruct((T, D), F32), jax.ShapeDtypeStruct((T, D), BF), jax.ShapeDtypeStruct((8, D), F32)],
        args=(dhpre, wup, dz2, xhat1, rstd1, vecs), push=push)


def _merge_bwd(dz1, p, ya, yb, conva, xhatb, rstdb, woa, wob, wo, vecs, push, tm=256):
    T = dz1.shape[0]

    def body(dz1_ref, ga_ref, gb_ref, ba_ref, ya_ref, yb_ref, conva_ref, xhat_ref, rstd_ref,
             woa_ref, wob_ref, wo_ref, vec_ref,
             dya_ref, dyb_ref, dg_ref, dba_ref, dconva_ref, du1_ref, st_ref):
        @pl.when(pl.program_id(0) == 0)
        def _():
            st_ref[...] = jnp.zeros_like(st_ref)

        dmerged = lax.dot_general(dz1_ref[...], wo_ref[...], NT_DIMS, preferred_element_type=F32)
        sa, sb = _sigmoid(ga_ref[...].astype(F32)), _sigmoid(gb_ref[...].astype(F32))
        dya = (dmerged * sa).astype(BF)
        dyb = (dmerged * sb).astype(BF)
        dya_ref[...] = dya
        dyb_ref[...] = dyb
        dg_ref[:, 0:D] = (dmerged * ya_ref[...].astype(F32) * (sa * (1.0 - sa))).astype(BF)
        dg_ref[:, D:2 * D] = (dmerged * yb_ref[...].astype(F32) * (sb * (1.0 - sb))).astype(BF)

        dyapre = lax.dot_general(dya, woa_ref[...], NT_DIMS, preferred_element_type=F32)
        dba_ref[...] = (dyapre * conva_ref[...]).astype(BF)
        dconva_ref[...] = dyapre * ba_ref[...].astype(F32)

        du3 = lax.dot_general(dyb, wob_ref[...], NT_DIMS, preferred_element_type=F32)
        xhat = xhat_ref[...]
        gamma = vec_ref[1:2, :]
        u2 = xhat * gamma + vec_ref[2:3, :]
        s = _sigmoid(u2)
        du2 = du3 * (s * (1.0 + u2 * (1.0 - s)))
        st_ref[0:1, :] += _colsum(du2 * xhat)
        st_ref[1:2, :] += _colsum(du2)
        du1 = _ln_bwd(du2, xhat, rstd_ref[...], gamma)
        st_ref[2:3, :] += _colsum(du1)
        du1_ref[...] = du1

    tok = pl.BlockSpec((tm, D), lambda i: (i, 0))
    wfull = pl.BlockSpec((D, D), lambda i: (0, 0))
    vec = pl.BlockSpec((8, D), lambda i: (0, 0))
    return _pallas(
        body, name="merge_bwd", grid=(T // tm,),
        in_specs=[tok, _seg(tm, 5), _seg(tm, 6), _seg(tm, 0), tok, tok, tok, tok, pl.BlockSpec((tm, 1), lambda i: (i, 0)),
                  wfull, wfull, wfull, vec],
        out_specs=[tok, tok, pl.BlockSpec((tm, 2 * D), lambda i: (i, 0)), tok, tok, tok, vec],
        out_shape=[jax.ShapeDtypeStruct((T, D), BF), jax.ShapeDtypeStruct((T, D), BF),
                   jax.ShapeDtypeStruct((T, 2 * D), BF), jax.ShapeDtypeStruct((T, D), BF),
                   jax.ShapeDtypeStruct((T, D), F32), jax.ShapeDtypeStruct((T, D), F32),
                   jax.ShapeDtypeStruct((8, D), F32)],
        args=(dz1, p, p, p, ya, yb, conva, xhatb, rstdb, woa, wob, wo, vecs), push=push)


def _rows8(v):
    out = v[0:8]
    for q in range(1, RC // 8):
        out = out + v[8 * q:8 * q + 8]
    return out


def _conv_bwd(dconva, du1, p, dba, dg, wa, wb, push, tt=256):
    T = p.shape[0]
    nsteps = T // tt

    def body(dca_ref, dca_n, du1_ref, du1_n, ca, va, vb, gb, dba_ref, dg_ref, wa_ref, wb_ref,
             dp_ref, gw_ref, cabuf, u0buf, dcabuf, du1buf, dcain, du0, gwa, gwb, shd):
        i = pl.program_id(0)
        first, last = i == 0, i == nsteps - 1

        @pl.when(first)
        def _():
            gwa[...] = jnp.zeros_like(gwa)
            gwb[...] = jnp.zeros_like(gwb)

        f = lambda ref: ref[...].astype(F32)
        cav, vav, vbv = f(ca), f(va), f(vb)
        cabuf[...] = cav * vav
        sg = _sigmoid(f(gb))
        u0buf[...] = vbv * sg
        dcabuf[0:tt, :] = dca_ref[...]
        dcabuf[tt:tt + HN, :] = jnp.where(last, 0.0, dca_n[...])
        du1buf[0:tt, :] = du1_ref[...]
        du1buf[tt:tt + HB, :] = jnp.where(last, 0.0, du1_n[...])

        def lane_body(cidx, carry):
            ls = pl.ds(pl.multiple_of(cidx * LANES, LANES), LANES)
            _shifted_copies(shd, du1buf, ls, tt + HB - 8)
            for r in range(tt // RC):
                rows = pl.ds(r * RC, RC)
                cin = cabuf[rows, ls]
                acc = jnp.zeros((RC, LANES), F32)
                for k in range(KA):
                    dout = dcabuf[pl.ds(r * RC + KA - 1 - k, RC), ls]
                    acc = acc + wa_ref[k:k + 1, ls] * dout
                    gwa[8 * k:8 * k + 8, ls] += _rows8(cin * dout)
                dcain[rows, ls] = acc
                uin = u0buf[rows, ls]
                acc = jnp.zeros((RC, LANES), F32)
                for k in range(KB):
                    dout = _tap(shd, du1buf, ls, r * RC + KB - 1 - k, RC)
                    acc = acc + wb_ref[k:k + 1, ls] * dout
                    gwb[8 * k:8 * k + 8, ls] += _rows8(uin * dout)
                du0[rows, ls] = acc
            return carry

        lax.fori_loop(0, D // LANES, lane_body, 0)
        dca_in = dcain[...]
        du0v = du0[...]
        dp_ref[:, 0:D] = dba_ref[...]
        dp_ref[:, D:2 * D] = (dca_in * vav).astype(BF)
        dp_ref[:, 2 * D:3 * D] = (dca_in * cav).astype(BF)
        dp_ref[:, 3 * D:4 * D] = (du0v * sg).astype(BF)
        dp_ref[:, 4 * D:5 * D] = (du0v * vbv * (sg * (1.0 - sg))).astype(BF)
        dp_ref[:, 5 * D:7 * D] = dg_ref[...]

        @pl.when(last)
        def _():
            gw_ref[...] = jnp.zeros_like(gw_ref)
            for k in range(KA):
                gw_ref[k:k + 1, :] = _colsum(gwa[8 * k:8 * k + 8, :])
            for k in range(KB):
                gw_ref[8 + k:9 + k, :] = _colsum(gwb[8 * k:8 * k + 8, :])

    full = lambda r: pl.BlockSpec((r, D), lambda i: (0, 0))
    tok = pl.BlockSpec((tt, D), lambda i: (i, 0))
    nxt = lambda h: pl.BlockSpec((h, D), lambda i: (jnp.minimum((i + 1) * (tt // h), T // h - 1), 0))
    return _pallas(
        body, name="conv_bwd", grid=(nsteps,),
        in_specs=[tok, nxt(HN), tok, nxt(HB),
                  _seg(tt, 1), _seg(tt, 2), _seg(tt, 3), _seg(tt, 4),
                  tok, pl.BlockSpec((tt, 2 * D), lambda i: (i, 0)), full(8), full(32)],
        out_specs=[pl.BlockSpec((tt, 7 * D), lambda i: (i, 0)), full(40)],
        out_shape=[jax.ShapeDtypeStruct((T, 7 * D), BF), jax.ShapeDtypeStruct((40, D), F32)],
        scratch_shapes=[pltpu.VMEM((tt, D), F32), pltpu.VMEM((tt, D), F32),
                        pltpu.VMEM((tt + HN, D), F32), pltpu.VMEM((tt + HB, D), F32),
                        pltpu.VMEM((tt, D), F32), pltpu.VMEM((tt, D), F32),
                        pltpu.VMEM((8 * KA, D), F32), pltpu.VMEM((8 * KB, D), F32),
                        pltpu.VMEM((8, HB + tt, LANES), F32)],
        args=(dconva, dconva, du1, du1, p, p, p, p, dba, dg, wa, wb), push=push)


def _inproj_bwd(dp, win, dz1, push, tm=512):
    T, cols = dp.shape

    def body(dp_ref, w_ref, dz1_ref, o_ref):
        o_ref[...] = ALPHA * dz1_ref[...] + lax.dot_general(dp_ref[...], w_ref[...], NT_DIMS,
                                                            preferred_element_type=F32)

    tok = pl.BlockSpec((tm, D), lambda i: (i, 0))
    return _pallas(
        body, name="inproj_bwd", grid=(T // tm,),
        in_specs=[pl.BlockSpec((tm, cols), lambda i: (i, 0)), _resident((D, cols)), tok],
        out_specs=[tok],
        out_shape=[jax.ShapeDtypeStruct((T, D), F32)],
        args=(dp, win, dz1), push=push)


def _adam_math(w, m, v, g):
    nm = ADAM_B1 * m + (1.0 - ADAM_B1) * g
    nv = ADAM_B2 * v + (1.0 - ADAM_B2) * (g * g)
    m_hat = nm / (1.0 - ADAM_B1 ** ADAM_STEP)
    v_hat = nv / (1.0 - ADAM_B2 ** ADAM_STEP)
    return -ADAM_LR * (m_hat / (jnp.sqrt(v_hat) + ADAM_EPS) + ADAM_WD * w), nm, nv


def _adamw(w, m, v, g32, landing, me, name, rb):
    R, C = w.shape
    nl = landing.shape[0]

    def body(me_ref, w_ref, m_ref, v_ref, own_ref, l_ref, g_ref, d_ref, nm_ref, nv_ref):
        g = own_ref[0]
        for k in range(nl):
            g = g + l_ref[k].astype(F32)
        g_ref[...] = g
        d_ref[...], nm_ref[...], nv_ref[...] = _adam_math(w_ref[...], m_ref[...], v_ref[...], g)

    blk = pl.BlockSpec((rb, C), lambda i, me_ref: (i, 0))
    grid_spec = pltpu.PrefetchScalarGridSpec(
        num_scalar_prefetch=1, grid=(R // rb,),
        in_specs=[blk, blk, blk, pl.BlockSpec((1, rb, C), lambda i, me_ref: (me_ref[0], i, 0)),
                  pl.BlockSpec((nl, rb, C), lambda i, me_ref: (0, i, 0))],
        out_specs=[blk] * 4)
    return pl.pallas_call(
        body, name=name, grid_spec=grid_spec, out_shape=[jax.ShapeDtypeStruct((R, C), F32)] * 4,
        compiler_params=_params(1))(me, w, m, v, g32, landing)


def _adamw_small(small_g, vec_w, vec_m, vec_v, conv_w, conv_m, conv_v):
    nv_ = len(vec_w)
    conv_rows = [(8, KA), (16, KB)]

    def body(*refs):
        g_ref = refs[0]
        w_refs, m_refs, v_refs = refs[1:10], refs[10:19], refs[19:28]
        out_refs, gsum = refs[28:64], refs[64]
        acc = g_ref[0]
        for j in range(1, NDEV):
            acc = acc + g_ref[j]
        gsum[...] = acc
        me = 4 * lax.axis_index("x") + 2 * lax.axis_index("y") + lax.axis_index("c")
        cols = pl.ds(pl.multiple_of(me * LANES, LANES), LANES)
        for i in range(nv_ + 2):
            if i < nv_:
                g = gsum[i:i + 1, :]
            else:
                r0, k = conv_rows[i - nv_]
                g = gsum[r0:r0 + k, cols]
            o = out_refs[4 * i:4 * i + 4]
            o[0][...] = g
            o[1][...], o[2][...], o[3][...] = _adam_math(w_refs[i][...], m_refs[i][...], v_refs[i][...], g)

    ws, ms, vs = list(vec_w) + list(conv_w), list(vec_m) + list(conv_m), list(vec_v) + list(conv_v)
    out_shape = [jax.ShapeDtypeStruct(w.shape, F32) for w in ws for _ in range(4)]
    return pl.pallas_call(
        body, name="adamw_small", out_shape=out_shape,
        scratch_shapes=[pltpu.VMEM(small_g.shape[1:], F32)])(small_g, *ws, *ms, *vs)


def _pad_rows(a, rows):
    return jnp.pad(a, ((0, rows - a.shape[0]), (0, 0)))


def _local_step(p, xb, mixed, x, target, win, wup, wdown, woa, wob, wo, wa, wb, vecs):
    yapre, conva, xhatb, rstdb, u3 = mixed
    ya, yb, merged, xhat1, rstd1, x1b = _post_mixer(yapre, u3, p, x, woa, wob, wo, vecs)
    r, h = _mlp_up(x1b, wup)
    dz2, dz2b, st2 = _mlp_down_loss(h, wdown, xhat1, target, vecs)

    by_owner = lambda g16: g16.reshape(NDEV, D // NDEV, D)
    dhpre = _mlp_down_bwd(dz2b, wdown, r)
    g_wdown = _tn_matmul(h, dz2b, NDEV, 512, D, True, False, "grad_w_down")
    (dz1, dz1b, st1), land_wdown = _mlp_up_bwd(dhpre, wup, dz2, xhat1, rstd1, vecs, _push(exch=[g_wdown[1]]))
    g_wup = _tn_matmul(x1b, dhpre, NDEV, D, 512, False, True, "grad_w_up")
    (dya, dyb, dg, dba, dconva, du1, stb), land_wup = _merge_bwd(
        dz1b, p, ya, yb, conva, xhatb, rstdb, woa, wob, wo, vecs, _push(exch=[g_wup[1]]))
    g_wo = _tn_matmul(merged, dz1b, 1, D, D, False, False, "grad_w_o")
    g_woa = _tn_matmul(yapre, dya, 1, D, D, False, False, "grad_w_out_a")
    g_wob = _tn_matmul(u3, dyb, 1, D, D, False, False, "grad_w_out_b")
    (dp, gw), land_sq = _conv_bwd(dconva, du1, p, dba, dg, wa, wb,
                                  _push(exch=[by_owner(g_woa[1]), by_owner(g_wob[1]), by_owner(g_wo[1])]))
    g_win = _tn_matmul(xb, dp, NDEV, D, 896, False, True, "grad_w_in")

    small = jnp.concatenate([stb[2:3], stb[0:2], st1[0:2], st2[0:3], gw], axis=0)
    (grad_x,), land_last = _inproj_bwd(dp, win, dz1, _push(exch=[g_win[1]], gath=[small]))
    grads = (g_win[0], g_wup[0], g_wdown[0], g_woa[0], g_wob[0], g_wo[0])
    return grad_x, grads, small, land_wdown + land_wup + land_sq + land_last


def kernel(x, w_in, conv_a_w, w_out_a, conv_b_w, conv_b_bias, ln_b_gamma, ln_b_beta, w_out_b, w_o, ln1_gamma, ln1_beta, w_up, w_down, ln2_gamma, ln2_beta, loss_target, m_w_in, m_conv_a_w, m_w_out_a, m_conv_b_w, m_conv_b_bias, m_ln_b_gamma, m_ln_b_beta, m_w_out_b, m_w_o, m_ln1_gamma, m_ln1_beta, m_w_up, m_w_down, m_ln2_gamma, m_ln2_beta, v_w_in, v_conv_a_w, v_w_out_a, v_conv_b_w, v_conv_b_bias, v_ln_b_gamma, v_ln_b_beta, v_w_out_b, v_w_o, v_ln1_gamma, v_ln1_beta, v_w_up, v_w_down, v_ln2_gamma, v_ln2_beta):
    T = x.shape[1]
    me = 4 * lax.axis_index("x") + 2 * lax.axis_index("y") + lax.axis_index("c")

    conv_shard = jnp.concatenate([_pad_rows(conv_a_w, 8), _pad_rows(conv_b_w, 32)], axis=0)
    p, xb, win_g, conv_g = _inproj_gather(x[0], w_in.astype(BF), conv_shard)
    conv_full = jnp.transpose(conv_g, (1, 0, 2)).reshape(40, D)
    vecs = jnp.stack([conv_b_bias, ln_b_gamma, ln_b_beta, ln1_gamma, ln1_beta, ln2_gamma, ln2_beta,
                      jnp.zeros_like(ln2_beta)])
    whole = lambda g: jnp.transpose(g, (1, 0, 2)).reshape(D, -1)
    mixed, (wup_g, wdown_g, woa_g, wob_g, wo_g) = _mixer_fwd(
        p, conv_full[0:8], conv_full[8:40], vecs,
        [w_up.astype(BF), w_down.astype(BF), w_out_a.astype(BF), w_out_b.astype(BF), w_o.astype(BF)])

    grad_x, grads, small, landing = _local_step(
        p, xb, mixed, x[0], loss_target[0], whole(win_g), whole(wup_g), wdown_g.reshape(NDEV * 512, D),
        woa_g.reshape(D, D), wob_g.reshape(D, D), wo_g.reshape(D, D), conv_full[0:8], conv_full[8:40], vecs)
    g_win, g_wup, g_wdown, g_woa, g_wob, g_wo = grads
    l_wdown, l_wup, l_woa, l_wob, l_wo, l_win, small_g = landing

    loss = lax.psum(0.5 / D * jnp.sum(small[7]), ("x", "y", "c"))

    me1 = me.astype(jnp.int32).reshape(1)
    by_owner = lambda g32: g32.reshape(NDEV, D // NDEV, D)
    r_win = _adamw(w_in, m_w_in, v_w_in, g_win, l_win, me1, "adamw_w_in", 256)
    r_wup = _adamw(w_up, m_w_up, v_w_up, g_wup, l_wup, me1, "adamw_w_up", 256)
    r_wdown = _adamw(w_down, m_w_down, v_w_down, g_wdown, l_wdown, me1, "adamw_w_down", 256)
    r_woa = _adamw(w_out_a, m_w_out_a, v_w_out_a, by_owner(g_woa), l_woa, me1, "adamw_w_out_a", 128)
    r_wob = _adamw(w_out_b, m_w_out_b, v_w_out_b, by_owner(g_wob), l_wob, me1, "adamw_w_out_b", 128)
    r_wo = _adamw(w_o, m_w_o, v_w_o, by_owner(g_wo), l_wo, me1, "adamw_w_o", 128)

    row = lambda vec: vec.reshape(1, D)
    small_out = _adamw_small(
        small_g,
        [row(a) for a in (conv_b_bias, ln_b_gamma, ln_b_beta, ln1_gamma, ln1_beta, ln2_gamma, ln2_beta)],
        [row(a) for a in (m_conv_b_bias, m_ln_b_gamma, m_ln_b_beta, m_ln1_gamma, m_ln1_beta, m_ln2_gamma, m_ln2_beta)],
        [row(a) for a in (v_conv_b_bias, v_ln_b_gamma, v_ln_b_beta, v_ln1_gamma, v_ln1_beta, v_ln2_gamma, v_ln2_beta)],
        [conv_a_w, conv_b_w], [m_conv_a_w, m_conv_b_w], [v_conv_a_w, v_conv_b_w])
    r_vec = [[small_out[4 * i + q].reshape(D) for q in range(4)] for i in range(7)]
    r_conva, r_convb = small_out[28:32], small_out[32:36]

    per_weight = []
    for q in range(4):
        per_weight.append([
            r_win[q], r_conva[q], r_woa[q], r_convb[q],
            r_vec[0][q], r_vec[1][q], r_vec[2][q], r_wob[q], r_wo[q], r_vec[3][q], r_vec[4][q],
            r_wup[q], r_wdown[q], r_vec[5][q], r_vec[6][q]])
    return (loss, grad_x[None], *per_weight[0], *per_weight[1], *per_weight[2], *per_weight[3])
```

```python
import functools

import jax
import jax.numpy as jnp
from jax import lax
from jax.experimental import pallas as pl
from jax.experimental.pallas import tpu as pltpu

F32 = jnp.float32
BF = jnp.bfloat16
D = 1024
NDEV = 8
ALPHA = 2.0 ** 0.25
LN_EPS = 1e-5
KA, KB = 3, 31
HA, HB = 16, 32
HN = 8
RC = 64
LANES = 128
VMEM_LIMIT = 56 * 1024 * 1024
MESH = pl.DeviceIdType.MESH
ADAM_LR, ADAM_B1, ADAM_B2, ADAM_EPS, ADAM_WD, ADAM_STEP = 0.001, 0.9, 0.999, 1e-08, 0.01, 10

ANY_SPEC = pl.BlockSpec(memory_space=pl.ANY)
NT_DIMS = (((1,), (1,)), ((), ()))
TN_DIMS = (((0,), (0,)), ((), ()))


def _params(n_axes):
    return pltpu.CompilerParams(dimension_semantics=("arbitrary",) * n_axes, vmem_limit_bytes=VMEM_LIMIT)


def _sigmoid(v):
    return 0.5 * jnp.tanh(0.5 * v) + 0.5


def _ln_fwd(z):
    mu = jnp.mean(z, axis=-1, keepdims=True)
    zc = z - mu
    var = jnp.mean(zc * zc, axis=-1, keepdims=True)
    rstd = lax.rsqrt(var + LN_EPS)
    return zc * rstd, rstd


def _ln_bwd(dy, xhat, rstd, gamma):
    dxhat = dy * gamma
    m1 = jnp.mean(dxhat, axis=-1, keepdims=True)
    m2 = jnp.mean(dxhat * xhat, axis=-1, keepdims=True)
    return rstd * (dxhat - m1 - xhat * m2)


def _colsum(v):
    return jnp.sum(v, axis=0, keepdims=True)


class _TwoLevelGather:
    def __init__(self, ins, outs, send_sems, recv_sems, local_sems):
        self.ins, self.outs = ins, outs
        self.send_sems, self.recv_sems, self.local_sems = send_sems, recv_sems, local_sems
        x, y, c = lax.axis_index("x"), lax.axis_index("y"), lax.axis_index("c")
        self.me, self.sibling, self.c = (x, y, c), (x, y, 1 - c), c
        self.chips = [(1 - x, y), (x, 1 - y), (1 - x, 1 - y)]
        self.n = len(ins)

    @staticmethod
    def out_shape(arrs):
        return [jax.ShapeDtypeStruct((NDEV,) + a.shape, a.dtype) for a in arrs]

    @staticmethod
    def scratch(n):
        return [pltpu.SemaphoreType.DMA((n, 7)), pltpu.SemaphoreType.DMA((n, 7)), pltpu.SemaphoreType.DMA((n,))]

    def _copy(self, a, k, block, to, src=None):
        px, py, pc = block
        rows = self.outs[a].at[4 * px + 2 * py + pc]
        return pltpu.make_async_remote_copy(
            src_ref=rows if src is None else src, dst_ref=rows,
            send_sem=self.send_sems.at[a, k], recv_sem=self.recv_sems.at[a, k],
            device_id=to, device_id_type=MESH)

    def _mine(self, a):
        x, y, c = self.me
        return pltpu.make_async_copy(self.ins[a], self.outs[a].at[4 * x + 2 * y + c], self.local_sems.at[a])

    def _first(self, a):
        cps = [self._copy(a, 0, self.me, self.sibling, src=self.ins[a])]
        return cps + [self._copy(a, 1 + j, self.me, (*chip, self.c), src=self.ins[a]) for j, chip in enumerate(self.chips)]

    def _passed(self, a, j):
        return self._copy(a, 4 + j, (*self.chips[j], self.c), self.sibling)

    def start(self, diagonal=True):
        for a in range(self.n):
            self._mine(a).start()
        for a in range(self.n):
            for cp in self._first(a)[:4 if diagonal else 3]:
                cp.start()

    def start_diagonal(self):
        for a in range(self.n):
            self._first(a)[3].start()

    def wait_ici(self, j):
        for a in range(self.n):
            self._copy(a, 1 + j, (*self.chips[j], self.c), self.me).wait_recv()

    def pass_on(self, j):
        for a in range(self.n):
            self._passed(a, j).start()

    def wait_sibling(self):
        for a in range(self.n):
            self._copy(a, 0, self.sibling, self.me).wait_recv()

    def wait_passed(self, j):
        for a in range(self.n):
            self._copy(a, 4 + j, (*self.chips[j], 1 - self.c), self.me).wait_recv()

    def drain(self):
        for a in range(self.n):
            for cp in self._first(a) + [self._passed(a, j) for j in range(3)]:
                cp.wait_send()
            self._mine(a).wait()

    def forward(self):
        for j in range(3):
            self.wait_ici(j)
            self.pass_on(j)

    def finish(self):
        self.wait_sibling()
        for j in range(3):
            self.wait_passed(j)
        self.drain()


class _Push:
    def __init__(self, exch=(), gath=()):
        self.exch, self.gath = list(exch), list(gath)
        self.n = len(self.exch) + len(self.gath)

    def operands(self):
        return self.exch + self.gath

    def out_shape(self):
        return ([jax.ShapeDtypeStruct((NDEV - 1,) + a.shape[1:], a.dtype) for a in self.exch]
                + [jax.ShapeDtypeStruct((NDEV,) + a.shape, a.dtype) for a in self.gath])

    def scratch(self):
        return [pltpu.SemaphoreType.DMA((self.n, 7)), pltpu.SemaphoreType.DMA((self.n, 7)),
                pltpu.SemaphoreType.DMA((max(len(self.gath), 1),))]

    def copies(self, ins, outs, send_sems, recv_sems, local_sems):
        x, y, c = lax.axis_index("x"), lax.axis_index("y"), lax.axis_index("c")
        me = 4 * x + 2 * y + c
        ne = len(self.exch)
        remote = []
        for k in range(1, NDEV):
            px = 1 - x if k & 4 else x
            py = 1 - y if k & 2 else y
            pc = 1 - c if k & 1 else c
            for a in range(self.n):
                src = ins[a].at[4 * px + 2 * py + pc] if a < ne else ins[a]
                dst = outs[a].at[k - 1] if a < ne else outs[a].at[me]
                remote.append(pltpu.make_async_remote_copy(
                    src_ref=src, dst_ref=dst, send_sem=send_sems.at[a, k - 1], recv_sem=recv_sems.at[a, k - 1],
                    device_id=(px, py, pc), device_id_type=MESH))
        local = [pltpu.make_async_copy(ins[a], outs[a].at[me], local_sems.at[a - ne]) for a in range(ne, self.n)]
        return remote, local


def _push(exch=(), gath=()):
    return _Push(exch, gath)


def _pallas(body, *, name, grid, in_specs, out_specs, out_shape, args, scratch_shapes=(), push=None):
    ni, no, ns = len(in_specs), len(out_specs), len(scratch_shapes)
    if push is None:
        outs = pl.pallas_call(
            body, name=name, grid=grid, in_specs=in_specs, out_specs=out_specs, out_shape=out_shape,
            scratch_shapes=list(scratch_shapes), compiler_params=_params(len(grid)))(*args)
        return list(outs), []
    npush = push.n

    def wrapped(*refs):
        ins, pins = refs[:ni], refs[ni:ni + npush]
        outs, pouts = refs[ni + npush:ni + npush + no], refs[ni + npush + no:ni + 2 * npush + no]
        scr, sems = refs[ni + 2 * npush + no:ni + 2 * npush + no + ns], refs[ni + 2 * npush + no + ns:]
        first = functools.reduce(jnp.logical_and, [pl.program_id(d) == 0 for d in range(len(grid))])
        last = functools.reduce(jnp.logical_and, [pl.program_id(d) == grid[d] - 1 for d in range(len(grid))])
        remote, local = push.copies(pins, pouts, *sems)

        @pl.when(first)
        def _():
            for cp in local + remote:
                cp.start()

        body(*ins, *outs, *scr)

        @pl.when(last)
        def _():
            for cp in remote + local:
                cp.wait()

    outs = pl.pallas_call(
        wrapped, name=name, grid=grid,
        in_specs=list(in_specs) + [ANY_SPEC] * npush, out_specs=list(out_specs) + [ANY_SPEC] * npush,
        out_shape=list(out_shape) + push.out_shape(), scratch_shapes=list(scratch_shapes) + push.scratch(),
        compiler_params=_params(len(grid)))(*args, *push.operands())
    return list(outs[:no]), list(outs[no:])


def _with_gather(body, late, *, name, nsteps, in_specs, out_specs, out_shape, args, scratch_shapes=()):
    ni, no, ns, n = len(in_specs), len(out_specs), len(scratch_shapes), len(late)
    pass_step = (7 * nsteps) // 8
    if not late:
        outs = pl.pallas_call(
            body, name=name, grid=(nsteps,), in_specs=in_specs, out_specs=out_specs, out_shape=out_shape,
            scratch_shapes=list(scratch_shapes), compiler_params=_params(1))(*args)
        return list(outs), []

    def wrapped(*refs):
        ins, outs = refs[:ni], refs[ni + n:ni + n + no]
        scr = refs[ni + 2 * n + no:ni + 2 * n + no + ns]
        gather = _TwoLevelGather(refs[ni:ni + n], refs[ni + n + no:ni + 2 * n + no], *refs[ni + 2 * n + no + ns:])
        step = pl.program_id(0)
        pl.when(step == 0)(gather.start)
        pl.when(step == pass_step)(gather.forward)
        body(*ins, *outs, *scr)
        pl.when(step == nsteps - 1)(gather.finish)

    outs = pl.pallas_call(
        wrapped, name=name, grid=(nsteps,),
        in_specs=list(in_specs) + [ANY_SPEC] * n, out_specs=list(out_specs) + [ANY_SPEC] * n,
        out_shape=list(out_shape) + _TwoLevelGather.out_shape(late),
        scratch_shapes=list(scratch_shapes) + _TwoLevelGather.scratch(n),
        compiler_params=_params(1))(*args, *late)
    return list(outs[:no]), list(outs[no:])


def _inproj_gather(x, w_shard, conv_shard, tm=1024):
    T = x.shape[0]
    ni = T // tm
    bw = w_shard.shape[1]
    cx, cy, cc = lax.axis_index("x"), lax.axis_index("y"), lax.axis_index("c")
    blk = lambda px, py, pc: 4 * px + 2 * py + pc
    order = [blk(cx, cy, cc), blk(cx, cy, 1 - cc)]
    for chip in [(1 - cx, cy), (cx, 1 - cy), (1 - cx, 1 - cy)]:
        order += [blk(*chip, cc), blk(*chip, 1 - cc)]
    order = jnp.stack(order).astype(jnp.int32)

    def body(order_ref, x_ref, w_ref, conv_ref, p_ref, xb_ref, wing_ref, convg_ref, xbs, wbuf, wsem, *sems):
        gather = _TwoLevelGather([w_ref, conv_ref], [wing_ref, convg_ref], *sems)
        j, i = pl.program_id(0), pl.program_id(1)

        def load(src):
            cp = pltpu.make_async_copy(src, wbuf, wsem)
            cp.start()
            cp.wait()

        def arrival(jj):
            if jj == 0:
                gather.start(diagonal=False)
                load(w_ref)
                return
            if jj == 1:
                gather.wait_sibling()
            elif jj % 2 == 0:
                if jj == 2:
                    gather.start_diagonal()
                gather.wait_ici(jj // 2 - 1)
                gather.pass_on(jj // 2 - 1)
            else:
                gather.wait_passed(jj // 2 - 1)
            load(wing_ref.at[order_ref[jj]])

        for jj in range(NDEV):
            pl.when(jnp.logical_and(j == jj, i == 0))(functools.partial(arrival, jj))

        @pl.when(j == 0)
        def _():
            xb = x_ref[...].astype(BF)
            xbs[i] = xb
            xb_ref[...] = xb

        p_ref[...] = jnp.dot(xbs[i], wbuf[...], preferred_element_type=F32).astype(BF)

        @pl.when(jnp.logical_and(j == NDEV - 1, i == ni - 1))
        def _():
            gather.drain()

    rows_once = lambda j, i, o: (jnp.where(j == 0, i, ni - 1), 0)
    grid_spec = pltpu.PrefetchScalarGridSpec(
        num_scalar_prefetch=1, grid=(NDEV, ni),
        in_specs=[pl.BlockSpec((tm, D), rows_once), ANY_SPEC, ANY_SPEC],
        out_specs=[pl.BlockSpec((tm, bw), lambda j, i, o: (i, o[j])), pl.BlockSpec((tm, D), rows_once),
                   ANY_SPEC, ANY_SPEC],
        scratch_shapes=[pltpu.VMEM((ni, tm, D), BF), pltpu.VMEM((D, bw), BF), pltpu.SemaphoreType.DMA(())]
        + _TwoLevelGather.scratch(2))
    p, xb, win_g, conv_g = pl.pallas_call(
        body, name="inproj_gather", grid_spec=grid_spec,
        out_shape=[jax.ShapeDtypeStruct((T, NDEV * bw), BF), jax.ShapeDtypeStruct((T, D), BF)]
        + _TwoLevelGather.out_shape([w_shard, conv_shard]),
        compiler_params=_params(2))(order, x, w_shard, conv_shard)
    return p, xb, win_g, conv_g


def _mixer_fwd(p, wa, wb, vecs, late, tt=256):
    T = p.shape[0]
    nt = T // tt

    def body(ba, ca, va, vb, gb, ca_p, va_p, vb_p, gb_p, wa_ref, wb_ref, vec_ref,
             yapre_ref, conva_ref, xhat_ref, rstd_ref, u3_ref, cabuf, u0buf, u1buf, shu):
        first = pl.program_id(0) == 0
        f = lambda ref: ref[...].astype(F32)
        cabuf[0:HA, :] = jnp.where(first, 0.0, f(ca_p) * f(va_p))
        cabuf[HA:HA + tt, :] = f(ca) * f(va)
        u0buf[0:HB, :] = jnp.where(first, 0.0, f(vb_p) * _sigmoid(f(gb_p)))
        u0buf[HB:HB + tt, :] = f(vb) * _sigmoid(f(gb))

        def lane_body(cidx, carry):
            ls = pl.ds(pl.multiple_of(cidx * LANES, LANES), LANES)
            _shifted_copies(shu, u0buf, ls, tt + HB - 8)
            for r in range(tt // RC):
                acc = jnp.zeros((RC, LANES), F32)
                for k in range(KA):
                    acc = acc + wa_ref[k:k + 1, ls] * cabuf[pl.ds(HA - (KA - 1) + k + r * RC, RC), ls]
                conva_ref[pl.ds(r * RC, RC), ls] = acc
                acc = jnp.zeros((RC, LANES), F32)
                for k in range(KB):
                    acc = acc + wb_ref[k:k + 1, ls] * _tap(shu, u0buf, ls, HB - (KB - 1) + k + r * RC, RC)
                u1buf[pl.ds(r * RC, RC), ls] = acc
            return carry

        lax.fori_loop(0, D // LANES, lane_body, 0)
        yapre_ref[...] = (f(ba) * conva_ref[...]).astype(BF)
        xhat, rstd = _ln_fwd(u1buf[...] + vec_ref[0:1, :])
        xhat_ref[...] = xhat
        rstd_ref[...] = rstd
        u2 = xhat * vec_ref[1:2, :] + vec_ref[2:3, :]
        u3_ref[...] = (u2 * _sigmoid(u2)).astype(BF)

    full = lambda r: pl.BlockSpec((r, D), lambda i: (0, 0))
    tok = pl.BlockSpec((tt, D), lambda i: (i, 0))
    return _with_gather(
        body, late, name="mixer_fwd", nsteps=nt,
        in_specs=[_seg(tt, 0), _seg(tt, 1), _seg(tt, 2), _seg(tt, 3), _seg(tt, 4),
                  _prev(tt, HA, 1), _prev(tt, HA, 2), _prev(tt, HB, 3), _prev(tt, HB, 4),
                  full(8), full(32), full(8)],
        out_specs=[tok, tok, tok, pl.BlockSpec((tt, 1), lambda i: (i, 0)), tok],
        out_shape=[jax.ShapeDtypeStruct((T, D), BF), jax.ShapeDtypeStruct((T, D), F32),
                   jax.ShapeDtypeStruct((T, D), F32), jax.ShapeDtypeStruct((T, 1), F32),
                   jax.ShapeDtypeStruct((T, D), BF)],
        scratch_shapes=[pltpu.VMEM((HA + tt, D), F32), pltpu.VMEM((HB + tt, D), F32), pltpu.VMEM((tt, D), F32),
                        pltpu.VMEM((8, HB + tt, LANES), F32)],
        args=(p, p, p, p, p, p, p, p, p, wa, wb, vecs))


def _seg(tt, s):
    return pl.BlockSpec((tt, D), lambda i: (i, s))


def _prev(tt, h, s):
    return pl.BlockSpec((h, D), lambda i: (jnp.maximum(i * (tt // h) - 1, 0), s))


def _shifted_copies(shbuf, src, ls, n):
    for s in range(1, 8):
        shbuf[s, 0:n, :] = src[pl.ds(s, n), ls]


def _tap(shbuf, src, ls, off, rows):
    s, q = off % 8, off // 8
    if s == 0:
        return src[pl.ds(off, rows), ls]
    return shbuf[s, pl.ds(8 * q, rows), :]


def _post_mixer(yapre, u3, p, x, woa, wob, wo, vecs, late, tm=512):
    T = x.shape[0]

    def body(yapre_ref, u3_ref, ga_ref, gb_ref, x_ref, woa_ref, wob_ref, wo_ref, vec_ref,
             ya_ref, yb_ref, merged_ref, xhat_ref, rstd_ref, x1b_ref):
        ya = jnp.dot(yapre_ref[...], woa_ref[...], preferred_element_type=F32)
        yb = jnp.dot(u3_ref[...], wob_ref[...], preferred_element_type=F32)
        ya_ref[...] = ya.astype(BF)
        yb_ref[...] = yb.astype(BF)
        merged = (_sigmoid(ga_ref[...].astype(F32)) * ya + _sigmoid(gb_ref[...].astype(F32)) * yb).astype(BF)
        merged_ref[...] = merged
        mix = jnp.dot(merged, wo_ref[...], preferred_element_type=F32)
        xhat, rstd = _ln_fwd(ALPHA * x_ref[...] + mix)
        xhat_ref[...] = xhat
        rstd_ref[...] = rstd
        x1b_ref[...] = (xhat * vec_ref[3:4, :] + vec_ref[4:5, :]).astype(BF)

    tok = pl.BlockSpec((tm, D), lambda i: (i, 0))
    wfull = _resident((D, D))
    one = pl.BlockSpec((tm, 1), lambda i: (i, 0))
    return _with_gather(
        body, late, name="post_mixer", nsteps=T // tm,
        in_specs=[tok, tok, _seg(tm, 5), _seg(tm, 6), tok, wfull, wfull, wfull, pl.BlockSpec((8, D), lambda i: (0, 0))],
        out_specs=[tok, tok, tok, tok, one, tok],
        out_shape=[jax.ShapeDtypeStruct((T, D), BF), jax.ShapeDtypeStruct((T, D), BF),
                   jax.ShapeDtypeStruct((T, D), BF), jax.ShapeDtypeStruct((T, D), F32),
                   jax.ShapeDtypeStruct((T, 1), F32), jax.ShapeDtypeStruct((T, D), BF)],
        args=(yapre, u3, p, p, x, woa, wob, wo, vecs))


def _mlp_up(x1b, wup, tm=512, tn=2048):
    T = x1b.shape[0]
    dff = wup.shape[1]

    def body(x_ref, w_ref, r_ref, h_ref):
        r = jnp.maximum(jnp.dot(x_ref[...], w_ref[...], preferred_element_type=F32), 0.0)
        r_ref[...] = r.astype(BF)
        h_ref[...] = (r * r).astype(BF)

    out = pl.BlockSpec((tm, tn), lambda j, i: (i, j))
    return pl.pallas_call(
        body, name="mlp_up", grid=(dff // tn, T // tm),
        in_specs=[pl.BlockSpec((tm, D), lambda j, i: (i, 0)), pl.BlockSpec((D, tn), lambda j, i: (0, j))],
        out_specs=[out, out],
        out_shape=[jax.ShapeDtypeStruct((T, dff), BF), jax.ShapeDtypeStruct((T, dff), BF)],
        compiler_params=_params(2))(x1b, wup)


def _resident(shape):
    return pl.BlockSpec(shape, lambda *_: (0,) * len(shape), pipeline_mode=pl.Buffered(1))


def _mlp_down_loss(h, wdown, xhat1, target, vecs, tm=512):
    T, dff = h.shape

    def body(h_ref, w_ref, xhat1_ref, tgt_ref, vec_ref, dz2_ref, dz2b_ref, st_ref):
        @pl.when(pl.program_id(0) == 0)
        def _():
            st_ref[...] = jnp.zeros_like(st_ref)

        ff = jnp.dot(h_ref[...], w_ref[...], preferred_element_type=F32)
        x1 = xhat1_ref[...] * vec_ref[3:4, :] + vec_ref[4:5, :]
        xhat2, rstd2 = _ln_fwd(ALPHA * x1 + ff)
        g2 = vec_ref[5:6, :]
        diff = xhat2 * g2 + vec_ref[6:7, :] - tgt_ref[...]
        dx2 = diff * (1.0 / D)
        st_ref[0:1, :] += _colsum(dx2 * xhat2)
        st_ref[1:2, :] += _colsum(dx2)
        st_ref[2:3, :] += _colsum(diff * diff)
        dz2 = _ln_bwd(dx2, xhat2, rstd2, g2)
        dz2_ref[...] = dz2
        dz2b_ref[...] = dz2.astype(BF)

    tok = pl.BlockSpec((tm, D), lambda i: (i, 0))
    vec = pl.BlockSpec((8, D), lambda i: (0, 0))
    return pl.pallas_call(
        body, name="mlp_down_loss", grid=(T // tm,),
        in_specs=[pl.BlockSpec((tm, dff), lambda i: (i, 0)), _resident((dff, D)), tok, tok, vec],
        out_specs=[tok, tok, vec],
        out_shape=[jax.ShapeDtypeStruct((T, D), F32), jax.ShapeDtypeStruct((T, D), BF), jax.ShapeDtypeStruct((8, D), F32)],
        compiler_params=_params(1))(h, wdown, xhat1, target, vecs)


def _mlp_down_bwd(dz2b, wdown, r, tm=512, tk=2048):
    T, dff = r.shape

    def body(dz_ref, w_ref, r_ref, o_ref):
        dh = lax.dot_general(dz_ref[...], w_ref[...], NT_DIMS, preferred_element_type=F32)
        o_ref[...] = (dh * (2.0 * r_ref[...].astype(F32))).astype(BF)

    blk = pl.BlockSpec((tm, tk), lambda j, i: (i, j))
    return pl.pallas_call(
        body, name="mlp_down_bwd", grid=(dff // tk, T // tm),
        in_specs=[pl.BlockSpec((tm, D), lambda j, i: (i, 0)), pl.BlockSpec((tk, D), lambda j, i: (j, 0)), blk],
        out_specs=blk,
        out_shape=jax.ShapeDtypeStruct((T, dff), BF),
        compiler_params=_params(2))(dz2b, wdown, r)


def _tn_matmul(a, b, nblk, a_bw, b_bw, a_blocked, b_blocked, name, tt=2048):
    T = a.shape[0]
    nt = T // tt

    def body(a_ref, b_ref, o32_ref, o16_ref):
        t = pl.program_id(1)

        @pl.when(t == 0)
        def _():
            o32_ref[...] = jnp.zeros_like(o32_ref)

        o32_ref[0] += lax.dot_general(a_ref[...], b_ref[...], TN_DIMS, preferred_element_type=F32)

        @pl.when(t == nt - 1)
        def _():
            o16_ref[...] = o32_ref[...].astype(BF)

    a_spec = pl.BlockSpec((tt, a_bw), (lambda j, t: (t, j)) if a_blocked else (lambda j, t: (t, 0)))
    b_spec = pl.BlockSpec((tt, b_bw), (lambda j, t: (t, j)) if b_blocked else (lambda j, t: (t, 0)))
    out = pl.BlockSpec((1, a_bw, b_bw), lambda j, t: (j, 0, 0))
    return pl.pallas_call(
        body, name=name, grid=(nblk, nt),
        in_specs=[a_spec, b_spec], out_specs=[out, out],
        out_shape=[jax.ShapeDtypeStruct((nblk, a_bw, b_bw), F32), jax.ShapeDtypeStruct((nblk, a_bw, b_bw), BF)],
        compiler_params=_params(2))(a, b)


def _mlp_up_bwd(dhpre, wup, dz2, xhat1, rstd1, vecs, push, tm=512):
    T, dff = dhpre.shape

    def body(dh_ref, w_ref, dz2_ref, xhat_ref, rstd_ref, vec_ref, dz1_ref, dz1b_ref, st_ref):
        @pl.when(pl.program_id(0) == 0)
        def _():
            st_ref[...] = jnp.zeros_like(st_ref)

        dx1 = lax.dot_general(dh_ref[...], w_ref[...], NT_DIMS, preferred_element_type=F32) + ALPHA * dz2_ref[...]
        xhat = xhat_ref[...]
        st_ref[0:1, :] += _colsum(dx1 * xhat)
        st_ref[1:2, :] += _colsum(dx1)
        dz1 = _ln_bwd(dx1, xhat, rstd_ref[...], vec_ref[3:4, :])
        dz1_ref[...] = dz1
        dz1b_ref[...] = dz1.astype(BF)

    tok = pl.BlockSpec((tm, D), lambda i: (i, 0))
    vec = pl.BlockSpec((8, D), lambda i: (0, 0))
    return _pallas(
        body, name="mlp_up_bwd", grid=(T // tm,),
        in_specs=[pl.BlockSpec((tm, dff), lambda i: (i, 0)), _resident((D, dff)),
                  tok, tok, pl.BlockSpec((tm, 1), lambda i: (i, 0)), vec],
        out_specs=[tok, tok, vec],
        out_shape=[jax.ShapeDtypeStruct((T, D), F32), jax.ShapeDtypeStruct((T, D), BF), jax.ShapeDtypeStruct((8, D), F32)],
        args=(dhpre, wup, dz2, xhat1, rstd1, vecs), push=push)


def _merge_bwd(dz1, p, ya, yb, conva, xhatb, rstdb, woa, wob, wo, vecs, push, tm=256):
    T = dz1.shape[0]

    def body(dz1_ref, ga_ref, gb_ref, ba_ref, ya_ref, yb_ref, conva_ref, xhat_ref, rstd_ref,
             woa_ref, wob_ref, wo_ref, vec_ref,
             dya_ref, dyb_ref, dg_ref, dba_ref, dconva_ref, du1_ref, st_ref):
        @pl.when(pl.program_id(0) == 0)
        def _():
            st_ref[...] = jnp.zeros_like(st_ref)

        dmerged = lax.dot_general(dz1_ref[...], wo_ref[...], NT_DIMS, preferred_element_type=F32)
        sa, sb = _sigmoid(ga_ref[...].astype(F32)), _sigmoid(gb_ref[...].astype(F32))
        dya = (dmerged * sa).astype(BF)
        dyb = (dmerged * sb).astype(BF)
        dya_ref[...] = dya
        dyb_ref[...] = dyb
        dg_ref[:, 0:D] = (dmerged * ya_ref[...].astype(F32) * (sa * (1.0 - sa))).astype(BF)
        dg_ref[:, D:2 * D] = (dmerged * yb_ref[...].astype(F32) * (sb * (1.0 - sb))).astype(BF)

        dyapre = lax.dot_general(dya, woa_ref[...], NT_DIMS, preferred_element_type=F32)
        dba_ref[...] = (dyapre * conva_ref[...]).astype(BF)
        dconva_ref[...] = dyapre * ba_ref[...].astype(F32)

        du3 = lax.dot_general(dyb, wob_ref[...], NT_DIMS, preferred_element_type=F32)
        xhat = xhat_ref[...]
        gamma = vec_ref[1:2, :]
        u2 = xhat * gamma + vec_ref[2:3, :]
        s = _sigmoid(u2)
        du2 = du3 * (s * (1.0 + u2 * (1.0 - s)))
        st_ref[0:1, :] += _colsum(du2 * xhat)
        st_ref[1:2, :] += _colsum(du2)
        du1 = _ln_bwd(du2, xhat, rstd_ref[...], gamma)
        st_ref[2:3, :] += _colsum(du1)
        du1_ref[...] = du1

    tok = pl.BlockSpec((tm, D), lambda i: (i, 0))
    wfull = pl.BlockSpec((D, D), lambda i: (0, 0))
    vec = pl.BlockSpec((8, D), lambda i: (0, 0))
    return _pallas(
        body, name="merge_bwd", grid=(T // tm,),
        in_specs=[tok, _seg(tm, 5), _seg(tm, 6), _seg(tm, 0), tok, tok, tok, tok, pl.BlockSpec((tm, 1), lambda i: (i, 0)),
                  wfull, wfull, wfull, vec],
        out_specs=[tok, tok, pl.BlockSpec((tm, 2 * D), lambda i: (i, 0)), tok, tok, tok, vec],
        out_shape=[jax.ShapeDtypeStruct((T, D), BF), jax.ShapeDtypeStruct((T, D), BF),
                   jax.ShapeDtypeStruct((T, 2 * D), BF), jax.ShapeDtypeStruct((T, D), BF),
                   jax.ShapeDtypeStruct((T, D), F32), jax.ShapeDtypeStruct((T, D), F32),
                   jax.ShapeDtypeStruct((8, D), F32)],
        args=(dz1, p, p, p, ya, yb, conva, xhatb, rstdb, woa, wob, wo, vecs), push=push)


def _rows8(v):
    out = v[0:8]
    for q in range(1, RC // 8):
        out = out + v[8 * q:8 * q + 8]
    return out


def _conv_bwd(dconva, du1, p, dba, dg, wa, wb, push, tt=256):
    T = p.shape[0]
    nsteps = T // tt

    def body(dca_ref, dca_n, du1_ref, du1_n, ca, va, vb, gb, dba_ref, dg_ref, wa_ref, wb_ref,
             dp_ref, gw_ref, cabuf, u0buf, dcabuf, du1buf, dcain, du0, gwa, gwb, shd):
        i = pl.program_id(0)
        first, last = i == 0, i == nsteps - 1

        @pl.when(first)
        def _():
            gwa[...] = jnp.zeros_like(gwa)
            gwb[...] = jnp.zeros_like(gwb)

        f = lambda ref: ref[...].astype(F32)
        cav, vav, vbv = f(ca), f(va), f(vb)
        cabuf[...] = cav * vav
        sg = _sigmoid(f(gb))
        u0buf[...] = vbv * sg
        dcabuf[0:tt, :] = dca_ref[...]
        dcabuf[tt:tt + HN, :] = jnp.where(last, 0.0, dca_n[...])
        du1buf[0:tt, :] = du1_ref[...]
        du1buf[tt:tt + HB, :] = jnp.where(last, 0.0, du1_n[...])

        def lane_body(cidx, carry):
            ls = pl.ds(pl.multiple_of(cidx * LANES, LANES), LANES)
            _shifted_copies(shd, du1buf, ls, tt + HB - 8)
            for r in range(tt // RC):
                rows = pl.ds(r * RC, RC)
                cin = cabuf[rows, ls]
                acc = jnp.zeros((RC, LANES), F32)
                for k in range(KA):
                    dout = dcabuf[pl.ds(r * RC + KA - 1 - k, RC), ls]
                    acc = acc + wa_ref[k:k + 1, ls] * dout
                    gwa[8 * k:8 * k + 8, ls] += _rows8(cin * dout)
                dcain[rows, ls] = acc
                uin = u0buf[rows, ls]
                acc = jnp.zeros((RC, LANES), F32)
                for k in range(KB):
                    dout = _tap(shd, du1buf, ls, r * RC + KB - 1 - k, RC)
                    acc = acc + wb_ref[k:k + 1, ls] * dout
                    gwb[8 * k:8 * k + 8, ls] += _rows8(uin * dout)
                du0[rows, ls] = acc
            return carry

        lax.fori_loop(0, D // LANES, lane_body, 0)
        dca_in = dcain[...]
        du0v = du0[...]
        dp_ref[:, 0:D] = dba_ref[...]
        dp_ref[:, D:2 * D] = (dca_in * vav).astype(BF)
        dp_ref[:, 2 * D:3 * D] = (dca_in * cav).astype(BF)
        dp_ref[:, 3 * D:4 * D] = (du0v * sg).astype(BF)
        dp_ref[:, 4 * D:5 * D] = (du0v * vbv * (sg * (1.0 - sg))).astype(BF)
        dp_ref[:, 5 * D:7 * D] = dg_ref[...]

        @pl.when(last)
        def _():
            gw_ref[...] = jnp.zeros_like(gw_ref)
            for k in range(KA):
                gw_ref[k:k + 1, :] = _colsum(gwa[8 * k:8 * k + 8, :])
            for k in range(KB):
                gw_ref[8 + k:9 + k, :] = _colsum(gwb[8 * k:8 * k + 8, :])

    full = lambda r: pl.BlockSpec((r, D), lambda i: (0, 0))
    tok = pl.BlockSpec((tt, D), lambda i: (i, 0))
    nxt = lambda h: pl.BlockSpec((h, D), lambda i: (jnp.minimum((i + 1) * (tt // h), T // h - 1), 0))
    return _pallas(
        body, name="conv_bwd", grid=(nsteps,),
        in_specs=[tok, nxt(HN), tok, nxt(HB),
                  _seg(tt, 1), _seg(tt, 2), _seg(tt, 3), _seg(tt, 4),
                  tok, pl.BlockSpec((tt, 2 * D), lambda i: (i, 0)), full(8), full(32)],
        out_specs=[pl.BlockSpec((tt, 7 * D), lambda i: (i, 0)), full(40)],
        out_shape=[jax.ShapeDtypeStruct((T, 7 * D), BF), jax.ShapeDtypeStruct((40, D), F32)],
        scratch_shapes=[pltpu.VMEM((tt, D), F32), pltpu.VMEM((tt, D), F32),
                        pltpu.VMEM((tt + HN, D), F32), pltpu.VMEM((tt + HB, D), F32),
                        pltpu.VMEM((tt, D), F32), pltpu.VMEM((tt, D), F32),
                        pltpu.VMEM((8 * KA, D), F32), pltpu.VMEM((8 * KB, D), F32),
                        pltpu.VMEM((8, HB + tt, LANES), F32)],
        args=(dconva, dconva, du1, du1, p, p, p, p, dba, dg, wa, wb), push=push)


def _inproj_bwd(dp, win, dz1, push, tm=512):
    T, cols = dp.shape

    def body(dp_ref, w_ref, dz1_ref, o_ref):
        o_ref[...] = ALPHA * dz1_ref[...] + lax.dot_general(dp_ref[...], w_ref[...], NT_DIMS,
                                                            preferred_element_type=F32)

    tok = pl.BlockSpec((tm, D), lambda i: (i, 0))
    return _pallas(
        body, name="inproj_bwd", grid=(T // tm,),
        in_specs=[pl.BlockSpec((tm, cols), lambda i: (i, 0)), _resident((D, cols)), tok],
        out_specs=[tok],
        out_shape=[jax.ShapeDtypeStruct((T, D), F32)],
        args=(dp, win, dz1), push=push)


def _adam_math(w, m, v, g):
    nm = ADAM_B1 * m + (1.0 - ADAM_B1) * g
    nv = ADAM_B2 * v + (1.0 - ADAM_B2) * (g * g)
    m_hat = nm / (1.0 - ADAM_B1 ** ADAM_STEP)
    v_hat = nv / (1.0 - ADAM_B2 ** ADAM_STEP)
    return -ADAM_LR * (m_hat / (jnp.sqrt(v_hat) + ADAM_EPS) + ADAM_WD * w), nm, nv


def _adamw(w, m, v, g32, landing, me, name, rb):
    R, C = w.shape
    nl = landing.shape[0]

    def body(me_ref, w_ref, m_ref, v_ref, own_ref, l_ref, g_ref, d_ref, nm_ref, nv_ref):
        g = own_ref[0]
        for k in range(nl):
            g = g + l_ref[k].astype(F32)
        g_ref[...] = g
        d_ref[...], nm_ref[...], nv_ref[...] = _adam_math(w_ref[...], m_ref[...], v_ref[...], g)

    blk = pl.BlockSpec((rb, C), lambda i, me_ref: (i, 0))
    grid_spec = pltpu.PrefetchScalarGridSpec(
        num_scalar_prefetch=1, grid=(R // rb,),
        in_specs=[blk, blk, blk, pl.BlockSpec((1, rb, C), lambda i, me_ref: (me_ref[0], i, 0)),
                  pl.BlockSpec((nl, rb, C), lambda i, me_ref: (0, i, 0))],
        out_specs=[blk] * 4)
    return pl.pallas_call(
        body, name=name, grid_spec=grid_spec, out_shape=[jax.ShapeDtypeStruct((R, C), F32)] * 4,
        compiler_params=_params(1))(me, w, m, v, g32, landing)


def _adamw_small(small_g, vec_w, vec_m, vec_v, conv_w, conv_m, conv_v):
    nv_ = len(vec_w)
    conv_rows = [(8, KA), (16, KB)]

    def body(*refs):
        g_ref = refs[0]
        w_refs, m_refs, v_refs = refs[1:10], refs[10:19], refs[19:28]
        out_refs, gsum = refs[28:64], refs[64]
        acc = g_ref[0]
        for j in range(1, NDEV):
            acc = acc + g_ref[j]
        gsum[...] = acc
        me = 4 * lax.axis_index("x") + 2 * lax.axis_index("y") + lax.axis_index("c")
        cols = pl.ds(pl.multiple_of(me * LANES, LANES), LANES)
        for i in range(nv_ + 2):
            if i < nv_:
                g = gsum[i:i + 1, :]
            else:
                r0, k = conv_rows[i - nv_]
                g = gsum[r0:r0 + k, cols]
            o = out_refs[4 * i:4 * i + 4]
            o[0][...] = g
            o[1][...], o[2][...], o[3][...] = _adam_math(w_refs[i][...], m_refs[i][...], v_refs[i][...], g)

    ws, ms, vs = list(vec_w) + list(conv_w), list(vec_m) + list(conv_m), list(vec_v) + list(conv_v)
    out_shape = [jax.ShapeDtypeStruct(w.shape, F32) for w in ws for _ in range(4)]
    return pl.pallas_call(
        body, name="adamw_small", out_shape=out_shape,
        scratch_shapes=[pltpu.VMEM(small_g.shape[1:], F32)])(small_g, *ws, *ms, *vs)


def _pad_rows(a, rows):
    return jnp.pad(a, ((0, rows - a.shape[0]), (0, 0)))


def _local_step(p, xb, mixed, post, x, target, win, wup, wdown, woa, wob, wo, wa, wb, vecs):
    yapre, conva, xhatb, rstdb, u3 = mixed
    ya, yb, merged, xhat1, rstd1, x1b = post
    r, h = _mlp_up(x1b, wup)
    dz2, dz2b, st2 = _mlp_down_loss(h, wdown, xhat1, target, vecs)

    by_owner = lambda g16: g16.reshape(NDEV, D // NDEV, D)
    dhpre = _mlp_down_bwd(dz2b, wdown, r)
    g_wdown = _tn_matmul(h, dz2b, NDEV, 512, D, True, False, "grad_w_down")
    (dz1, dz1b, st1), land_wdown = _mlp_up_bwd(dhpre, wup, dz2, xhat1, rstd1, vecs, _push(exch=[g_wdown[1]]))
    g_wup = _tn_matmul(x1b, dhpre, NDEV, D, 512, False, True, "grad_w_up")
    (dya, dyb, dg, dba, dconva, du1, stb), land_wup = _merge_bwd(
        dz1b, p, ya, yb, conva, xhatb, rstdb, woa, wob, wo, vecs, _push(exch=[g_wup[1]]))
    g_wo = _tn_matmul(merged, dz1b, 1, D, D, False, False, "grad_w_o")
    g_woa = _tn_matmul(yapre, dya, 1, D, D, False, False, "grad_w_out_a")
    g_wob = _tn_matmul(u3, dyb, 1, D, D, False, False, "grad_w_out_b")
    (dp, gw), land_sq = _conv_bwd(dconva, du1, p, dba, dg, wa, wb,
                                  _push(exch=[by_owner(g_woa[1]), by_owner(g_wob[1]), by_owner(g_wo[1])]))
    g_win = _tn_matmul(xb, dp, NDEV, D, 896, False, True, "grad_w_in")

    small = jnp.concatenate([stb[2:3], stb[0:2], st1[0:2], st2[0:3], gw], axis=0)
    (grad_x,), land_last = _inproj_bwd(dp, win, dz1, _push(exch=[g_win[1]], gath=[small]))
    grads = (g_win[0], g_wup[0], g_wdown[0], g_woa[0], g_wob[0], g_wo[0])
    return grad_x, grads, small, land_wdown + land_wup + land_sq + land_last


def kernel(x, w_in, conv_a_w, w_out_a, conv_b_w, conv_b_bias, ln_b_gamma, ln_b_beta, w_out_b, w_o, ln1_gamma, ln1_beta, w_up, w_down, ln2_gamma, ln2_beta, loss_target, m_w_in, m_conv_a_w, m_w_out_a, m_conv_b_w, m_conv_b_bias, m_ln_b_gamma, m_ln_b_beta, m_w_out_b, m_w_o, m_ln1_gamma, m_ln1_beta, m_w_up, m_w_down, m_ln2_gamma, m_ln2_beta, v_w_in, v_conv_a_w, v_w_out_a, v_conv_b_w, v_conv_b_bias, v_ln_b_gamma, v_ln_b_beta, v_w_out_b, v_w_o, v_ln1_gamma, v_ln1_beta, v_w_up, v_w_down, v_ln2_gamma, v_ln2_beta):
    T = x.shape[1]
    me = 4 * lax.axis_index("x") + 2 * lax.axis_index("y") + lax.axis_index("c")

    conv_shard = jnp.concatenate([_pad_rows(conv_a_w, 8), _pad_rows(conv_b_w, 32)], axis=0)
    p, xb, win_g, conv_g = _inproj_gather(x[0], w_in.astype(BF), conv_shard)
    conv_full = jnp.transpose(conv_g, (1, 0, 2)).reshape(40, D)
    vecs = jnp.stack([conv_b_bias, ln_b_gamma, ln_b_beta, ln1_gamma, ln1_beta, ln2_gamma, ln2_beta,
                      jnp.zeros_like(ln2_beta)])
    whole = lambda g: jnp.transpose(g, (1, 0, 2)).reshape(D, -1)
    mixed, (woa_g, wob_g, wo_g, wup_g, wdown_g) = _mixer_fwd(
        p, conv_full[0:8], conv_full[8:40], vecs,
        [w_out_a.astype(BF), w_out_b.astype(BF), w_o.astype(BF), w_up.astype(BF), w_down.astype(BF)])
    woa, wob, wo = woa_g.reshape(D, D), wob_g.reshape(D, D), wo_g.reshape(D, D)
    post, _ = _post_mixer(mixed[0], mixed[4], p, x[0], woa, wob, wo, vecs, [])

    grad_x, grads, small, landing = _local_step(
        p, xb, mixed, post, x[0], loss_target[0], whole(win_g), whole(wup_g), wdown_g.reshape(NDEV * 512, D),
        woa, wob, wo, conv_full[0:8], conv_full[8:40], vecs)
    g_win, g_wup, g_wdown, g_woa, g_wob, g_wo = grads
    l_wdown, l_wup, l_woa, l_wob, l_wo, l_win, small_g = landing

    loss = lax.psum(0.5 / D * jnp.sum(small[7]), ("x", "y", "c"))

    me1 = me.astype(jnp.int32).reshape(1)
    by_owner = lambda g32: g32.reshape(NDEV, D // NDEV, D)
    r_win = _adamw(w_in, m_w_in, v_w_in, g_win, l_win, me1, "adamw_w_in", 256)
    r_wup = _adamw(w_up, m_w_up, v_w_up, g_wup, l_wup, me1, "adamw_w_up", 256)
    r_wdown = _adamw(w_down, m_w_down, v_w_down, g_wdown, l_wdown, me1, "adamw_w_down", 256)
    r_woa = _adamw(w_out_a, m_w_out_a, v_w_out_a, by_owner(g_woa), l_woa, me1, "adamw_w_out_a", 128)
    r_wob = _adamw(w_out_b, m_w_out_b, v_w_out_b, by_owner(g_wob), l_wob, me1, "adamw_w_out_b", 128)
    r_wo = _adamw(w_o, m_w_o, v_w_o, by_owner(g_wo), l_wo, me1, "adamw_w_o", 128)

    row = lambda vec: vec.reshape(1, D)
    small_out = _adamw_small(
        small_g,
        [row(a) for a in (conv_b_bias, ln_b_gamma, ln_b_beta, ln1_gamma, ln1_beta, ln2_gamma, ln2_beta)],
        [row(a) for a in (m_conv_b_bias, m_ln_b_gamma, m_ln_b_beta, m_ln1_gamma, m_ln1_beta, m_ln2_gamma, m_ln2_beta)],
        [row(a) for a in (v_conv_b_bias, v_ln_b_gamma, v_ln_b_beta, v_ln1_gamma, v_ln1_beta, v_ln2_gamma, v_ln2_beta)],
        [conv_a_w, conv_b_w], [m_conv_a_w, m_conv_b_w], [v_conv_a_w, v_conv_b_w])
    r_vec = [[small_out[4 * i + q].reshape(D) for q in range(4)] for i in range(7)]
    r_conva, r_convb = small_out[28:32], small_out[32:36]

    per_weight = []
    for q in range(4):
        per_weight.append([
            r_win[q], r_conva[q], r_woa[q], r_convb[q],
            r_vec[0][q], r_vec[1][q], r_vec[2][q], r_wob[q], r_wo[q], r_vec[3][q], r_vec[4][q],
            r_wup[q], r_wdown[q], r_vec[5][q], r_vec[6][q]])
    return (loss, grad_x[None], *per_weight[0], *per_weight[1], *per_weight[2], *per_weight[3])
```

```python
import functools

import jax
import jax.numpy as jnp
from jax import lax
from jax.experimental import pallas as pl
from jax.experimental.pallas import tpu as pltpu

F32 = jnp.float32
BF = jnp.bfloat16
D = 1024
NDEV = 8
ALPHA = 2.0 ** 0.25
LN_EPS = 1e-5
KA, KB = 3, 31
HA, HB = 16, 32
HN = 8
RC = 64
LANES = 128
VMEM_LIMIT = 56 * 1024 * 1024
MESH = pl.DeviceIdType.MESH
ADAM_LR, ADAM_B1, ADAM_B2, ADAM_EPS, ADAM_WD, ADAM_STEP = 0.001, 0.9, 0.999, 1e-08, 0.01, 10

ANY_SPEC = pl.BlockSpec(memory_space=pl.ANY)
NT_DIMS = (((1,), (1,)), ((), ()))
TN_DIMS = (((0,), (0,)), ((), ()))


def _params(n_axes):
    return pltpu.CompilerParams(dimension_semantics=("arbitrary",) * n_axes, vmem_limit_bytes=VMEM_LIMIT)


def _sigmoid(v):
    return 0.5 * jnp.tanh(0.5 * v) + 0.5


def _ln_fwd(z):
    mu = jnp.mean(z, axis=-1, keepdims=True)
    zc = z - mu
    var = jnp.mean(zc * zc, axis=-1, keepdims=True)
    rstd = lax.rsqrt(var + LN_EPS)
    return zc * rstd, rstd


def _ln_bwd(dy, xhat, rstd, gamma):
    dxhat = dy * gamma
    m1 = jnp.mean(dxhat, axis=-1, keepdims=True)
    m2 = jnp.mean(dxhat * xhat, axis=-1, keepdims=True)
    return rstd * (dxhat - m1 - xhat * m2)


def _colsum(v):
    return jnp.sum(v, axis=0, keepdims=True)


class _TwoLevelGather:
    def __init__(self, ins, outs, send_sems, recv_sems, local_sems):
        self.ins, self.outs = ins, outs
        self.send_sems, self.recv_sems, self.local_sems = send_sems, recv_sems, local_sems
        x, y, c = lax.axis_index("x"), lax.axis_index("y"), lax.axis_index("c")
        self.me, self.sibling, self.c = (x, y, c), (x, y, 1 - c), c
        self.chips = [(1 - x, y), (x, 1 - y), (1 - x, 1 - y)]
        self.n = len(ins)

    @staticmethod
    def out_shape(arrs):
        return [jax.ShapeDtypeStruct((NDEV,) + a.shape, a.dtype) for a in arrs]

    @staticmethod
    def scratch(n):
        return [pltpu.SemaphoreType.DMA((n, 7)), pltpu.SemaphoreType.DMA((n, 7)), pltpu.SemaphoreType.DMA((n,))]

    def _copy(self, a, k, block, to, src=None):
        px, py, pc = block
        rows = self.outs[a].at[4 * px + 2 * py + pc]
        return pltpu.make_async_remote_copy(
            src_ref=rows if src is None else src, dst_ref=rows,
            send_sem=self.send_sems.at[a, k], recv_sem=self.recv_sems.at[a, k],
            device_id=to, device_id_type=MESH)

    def _mine(self, a):
        x, y, c = self.me
        return pltpu.make_async_copy(self.ins[a], self.outs[a].at[4 * x + 2 * y + c], self.local_sems.at[a])

    def _first(self, a):
        cps = [self._copy(a, 0, self.me, self.sibling, src=self.ins[a])]
        return cps + [self._copy(a, 1 + j, self.me, (*chip, self.c), src=self.ins[a]) for j, chip in enumerate(self.chips)]

    def _passed(self, a, j):
        return self._copy(a, 4 + j, (*self.chips[j], self.c), self.sibling)

    def start(self, diagonal=True):
        for a in range(self.n):
            self._mine(a).start()
        for a in range(self.n):
            for cp in self._first(a)[:4 if diagonal else 3]:
                cp.start()

    def start_diagonal(self):
        for a in range(self.n):
            self._first(a)[3].start()

    def wait_ici(self, j):
        for a in range(self.n):
            self._copy(a, 1 + j, (*self.chips[j], self.c), self.me).wait_recv()

    def pass_on(self, j):
        for a in range(self.n):
            self._passed(a, j).start()

    def wait_sibling(self):
        for a in range(self.n):
            self._copy(a, 0, self.sibling, self.me).wait_recv()

    def wait_passed(self, j):
        for a in range(self.n):
            self._copy(a, 4 + j, (*self.chips[j], 1 - self.c), self.me).wait_recv()

    def drain(self):
        for a in range(self.n):
            for cp in self._first(a) + [self._passed(a, j) for j in range(3)]:
                cp.wait_send()
            self._mine(a).wait()

    def forward(self):
        for j in range(3):
            self.wait_ici(j)
            self.pass_on(j)

    def finish(self):
        self.wait_sibling()
        for j in range(3):
            self.wait_passed(j)
        self.drain()


class _Push:
    def __init__(self, exch=(), gath=()):
        self.exch, self.gath = list(exch), list(gath)
        self.n = len(self.exch) + len(self.gath)

    def operands(self):
        return self.exch + self.gath

    def out_shape(self):
        return ([jax.ShapeDtypeStruct((NDEV - 1,) + a.shape[1:], a.dtype) for a in self.exch]
                + [jax.ShapeDtypeStruct((NDEV,) + a.shape, a.dtype) for a in self.gath])

    def scratch(self):
        return [pltpu.SemaphoreType.DMA((self.n, 7)), pltpu.SemaphoreType.DMA((self.n, 7)),
                pltpu.SemaphoreType.DMA((max(len(self.gath), 1),))]

    def copies(self, ins, outs, send_sems, recv_sems, local_sems):
        x, y, c = lax.axis_index("x"), lax.axis_index("y"), lax.axis_index("c")
        me = 4 * x + 2 * y + c
        ne = len(self.exch)
        remote = []
        for k in range(1, NDEV):
            px = 1 - x if k & 4 else x
            py = 1 - y if k & 2 else y
            pc = 1 - c if k & 1 else c
            for a in range(self.n):
                src = ins[a].at[4 * px + 2 * py + pc] if a < ne else ins[a]
                dst = outs[a].at[k - 1] if a < ne else outs[a].at[me]
                remote.append(pltpu.make_async_remote_copy(
                    src_ref=src, dst_ref=dst, send_sem=send_sems.at[a, k - 1], recv_sem=recv_sems.at[a, k - 1],
                    device_id=(px, py, pc), device_id_type=MESH))
        local = [pltpu.make_async_copy(ins[a], outs[a].at[me], local_sems.at[a - ne]) for a in range(ne, self.n)]
        return remote, local


def _push(exch=(), gath=()):
    return _Push(exch, gath)


def _pallas(body, *, name, grid, in_specs, out_specs, out_shape, args, scratch_shapes=(), push=None):
    ni, no, ns = len(in_specs), len(out_specs), len(scratch_shapes)
    if push is None:
        outs = pl.pallas_call(
            body, name=name, grid=grid, in_specs=in_specs, out_specs=out_specs, out_shape=out_shape,
            scratch_shapes=list(scratch_shapes), compiler_params=_params(len(grid)))(*args)
        return list(outs), []
    npush = push.n

    def wrapped(*refs):
        ins, pins = refs[:ni], refs[ni:ni + npush]
        outs, pouts = refs[ni + npush:ni + npush + no], refs[ni + npush + no:ni + 2 * npush + no]
        scr, sems = refs[ni + 2 * npush + no:ni + 2 * npush + no + ns], refs[ni + 2 * npush + no + ns:]
        first = functools.reduce(jnp.logical_and, [pl.program_id(d) == 0 for d in range(len(grid))])
        last = functools.reduce(jnp.logical_and, [pl.program_id(d) == grid[d] - 1 for d in range(len(grid))])
        remote, local = push.copies(pins, pouts, *sems)

        @pl.when(first)
        def _():
            for cp in local + remote:
                cp.start()

        body(*ins, *outs, *scr)

        @pl.when(last)
        def _():
            for cp in remote + local:
                cp.wait()

    outs = pl.pallas_call(
        wrapped, name=name, grid=grid,
        in_specs=list(in_specs) + [ANY_SPEC] * npush, out_specs=list(out_specs) + [ANY_SPEC] * npush,
        out_shape=list(out_shape) + push.out_shape(), scratch_shapes=list(scratch_shapes) + push.scratch(),
        compiler_params=_params(len(grid)))(*args, *push.operands())
    return list(outs[:no]), list(outs[no:])


def _with_gather(body, late, *, name, nsteps, in_specs, out_specs, out_shape, args, scratch_shapes=()):
    ni, no, ns, n = len(in_specs), len(out_specs), len(scratch_shapes), len(late)
    pass_step = (7 * nsteps) // 8
    if not late:
        outs = pl.pallas_call(
            body, name=name, grid=(nsteps,), in_specs=in_specs, out_specs=out_specs, out_shape=out_shape,
            scratch_shapes=list(scratch_shapes), compiler_params=_params(1))(*args)
        return list(outs), []

    def wrapped(*refs):
        ins, outs = refs[:ni], refs[ni + n:ni + n + no]
        scr = refs[ni + 2 * n + no:ni + 2 * n + no + ns]
        gather = _TwoLevelGather(refs[ni:ni + n], refs[ni + n + no:ni + 2 * n + no], *refs[ni + 2 * n + no + ns:])
        step = pl.program_id(0)
        pl.when(step == 0)(gather.start)
        pl.when(step == pass_step)(gather.forward)
        body(*ins, *outs, *scr)
        pl.when(step == nsteps - 1)(gather.finish)

    outs = pl.pallas_call(
        wrapped, name=name, grid=(nsteps,),
        in_specs=list(in_specs) + [ANY_SPEC] * n, out_specs=list(out_specs) + [ANY_SPEC] * n,
        out_shape=list(out_shape) + _TwoLevelGather.out_shape(late),
        scratch_shapes=list(scratch_shapes) + _TwoLevelGather.scratch(n),
        compiler_params=_params(1))(*args, *late)
    return list(outs[:no]), list(outs[no:])


def _inproj_gather(x, w_shard, conv_shard, tm=1024):
    T = x.shape[0]
    ni = T // tm
    bw = w_shard.shape[1]
    cx, cy, cc = lax.axis_index("x"), lax.axis_index("y"), lax.axis_index("c")
    blk = lambda px, py, pc: 4 * px + 2 * py + pc
    order = [blk(cx, cy, cc), blk(cx, cy, 1 - cc)]
    for chip in [(1 - cx, cy), (cx, 1 - cy), (1 - cx, 1 - cy)]:
        order += [blk(*chip, cc), blk(*chip, 1 - cc)]
    order = jnp.stack(order).astype(jnp.int32)

    def body(order_ref, x_ref, w_ref, conv_ref, p_ref, xb_ref, wing_ref, convg_ref, xbs, wbuf, wsem, *sems):
        gather = _TwoLevelGather([w_ref, conv_ref], [wing_ref, convg_ref], *sems)
        j, i = pl.program_id(0), pl.program_id(1)

        def load(src):
            cp = pltpu.make_async_copy(src, wbuf, wsem)
            cp.start()
            cp.wait()

        def arrival(jj):
            if jj == 0:
                gather.start(diagonal=False)
                load(w_ref)
                return
            if jj == 1:
                gather.wait_sibling()
            elif jj % 2 == 0:
                if jj == 2:
                    gather.start_diagonal()
                gather.wait_ici(jj // 2 - 1)
                gather.pass_on(jj // 2 - 1)
            else:
                gather.wait_passed(jj // 2 - 1)
            load(wing_ref.at[order_ref[jj]])

        for jj in range(NDEV):
            pl.when(jnp.logical_and(j == jj, i == 0))(functools.partial(arrival, jj))

        @pl.when(j == 0)
        def _():
            xb = x_ref[...].astype(BF)
            xbs[i] = xb
            xb_ref[...] = xb

        p_ref[...] = jnp.dot(xbs[i], wbuf[...], preferred_element_type=F32).astype(BF)

        @pl.when(jnp.logical_and(j == NDEV - 1, i == ni - 1))
        def _():
            gather.drain()

    rows_once = lambda j, i, o: (jnp.where(j == 0, i, ni - 1), 0)
    grid_spec = pltpu.PrefetchScalarGridSpec(
        num_scalar_prefetch=1, grid=(NDEV, ni),
        in_specs=[pl.BlockSpec((tm, D), rows_once), ANY_SPEC, ANY_SPEC],
        out_specs=[pl.BlockSpec((tm, bw), lambda j, i, o: (i, o[j])), pl.BlockSpec((tm, D), rows_once),
                   ANY_SPEC, ANY_SPEC],
        scratch_shapes=[pltpu.VMEM((ni, tm, D), BF), pltpu.VMEM((D, bw), BF), pltpu.SemaphoreType.DMA(())]
        + _TwoLevelGather.scratch(2))
    p, xb, win_g, conv_g = pl.pallas_call(
        body, name="inproj_gather", grid_spec=grid_spec,
        out_shape=[jax.ShapeDtypeStruct((T, NDEV * bw), BF), jax.ShapeDtypeStruct((T, D), BF)]
        + _TwoLevelGather.out_shape([w_shard, conv_shard]),
        compiler_params=_params(2))(order, x, w_shard, conv_shard)
    return p, xb, win_g, conv_g


def _mixer_fwd(p, wa, wb, vecs, late, tt=256):
    T = p.shape[0]
    nt = T // tt

    def body(ba, ca, va, vb, gb, ca_p, va_p, vb_p, gb_p, wa_ref, wb_ref, vec_ref,
             yapre_ref, conva_ref, xhat_ref, rstd_ref, u3_ref, cabuf, u0buf, u1buf, shu):
        first = pl.program_id(0) == 0
        f = lambda ref: ref[...].astype(F32)
        cabuf[0:HA, :] = jnp.where(first, 0.0, f(ca_p) * f(va_p))
        cabuf[HA:HA + tt, :] = f(ca) * f(va)
        u0buf[0:HB, :] = jnp.where(first, 0.0, f(vb_p) * _sigmoid(f(gb_p)))
        u0buf[HB:HB + tt, :] = f(vb) * _sigmoid(f(gb))

        def lane_body(cidx, carry):
            ls = pl.ds(pl.multiple_of(cidx * LANES, LANES), LANES)
            _shifted_copies(shu, u0buf, ls, tt + HB - 8)
            for r in range(tt // RC):
                acc = jnp.zeros((RC, LANES), F32)
                for k in range(KA):
                    acc = acc + wa_ref[k:k + 1, ls] * cabuf[pl.ds(HA - (KA - 1) + k + r * RC, RC), ls]
                conva_ref[pl.ds(r * RC, RC), ls] = acc
                acc = jnp.zeros((RC, LANES), F32)
                for k in range(KB):
                    acc = acc + wb_ref[k:k + 1, ls] * _tap(shu, u0buf, ls, HB - (KB - 1) + k + r * RC, RC)
                u1buf[pl.ds(r * RC, RC), ls] = acc
            return carry

        lax.fori_loop(0, D // LANES, lane_body, 0)
        yapre_ref[...] = (f(ba) * conva_ref[...]).astype(BF)
        xhat, rstd = _ln_fwd(u1buf[...] + vec_ref[0:1, :])
        xhat_ref[...] = xhat
        rstd_ref[...] = rstd
        u2 = xhat * vec_ref[1:2, :] + vec_ref[2:3, :]
        u3_ref[...] = (u2 * _sigmoid(u2)).astype(BF)

    full = lambda r: pl.BlockSpec((r, D), lambda i: (0, 0))
    tok = pl.BlockSpec((tt, D), lambda i: (i, 0))
    return _with_gather(
        body, late, name="mixer_fwd", nsteps=nt,
        in_specs=[_seg(tt, 0), _seg(tt, 1), _seg(tt, 2), _seg(tt, 3), _seg(tt, 4),
                  _prev(tt, HA, 1), _prev(tt, HA, 2), _prev(tt, HB, 3), _prev(tt, HB, 4),
                  full(8), full(32), full(8)],
        out_specs=[tok, tok, tok, pl.BlockSpec((tt, 1), lambda i: (i, 0)), tok],
        out_shape=[jax.ShapeDtypeStruct((T, D), BF), jax.ShapeDtypeStruct((T, D), F32),
                   jax.ShapeDtypeStruct((T, D), F32), jax.ShapeDtypeStruct((T, 1), F32),
                   jax.ShapeDtypeStruct((T, D), BF)],
        scratch_shapes=[pltpu.VMEM((HA + tt, D), F32), pltpu.VMEM((HB + tt, D), F32), pltpu.VMEM((tt, D), F32),
                        pltpu.VMEM((8, HB + tt, LANES), F32)],
        args=(p, p, p, p, p, p, p, p, p, wa, wb, vecs))


def _seg(tt, s):
    return pl.BlockSpec((tt, D), lambda i: (i, s))


def _prev(tt, h, s):
    return pl.BlockSpec((h, D), lambda i: (jnp.maximum(i * (tt // h) - 1, 0), s))


def _shifted_copies(shbuf, src, ls, n):
    for s in range(1, 8):
        shbuf[s, 0:n, :] = src[pl.ds(s, n), ls]


def _tap(shbuf, src, ls, off, rows):
    s, q = off % 8, off // 8
    if s == 0:
        return src[pl.ds(off, rows), ls]
    return shbuf[s, pl.ds(8 * q, rows), :]


def _post_mixer(yapre, u3, p, x, woa, wob, wo, vecs, late, tm=512):
    T = x.shape[0]

    def body(yapre_ref, u3_ref, ga_ref, gb_ref, x_ref, woa_ref, wob_ref, wo_ref, vec_ref,
             ya_ref, yb_ref, merged_ref, xhat_ref, rstd_ref, x1b_ref):
        ya = jnp.dot(yapre_ref[...], woa_ref[...], preferred_element_type=F32)
        yb = jnp.dot(u3_ref[...], wob_ref[...], preferred_element_type=F32)
        ya_ref[...] = ya.astype(BF)
        yb_ref[...] = yb.astype(BF)
        merged = (_sigmoid(ga_ref[...].astype(F32)) * ya + _sigmoid(gb_ref[...].astype(F32)) * yb).astype(BF)
        merged_ref[...] = merged
        mix = jnp.dot(merged, wo_ref[...], preferred_element_type=F32)
        xhat, rstd = _ln_fwd(ALPHA * x_ref[...] + mix)
        xhat_ref[...] = xhat
        rstd_ref[...] = rstd
        x1b_ref[...] = (xhat * vec_ref[3:4, :] + vec_ref[4:5, :]).astype(BF)

    tok = pl.BlockSpec((tm, D), lambda i: (i, 0))
    wfull = _resident((D, D))
    one = pl.BlockSpec((tm, 1), lambda i: (i, 0))
    return _with_gather(
        body, late, name="post_mixer", nsteps=T // tm,
        in_specs=[tok, tok, _seg(tm, 5), _seg(tm, 6), tok, wfull, wfull, wfull, pl.BlockSpec((8, D), lambda i: (0, 0))],
        out_specs=[tok, tok, tok, tok, one, tok],
        out_shape=[jax.ShapeDtypeStruct((T, D), BF), jax.ShapeDtypeStruct((T, D), BF),
                   jax.ShapeDtypeStruct((T, D), BF), jax.ShapeDtypeStruct((T, D), F32),
                   jax.ShapeDtypeStruct((T, 1), F32), jax.ShapeDtypeStruct((T, D), BF)],
        args=(yapre, u3, p, p, x, woa, wob, wo, vecs))


def _mlp_up(x1b, wup, tm=512, tn=2048):
    T = x1b.shape[0]
    dff = wup.shape[1]

    def body(x_ref, w_ref, h_ref):
        r = jnp.maximum(jnp.dot(x_ref[...], w_ref[...], preferred_element_type=F32), 0.0)
        h_ref[...] = (r * r).astype(BF)

    return pl.pallas_call(
        body, name="mlp_up", grid=(dff // tn, T // tm),
        in_specs=[pl.BlockSpec((tm, D), lambda j, i: (i, 0)), pl.BlockSpec((D, tn), lambda j, i: (0, j))],
        out_specs=pl.BlockSpec((tm, tn), lambda j, i: (i, j)),
        out_shape=jax.ShapeDtypeStruct((T, dff), BF),
        compiler_params=_params(2))(x1b, wup)


def _resident(shape):
    return pl.BlockSpec(shape, lambda *_: (0,) * len(shape), pipeline_mode=pl.Buffered(1))


def _mlp_down_loss(h, wdown, xhat1, target, vecs, tm=512):
    T, dff = h.shape

    def body(h_ref, w_ref, xhat1_ref, tgt_ref, vec_ref, dz2b_ref, st_ref):
        @pl.when(pl.program_id(0) == 0)
        def _():
            st_ref[...] = jnp.zeros_like(st_ref)

        ff = jnp.dot(h_ref[...], w_ref[...], preferred_element_type=F32)
        x1 = xhat1_ref[...] * vec_ref[3:4, :] + vec_ref[4:5, :]
        xhat2, rstd2 = _ln_fwd(ALPHA * x1 + ff)
        g2 = vec_ref[5:6, :]
        diff = xhat2 * g2 + vec_ref[6:7, :] - tgt_ref[...]
        dx2 = diff * (1.0 / D)
        st_ref[0:1, :] += _colsum(dx2 * xhat2)
        st_ref[1:2, :] += _colsum(dx2)
        st_ref[2:3, :] += _colsum(diff * diff)
        dz2b_ref[...] = _ln_bwd(dx2, xhat2, rstd2, g2).astype(BF)

    tok = pl.BlockSpec((tm, D), lambda i: (i, 0))
    vec = pl.BlockSpec((8, D), lambda i: (0, 0))
    return pl.pallas_call(
        body, name="mlp_down_loss", grid=(T // tm,),
        in_specs=[pl.BlockSpec((tm, dff), lambda i: (i, 0)), _resident((dff, D)), tok, tok, vec],
        out_specs=[tok, vec],
        out_shape=[jax.ShapeDtypeStruct((T, D), BF), jax.ShapeDtypeStruct((8, D), F32)],
        compiler_params=_params(1))(h, wdown, xhat1, target, vecs)


def _mlp_down_bwd(dz2b, wdown, h, tm=512, tk=2048):
    T, dff = h.shape

    def body(dz_ref, w_ref, h_ref, o_ref):
        dh = lax.dot_general(dz_ref[...], w_ref[...], NT_DIMS, preferred_element_type=F32)
        o_ref[...] = (dh * (2.0 * jnp.sqrt(h_ref[...].astype(F32)))).astype(BF)

    blk = pl.BlockSpec((tm, tk), lambda j, i: (i, j))
    return pl.pallas_call(
        body, name="mlp_down_bwd", grid=(dff // tk, T // tm),
        in_specs=[pl.BlockSpec((tm, D), lambda j, i: (i, 0)), pl.BlockSpec((tk, D), lambda j, i: (j, 0)), blk],
        out_specs=blk,
        out_shape=jax.ShapeDtypeStruct((T, dff), BF),
        compiler_params=_params(2))(dz2b, wdown, h)


def _tn_matmul(a, b, nblk, a_bw, b_bw, a_blocked, b_blocked, name, tt=2048):
    T = a.shape[0]
    nt = T // tt

    def body(a_ref, b_ref, o32_ref, o16_ref):
        t = pl.program_id(1)

        @pl.when(t == 0)
        def _():
            o32_ref[...] = jnp.zeros_like(o32_ref)

        o32_ref[0] += lax.dot_general(a_ref[...], b_ref[...], TN_DIMS, preferred_element_type=F32)

        @pl.when(t == nt - 1)
        def _():
            o16_ref[...] = o32_ref[...].astype(BF)

    a_spec = pl.BlockSpec((tt, a_bw), (lambda j, t: (t, j)) if a_blocked else (lambda j, t: (t, 0)))
    b_spec = pl.BlockSpec((tt, b_bw), (lambda j, t: (t, j)) if b_blocked else (lambda j, t: (t, 0)))
    out = pl.BlockSpec((1, a_bw, b_bw), lambda j, t: (j, 0, 0))
    return pl.pallas_call(
        body, name=name, grid=(nblk, nt),
        in_specs=[a_spec, b_spec], out_specs=[out, out],
        out_shape=[jax.ShapeDtypeStruct((nblk, a_bw, b_bw), F32), jax.ShapeDtypeStruct((nblk, a_bw, b_bw), BF)],
        compiler_params=_params(2))(a, b)


def _tn_matmul_tiles_outer(a, b, nblk, a_bw, b_bw, a_blocked, name, tt=2048):
    T = a.shape[0]
    nt = T // tt

    def body(a_ref, b_ref, o32_ref, o16_ref, acc):
        t, j = pl.program_id(0), pl.program_id(1)
        prod = lax.dot_general(a_ref[...], b_ref[...], TN_DIMS, preferred_element_type=F32)

        @pl.when(t == 0)
        def _():
            acc[j] = prod

        @pl.when(t > 0)
        def _():
            acc[j] += prod

        @pl.when(t == nt - 1)
        def _():
            o32_ref[0] = acc[j]
            o16_ref[0] = acc[j].astype(BF)

    a_spec = pl.BlockSpec((tt, a_bw), (lambda t, j: (t, j)) if a_blocked else (lambda t, j: (t, 0)))
    b_spec = pl.BlockSpec((tt, b_bw), (lambda t, j: (t, 0)) if a_blocked else (lambda t, j: (t, j)))
    out = pl.BlockSpec((1, a_bw, b_bw), lambda t, j: (jnp.where(t == nt - 1, j, 0), 0, 0))
    return pl.pallas_call(
        body, name=name, grid=(nt, nblk),
        in_specs=[a_spec, b_spec], out_specs=[out, out],
        out_shape=[jax.ShapeDtypeStruct((nblk, a_bw, b_bw), F32), jax.ShapeDtypeStruct((nblk, a_bw, b_bw), BF)],
        scratch_shapes=[pltpu.VMEM((nblk, a_bw, b_bw), F32)],
        compiler_params=_params(2))(a, b)


def _mlp_up_bwd(dhpre, wup, dz2, xhat1, rstd1, vecs, push, tm=512):
    T, dff = dhpre.shape

    def body(dh_ref, w_ref, dz2_ref, xhat_ref, rstd_ref, vec_ref, dz1b_ref, st_ref):
        @pl.when(pl.program_id(0) == 0)
        def _():
            st_ref[...] = jnp.zeros_like(st_ref)

        dx1 = (lax.dot_general(dh_ref[...], w_ref[...], NT_DIMS, preferred_element_type=F32)
               + ALPHA * dz2_ref[...].astype(F32))
        xhat = xhat_ref[...]
        st_ref[0:1, :] += _colsum(dx1 * xhat)
        st_ref[1:2, :] += _colsum(dx1)
        dz1b_ref[...] = _ln_bwd(dx1, xhat, rstd_ref[...], vec_ref[3:4, :]).astype(BF)

    tok = pl.BlockSpec((tm, D), lambda i: (i, 0))
    vec = pl.BlockSpec((8, D), lambda i: (0, 0))
    return _pallas(
        body, name="mlp_up_bwd", grid=(T // tm,),
        in_specs=[pl.BlockSpec((tm, dff), lambda i: (i, 0)), _resident((D, dff)),
                  tok, tok, pl.BlockSpec((tm, 1), lambda i: (i, 0)), vec],
        out_specs=[tok, vec],
        out_shape=[jax.ShapeDtypeStruct((T, D), BF), jax.ShapeDtypeStruct((8, D), F32)],
        args=(dhpre, wup, dz2, xhat1, rstd1, vecs), push=push)


def _merge_bwd(dz1, p, ya, yb, conva, xhatb, rstdb, woa, wob, wo, vecs, push, tm=256):
    T = dz1.shape[0]

    def body(dz1_ref, ga_ref, gb_ref, ba_ref, ya_ref, yb_ref, conva_ref, xhat_ref, rstd_ref,
             woa_ref, wob_ref, wo_ref, vec_ref,
             dya_ref, dyb_ref, dg_ref, dba_ref, dconva_ref, du1_ref, st_ref):
        @pl.when(pl.program_id(0) == 0)
        def _():
            st_ref[...] = jnp.zeros_like(st_ref)

        dmerged = lax.dot_general(dz1_ref[...], wo_ref[...], NT_DIMS, preferred_element_type=F32)
        sa, sb = _sigmoid(ga_ref[...].astype(F32)), _sigmoid(gb_ref[...].astype(F32))
        dya = (dmerged * sa).astype(BF)
        dyb = (dmerged * sb).astype(BF)
        dya_ref[...] = dya
        dyb_ref[...] = dyb
        dg_ref[:, 0:D] = (dmerged * ya_ref[...].astype(F32) * (sa * (1.0 - sa))).astype(BF)
        dg_ref[:, D:2 * D] = (dmerged * yb_ref[...].astype(F32) * (sb * (1.0 - sb))).astype(BF)

        dyapre = lax.dot_general(dya, woa_ref[...], NT_DIMS, preferred_element_type=F32)
        dba_ref[...] = (dyapre * conva_ref[...]).astype(BF)
        dconva_ref[...] = dyapre * ba_ref[...].astype(F32)

        du3 = lax.dot_general(dyb, wob_ref[...], NT_DIMS, preferred_element_type=F32)
        xhat = xhat_ref[...]
        gamma = vec_ref[1:2, :]
        u2 = xhat * gamma + vec_ref[2:3, :]
        s = _sigmoid(u2)
        du2 = du3 * (s * (1.0 + u2 * (1.0 - s)))
        st_ref[0:1, :] += _colsum(du2 * xhat)
        st_ref[1:2, :] += _colsum(du2)
        du1 = _ln_bwd(du2, xhat, rstd_ref[...], gamma)
        st_ref[2:3, :] += _colsum(du1)
        du1_ref[...] = du1

    tok = pl.BlockSpec((tm, D), lambda i: (i, 0))
    wfull = pl.BlockSpec((D, D), lambda i: (0, 0))
    vec = pl.BlockSpec((8, D), lambda i: (0, 0))
    return _pallas(
        body, name="merge_bwd", grid=(T // tm,),
        in_specs=[tok, _seg(tm, 5), _seg(tm, 6), _seg(tm, 0), tok, tok, tok, tok, pl.BlockSpec((tm, 1), lambda i: (i, 0)),
                  wfull, wfull, wfull, vec],
        out_specs=[tok, tok, pl.BlockSpec((tm, 2 * D), lambda i: (i, 0)), tok, tok, tok, vec],
        out_shape=[jax.ShapeDtypeStruct((T, D), BF), jax.ShapeDtypeStruct((T, D), BF),
                   jax.ShapeDtypeStruct((T, 2 * D), BF), jax.ShapeDtypeStruct((T, D), BF),
                   jax.ShapeDtypeStruct((T, D), F32), jax.ShapeDtypeStruct((T, D), F32),
                   jax.ShapeDtypeStruct((8, D), F32)],
        args=(dz1, p, p, p, ya, yb, conva, xhatb, rstdb, woa, wob, wo, vecs), push=push)


def _rows8(v):
    out = v[0:8]
    for q in range(1, RC // 8):
        out = out + v[8 * q:8 * q + 8]
    return out


def _conv_bwd(dconva, du1, p, dba, dg, wa, wb, push, tt=256):
    T = p.shape[0]
    nsteps = T // tt

    def body(dca_ref, dca_n, du1_ref, du1_n, ca, va, vb, gb, dba_ref, dg_ref, wa_ref, wb_ref,
             dp_ref, gw_ref, cabuf, u0buf, dcabuf, du1buf, dcain, du0, gwa, gwb, shd):
        i = pl.program_id(0)
        first, last = i == 0, i == nsteps - 1

        @pl.when(first)
        def _():
            gwa[...] = jnp.zeros_like(gwa)
            gwb[...] = jnp.zeros_like(gwb)

        f = lambda ref: ref[...].astype(F32)
        cav, vav, vbv = f(ca), f(va), f(vb)
        cabuf[...] = cav * vav
        sg = _sigmoid(f(gb))
        u0buf[...] = vbv * sg
        dcabuf[0:tt, :] = dca_ref[...]
        dcabuf[tt:tt + HN, :] = jnp.where(last, 0.0, dca_n[...])
        du1buf[0:tt, :] = du1_ref[...]
        du1buf[tt:tt + HB, :] = jnp.where(last, 0.0, du1_n[...])

        def lane_body(cidx, carry):
            ls = pl.ds(pl.multiple_of(cidx * LANES, LANES), LANES)
            _shifted_copies(shd, du1buf, ls, tt + HB - 8)
            for r in range(tt // RC):
                rows = pl.ds(r * RC, RC)
                cin = cabuf[rows, ls]
                acc = jnp.zeros((RC, LANES), F32)
                for k in range(KA):
                    dout = dcabuf[pl.ds(r * RC + KA - 1 - k, RC), ls]
                    acc = acc + wa_ref[k:k + 1, ls] * dout
                    gwa[8 * k:8 * k + 8, ls] += _rows8(cin * dout)
                dcain[rows, ls] = acc
                uin = u0buf[rows, ls]
                acc = jnp.zeros((RC, LANES), F32)
                for k in range(KB):
                    dout = _tap(shd, du1buf, ls, r * RC + KB - 1 - k, RC)
                    acc = acc + wb_ref[k:k + 1, ls] * dout
                    gwb[8 * k:8 * k + 8, ls] += _rows8(uin * dout)
                du0[rows, ls] = acc
            return carry

        lax.fori_loop(0, D // LANES, lane_body, 0)
        dca_in = dcain[...]
        du0v = du0[...]
        dp_ref[:, 0:D] = dba_ref[...]
        dp_ref[:, D:2 * D] = (dca_in * vav).astype(BF)
        dp_ref[:, 2 * D:3 * D] = (dca_in * cav).astype(BF)
        dp_ref[:, 3 * D:4 * D] = (du0v * sg).astype(BF)
        dp_ref[:, 4 * D:5 * D] = (du0v * vbv * (sg * (1.0 - sg))).astype(BF)
        dp_ref[:, 5 * D:7 * D] = dg_ref[...]

        @pl.when(last)
        def _():
            gw_ref[...] = jnp.zeros_like(gw_ref)
            for k in range(KA):
                gw_ref[k:k + 1, :] = _colsum(gwa[8 * k:8 * k + 8, :])
            for k in range(KB):
                gw_ref[8 + k:9 + k, :] = _colsum(gwb[8 * k:8 * k + 8, :])

    full = lambda r: pl.BlockSpec((r, D), lambda i: (0, 0))
    tok = pl.BlockSpec((tt, D), lambda i: (i, 0))
    nxt = lambda h: pl.BlockSpec((h, D), lambda i: (jnp.minimum((i + 1) * (tt // h), T // h - 1), 0))
    return _pallas(
        body, name="conv_bwd", grid=(nsteps,),
        in_specs=[tok, nxt(HN), tok, nxt(HB),
                  _seg(tt, 1), _seg(tt, 2), _seg(tt, 3), _seg(tt, 4),
                  tok, pl.BlockSpec((tt, 2 * D), lambda i: (i, 0)), full(8), full(32)],
        out_specs=[pl.BlockSpec((tt, 7 * D), lambda i: (i, 0)), full(40)],
        out_shape=[jax.ShapeDtypeStruct((T, 7 * D), BF), jax.ShapeDtypeStruct((40, D), F32)],
        scratch_shapes=[pltpu.VMEM((tt, D), F32), pltpu.VMEM((tt, D), F32),
                        pltpu.VMEM((tt + HN, D), F32), pltpu.VMEM((tt + HB, D), F32),
                        pltpu.VMEM((tt, D), F32), pltpu.VMEM((tt, D), F32),
                        pltpu.VMEM((8 * KA, D), F32), pltpu.VMEM((8 * KB, D), F32),
                        pltpu.VMEM((8, HB + tt, LANES), F32)],
        args=(dconva, dconva, du1, du1, p, p, p, p, dba, dg, wa, wb), push=push)


def _inproj_bwd(dp, win, dz1, push, tm=512):
    T, cols = dp.shape

    def body(dp_ref, w_ref, dz1_ref, o_ref):
        o_ref[...] = ALPHA * dz1_ref[...].astype(F32) + lax.dot_general(dp_ref[...], w_ref[...], NT_DIMS,
                                                                        preferred_element_type=F32)

    tok = pl.BlockSpec((tm, D), lambda i: (i, 0))
    return _pallas(
        body, name="inproj_bwd", grid=(T // tm,),
        in_specs=[pl.BlockSpec((tm, cols), lambda i: (i, 0)), _resident((D, cols)), tok],
        out_specs=[tok],
        out_shape=[jax.ShapeDtypeStruct((T, D), F32)],
        args=(dp, win, dz1), push=push)


def _adam_math(w, m, v, g):
    nm = ADAM_B1 * m + (1.0 - ADAM_B1) * g
    nv = ADAM_B2 * v + (1.0 - ADAM_B2) * (g * g)
    m_hat = nm / (1.0 - ADAM_B1 ** ADAM_STEP)
    v_hat = nv / (1.0 - ADAM_B2 ** ADAM_STEP)
    return -ADAM_LR * (m_hat / (jnp.sqrt(v_hat) + ADAM_EPS) + ADAM_WD * w), nm, nv


def _adamw(w, m, v, g32, landing, me, name, rb):
    R, C = w.shape
    nl = landing.shape[0]

    def body(me_ref, w_ref, m_ref, v_ref, own_ref, l_ref, g_ref, d_ref, nm_ref, nv_ref):
        g = own_ref[0]
        for k in range(nl):
            g = g + l_ref[k].astype(F32)
        g_ref[...] = g
        d_ref[...], nm_ref[...], nv_ref[...] = _adam_math(w_ref[...], m_ref[...], v_ref[...], g)

    blk = pl.BlockSpec((rb, C), lambda i, me_ref: (i, 0))
    grid_spec = pltpu.PrefetchScalarGridSpec(
        num_scalar_prefetch=1, grid=(R // rb,),
        in_specs=[blk, blk, blk, pl.BlockSpec((1, rb, C), lambda i, me_ref: (me_ref[0], i, 0)),
                  pl.BlockSpec((nl, rb, C), lambda i, me_ref: (0, i, 0))],
        out_specs=[blk] * 4)
    return pl.pallas_call(
        body, name=name, grid_spec=grid_spec, out_shape=[jax.ShapeDtypeStruct((R, C), F32)] * 4,
        compiler_params=_params(1))(me, w, m, v, g32, landing)


def _adamw_small(small_g, vec_w, vec_m, vec_v, conv_w, conv_m, conv_v):
    nv_ = len(vec_w)
    conv_rows = [(8, KA), (16, KB)]

    def body(*refs):
        g_ref = refs[0]
        w_refs, m_refs, v_refs = refs[1:10], refs[10:19], refs[19:28]
        out_refs, gsum = refs[28:64], refs[64]
        acc = g_ref[0]
        for j in range(1, NDEV):
            acc = acc + g_ref[j]
        gsum[...] = acc
        me = 4 * lax.axis_index("x") + 2 * lax.axis_index("y") + lax.axis_index("c")
        cols = pl.ds(pl.multiple_of(me * LANES, LANES), LANES)
        for i in range(nv_ + 2):
            if i < nv_:
                g = gsum[i:i + 1, :]
            else:
                r0, k = conv_rows[i - nv_]
                g = gsum[r0:r0 + k, cols]
            o = out_refs[4 * i:4 * i + 4]
            o[0][...] = g
            o[1][...], o[2][...], o[3][...] = _adam_math(w_refs[i][...], m_refs[i][...], v_refs[i][...], g)

    ws, ms, vs = list(vec_w) + list(conv_w), list(vec_m) + list(conv_m), list(vec_v) + list(conv_v)
    out_shape = [jax.ShapeDtypeStruct(w.shape, F32) for w in ws for _ in range(4)]
    return pl.pallas_call(
        body, name="adamw_small", out_shape=out_shape,
        scratch_shapes=[pltpu.VMEM(small_g.shape[1:], F32)])(small_g, *ws, *ms, *vs)


def _pad_rows(a, rows):
    return jnp.pad(a, ((0, rows - a.shape[0]), (0, 0)))


def _local_step(p, xb, mixed, post, x, target, win, wup, wdown, woa, wob, wo, wa, wb, vecs):
    yapre, conva, xhatb, rstdb, u3 = mixed
    ya, yb, merged, xhat1, rstd1, x1b = post
    h = _mlp_up(x1b, wup)
    dz2b, st2 = _mlp_down_loss(h, wdown, xhat1, target, vecs)

    by_owner = lambda g16: g16.reshape(NDEV, D // NDEV, D)
    dhpre = _mlp_down_bwd(dz2b, wdown, h)
    g_wdown = _tn_matmul_tiles_outer(h, dz2b, NDEV, 512, D, True, "grad_w_down")
    (dz1b, st1), land_wdown = _mlp_up_bwd(dhpre, wup, dz2b, xhat1, rstd1, vecs, _push(exch=[g_wdown[1]]))
    g_wup = _tn_matmul_tiles_outer(x1b, dhpre, NDEV, D, 512, False, "grad_w_up")
    (dya, dyb, dg, dba, dconva, du1, stb), land_wup = _merge_bwd(
        dz1b, p, ya, yb, conva, xhatb, rstdb, woa, wob, wo, vecs, _push(exch=[g_wup[1]]))
    g_wo = _tn_matmul(merged, dz1b, 1, D, D, False, False, "grad_w_o")
    g_woa = _tn_matmul(yapre, dya, 1, D, D, False, False, "grad_w_out_a")
    g_wob = _tn_matmul(u3, dyb, 1, D, D, False, False, "grad_w_out_b")
    (dp, gw), land_sq = _conv_bwd(dconva, du1, p, dba, dg, wa, wb,
                                  _push(exch=[by_owner(g_woa[1]), by_owner(g_wob[1]), by_owner(g_wo[1])]))
    g_win = _tn_matmul(xb, dp, NDEV, D, 896, False, True, "grad_w_in")

    small = jnp.concatenate([stb[2:3], stb[0:2], st1[0:2], st2[0:3], gw], axis=0)
    (grad_x,), land_last = _inproj_bwd(dp, win, dz1b, _push(exch=[g_win[1]], gath=[small]))
    grads = (g_win[0], g_wup[0], g_wdown[0], g_woa[0], g_wob[0], g_wo[0])
    return grad_x, grads, small, land_wdown + land_wup + land_sq + land_last


def kernel(x, w_in, conv_a_w, w_out_a, conv_b_w, conv_b_bias, ln_b_gamma, ln_b_beta, w_out_b, w_o, ln1_gamma, ln1_beta, w_up, w_down, ln2_gamma, ln2_beta, loss_target, m_w_in, m_conv_a_w, m_w_out_a, m_conv_b_w, m_conv_b_bias, m_ln_b_gamma, m_ln_b_beta, m_w_out_b, m_w_o, m_ln1_gamma, m_ln1_beta, m_w_up, m_w_down, m_ln2_gamma, m_ln2_beta, v_w_in, v_conv_a_w, v_w_out_a, v_conv_b_w, v_conv_b_bias, v_ln_b_gamma, v_ln_b_beta, v_w_out_b, v_w_o, v_ln1_gamma, v_ln1_beta, v_w_up, v_w_down, v_ln2_gamma, v_ln2_beta):
    T = x.shape[1]
    me = 4 * lax.axis_index("x") + 2 * lax.axis_index("y") + lax.axis_index("c")

    conv_shard = jnp.concatenate([_pad_rows(conv_a_w, 8), _pad_rows(conv_b_w, 32)], axis=0)
    p, xb, win_g, conv_g = _inproj_gather(x[0], w_in.astype(BF), conv_shard)
    conv_full = jnp.transpose(conv_g, (1, 0, 2)).reshape(40, D)
    vecs = jnp.stack([conv_b_bias, ln_b_gamma, ln_b_beta, ln1_gamma, ln1_beta, ln2_gamma, ln2_beta,
                      jnp.zeros_like(ln2_beta)])
    whole = lambda g: jnp.transpose(g, (1, 0, 2)).reshape(D, -1)
    mixed, (woa_g, wob_g, wo_g, wup_g, wdown_g) = _mixer_fwd(
        p, conv_full[0:8], conv_full[8:40], vecs,
        [w_out_a.astype(BF), w_out_b.astype(BF), w_o.astype(BF), w_up.astype(BF), w_down.astype(BF)])
    woa, wob, wo = woa_g.reshape(D, D), wob_g.reshape(D, D), wo_g.reshape(D, D)
    post, _ = _post_mixer(mixed[0], mixed[4], p, x[0], woa, wob, wo, vecs, [])

    grad_x, grads, small, landing = _local_step(
        p, xb, mixed, post, x[0], loss_target[0], whole(win_g), whole(wup_g), wdown_g.reshape(NDEV * 512, D),
        woa, wob, wo, conv_full[0:8], conv_full[8:40], vecs)
    g_win, g_wup, g_wdown, g_woa, g_wob, g_wo = grads
    l_wdown, l_wup, l_woa, l_wob, l_wo, l_win, small_g = landing

    loss = lax.psum(0.5 / D * jnp.sum(small[7]), ("x", "y", "c"))

    me1 = me.astype(jnp.int32).reshape(1)
    by_owner = lambda g32: g32.reshape(NDEV, D // NDEV, D)
    r_win = _adamw(w_in, m_w_in, v_w_in, g_win, l_win, me1, "adamw_w_in", 256)
    r_wup = _adamw(w_up, m_w_up, v_w_up, g_wup, l_wup, me1, "adamw_w_up", 256)
    r_wdown = _adamw(w_down, m_w_down, v_w_down, g_wdown, l_wdown, me1, "adamw_w_down", 256)
    r_woa = _adamw(w_out_a, m_w_out_a, v_w_out_a, by_owner(g_woa), l_woa, me1, "adamw_w_out_a", 128)
    r_wob = _adamw(w_out_b, m_w_out_b, v_w_out_b, by_owner(g_wob), l_wob, me1, "adamw_w_out_b", 128)
    r_wo = _adamw(w_o, m_w_o, v_w_o, by_owner(g_wo), l_wo, me1, "adamw_w_o", 128)

    row = lambda vec: vec.reshape(1, D)
    small_out = _adamw_small(
        small_g,
        [row(a) for a in (conv_b_bias, ln_b_gamma, ln_b_beta, ln1_gamma, ln1_beta, ln2_gamma, ln2_beta)],
        [row(a) for a in (m_conv_b_bias, m_ln_b_gamma, m_ln_b_beta, m_ln1_gamma, m_ln1_beta, m_ln2_gamma, m_ln2_beta)],
        [row(a) for a in (v_conv_b_bias, v_ln_b_gamma, v_ln_b_beta, v_ln1_gamma, v_ln1_beta, v_ln2_gamma, v_ln2_beta)],
        [conv_a_w, conv_b_w], [m_conv_a_w, m_conv_b_w], [v_conv_a_w, v_conv_b_w])
    r_vec = [[small_out[4 * i + q].reshape(D) for q in range(4)] for i in range(7)]
    r_conva, r_convb = small_out[28:32], small_out[32:36]

    per_weight = []
    for q in range(4):
        per_weight.append([
            r_win[q], r_conva[q], r_woa[q], r_convb[q],
            r_vec[0][q], r_vec[1][q], r_vec[2][q], r_wob[q], r_wo[q], r_vec[3][q], r_vec[4][q],
            r_wup[q], r_wdown[q], r_vec[5][q], r_vec[6][q]])
    return (loss, grad_x[None], *per_weight[0], *per_weight[1], *per_weight[2], *per_weight[3])
```

```python
import functools

import jax
import jax.numpy as jnp
from jax import lax
from jax.experimental import pallas as pl
from jax.experimental.pallas import tpu as pltpu

F32 = jnp.float32
BF = jnp.bfloat16
D = 1024
NDEV = 8
ALPHA = 2.0 ** 0.25
LN_EPS = 1e-5
KA, KB = 3, 31
HA, HB = 16, 32
HN = 8
RC = 64
LANES = 128
VMEM_LIMIT = 56 * 1024 * 1024
MESH = pl.DeviceIdType.MESH
ADAM_LR, ADAM_B1, ADAM_B2, ADAM_EPS, ADAM_WD, ADAM_STEP = 0.001, 0.9, 0.999, 1e-08, 0.01, 10

ANY_SPEC = pl.BlockSpec(memory_space=pl.ANY)
NT_DIMS = (((1,), (1,)), ((), ()))
TN_DIMS = (((0,), (0,)), ((), ()))


def _params(n_axes):
    return pltpu.CompilerParams(dimension_semantics=("arbitrary",) * n_axes, vmem_limit_bytes=VMEM_LIMIT)


def _sigmoid(v):
    return 0.5 * jnp.tanh(0.5 * v) + 0.5


def _ln_fwd(z):
    mu = jnp.mean(z, axis=-1, keepdims=True)
    zc = z - mu
    var = jnp.mean(zc * zc, axis=-1, keepdims=True)
    rstd = lax.rsqrt(var + LN_EPS)
    return zc * rstd, rstd


def _ln_bwd(dy, xhat, rstd, gamma):
    dxhat = dy * gamma
    m1 = jnp.mean(dxhat, axis=-1, keepdims=True)
    m2 = jnp.mean(dxhat * xhat, axis=-1, keepdims=True)
    return rstd * (dxhat - m1 - xhat * m2)


def _colsum(v):
    return jnp.sum(v, axis=0, keepdims=True)


class _TwoLevelGather:
    def __init__(self, ins, outs, send_sems, recv_sems, local_sems):
        self.ins, self.outs = ins, outs
        self.send_sems, self.recv_sems, self.local_sems = send_sems, recv_sems, local_sems
        x, y, c = lax.axis_index("x"), lax.axis_index("y"), lax.axis_index("c")
        self.me, self.sibling, self.c = (x, y, c), (x, y, 1 - c), c
        self.chips = [(1 - x, y), (x, 1 - y), (1 - x, 1 - y)]
        self.n = len(ins)

    @staticmethod
    def out_shape(arrs):
        return [jax.ShapeDtypeStruct((NDEV,) + a.shape, a.dtype) for a in arrs]

    @staticmethod
    def scratch(n):
        return [pltpu.SemaphoreType.DMA((n, 7)), pltpu.SemaphoreType.DMA((n, 7)), pltpu.SemaphoreType.DMA((n,))]

    def _copy(self, a, k, block, to, src=None):
        px, py, pc = block
        rows = self.outs[a].at[4 * px + 2 * py + pc]
        return pltpu.make_async_remote_copy(
            src_ref=rows if src is None else src, dst_ref=rows,
            send_sem=self.send_sems.at[a, k], recv_sem=self.recv_sems.at[a, k],
            device_id=to, device_id_type=MESH)

    def _mine(self, a):
        x, y, c = self.me
        return pltpu.make_async_copy(self.ins[a], self.outs[a].at[4 * x + 2 * y + c], self.local_sems.at[a])

    def _first(self, a):
        cps = [self._copy(a, 0, self.me, self.sibling, src=self.ins[a])]
        return cps + [self._copy(a, 1 + j, self.me, (*chip, self.c), src=self.ins[a]) for j, chip in enumerate(self.chips)]

    def _passed(self, a, j):
        return self._copy(a, 4 + j, (*self.chips[j], self.c), self.sibling)

    def start(self, diagonal=True):
        for a in range(self.n):
            self._mine(a).start()
        for a in range(self.n):
            for cp in self._first(a)[:4 if diagonal else 3]:
                cp.start()

    def start_diagonal(self):
        for a in range(self.n):
            self._first(a)[3].start()

    def wait_ici(self, j):
        for a in range(self.n):
            self._copy(a, 1 + j, (*self.chips[j], self.c), self.me).wait_recv()

    def pass_on(self, j):
        for a in range(self.n):
            self._passed(a, j).start()

    def wait_sibling(self):
        for a in range(self.n):
            self._copy(a, 0, self.sibling, self.me).wait_recv()

    def wait_passed(self, j):
        for a in range(self.n):
            self._copy(a, 4 + j, (*self.chips[j], 1 - self.c), self.me).wait_recv()

    def drain(self):
        for a in range(self.n):
            for cp in self._first(a) + [self._passed(a, j) for j in range(3)]:
                cp.wait_send()
            self._mine(a).wait()

    def forward(self):
        for j in range(3):
            self.wait_ici(j)
            self.pass_on(j)

    def finish(self):
        self.wait_sibling()
        for j in range(3):
            self.wait_passed(j)
        self.drain()


class _Push:
    def __init__(self, exch=(), gath=()):
        self.exch, self.gath = list(exch), list(gath)
        self.n = len(self.exch) + len(self.gath)

    def operands(self):
        return self.exch + self.gath

    def out_shape(self):
        return ([jax.ShapeDtypeStruct((NDEV - 1,) + a.shape[1:], a.dtype) for a in self.exch]
                + [jax.ShapeDtypeStruct((NDEV,) + a.shape, a.dtype) for a in self.gath])

    def scratch(self):
        return [pltpu.SemaphoreType.DMA((self.n, 7)), pltpu.SemaphoreType.DMA((self.n, 7)),
                pltpu.SemaphoreType.DMA((max(len(self.gath), 1),))]

    def copies(self, ins, outs, send_sems, recv_sems, local_sems):
        x, y, c = lax.axis_index("x"), lax.axis_index("y"), lax.axis_index("c")
        me = 4 * x + 2 * y + c
        ne = len(self.exch)
        remote = []
        for k in range(1, NDEV):
            px = 1 - x if k & 4 else x
            py = 1 - y if k & 2 else y
            pc = 1 - c if k & 1 else c
            for a in range(self.n):
                src = ins[a].at[4 * px + 2 * py + pc] if a < ne else ins[a]
                dst = outs[a].at[k - 1] if a < ne else outs[a].at[me]
                remote.append(pltpu.make_async_remote_copy(
                    src_ref=src, dst_ref=dst, send_sem=send_sems.at[a, k - 1], recv_sem=recv_sems.at[a, k - 1],
                    device_id=(px, py, pc), device_id_type=MESH))
        local = [pltpu.make_async_copy(ins[a], outs[a].at[me], local_sems.at[a - ne]) for a in range(ne, self.n)]
        return remote, local


def _push(exch=(), gath=()):
    return _Push(exch, gath)


def _pallas(body, *, name, grid, in_specs, out_specs, out_shape, args, scratch_shapes=(), push=None):
    ni, no, ns = len(in_specs), len(out_specs), len(scratch_shapes)
    if push is None:
        outs = pl.pallas_call(
            body, name=name, grid=grid, in_specs=in_specs, out_specs=out_specs, out_shape=out_shape,
            scratch_shapes=list(scratch_shapes), compiler_params=_params(len(grid)))(*args)
        return list(outs), []
    npush = push.n

    def wrapped(*refs):
        ins, pins = refs[:ni], refs[ni:ni + npush]
        outs, pouts = refs[ni + npush:ni + npush + no], refs[ni + npush + no:ni + 2 * npush + no]
        scr, sems = refs[ni + 2 * npush + no:ni + 2 * npush + no + ns], refs[ni + 2 * npush + no + ns:]
        first = functools.reduce(jnp.logical_and, [pl.program_id(d) == 0 for d in range(len(grid))])
        last = functools.reduce(jnp.logical_and, [pl.program_id(d) == grid[d] - 1 for d in range(len(grid))])
        remote, local = push.copies(pins, pouts, *sems)

        @pl.when(first)
        def _():
            for cp in local + remote:
                cp.start()

        body(*ins, *outs, *scr)

        @pl.when(last)
        def _():
            for cp in remote + local:
                cp.wait()

    outs = pl.pallas_call(
        wrapped, name=name, grid=grid,
        in_specs=list(in_specs) + [ANY_SPEC] * npush, out_specs=list(out_specs) + [ANY_SPEC] * npush,
        out_shape=list(out_shape) + push.out_shape(), scratch_shapes=list(scratch_shapes) + push.scratch(),
        compiler_params=_params(len(grid)))(*args, *push.operands())
    return list(outs[:no]), list(outs[no:])


def _with_gather(body, late, *, name, nsteps, in_specs, out_specs, out_shape, args, scratch_shapes=()):
    ni, no, ns, n = len(in_specs), len(out_specs), len(scratch_shapes), len(late)
    pass_step = (7 * nsteps) // 8
    if not late:
        outs = pl.pallas_call(
            body, name=name, grid=(nsteps,), in_specs=in_specs, out_specs=out_specs, out_shape=out_shape,
            scratch_shapes=list(scratch_shapes), compiler_params=_params(1))(*args)
        return list(outs), []

    def wrapped(*refs):
        ins, outs = refs[:ni], refs[ni + n:ni + n + no]
        scr = refs[ni + 2 * n + no:ni + 2 * n + no + ns]
        gather = _TwoLevelGather(refs[ni:ni + n], refs[ni + n + no:ni + 2 * n + no], *refs[ni + 2 * n + no + ns:])
        step = pl.program_id(0)
        pl.when(step == 0)(gather.start)
        pl.when(step == pass_step)(gather.forward)
        body(*ins, *outs, *scr)
        pl.when(step == nsteps - 1)(gather.finish)

    outs = pl.pallas_call(
        wrapped, name=name, grid=(nsteps,),
        in_specs=list(in_specs) + [ANY_SPEC] * n, out_specs=list(out_specs) + [ANY_SPEC] * n,
        out_shape=list(out_shape) + _TwoLevelGather.out_shape(late),
        scratch_shapes=list(scratch_shapes) + _TwoLevelGather.scratch(n),
        compiler_params=_params(1))(*args, *late)
    return list(outs[:no]), list(outs[no:])


def _inproj_gather(x, w_shard, conv_shard, tm=1024):
    T = x.shape[0]
    ni = T // tm
    bw = w_shard.shape[1]
    cx, cy, cc = lax.axis_index("x"), lax.axis_index("y"), lax.axis_index("c")
    blk = lambda px, py, pc: 4 * px + 2 * py + pc
    order = [blk(cx, cy, cc), blk(cx, cy, 1 - cc)]
    for chip in [(1 - cx, cy), (cx, 1 - cy), (1 - cx, 1 - cy)]:
        order += [blk(*chip, cc), blk(*chip, 1 - cc)]
    order = jnp.stack(order).astype(jnp.int32)

    def body(order_ref, x_ref, w_ref, conv_ref, p_ref, xb_ref, wing_ref, convg_ref, xbs, wbuf, wsem, *sems):
        gather = _TwoLevelGather([w_ref, conv_ref], [wing_ref, convg_ref], *sems)
        j, i = pl.program_id(0), pl.program_id(1)

        def load(src):
            cp = pltpu.make_async_copy(src, wbuf, wsem)
            cp.start()
            cp.wait()

        def arrival(jj):
            if jj == 0:
                gather.start(diagonal=False)
                load(w_ref)
                return
            if jj == 1:
                gather.wait_sibling()
            elif jj % 2 == 0:
                if jj == 2:
                    gather.start_diagonal()
                gather.wait_ici(jj // 2 - 1)
                gather.pass_on(jj // 2 - 1)
            else:
                gather.wait_passed(jj // 2 - 1)
            load(wing_ref.at[order_ref[jj]])

        for jj in range(NDEV):
            pl.when(jnp.logical_and(j == jj, i == 0))(functools.partial(arrival, jj))

        @pl.when(j == 0)
        def _():
            xb = x_ref[...].astype(BF)
            xbs[i] = xb
            xb_ref[...] = xb

        p_ref[...] = jnp.dot(xbs[i], wbuf[...], preferred_element_type=F32).astype(BF)

        @pl.when(jnp.logical_and(j == NDEV - 1, i == ni - 1))
        def _():
            gather.drain()

    rows_once = lambda j, i, o: (jnp.where(j == 0, i, ni - 1), 0)
    grid_spec = pltpu.PrefetchScalarGridSpec(
        num_scalar_prefetch=1, grid=(NDEV, ni),
        in_specs=[pl.BlockSpec((tm, D), rows_once), ANY_SPEC, ANY_SPEC],
        out_specs=[pl.BlockSpec((tm, bw), lambda j, i, o: (i, o[j])), pl.BlockSpec((tm, D), rows_once),
                   ANY_SPEC, ANY_SPEC],
        scratch_shapes=[pltpu.VMEM((ni, tm, D), BF), pltpu.VMEM((D, bw), BF), pltpu.SemaphoreType.DMA(())]
        + _TwoLevelGather.scratch(2))
    p, xb, win_g, conv_g = pl.pallas_call(
        body, name="inproj_gather", grid_spec=grid_spec,
        out_shape=[jax.ShapeDtypeStruct((T, NDEV * bw), BF), jax.ShapeDtypeStruct((T, D), BF)]
        + _TwoLevelGather.out_shape([w_shard, conv_shard]),
        compiler_params=_params(2))(order, x, w_shard, conv_shard)
    return p, xb, win_g, conv_g


def _mixer_fwd(p, wa, wb, vecs, late, tt=256):
    T = p.shape[0]
    nt = T // tt

    def body(ba, ca, va, vb, gb, ca_p, va_p, vb_p, gb_p, wa_ref, wb_ref, vec_ref,
             yapre_ref, conva_ref, xhat_ref, rstd_ref, u3_ref, cabuf, u0buf, u1buf, shu):
        first = pl.program_id(0) == 0
        f = lambda ref: ref[...].astype(F32)
        cabuf[0:HA, :] = jnp.where(first, 0.0, f(ca_p) * f(va_p))
        cabuf[HA:HA + tt, :] = f(ca) * f(va)
        u0buf[0:HB, :] = jnp.where(first, 0.0, f(vb_p) * _sigmoid(f(gb_p)))
        u0buf[HB:HB + tt, :] = f(vb) * _sigmoid(f(gb))

        def lane_body(cidx, carry):
            ls = pl.ds(pl.multiple_of(cidx * LANES, LANES), LANES)
            _shifted_copies(shu, u0buf, ls, tt + HB - 8)
            for r in range(tt // RC):
                acc = jnp.zeros((RC, LANES), F32)
                for k in range(KA):
                    acc = acc + wa_ref[k:k + 1, ls] * cabuf[pl.ds(HA - (KA - 1) + k + r * RC, RC), ls]
                conva_ref[pl.ds(r * RC, RC), ls] = acc
                acc = jnp.zeros((RC, LANES), F32)
                for k in range(KB):
                    acc = acc + wb_ref[k:k + 1, ls] * _tap(shu, u0buf, ls, HB - (KB - 1) + k + r * RC, RC)
                u1buf[pl.ds(r * RC, RC), ls] = acc
            return carry

        lax.fori_loop(0, D // LANES, lane_body, 0)
        yapre_ref[...] = (f(ba) * conva_ref[...]).astype(BF)
        xhat, rstd = _ln_fwd(u1buf[...] + vec_ref[0:1, :])
        xhat_ref[...] = xhat
        rstd_ref[...] = rstd
        u2 = xhat * vec_ref[1:2, :] + vec_ref[2:3, :]
        u3_ref[...] = (u2 * _sigmoid(u2)).astype(BF)

    full = lambda r: pl.BlockSpec((r, D), lambda i: (0, 0))
    tok = pl.BlockSpec((tt, D), lambda i: (i, 0))
    return _with_gather(
        body, late, name="mixer_fwd", nsteps=nt,
        in_specs=[_seg(tt, 0), _seg(tt, 1), _seg(tt, 2), _seg(tt, 3), _seg(tt, 4),
                  _prev(tt, HA, 1), _prev(tt, HA, 2), _prev(tt, HB, 3), _prev(tt, HB, 4),
                  full(8), full(32), full(8)],
        out_specs=[tok, tok, tok, pl.BlockSpec((tt, 1), lambda i: (i, 0)), tok],
        out_shape=[jax.ShapeDtypeStruct((T, D), BF), jax.ShapeDtypeStruct((T, D), F32),
                   jax.ShapeDtypeStruct((T, D), F32), jax.ShapeDtypeStruct((T, 1), F32),
                   jax.ShapeDtypeStruct((T, D), BF)],
        scratch_shapes=[pltpu.VMEM((HA + tt, D), F32), pltpu.VMEM((HB + tt, D), F32), pltpu.VMEM((tt, D), F32),
                        pltpu.VMEM((8, HB + tt, LANES), F32)],
        args=(p, p, p, p, p, p, p, p, p, wa, wb, vecs))


def _seg(tt, s):
    return pl.BlockSpec((tt, D), lambda i: (i, s))


def _prev(tt, h, s):
    return pl.BlockSpec((h, D), lambda i: (jnp.maximum(i * (tt // h) - 1, 0), s))


def _shifted_copies(shbuf, src, ls, n):
    for s in range(1, 8):
        shbuf[s, 0:n, :] = src[pl.ds(s, n), ls]


def _tap(shbuf, src, ls, off, rows):
    s, q = off % 8, off // 8
    if s == 0:
        return src[pl.ds(off, rows), ls]
    return shbuf[s, pl.ds(8 * q, rows), :]


def _post_mixer(yapre, u3, p, x, woa, wob, wo, vecs, late, tm=512):
    T = x.shape[0]

    def body(yapre_ref, u3_ref, ga_ref, gb_ref, x_ref, woa_ref, wob_ref, wo_ref, vec_ref,
             ya_ref, yb_ref, merged_ref, xhat_ref, rstd_ref, x1b_ref):
        ya = jnp.dot(yapre_ref[...], woa_ref[...], preferred_element_type=F32)
        yb = jnp.dot(u3_ref[...], wob_ref[...], preferred_element_type=F32)
        ya_ref[...] = ya.astype(BF)
        yb_ref[...] = yb.astype(BF)
        merged = (_sigmoid(ga_ref[...].astype(F32)) * ya + _sigmoid(gb_ref[...].astype(F32)) * yb).astype(BF)
        merged_ref[...] = merged
        mix = jnp.dot(merged, wo_ref[...], preferred_element_type=F32)
        xhat, rstd = _ln_fwd(ALPHA * x_ref[...] + mix)
        xhat_ref[...] = xhat
        rstd_ref[...] = rstd
        x1b_ref[...] = (xhat * vec_ref[3:4, :] + vec_ref[4:5, :]).astype(BF)

    tok = pl.BlockSpec((tm, D), lambda i: (i, 0))
    wfull = _resident((D, D))
    one = pl.BlockSpec((tm, 1), lambda i: (i, 0))
    return _with_gather(
        body, late, name="post_mixer", nsteps=T // tm,
        in_specs=[tok, tok, _seg(tm, 5), _seg(tm, 6), tok, wfull, wfull, wfull, pl.BlockSpec((8, D), lambda i: (0, 0))],
        out_specs=[tok, tok, tok, tok, one, tok],
        out_shape=[jax.ShapeDtypeStruct((T, D), BF), jax.ShapeDtypeStruct((T, D), BF),
                   jax.ShapeDtypeStruct((T, D), BF), jax.ShapeDtypeStruct((T, D), F32),
                   jax.ShapeDtypeStruct((T, 1), F32), jax.ShapeDtypeStruct((T, D), BF)],
        args=(yapre, u3, p, p, x, woa, wob, wo, vecs))


def _mlp_up(x1b, wup, tm=512, tn=2048):
    T = x1b.shape[0]
    dff = wup.shape[1]

    def body(x_ref, w_ref, r_ref, h_ref):
        r = jnp.maximum(jnp.dot(x_ref[...], w_ref[...], preferred_element_type=F32), 0.0)
        r_ref[...] = r.astype(BF)
        h_ref[...] = (r * r).astype(BF)

    out = pl.BlockSpec((tm, tn), lambda j, i: (i, j))
    return pl.pallas_call(
        body, name="mlp_up", grid=(dff // tn, T // tm),
        in_specs=[pl.BlockSpec((tm, D), lambda j, i: (i, 0)), pl.BlockSpec((D, tn), lambda j, i: (0, j))],
        out_specs=[out, out],
        out_shape=[jax.ShapeDtypeStruct((T, dff), BF), jax.ShapeDtypeStruct((T, dff), BF)],
        compiler_params=_params(2))(x1b, wup)


def _resident(shape):
    return pl.BlockSpec(shape, lambda *_: (0,) * len(shape), pipeline_mode=pl.Buffered(1))


def _mlp_down_loss(h, wdown, xhat1, target, vecs, tm=512):
    T, dff = h.shape

    def body(h_ref, w_ref, xhat1_ref, tgt_ref, vec_ref, dz2b_ref, st_ref):
        @pl.when(pl.program_id(0) == 0)
        def _():
            st_ref[...] = jnp.zeros_like(st_ref)

        ff = jnp.dot(h_ref[...], w_ref[...], preferred_element_type=F32)
        x1 = xhat1_ref[...] * vec_ref[3:4, :] + vec_ref[4:5, :]
        xhat2, rstd2 = _ln_fwd(ALPHA * x1 + ff)
        g2 = vec_ref[5:6, :]
        diff = xhat2 * g2 + vec_ref[6:7, :] - tgt_ref[...]
        dx2 = diff * (1.0 / D)
        st_ref[0:1, :] += _colsum(dx2 * xhat2)
        st_ref[1:2, :] += _colsum(dx2)
        st_ref[2:3, :] += _colsum(diff * diff)
        dz2b_ref[...] = _ln_bwd(dx2, xhat2, rstd2, g2).astype(BF)

    tok = pl.BlockSpec((tm, D), lambda i: (i, 0))
    vec = pl.BlockSpec((8, D), lambda i: (0, 0))
    return pl.pallas_call(
        body, name="mlp_down_loss", grid=(T // tm,),
        in_specs=[pl.BlockSpec((tm, dff), lambda i: (i, 0)), _resident((dff, D)), tok, tok, vec],
        out_specs=[tok, vec],
        out_shape=[jax.ShapeDtypeStruct((T, D), BF), jax.ShapeDtypeStruct((8, D), F32)],
        compiler_params=_params(1))(h, wdown, xhat1, target, vecs)


def _mlp_down_bwd(dz2b, wdown, r, tm=512, tk=2048):
    T, dff = r.shape

    def body(dz_ref, w_ref, r_ref, o_ref):
        dh = lax.dot_general(dz_ref[...], w_ref[...], NT_DIMS, preferred_element_type=F32)
        o_ref[...] = (dh * (2.0 * r_ref[...].astype(F32))).astype(BF)

    blk = pl.BlockSpec((tm, tk), lambda j, i: (i, j))
    return pl.pallas_call(
        body, name="mlp_down_bwd", grid=(dff // tk, T // tm),
        in_specs=[pl.BlockSpec((tm, D), lambda j, i: (i, 0)), pl.BlockSpec((tk, D), lambda j, i: (j, 0)), blk],
        out_specs=blk,
        out_shape=jax.ShapeDtypeStruct((T, dff), BF),
        compiler_params=_params(2))(dz2b, wdown, r)


def _tn_matmul(a, b, nblk, a_bw, b_bw, a_blocked, b_blocked, name, tt=2048):
    T = a.shape[0]
    nt = T // tt

    def body(a_ref, b_ref, o32_ref, o16_ref):
        t = pl.program_id(1)

        @pl.when(t == 0)
        def _():
            o32_ref[...] = jnp.zeros_like(o32_ref)

        o32_ref[0] += lax.dot_general(a_ref[...], b_ref[...], TN_DIMS, preferred_element_type=F32)

        @pl.when(t == nt - 1)
        def _():
            o16_ref[...] = o32_ref[...].astype(BF)

    a_spec = pl.BlockSpec((tt, a_bw), (lambda j, t: (t, j)) if a_blocked else (lambda j, t: (t, 0)))
    b_spec = pl.BlockSpec((tt, b_bw), (lambda j, t: (t, j)) if b_blocked else (lambda j, t: (t, 0)))
    out = pl.BlockSpec((1, a_bw, b_bw), lambda j, t: (j, 0, 0))
    return pl.pallas_call(
        body, name=name, grid=(nblk, nt),
        in_specs=[a_spec, b_spec], out_specs=[out, out],
        out_shape=[jax.ShapeDtypeStruct((nblk, a_bw, b_bw), F32), jax.ShapeDtypeStruct((nblk, a_bw, b_bw), BF)],
        compiler_params=_params(2))(a, b)


def _tn_matmul_tiles_outer(a, b, nblk, a_bw, b_bw, a_blocked, name, tt=2048):
    T = a.shape[0]
    nt = T // tt

    def body(a_ref, b_ref, o32_ref, o16_ref, acc):
        t, j = pl.program_id(0), pl.program_id(1)
        prod = lax.dot_general(a_ref[...], b_ref[...], TN_DIMS, preferred_element_type=F32)

        @pl.when(t == 0)
        def _():
            acc[j] = prod

        @pl.when(t > 0)
        def _():
            acc[j] += prod

        @pl.when(t == nt - 1)
        def _():
            o32_ref[0] = acc[j]
            o16_ref[0] = acc[j].astype(BF)

    a_spec = pl.BlockSpec((tt, a_bw), (lambda t, j: (t, j)) if a_blocked else (lambda t, j: (t, 0)))
    b_spec = pl.BlockSpec((tt, b_bw), (lambda t, j: (t, 0)) if a_blocked else (lambda t, j: (t, j)))
    out = pl.BlockSpec((1, a_bw, b_bw), lambda t, j: (jnp.where(t == nt - 1, j, 0), 0, 0))
    return pl.pallas_call(
        body, name=name, grid=(nt, nblk),
        in_specs=[a_spec, b_spec], out_specs=[out, out],
        out_shape=[jax.ShapeDtypeStruct((nblk, a_bw, b_bw), F32), jax.ShapeDtypeStruct((nblk, a_bw, b_bw), BF)],
        scratch_shapes=[pltpu.VMEM((nblk, a_bw, b_bw), F32)],
        compiler_params=_params(2))(a, b)


def _mlp_up_bwd(dhpre, wup_t, dz2, xhat1, rstd1, vecs, push, tm=512):
    T, dff = dhpre.shape

    def body(dh_ref, w_ref, dz2_ref, xhat_ref, rstd_ref, vec_ref, dz1b_ref, st_ref):
        @pl.when(pl.program_id(0) == 0)
        def _():
            st_ref[...] = jnp.zeros_like(st_ref)

        dx1 = jnp.dot(dh_ref[...], w_ref[...], preferred_element_type=F32) + ALPHA * dz2_ref[...].astype(F32)
        xhat = xhat_ref[...]
        st_ref[0:1, :] += _colsum(dx1 * xhat)
        st_ref[1:2, :] += _colsum(dx1)
        dz1b_ref[...] = _ln_bwd(dx1, xhat, rstd_ref[...], vec_ref[3:4, :]).astype(BF)

    tok = pl.BlockSpec((tm, D), lambda i: (i, 0))
    vec = pl.BlockSpec((8, D), lambda i: (0, 0))
    return _pallas(
        body, name="mlp_up_bwd", grid=(T // tm,),
        in_specs=[pl.BlockSpec((tm, dff), lambda i: (i, 0)), _resident((dff, D)),
                  tok, tok, pl.BlockSpec((tm, 1), lambda i: (i, 0)), vec],
        out_specs=[tok, vec],
        out_shape=[jax.ShapeDtypeStruct((T, D), BF), jax.ShapeDtypeStruct((8, D), F32)],
        args=(dhpre, wup_t, dz2, xhat1, rstd1, vecs), push=push)


def _merge_bwd(dz1, p, ya, yb, conva, xhatb, rstdb, woa, wob, wo, vecs, push, tm=256):
    T = dz1.shape[0]

    def body(dz1_ref, ga_ref, gb_ref, ba_ref, ya_ref, yb_ref, conva_ref, xhat_ref, rstd_ref,
             woa_ref, wob_ref, wo_ref, vec_ref,
             dya_ref, dyb_ref, dg_ref, dba_ref, dconva_ref, du1_ref, st_ref):
        @pl.when(pl.program_id(0) == 0)
        def _():
            st_ref[...] = jnp.zeros_like(st_ref)

        dmerged = lax.dot_general(dz1_ref[...], wo_ref[...], NT_DIMS, preferred_element_type=F32)
        sa, sb = _sigmoid(ga_ref[...].astype(F32)), _sigmoid(gb_ref[...].astype(F32))
        dya = (dmerged * sa).astype(BF)
        dyb = (dmerged * sb).astype(BF)
        dya_ref[...] = dya
        dyb_ref[...] = dyb
        dg_ref[:, 0:D] = (dmerged * ya_ref[...].astype(F32) * (sa * (1.0 - sa))).astype(BF)
        dg_ref[:, D:2 * D] = (dmerged * yb_ref[...].astype(F32) * (sb * (1.0 - sb))).astype(BF)

        dyapre = lax.dot_general(dya, woa_ref[...], NT_DIMS, preferred_element_type=F32)
        dba_ref[...] = (dyapre * conva_ref[...]).astype(BF)
        dconva_ref[...] = dyapre * ba_ref[...].astype(F32)

        du3 = lax.dot_general(dyb, wob_ref[...], NT_DIMS, preferred_element_type=F32)
        xhat = xhat_ref[...]
        gamma = vec_ref[1:2, :]
        u2 = xhat * gamma + vec_ref[2:3, :]
        s = _sigmoid(u2)
        du2 = du3 * (s * (1.0 + u2 * (1.0 - s)))
        st_ref[0:1, :] += _colsum(du2 * xhat)
        st_ref[1:2, :] += _colsum(du2)
        du1 = _ln_bwd(du2, xhat, rstd_ref[...], gamma)
        st_ref[2:3, :] += _colsum(du1)
        du1_ref[...] = du1

    tok = pl.BlockSpec((tm, D), lambda i: (i, 0))
    wfull = pl.BlockSpec((D, D), lambda i: (0, 0))
    vec = pl.BlockSpec((8, D), lambda i: (0, 0))
    return _pallas(
        body, name="merge_bwd", grid=(T // tm,),
        in_specs=[tok, _seg(tm, 5), _seg(tm, 6), _seg(tm, 0), tok, tok, tok, tok, pl.BlockSpec((tm, 1), lambda i: (i, 0)),
                  wfull, wfull, wfull, vec],
        out_specs=[tok, tok, pl.BlockSpec((tm, 2 * D), lambda i: (i, 0)), tok, tok, tok, vec],
        out_shape=[jax.ShapeDtypeStruct((T, D), BF), jax.ShapeDtypeStruct((T, D), BF),
                   jax.ShapeDtypeStruct((T, 2 * D), BF), jax.ShapeDtypeStruct((T, D), BF),
                   jax.ShapeDtypeStruct((T, D), F32), jax.ShapeDtypeStruct((T, D), F32),
                   jax.ShapeDtypeStruct((8, D), F32)],
        args=(dz1, p, p, p, ya, yb, conva, xhatb, rstdb, woa, wob, wo, vecs), push=push)


def _rows8(v):
    out = v[0:8]
    for q in range(1, RC // 8):
        out = out + v[8 * q:8 * q + 8]
    return out


def _conv_bwd(dconva, du1, p, dba, dg, wa, wb, push, tt=256):
    T = p.shape[0]
    nsteps = T // tt

    def body(dca_ref, dca_n, du1_ref, du1_n, ca, va, vb, gb, dba_ref, dg_ref, wa_ref, wb_ref,
             dp_ref, gw_ref, cabuf, u0buf, dcabuf, du1buf, dcain, du0, gwa, gwb, shd):
        i = pl.program_id(0)
        first, last = i == 0, i == nsteps - 1

        @pl.when(first)
        def _():
            gwa[...] = jnp.zeros_like(gwa)
            gwb[...] = jnp.zeros_like(gwb)

        f = lambda ref: ref[...].astype(F32)
        cav, vav, vbv = f(ca), f(va), f(vb)
        cabuf[...] = cav * vav
        sg = _sigmoid(f(gb))
        u0buf[...] = vbv * sg
        dcabuf[0:tt, :] = dca_ref[...]
        dcabuf[tt:tt + HN, :] = jnp.where(last, 0.0, dca_n[...])
        du1buf[0:tt, :] = du1_ref[...]
        du1buf[tt:tt + HB, :] = jnp.where(last, 0.0, du1_n[...])

        def lane_body(cidx, carry):
            ls = pl.ds(pl.multiple_of(cidx * LANES, LANES), LANES)
            _shifted_copies(shd, du1buf, ls, tt + HB - 8)
            for r in range(tt // RC):
                rows = pl.ds(r * RC, RC)
                cin = cabuf[rows, ls]
                acc = jnp.zeros((RC, LANES), F32)
                for k in range(KA):
                    dout = dcabuf[pl.ds(r * RC + KA - 1 - k, RC), ls]
                    acc = acc + wa_ref[k:k + 1, ls] * dout
                    gwa[8 * k:8 * k + 8, ls] += _rows8(cin * dout)
                dcain[rows, ls] = acc
                uin = u0buf[rows, ls]
                acc = jnp.zeros((RC, LANES), F32)
                for k in range(KB):
                    dout = _tap(shd, du1buf, ls, r * RC + KB - 1 - k, RC)
                    acc = acc + wb_ref[k:k + 1, ls] * dout
                    gwb[8 * k:8 * k + 8, ls] += _rows8(uin * dout)
                du0[rows, ls] = acc
            return carry

        lax.fori_loop(0, D // LANES, lane_body, 0)
        dca_in = dcain[...]
        du0v = du0[...]
        dp_ref[:, 0:D] = dba_ref[...]
        dp_ref[:, D:2 * D] = (dca_in * vav).astype(BF)
        dp_ref[:, 2 * D:3 * D] = (dca_in * cav).astype(BF)
        dp_ref[:, 3 * D:4 * D] = (du0v * sg).astype(BF)
        dp_ref[:, 4 * D:5 * D] = (du0v * vbv * (sg * (1.0 - sg))).astype(BF)
        dp_ref[:, 5 * D:7 * D] = dg_ref[...]

        @pl.when(last)
        def _():
            gw_ref[...] = jnp.zeros_like(gw_ref)
            for k in range(KA):
                gw_ref[k:k + 1, :] = _colsum(gwa[8 * k:8 * k + 8, :])
            for k in range(KB):
                gw_ref[8 + k:9 + k, :] = _colsum(gwb[8 * k:8 * k + 8, :])

    full = lambda r: pl.BlockSpec((r, D), lambda i: (0, 0))
    tok = pl.BlockSpec((tt, D), lambda i: (i, 0))
    nxt = lambda h: pl.BlockSpec((h, D), lambda i: (jnp.minimum((i + 1) * (tt // h), T // h - 1), 0))
    return _pallas(
        body, name="conv_bwd", grid=(nsteps,),
        in_specs=[tok, nxt(HN), tok, nxt(HB),
                  _seg(tt, 1), _seg(tt, 2), _seg(tt, 3), _seg(tt, 4),
                  tok, pl.BlockSpec((tt, 2 * D), lambda i: (i, 0)), full(8), full(32)],
        out_specs=[pl.BlockSpec((tt, 7 * D), lambda i: (i, 0)), full(40)],
        out_shape=[jax.ShapeDtypeStruct((T, 7 * D), BF), jax.ShapeDtypeStruct((40, D), F32)],
        scratch_shapes=[pltpu.VMEM((tt, D), F32), pltpu.VMEM((tt, D), F32),
                        pltpu.VMEM((tt + HN, D), F32), pltpu.VMEM((tt + HB, D), F32),
                        pltpu.VMEM((tt, D), F32), pltpu.VMEM((tt, D), F32),
                        pltpu.VMEM((8 * KA, D), F32), pltpu.VMEM((8 * KB, D), F32),
                        pltpu.VMEM((8, HB + tt, LANES), F32)],
        args=(dconva, dconva, du1, du1, p, p, p, p, dba, dg, wa, wb), push=push)


def _inproj_bwd(dp, win_t, dz1, push, tm=512):
    T, cols = dp.shape

    def body(dp_ref, w_ref, dz1_ref, o_ref):
        o_ref[...] = ALPHA * dz1_ref[...].astype(F32) + jnp.dot(dp_ref[...], w_ref[...], preferred_element_type=F32)

    tok = pl.BlockSpec((tm, D), lambda i: (i, 0))
    return _pallas(
        body, name="inproj_bwd", grid=(T // tm,),
        in_specs=[pl.BlockSpec((tm, cols), lambda i: (i, 0)), _resident((cols, D)), tok],
        out_specs=[tok],
        out_shape=[jax.ShapeDtypeStruct((T, D), F32)],
        args=(dp, win_t, dz1), push=push)


def _adam_math(w, m, v, g):
    nm = ADAM_B1 * m + (1.0 - ADAM_B1) * g
    nv = ADAM_B2 * v + (1.0 - ADAM_B2) * (g * g)
    m_hat = nm / (1.0 - ADAM_B1 ** ADAM_STEP)
    v_hat = nv / (1.0 - ADAM_B2 ** ADAM_STEP)
    return -ADAM_LR * (m_hat / (jnp.sqrt(v_hat) + ADAM_EPS) + ADAM_WD * w), nm, nv


def _adamw(w, m, v, g32, landing, me, name, rb):
    R, C = w.shape
    nl = landing.shape[0]

    def body(me_ref, w_ref, m_ref, v_ref, own_ref, l_ref, g_ref, d_ref, nm_ref, nv_ref):
        g = own_ref[0]
        for k in range(nl):
            g = g + l_ref[k].astype(F32)
        g_ref[...] = g
        d_ref[...], nm_ref[...], nv_ref[...] = _adam_math(w_ref[...], m_ref[...], v_ref[...], g)

    blk = pl.BlockSpec((rb, C), lambda i, me_ref: (i, 0))
    grid_spec = pltpu.PrefetchScalarGridSpec(
        num_scalar_prefetch=1, grid=(R // rb,),
        in_specs=[blk, blk, blk, pl.BlockSpec((1, rb, C), lambda i, me_ref: (me_ref[0], i, 0)),
                  pl.BlockSpec((nl, rb, C), lambda i, me_ref: (0, i, 0))],
        out_specs=[blk] * 4)
    return pl.pallas_call(
        body, name=name, grid_spec=grid_spec, out_shape=[jax.ShapeDtypeStruct((R, C), F32)] * 4,
        compiler_params=_params(1))(me, w, m, v, g32, landing)


def _adamw_small(small_g, vec_w, vec_m, vec_v, conv_w, conv_m, conv_v):
    nv_ = len(vec_w)
    conv_rows = [(8, KA), (16, KB)]

    def body(*refs):
        g_ref = refs[0]
        w_refs, m_refs, v_refs = refs[1:10], refs[10:19], refs[19:28]
        out_refs, gsum = refs[28:64], refs[64]
        acc = g_ref[0]
        for j in range(1, NDEV):
            acc = acc + g_ref[j]
        gsum[...] = acc
        me = 4 * lax.axis_index("x") + 2 * lax.axis_index("y") + lax.axis_index("c")
        cols = pl.ds(pl.multiple_of(me * LANES, LANES), LANES)
        for i in range(nv_ + 2):
            if i < nv_:
                g = gsum[i:i + 1, :]
            else:
                r0, k = conv_rows[i - nv_]
                g = gsum[r0:r0 + k, cols]
            o = out_refs[4 * i:4 * i + 4]
            o[0][...] = g
            o[1][...], o[2][...], o[3][...] = _adam_math(w_refs[i][...], m_refs[i][...], v_refs[i][...], g)

    ws, ms, vs = list(vec_w) + list(conv_w), list(vec_m) + list(conv_m), list(vec_v) + list(conv_v)
    out_shape = [jax.ShapeDtypeStruct(w.shape, F32) for w in ws for _ in range(4)]
    return pl.pallas_call(
        body, name="adamw_small", out_shape=out_shape,
        scratch_shapes=[pltpu.VMEM(small_g.shape[1:], F32)])(small_g, *ws, *ms, *vs)


def _pad_rows(a, rows):
    return jnp.pad(a, ((0, rows - a.shape[0]), (0, 0)))


def _local_step(p, xb, mixed, post, x, target, win_t, wup, wup_t, wdown, woa, wob, wo, wa, wb, vecs):
    yapre, conva, xhatb, rstdb, u3 = mixed
    ya, yb, merged, xhat1, rstd1, x1b = post
    r, h = _mlp_up(x1b, wup)
    dz2b, st2 = _mlp_down_loss(h, wdown, xhat1, target, vecs)

    by_owner = lambda g16: g16.reshape(NDEV, D // NDEV, D)
    dhpre = _mlp_down_bwd(dz2b, wdown, r)
    g_wdown = _tn_matmul_tiles_outer(h, dz2b, NDEV, 512, D, True, "grad_w_down")
    (dz1b, st1), land_wdown = _mlp_up_bwd(dhpre, wup_t, dz2b, xhat1, rstd1, vecs, _push(exch=[g_wdown[1]]))
    g_wup = _tn_matmul_tiles_outer(x1b, dhpre, NDEV, D, 512, False, "grad_w_up")
    (dya, dyb, dg, dba, dconva, du1, stb), land_wup = _merge_bwd(
        dz1b, p, ya, yb, conva, xhatb, rstdb, woa, wob, wo, vecs, _push(exch=[g_wup[1]]))
    g_wo = _tn_matmul(merged, dz1b, 1, D, D, False, False, "grad_w_o")
    g_woa = _tn_matmul(yapre, dya, 1, D, D, False, False, "grad_w_out_a")
    g_wob = _tn_matmul(u3, dyb, 1, D, D, False, False, "grad_w_out_b")
    (dp, gw), land_sq = _conv_bwd(dconva, du1, p, dba, dg, wa, wb,
                                  _push(exch=[by_owner(g_woa[1]), by_owner(g_wob[1]), by_owner(g_wo[1])]))
    g_win = _tn_matmul(xb, dp, NDEV, D, 896, False, True, "grad_w_in")

    small = jnp.concatenate([stb[2:3], stb[0:2], st1[0:2], st2[0:3], gw], axis=0)
    (grad_x,), land_last = _inproj_bwd(dp, win_t, dz1b, _push(exch=[g_win[1]], gath=[small]))
    grads = (g_win[0], g_wup[0], g_wdown[0], g_woa[0], g_wob[0], g_wo[0])
    return grad_x, grads, small, land_wdown + land_wup + land_sq + land_last


def kernel(x, w_in, conv_a_w, w_out_a, conv_b_w, conv_b_bias, ln_b_gamma, ln_b_beta, w_out_b, w_o, ln1_gamma, ln1_beta, w_up, w_down, ln2_gamma, ln2_beta, loss_target, m_w_in, m_conv_a_w, m_w_out_a, m_conv_b_w, m_conv_b_bias, m_ln_b_gamma, m_ln_b_beta, m_w_out_b, m_w_o, m_ln1_gamma, m_ln1_beta, m_w_up, m_w_down, m_ln2_gamma, m_ln2_beta, v_w_in, v_conv_a_w, v_w_out_a, v_conv_b_w, v_conv_b_bias, v_ln_b_gamma, v_ln_b_beta, v_w_out_b, v_w_o, v_ln1_gamma, v_ln1_beta, v_w_up, v_w_down, v_ln2_gamma, v_ln2_beta):
    T = x.shape[1]
    me = 4 * lax.axis_index("x") + 2 * lax.axis_index("y") + lax.axis_index("c")

    conv_shard = jnp.concatenate([_pad_rows(conv_a_w, 8), _pad_rows(conv_b_w, 32)], axis=0)
    p, xb, win_g, conv_g = _inproj_gather(x[0], w_in.astype(BF), conv_shard)
    conv_full = jnp.transpose(conv_g, (1, 0, 2)).reshape(40, D)
    vecs = jnp.stack([conv_b_bias, ln_b_gamma, ln_b_beta, ln1_gamma, ln1_beta, ln2_gamma, ln2_beta,
                      jnp.zeros_like(ln2_beta)])
    whole = lambda g: jnp.transpose(g, (1, 0, 2)).reshape(D, -1)
    whole_t = lambda g: jnp.transpose(g, (0, 2, 1)).reshape(-1, D)
    mixed, (woa_g, wob_g, wo_g, wup_g, wdown_g) = _mixer_fwd(
        p, conv_full[0:8], conv_full[8:40], vecs,
        [w_out_a.astype(BF), w_out_b.astype(BF), w_o.astype(BF), w_up.astype(BF), w_down.astype(BF)])
    woa, wob, wo = woa_g.reshape(D, D), wob_g.reshape(D, D), wo_g.reshape(D, D)
    post, _ = _post_mixer(mixed[0], mixed[4], p, x[0], woa, wob, wo, vecs, [])

    grad_x, grads, small, landing = _local_step(
        p, xb, mixed, post, x[0], loss_target[0], whole_t(win_g), whole(wup_g), whole_t(wup_g),
        wdown_g.reshape(NDEV * 512, D),
        woa, wob, wo, conv_full[0:8], conv_full[8:40], vecs)
    g_win, g_wup, g_wdown, g_woa, g_wob, g_wo = grads
    l_wdown, l_wup, l_woa, l_wob, l_wo, l_win, small_g = landing

    loss = lax.psum(0.5 / D * jnp.sum(small[7]), ("x", "y", "c"))

    me1 = me.astype(jnp.int32).reshape(1)
    by_owner = lambda g32: g32.reshape(NDEV, D // NDEV, D)
    r_win = _adamw(w_in, m_w_in, v_w_in, g_win, l_win, me1, "adamw_w_in", 256)
    r_wup = _adamw(w_up, m_w_up, v_w_up, g_wup, l_wup, me1, "adamw_w_up", 256)
    r_wdown = _adamw(w_down, m_w_down, v_w_down, g_wdown, l_wdown, me1, "adamw_w_down", 256)
    r_woa = _adamw(w_out_a, m_w_out_a, v_w_out_a, by_owner(g_woa), l_woa, me1, "adamw_w_out_a", 128)
    r_wob = _adamw(w_out_b, m_w_out_b, v_w_out_b, by_owner(g_wob), l_wob, me1, "adamw_w_out_b", 128)
    r_wo = _adamw(w_o, m_w_o, v_w_o, by_owner(g_wo), l_wo, me1, "adamw_w_o", 128)

    row = lambda vec: vec.reshape(1, D)
    small_out = _adamw_small(
        small_g,
        [row(a) for a in (conv_b_bias, ln_b_gamma, ln_b_beta, ln1_gamma, ln1_beta, ln2_gamma, ln2_beta)],
        [row(a) for a in (m_conv_b_bias, m_ln_b_gamma, m_ln_b_beta, m_ln1_gamma, m_ln1_beta, m_ln2_gamma, m_ln2_beta)],
        [row(a) for a in (v_conv_b_bias, v_ln_b_gamma, v_ln_b_beta, v_ln1_gamma, v_ln1_beta, v_ln2_gamma, v_ln2_beta)],
        [conv_a_w, conv_b_w], [m_conv_a_w, m_conv_b_w], [v_conv_a_w, v_conv_b_w])
    r_vec = [[small_out[4 * i + q].reshape(D) for q in range(4)] for i in range(7)]
    r_conva, r_convb = small_out[28:32], small_out[32:36]

    per_weight = []
    for q in range(4):
        per_weight.append([
            r_win[q], r_conva[q], r_woa[q], r_convb[q],
            r_vec[0][q], r_vec[1][q], r_vec[2][q], r_wob[q], r_wo[q], r_vec[3][q], r_vec[4][q],
            r_wup[q], r_wdown[q], r_vec[5][q], r_vec[6][q]])
    return (loss, grad_x[None], *per_weight[0], *per_weight[1], *per_weight[2], *per_weight[3])
```

```python
import functools

import jax
import jax.numpy as jnp
from jax import lax
from jax.experimental import pallas as pl
from jax.experimental.pallas import tpu as pltpu

F32 = jnp.float32
BF = jnp.bfloat16
D = 1024
NDEV = 8
ALPHA = 2.0 ** 0.25
LN_EPS = 1e-5
KA, KB = 3, 31
HA, HB = 16, 32
HN = 8
RC = 64
LANES = 128
VMEM_LIMIT = 56 * 1024 * 1024
MESH = pl.DeviceIdType.MESH
ADAM_LR, ADAM_B1, ADAM_B2, ADAM_EPS, ADAM_WD, ADAM_STEP = 0.001, 0.9, 0.999, 1e-08, 0.01, 10

ANY_SPEC = pl.BlockSpec(memory_space=pl.ANY)
NT_DIMS = (((1,), (1,)), ((), ()))
TN_DIMS = (((0,), (0,)), ((), ()))


def _params(n_axes):
    return pltpu.CompilerParams(dimension_semantics=("arbitrary",) * n_axes, vmem_limit_bytes=VMEM_LIMIT)


def _sigmoid(v):
    return 0.5 * jnp.tanh(0.5 * v) + 0.5


def _ln_fwd(z):
    mu = jnp.mean(z, axis=-1, keepdims=True)
    zc = z - mu
    var = jnp.mean(zc * zc, axis=-1, keepdims=True)
    rstd = lax.rsqrt(var + LN_EPS)
    return zc * rstd, rstd


def _ln_bwd(dy, xhat, rstd, gamma):
    dxhat = dy * gamma
    m1 = jnp.mean(dxhat, axis=-1, keepdims=True)
    m2 = jnp.mean(dxhat * xhat, axis=-1, keepdims=True)
    return rstd * (dxhat - m1 - xhat * m2)


def _colsum(v):
    return jnp.sum(v, axis=0, keepdims=True)


class _TwoLevelGather:
    def __init__(self, ins, outs, send_sems, recv_sems, local_sems):
        self.ins, self.outs = ins, outs
        self.send_sems, self.recv_sems, self.local_sems = send_sems, recv_sems, local_sems
        x, y, c = lax.axis_index("x"), lax.axis_index("y"), lax.axis_index("c")
        self.me, self.sibling, self.c = (x, y, c), (x, y, 1 - c), c
        self.chips = [(1 - x, y), (x, 1 - y), (1 - x, 1 - y)]
        self.n = len(ins)

    @staticmethod
    def out_shape(arrs):
        return [jax.ShapeDtypeStruct((NDEV,) + a.shape, a.dtype) for a in arrs]

    @staticmethod
    def scratch(n):
        return [pltpu.SemaphoreType.DMA((n, 7)), pltpu.SemaphoreType.DMA((n, 7)), pltpu.SemaphoreType.DMA((n,))]

    def _copy(self, a, k, block, to, src=None):
        px, py, pc = block
        rows = self.outs[a].at[4 * px + 2 * py + pc]
        return pltpu.make_async_remote_copy(
            src_ref=rows if src is None else src, dst_ref=rows,
            send_sem=self.send_sems.at[a, k], recv_sem=self.recv_sems.at[a, k],
            device_id=to, device_id_type=MESH)

    def _mine(self, a):
        x, y, c = self.me
        return pltpu.make_async_copy(self.ins[a], self.outs[a].at[4 * x + 2 * y + c], self.local_sems.at[a])

    def _first(self, a):
        cps = [self._copy(a, 0, self.me, self.sibling, src=self.ins[a])]
        return cps + [self._copy(a, 1 + j, self.me, (*chip, self.c), src=self.ins[a]) for j, chip in enumerate(self.chips)]

    def _passed(self, a, j):
        return self._copy(a, 4 + j, (*self.chips[j], self.c), self.sibling)

    def start(self, diagonal=True):
        for a in range(self.n):
            self._mine(a).start()
        for a in range(self.n):
            for cp in self._first(a)[:4 if diagonal else 3]:
                cp.start()

    def start_diagonal(self):
        for a in range(self.n):
            self._first(a)[3].start()

    def wait_ici(self, j):
        for a in range(self.n):
            self._copy(a, 1 + j, (*self.chips[j], self.c), self.me).wait_recv()

    def pass_on(self, j):
        for a in range(self.n):
            self._passed(a, j).start()

    def wait_sibling(self):
        for a in range(self.n):
            self._copy(a, 0, self.sibling, self.me).wait_recv()

    def wait_passed(self, j):
        for a in range(self.n):
            self._copy(a, 4 + j, (*self.chips[j], 1 - self.c), self.me).wait_recv()

    def drain(self):
        for a in range(self.n):
            for cp in self._first(a) + [self._passed(a, j) for j in range(3)]:
                cp.wait_send()
            self._mine(a).wait()

    def forward(self):
        for j in range(3):
            self.wait_ici(j)
            self.pass_on(j)

    def finish(self):
        self.wait_sibling()
        for j in range(3):
            self.wait_passed(j)
        self.drain()


class _Push:
    def __init__(self, exch=(), gath=()):
        self.exch, self.gath = list(exch), list(gath)
        self.n = len(self.exch) + len(self.gath)

    def operands(self):
        return self.exch + self.gath

    def out_shape(self):
        return ([jax.ShapeDtypeStruct((NDEV - 1,) + a.shape[1:], a.dtype) for a in self.exch]
                + [jax.ShapeDtypeStruct((NDEV,) + a.shape, a.dtype) for a in self.gath])

    def scratch(self):
        return [pltpu.SemaphoreType.DMA((self.n, 7)), pltpu.SemaphoreType.DMA((self.n, 7)),
                pltpu.SemaphoreType.DMA((max(len(self.gath), 1),))]

    def copies(self, ins, outs, send_sems, recv_sems, local_sems):
        x, y, c = lax.axis_index("x"), lax.axis_index("y"), lax.axis_index("c")
        me = 4 * x + 2 * y + c
        ne = len(self.exch)
        remote = []
        for k in range(1, NDEV):
            px = 1 - x if k & 4 else x
            py = 1 - y if k & 2 else y
            pc = 1 - c if k & 1 else c
            for a in range(self.n):
                src = ins[a].at[4 * px + 2 * py + pc] if a < ne else ins[a]
                dst = outs[a].at[k - 1] if a < ne else outs[a].at[me]
                remote.append(pltpu.make_async_remote_copy(
                    src_ref=src, dst_ref=dst, send_sem=send_sems.at[a, k - 1], recv_sem=recv_sems.at[a, k - 1],
                    device_id=(px, py, pc), device_id_type=MESH))
        local = [pltpu.make_async_copy(ins[a], outs[a].at[me], local_sems.at[a - ne]) for a in range(ne, self.n)]
        return remote, local


def _push(exch=(), gath=()):
    return _Push(exch, gath)


def _pallas(body, *, name, grid, in_specs, out_specs, out_shape, args, scratch_shapes=(), push=None):
    ni, no, ns = len(in_specs), len(out_specs), len(scratch_shapes)
    if push is None:
        outs = pl.pallas_call(
            body, name=name, grid=grid, in_specs=in_specs, out_specs=out_specs, out_shape=out_shape,
            scratch_shapes=list(scratch_shapes), compiler_params=_params(len(grid)))(*args)
        return list(outs), []
    npush = push.n

    def wrapped(*refs):
        ins, pins = refs[:ni], refs[ni:ni + npush]
        outs, pouts = refs[ni + npush:ni + npush + no], refs[ni + npush + no:ni + 2 * npush + no]
        scr, sems = refs[ni + 2 * npush + no:ni + 2 * npush + no + ns], refs[ni + 2 * npush + no + ns:]
        first = functools.reduce(jnp.logical_and, [pl.program_id(d) == 0 for d in range(len(grid))])
        last = functools.reduce(jnp.logical_and, [pl.program_id(d) == grid[d] - 1 for d in range(len(grid))])
        remote, local = push.copies(pins, pouts, *sems)

        @pl.when(first)
        def _():
            for cp in local + remote:
                cp.start()

        body(*ins, *outs, *scr)

        @pl.when(last)
        def _():
            for cp in remote + local:
                cp.wait()

    outs = pl.pallas_call(
        wrapped, name=name, grid=grid,
        in_specs=list(in_specs) + [ANY_SPEC] * npush, out_specs=list(out_specs) + [ANY_SPEC] * npush,
        out_shape=list(out_shape) + push.out_shape(), scratch_shapes=list(scratch_shapes) + push.scratch(),
        compiler_params=_params(len(grid)))(*args, *push.operands())
    return list(outs[:no]), list(outs[no:])


def _with_gather(body, late, *, name, nsteps, in_specs, out_specs, out_shape, args, scratch_shapes=()):
    ni, no, ns, n = len(in_specs), len(out_specs), len(scratch_shapes), len(late)
    pass_step = (7 * nsteps) // 8
    if not late:
        outs = pl.pallas_call(
            body, name=name, grid=(nsteps,), in_specs=in_specs, out_specs=out_specs, out_shape=out_shape,
            scratch_shapes=list(scratch_shapes), compiler_params=_params(1))(*args)
        return list(outs), []

    def wrapped(*refs):
        ins, outs = refs[:ni], refs[ni + n:ni + n + no]
        scr = refs[ni + 2 * n + no:ni + 2 * n + no + ns]
        gather = _TwoLevelGather(refs[ni:ni + n], refs[ni + n + no:ni + 2 * n + no], *refs[ni + 2 * n + no + ns:])
        step = pl.program_id(0)
        pl.when(step == 0)(gather.start)
        pl.when(step == pass_step)(gather.forward)
        body(*ins, *outs, *scr)
        pl.when(step == nsteps - 1)(gather.finish)

    outs = pl.pallas_call(
        wrapped, name=name, grid=(nsteps,),
        in_specs=list(in_specs) + [ANY_SPEC] * n, out_specs=list(out_specs) + [ANY_SPEC] * n,
        out_shape=list(out_shape) + _TwoLevelGather.out_shape(late),
        scratch_shapes=list(scratch_shapes) + _TwoLevelGather.scratch(n),
        compiler_params=_params(1))(*args, *late)
    return list(outs[:no]), list(outs[no:])


def _inproj_gather(x, w_shard, conv_shard, tm=1024):
    T = x.shape[0]
    ni = T // tm
    bw = w_shard.shape[1]
    cx, cy, cc = lax.axis_index("x"), lax.axis_index("y"), lax.axis_index("c")
    blk = lambda px, py, pc: 4 * px + 2 * py + pc
    order = [blk(cx, cy, cc), blk(cx, cy, 1 - cc)]
    for chip in [(1 - cx, cy), (cx, 1 - cy), (1 - cx, 1 - cy)]:
        order += [blk(*chip, cc), blk(*chip, 1 - cc)]
    order = jnp.stack(order).astype(jnp.int32)

    def body(order_ref, x_ref, w_ref, conv_ref, p_ref, xb_ref, wing_ref, convg_ref, xbs, wbuf, wsem, *sems):
        gather = _TwoLevelGather([w_ref, conv_ref], [wing_ref, convg_ref], *sems)
        j, i = pl.program_id(0), pl.program_id(1)

        def load(src):
            cp = pltpu.make_async_copy(src, wbuf, wsem)
            cp.start()
            cp.wait()

        def arrival(jj):
            if jj == 0:
                gather.start(diagonal=False)
                load(w_ref)
                return
            if jj == 1:
                gather.wait_sibling()
            elif jj % 2 == 0:
                if jj == 2:
                    gather.start_diagonal()
                gather.wait_ici(jj // 2 - 1)
                gather.pass_on(jj // 2 - 1)
            else:
                gather.wait_passed(jj // 2 - 1)
            load(wing_ref.at[order_ref[jj]])

        for jj in range(NDEV):
            pl.when(jnp.logical_and(j == jj, i == 0))(functools.partial(arrival, jj))

        @pl.when(j == 0)
        def _():
            xb = x_ref[...].astype(BF)
            xbs[i] = xb
            xb_ref[...] = xb

        p_ref[...] = jnp.dot(xbs[i], wbuf[...], preferred_element_type=F32).astype(BF)

        @pl.when(jnp.logical_and(j == NDEV - 1, i == ni - 1))
        def _():
            gather.drain()

    rows_once = lambda j, i, o: (jnp.where(j == 0, i, ni - 1), 0)
    grid_spec = pltpu.PrefetchScalarGridSpec(
        num_scalar_prefetch=1, grid=(NDEV, ni),
        in_specs=[pl.BlockSpec((tm, D), rows_once), ANY_SPEC, ANY_SPEC],
        out_specs=[pl.BlockSpec((tm, bw), lambda j, i, o: (i, o[j])), pl.BlockSpec((tm, D), rows_once),
                   ANY_SPEC, ANY_SPEC],
        scratch_shapes=[pltpu.VMEM((ni, tm, D), BF), pltpu.VMEM((D, bw), BF), pltpu.SemaphoreType.DMA(())]
        + _TwoLevelGather.scratch(2))
    p, xb, win_g, conv_g = pl.pallas_call(
        body, name="inproj_gather", grid_spec=grid_spec,
        out_shape=[jax.ShapeDtypeStruct((T, NDEV * bw), BF), jax.ShapeDtypeStruct((T, D), BF)]
        + _TwoLevelGather.out_shape([w_shard, conv_shard]),
        compiler_params=_params(2))(order, x, w_shard, conv_shard)
    return p, xb, win_g, conv_g


def _mixer_fwd(p, wa, wb, vecs, late, tt=256):
    T = p.shape[0]
    nt = T // tt

    def body(ba, ca, va, vb, gb, ca_p, va_p, vb_p, gb_p, wa_ref, wb_ref, vec_ref,
             yapre_ref, conva_ref, xhat_ref, rstd_ref, u3_ref, cabuf, u0buf, u1buf, shu):
        first = pl.program_id(0) == 0
        f = lambda ref: ref[...].astype(F32)
        cabuf[0:HA, :] = jnp.where(first, 0.0, f(ca_p) * f(va_p))
        cabuf[HA:HA + tt, :] = f(ca) * f(va)
        u0buf[0:HB, :] = jnp.where(first, 0.0, f(vb_p) * _sigmoid(f(gb_p)))
        u0buf[HB:HB + tt, :] = f(vb) * _sigmoid(f(gb))

        def lane_body(cidx, carry):
            ls = pl.ds(pl.multiple_of(cidx * LANES, LANES), LANES)
            _shifted_copies(shu, u0buf, ls, tt + HB - 8)
            for r in range(tt // RC):
                acc = jnp.zeros((RC, LANES), F32)
                for k in range(KA):
                    acc = acc + wa_ref[k:k + 1, ls] * cabuf[pl.ds(HA - (KA - 1) + k + r * RC, RC), ls]
                conva_ref[pl.ds(r * RC, RC), ls] = acc
                acc = jnp.zeros((RC, LANES), F32)
                for k in range(KB):
                    acc = acc + wb_ref[k:k + 1, ls] * _tap(shu, u0buf, ls, HB - (KB - 1) + k + r * RC, RC)
                u1buf[pl.ds(r * RC, RC), ls] = acc
            return carry

        lax.fori_loop(0, D // LANES, lane_body, 0)
        yapre_ref[...] = (f(ba) * conva_ref[...]).astype(BF)
        xhat, rstd = _ln_fwd(u1buf[...] + vec_ref[0:1, :])
        xhat_ref[...] = xhat
        rstd_ref[...] = rstd
        u2 = xhat * vec_ref[1:2, :] + vec_ref[2:3, :]
        u3_ref[...] = (u2 * _sigmoid(u2)).astype(BF)

    full = lambda r: pl.BlockSpec((r, D), lambda i: (0, 0))
    tok = pl.BlockSpec((tt, D), lambda i: (i, 0))
    return _with_gather(
        body, late, name="mixer_fwd", nsteps=nt,
        in_specs=[_seg(tt, 0), _seg(tt, 1), _seg(tt, 2), _seg(tt, 3), _seg(tt, 4),
                  _prev(tt, HA, 1), _prev(tt, HA, 2), _prev(tt, HB, 3), _prev(tt, HB, 4),
                  full(8), full(32), full(8)],
        out_specs=[tok, tok, tok, pl.BlockSpec((tt, 1), lambda i: (i, 0)), tok],
        out_shape=[jax.ShapeDtypeStruct((T, D), BF), jax.ShapeDtypeStruct((T, D), F32),
                   jax.ShapeDtypeStruct((T, D), F32), jax.ShapeDtypeStruct((T, 1), F32),
                   jax.ShapeDtypeStruct((T, D), BF)],
        scratch_shapes=[pltpu.VMEM((HA + tt, D), F32), pltpu.VMEM((HB + tt, D), F32), pltpu.VMEM((tt, D), F32),
                        pltpu.VMEM((8, HB + tt, LANES), F32)],
        args=(p, p, p, p, p, p, p, p, p, wa, wb, vecs))


def _seg(tt, s):
    return pl.BlockSpec((tt, D), lambda i: (i, s))


def _prev(tt, h, s):
    return pl.BlockSpec((h, D), lambda i: (jnp.maximum(i * (tt // h) - 1, 0), s))


def _shifted_copies(shbuf, src, ls, n):
    for s in range(1, 8):
        shbuf[s, 0:n, :] = src[pl.ds(s, n), ls]


def _tap(shbuf, src, ls, off, rows):
    s, q = off % 8, off // 8
    if s == 0:
        return src[pl.ds(off, rows), ls]
    return shbuf[s, pl.ds(8 * q, rows), :]


def _post_mixer(yapre, u3, p, x, woa, wob, wo, vecs, late, tm=512):
    T = x.shape[0]

    def body(yapre_ref, u3_ref, ga_ref, gb_ref, x_ref, woa_ref, wob_ref, wo_ref, vec_ref,
             ya_ref, yb_ref, merged_ref, xhat_ref, rstd_ref, x1b_ref):
        ya = jnp.dot(yapre_ref[...], woa_ref[...], preferred_element_type=F32)
        yb = jnp.dot(u3_ref[...], wob_ref[...], preferred_element_type=F32)
        ya_ref[...] = ya.astype(BF)
        yb_ref[...] = yb.astype(BF)
        merged = (_sigmoid(ga_ref[...].astype(F32)) * ya + _sigmoid(gb_ref[...].astype(F32)) * yb).astype(BF)
        merged_ref[...] = merged
        mix = jnp.dot(merged, wo_ref[...], preferred_element_type=F32)
        xhat, rstd = _ln_fwd(ALPHA * x_ref[...] + mix)
        xhat_ref[...] = xhat
        rstd_ref[...] = rstd
        x1b_ref[...] = (xhat * vec_ref[3:4, :] + vec_ref[4:5, :]).astype(BF)

    tok = pl.BlockSpec((tm, D), lambda i: (i, 0))
    wfull = _resident((D, D))
    one = pl.BlockSpec((tm, 1), lambda i: (i, 0))
    return _with_gather(
        body, late, name="post_mixer", nsteps=T // tm,
        in_specs=[tok, tok, _seg(tm, 5), _seg(tm, 6), tok, wfull, wfull, wfull, pl.BlockSpec((8, D), lambda i: (0, 0))],
        out_specs=[tok, tok, tok, tok, one, tok],
        out_shape=[jax.ShapeDtypeStruct((T, D), BF), jax.ShapeDtypeStruct((T, D), BF),
                   jax.ShapeDtypeStruct((T, D), BF), jax.ShapeDtypeStruct((T, D), F32),
                   jax.ShapeDtypeStruct((T, 1), F32), jax.ShapeDtypeStruct((T, D), BF)],
        args=(yapre, u3, p, p, x, woa, wob, wo, vecs))


def _mlp_up(x1b, wup, tm=512, tn=2048):
    T = x1b.shape[0]
    dff = wup.shape[1]

    def body(x_ref, w_ref, r_ref, h_ref):
        r = jnp.maximum(jnp.dot(x_ref[...], w_ref[...], preferred_element_type=F32), 0.0)
        r_ref[...] = r.astype(BF)
        h_ref[...] = (r * r).astype(BF)

    out = pl.BlockSpec((tm, tn), lambda j, i: (i, j))
    return pl.pallas_call(
        body, name="mlp_up", grid=(dff // tn, T // tm),
        in_specs=[pl.BlockSpec((tm, D), lambda j, i: (i, 0)), pl.BlockSpec((D, tn), lambda j, i: (0, j))],
        out_specs=[out, out],
        out_shape=[jax.ShapeDtypeStruct((T, dff), BF), jax.ShapeDtypeStruct((T, dff), BF)],
        compiler_params=_params(2))(x1b, wup)


def _resident(shape):
    return pl.BlockSpec(shape, lambda *_: (0,) * len(shape), pipeline_mode=pl.Buffered(1))


def _mlp_down_loss(h, wdown, xhat1, target, vecs, tm=512):
    T, dff = h.shape

    def body(h_ref, w_ref, xhat1_ref, tgt_ref, vec_ref, dz2b_ref, st_ref):
        @pl.when(pl.program_id(0) == 0)
        def _():
            st_ref[...] = jnp.zeros_like(st_ref)

        ff = jnp.dot(h_ref[...], w_ref[...], preferred_element_type=F32)
        x1 = xhat1_ref[...] * vec_ref[3:4, :] + vec_ref[4:5, :]
        xhat2, rstd2 = _ln_fwd(ALPHA * x1 + ff)
        g2 = vec_ref[5:6, :]
        diff = xhat2 * g2 + vec_ref[6:7, :] - tgt_ref[...]
        dx2 = diff * (1.0 / D)
        st_ref[0:1, :] += _colsum(dx2 * xhat2)
        st_ref[1:2, :] += _colsum(dx2)
        st_ref[2:3, :] += _colsum(diff * diff)
        dz2b_ref[...] = _ln_bwd(dx2, xhat2, rstd2, g2).astype(BF)

    tok = pl.BlockSpec((tm, D), lambda i: (i, 0))
    vec = pl.BlockSpec((8, D), lambda i: (0, 0))
    return pl.pallas_call(
        body, name="mlp_down_loss", grid=(T // tm,),
        in_specs=[pl.BlockSpec((tm, dff), lambda i: (i, 0)), _resident((dff, D)), tok, tok, vec],
        out_specs=[tok, vec],
        out_shape=[jax.ShapeDtypeStruct((T, D), BF), jax.ShapeDtypeStruct((8, D), F32)],
        compiler_params=_params(1))(h, wdown, xhat1, target, vecs)


def _mlp_down_bwd(dz2b, wdown, r, tm=512, tk=2048):
    T, dff = r.shape

    def body(dz_ref, w_ref, r_ref, o_ref):
        dh = lax.dot_general(dz_ref[...], w_ref[...], NT_DIMS, preferred_element_type=F32)
        o_ref[...] = (dh * (2.0 * r_ref[...].astype(F32))).astype(BF)

    blk = pl.BlockSpec((tm, tk), lambda j, i: (i, j))
    return pl.pallas_call(
        body, name="mlp_down_bwd", grid=(dff // tk, T // tm),
        in_specs=[pl.BlockSpec((tm, D), lambda j, i: (i, 0)), pl.BlockSpec((tk, D), lambda j, i: (j, 0)), blk],
        out_specs=blk,
        out_shape=jax.ShapeDtypeStruct((T, dff), BF),
        compiler_params=_params(2))(dz2b, wdown, r)


def _tn_matmul(a, b, nblk, a_bw, b_bw, a_blocked, b_blocked, name, tt=2048):
    T = a.shape[0]
    nt = T // tt

    def body(a_ref, b_ref, o32_ref, o16_ref):
        t = pl.program_id(1)

        @pl.when(t == 0)
        def _():
            o32_ref[...] = jnp.zeros_like(o32_ref)

        o32_ref[0] += lax.dot_general(a_ref[...], b_ref[...], TN_DIMS, preferred_element_type=F32)

        @pl.when(t == nt - 1)
        def _():
            o16_ref[...] = o32_ref[...].astype(BF)

    a_spec = pl.BlockSpec((tt, a_bw), (lambda j, t: (t, j)) if a_blocked else (lambda j, t: (t, 0)))
    b_spec = pl.BlockSpec((tt, b_bw), (lambda j, t: (t, j)) if b_blocked else (lambda j, t: (t, 0)))
    out = pl.BlockSpec((1, a_bw, b_bw), lambda j, t: (j, 0, 0))
    return pl.pallas_call(
        body, name=name, grid=(nblk, nt),
        in_specs=[a_spec, b_spec], out_specs=[out, out],
        out_shape=[jax.ShapeDtypeStruct((nblk, a_bw, b_bw), F32), jax.ShapeDtypeStruct((nblk, a_bw, b_bw), BF)],
        compiler_params=_params(2))(a, b)


def _grad_w_in_send(xb, dp, bw, tt=2048):
    T = xb.shape[0]
    nt = T // tt
    flip = lambda v, bit: 1 - v if bit else v

    def peer(k):
        return (flip(lax.axis_index("x"), k & 4), flip(lax.axis_index("y"), k & 2), flip(lax.axis_index("c"), k & 1))

    block_of = lambda dev: 4 * dev[0] + 2 * dev[1] + dev[2]
    order = jnp.stack([block_of(peer(NDEV - 1 - q)) for q in range(NDEV)]).astype(jnp.int32)

    def body(order_ref, a_ref, b_ref, o32_ref, land_ref, stage, send_sems, recv_sems):
        q, t = pl.program_id(0), pl.program_id(1)

        def copy(k):
            return pltpu.make_async_remote_copy(
                src_ref=stage.at[NDEV - 1 - k], dst_ref=land_ref.at[k - 1],
                send_sem=send_sems.at[k - 1], recv_sem=recv_sems.at[k - 1],
                device_id=peer(k), device_id_type=MESH)

        @pl.when(t == 0)
        def _():
            o32_ref[...] = jnp.zeros_like(o32_ref)

        o32_ref[0] += lax.dot_general(a_ref[...], b_ref[...], TN_DIMS, preferred_element_type=F32)

        def send(qq):
            stage[qq] = o32_ref[0].astype(BF)
            copy(NDEV - 1 - qq).start()

        for qq in range(NDEV - 1):
            pl.when(jnp.logical_and(q == qq, t == nt - 1))(functools.partial(send, qq))

        @pl.when(jnp.logical_and(q == NDEV - 1, t == nt - 1))
        def _():
            for k in range(1, NDEV):
                copy(k).wait()

    grid_spec = pltpu.PrefetchScalarGridSpec(
        num_scalar_prefetch=1, grid=(NDEV, nt),
        in_specs=[pl.BlockSpec((tt, D), lambda q, t, o: (t, 0)), pl.BlockSpec((tt, bw), lambda q, t, o: (t, o[q]))],
        out_specs=[pl.BlockSpec((1, D, bw), lambda q, t, o: (o[q], 0, 0)), ANY_SPEC],
        scratch_shapes=[pltpu.VMEM((NDEV - 1, D, bw), BF), pltpu.SemaphoreType.DMA((NDEV - 1,)),
                        pltpu.SemaphoreType.DMA((NDEV - 1,))])
    return pl.pallas_call(
        body, name="grad_w_in", grid_spec=grid_spec,
        out_shape=[jax.ShapeDtypeStruct((NDEV, D, bw), F32), jax.ShapeDtypeStruct((NDEV - 1, D, bw), BF)],
        compiler_params=_params(2))(order, xb, dp)


def _tn_matmul_tiles_outer(a, b, nblk, a_bw, b_bw, a_blocked, name, tt=2048):
    T = a.shape[0]
    nt = T // tt

    def body(a_ref, b_ref, o32_ref, o16_ref, acc):
        t, j = pl.program_id(0), pl.program_id(1)
        prod = lax.dot_general(a_ref[...], b_ref[...], TN_DIMS, preferred_element_type=F32)

        @pl.when(t == 0)
        def _():
            acc[j] = prod

        @pl.when(t > 0)
        def _():
            acc[j] += prod

        @pl.when(t == nt - 1)
        def _():
            o32_ref[0] = acc[j]
            o16_ref[0] = acc[j].astype(BF)

    a_spec = pl.BlockSpec((tt, a_bw), (lambda t, j: (t, j)) if a_blocked else (lambda t, j: (t, 0)))
    b_spec = pl.BlockSpec((tt, b_bw), (lambda t, j: (t, 0)) if a_blocked else (lambda t, j: (t, j)))
    out = pl.BlockSpec((1, a_bw, b_bw), lambda t, j: (jnp.where(t == nt - 1, j, 0), 0, 0))
    return pl.pallas_call(
        body, name=name, grid=(nt, nblk),
        in_specs=[a_spec, b_spec], out_specs=[out, out],
        out_shape=[jax.ShapeDtypeStruct((nblk, a_bw, b_bw), F32), jax.ShapeDtypeStruct((nblk, a_bw, b_bw), BF)],
        scratch_shapes=[pltpu.VMEM((nblk, a_bw, b_bw), F32)],
        compiler_params=_params(2))(a, b)


def _mlp_up_bwd(dhpre, wup_t, dz2, xhat1, rstd1, vecs, push, tm=512):
    T, dff = dhpre.shape

    def body(dh_ref, w_ref, dz2_ref, xhat_ref, rstd_ref, vec_ref, dz1b_ref, st_ref):
        @pl.when(pl.program_id(0) == 0)
        def _():
            st_ref[...] = jnp.zeros_like(st_ref)

        dx1 = jnp.dot(dh_ref[...], w_ref[...], preferred_element_type=F32) + ALPHA * dz2_ref[...].astype(F32)
        xhat = xhat_ref[...]
        st_ref[0:1, :] += _colsum(dx1 * xhat)
        st_ref[1:2, :] += _colsum(dx1)
        dz1b_ref[...] = _ln_bwd(dx1, xhat, rstd_ref[...], vec_ref[3:4, :]).astype(BF)

    tok = pl.BlockSpec((tm, D), lambda i: (i, 0))
    vec = pl.BlockSpec((8, D), lambda i: (0, 0))
    return _pallas(
        body, name="mlp_up_bwd", grid=(T // tm,),
        in_specs=[pl.BlockSpec((tm, dff), lambda i: (i, 0)), _resident((dff, D)),
                  tok, tok, pl.BlockSpec((tm, 1), lambda i: (i, 0)), vec],
        out_specs=[tok, vec],
        out_shape=[jax.ShapeDtypeStruct((T, D), BF), jax.ShapeDtypeStruct((8, D), F32)],
        args=(dhpre, wup_t, dz2, xhat1, rstd1, vecs), push=push)


def _merge_bwd(dz1, p, ya, yb, conva, xhatb, rstdb, woa, wob, wo, vecs, push, tm=256):
    T = dz1.shape[0]

    def body(dz1_ref, ga_ref, gb_ref, ba_ref, ya_ref, yb_ref, conva_ref, xhat_ref, rstd_ref,
             woa_ref, wob_ref, wo_ref, vec_ref,
             dya_ref, dyb_ref, dg_ref, dba_ref, dconva_ref, du1_ref, st_ref):
        @pl.when(pl.program_id(0) == 0)
        def _():
            st_ref[...] = jnp.zeros_like(st_ref)

        dmerged = lax.dot_general(dz1_ref[...], wo_ref[...], NT_DIMS, preferred_element_type=F32)
        sa, sb = _sigmoid(ga_ref[...].astype(F32)), _sigmoid(gb_ref[...].astype(F32))
        dya = (dmerged * sa).astype(BF)
        dyb = (dmerged * sb).astype(BF)
        dya_ref[...] = dya
        dyb_ref[...] = dyb
        dg_ref[:, 0:D] = (dmerged * ya_ref[...].astype(F32) * (sa * (1.0 - sa))).astype(BF)
        dg_ref[:, D:2 * D] = (dmerged * yb_ref[...].astype(F32) * (sb * (1.0 - sb))).astype(BF)

        dyapre = lax.dot_general(dya, woa_ref[...], NT_DIMS, preferred_element_type=F32)
        dba_ref[...] = (dyapre * conva_ref[...]).astype(BF)
        dconva_ref[...] = dyapre * ba_ref[...].astype(F32)

        du3 = lax.dot_general(dyb, wob_ref[...], NT_DIMS, preferred_element_type=F32)
        xhat = xhat_ref[...]
        gamma = vec_ref[1:2, :]
        u2 = xhat * gamma + vec_ref[2:3, :]
        s = _sigmoid(u2)
        du2 = du3 * (s * (1.0 + u2 * (1.0 - s)))
        st_ref[0:1, :] += _colsum(du2 * xhat)
        st_ref[1:2, :] += _colsum(du2)
        du1 = _ln_bwd(du2, xhat, rstd_ref[...], gamma)
        st_ref[2:3, :] += _colsum(du1)
        du1_ref[...] = du1

    tok = pl.BlockSpec((tm, D), lambda i: (i, 0))
    wfull = pl.BlockSpec((D, D), lambda i: (0, 0))
    vec = pl.BlockSpec((8, D), lambda i: (0, 0))
    return _pallas(
        body, name="merge_bwd", grid=(T // tm,),
        in_specs=[tok, _seg(tm, 5), _seg(tm, 6), _seg(tm, 0), tok, tok, tok, tok, pl.BlockSpec((tm, 1), lambda i: (i, 0)),
                  wfull, wfull, wfull, vec],
        out_specs=[tok, tok, pl.BlockSpec((tm, 2 * D), lambda i: (i, 0)), tok, tok, tok, vec],
        out_shape=[jax.ShapeDtypeStruct((T, D), BF), jax.ShapeDtypeStruct((T, D), BF),
                   jax.ShapeDtypeStruct((T, 2 * D), BF), jax.ShapeDtypeStruct((T, D), BF),
                   jax.ShapeDtypeStruct((T, D), F32), jax.ShapeDtypeStruct((T, D), F32),
                   jax.ShapeDtypeStruct((8, D), F32)],
        args=(dz1, p, p, p, ya, yb, conva, xhatb, rstdb, woa, wob, wo, vecs), push=push)


def _rows8(v):
    out = v[0:8]
    for q in range(1, RC // 8):
        out = out + v[8 * q:8 * q + 8]
    return out


def _conv_bwd(dconva, du1, p, dba, dg, wa, wb, push, tt=256):
    T = p.shape[0]
    nsteps = T // tt

    def body(dca_ref, dca_n, du1_ref, du1_n, ca, va, vb, gb, dba_ref, dg_ref, wa_ref, wb_ref,
             dp_ref, gw_ref, cabuf, u0buf, dcabuf, du1buf, dcain, du0, gwa, gwb, shd):
        i = pl.program_id(0)
        first, last = i == 0, i == nsteps - 1

        @pl.when(first)
        def _():
            gwa[...] = jnp.zeros_like(gwa)
            gwb[...] = jnp.zeros_like(gwb)

        f = lambda ref: ref[...].astype(F32)
        cav, vav, vbv = f(ca), f(va), f(vb)
        cabuf[...] = cav * vav
        sg = _sigmoid(f(gb))
        u0buf[...] = vbv * sg
        dcabuf[0:tt, :] = dca_ref[...]
        dcabuf[tt:tt + HN, :] = jnp.where(last, 0.0, dca_n[...])
        du1buf[0:tt, :] = du1_ref[...]
        du1buf[tt:tt + HB, :] = jnp.where(last, 0.0, du1_n[...])

        def lane_body(cidx, carry):
            ls = pl.ds(pl.multiple_of(cidx * LANES, LANES), LANES)
            _shifted_copies(shd, du1buf, ls, tt + HB - 8)
            for r in range(tt // RC):
                rows = pl.ds(r * RC, RC)
                cin = cabuf[rows, ls]
                acc = jnp.zeros((RC, LANES), F32)
                for k in range(KA):
                    dout = dcabuf[pl.ds(r * RC + KA - 1 - k, RC), ls]
                    acc = acc + wa_ref[k:k + 1, ls] * dout
                    gwa[8 * k:8 * k + 8, ls] += _rows8(cin * dout)
                dcain[rows, ls] = acc
                uin = u0buf[rows, ls]
                acc = jnp.zeros((RC, LANES), F32)
                for k in range(KB):
                    dout = _tap(shd, du1buf, ls, r * RC + KB - 1 - k, RC)
                    acc = acc + wb_ref[k:k + 1, ls] * dout
                    gwb[8 * k:8 * k + 8, ls] += _rows8(uin * dout)
                du0[rows, ls] = acc
            return carry

        lax.fori_loop(0, D // LANES, lane_body, 0)
        dca_in = dcain[...]
        du0v = du0[...]
        dp_ref[:, 0:D] = dba_ref[...]
        dp_ref[:, D:2 * D] = (dca_in * vav).astype(BF)
        dp_ref[:, 2 * D:3 * D] = (dca_in * cav).astype(BF)
        dp_ref[:, 3 * D:4 * D] = (du0v * sg).astype(BF)
        dp_ref[:, 4 * D:5 * D] = (du0v * vbv * (sg * (1.0 - sg))).astype(BF)
        dp_ref[:, 5 * D:7 * D] = dg_ref[...]

        @pl.when(last)
        def _():
            gw_ref[...] = jnp.zeros_like(gw_ref)
            for k in range(KA):
                gw_ref[k:k + 1, :] = _colsum(gwa[8 * k:8 * k + 8, :])
            for k in range(KB):
                gw_ref[8 + k:9 + k, :] = _colsum(gwb[8 * k:8 * k + 8, :])

    full = lambda r: pl.BlockSpec((r, D), lambda i: (0, 0))
    tok = pl.BlockSpec((tt, D), lambda i: (i, 0))
    nxt = lambda h: pl.BlockSpec((h, D), lambda i: (jnp.minimum((i + 1) * (tt // h), T // h - 1), 0))
    return _pallas(
        body, name="conv_bwd", grid=(nsteps,),
        in_specs=[tok, nxt(HN), tok, nxt(HB),
                  _seg(tt, 1), _seg(tt, 2), _seg(tt, 3), _seg(tt, 4),
                  tok, pl.BlockSpec((tt, 2 * D), lambda i: (i, 0)), full(8), full(32)],
        out_specs=[pl.BlockSpec((tt, 7 * D), lambda i: (i, 0)), full(40)],
        out_shape=[jax.ShapeDtypeStruct((T, 7 * D), BF), jax.ShapeDtypeStruct((40, D), F32)],
        scratch_shapes=[pltpu.VMEM((tt, D), F32), pltpu.VMEM((tt, D), F32),
                        pltpu.VMEM((tt + HN, D), F32), pltpu.VMEM((tt + HB, D), F32),
                        pltpu.VMEM((tt, D), F32), pltpu.VMEM((tt, D), F32),
                        pltpu.VMEM((8 * KA, D), F32), pltpu.VMEM((8 * KB, D), F32),
                        pltpu.VMEM((8, HB + tt, LANES), F32)],
        args=(dconva, dconva, du1, du1, p, p, p, p, dba, dg, wa, wb), push=push)


def _inproj_bwd(dp, win_t, dz1, push, tm=512):
    T, cols = dp.shape

    def body(dp_ref, w_ref, dz1_ref, o_ref):
        o_ref[...] = ALPHA * dz1_ref[...].astype(F32) + jnp.dot(dp_ref[...], w_ref[...], preferred_element_type=F32)

    tok = pl.BlockSpec((tm, D), lambda i: (i, 0))
    return _pallas(
        body, name="inproj_bwd", grid=(T // tm,),
        in_specs=[pl.BlockSpec((tm, cols), lambda i: (i, 0)), _resident((cols, D)), tok],
        out_specs=[tok],
        out_shape=[jax.ShapeDtypeStruct((T, D), F32)],
        args=(dp, win_t, dz1), push=push)


def _adam_math(w, m, v, g):
    nm = ADAM_B1 * m + (1.0 - ADAM_B1) * g
    nv = ADAM_B2 * v + (1.0 - ADAM_B2) * (g * g)
    m_hat = nm / (1.0 - ADAM_B1 ** ADAM_STEP)
    v_hat = nv / (1.0 - ADAM_B2 ** ADAM_STEP)
    return -ADAM_LR * (m_hat / (jnp.sqrt(v_hat) + ADAM_EPS) + ADAM_WD * w), nm, nv


def _adamw(w, m, v, g32, landing, me, name, rb):
    R, C = w.shape
    nl = landing.shape[0]

    def body(me_ref, w_ref, m_ref, v_ref, own_ref, l_ref, g_ref, d_ref, nm_ref, nv_ref):
        g = own_ref[0]
        for k in range(nl):
            g = g + l_ref[k].astype(F32)
        g_ref[...] = g
        d_ref[...], nm_ref[...], nv_ref[...] = _adam_math(w_ref[...], m_ref[...], v_ref[...], g)

    blk = pl.BlockSpec((rb, C), lambda i, me_ref: (i, 0))
    grid_spec = pltpu.PrefetchScalarGridSpec(
        num_scalar_prefetch=1, grid=(R // rb,),
        in_specs=[blk, blk, blk, pl.BlockSpec((1, rb, C), lambda i, me_ref: (me_ref[0], i, 0)),
                  pl.BlockSpec((nl, rb, C), lambda i, me_ref: (0, i, 0))],
        out_specs=[blk] * 4)
    return pl.pallas_call(
        body, name=name, grid_spec=grid_spec, out_shape=[jax.ShapeDtypeStruct((R, C), F32)] * 4,
        compiler_params=_params(1))(me, w, m, v, g32, landing)


def _adamw_small(small_g, vec_w, vec_m, vec_v, conv_w, conv_m, conv_v):
    nv_ = len(vec_w)
    conv_rows = [(8, KA), (16, KB)]

    def body(*refs):
        g_ref = refs[0]
        w_refs, m_refs, v_refs = refs[1:10], refs[10:19], refs[19:28]
        out_refs, gsum = refs[28:64], refs[64]
        acc = g_ref[0]
        for j in range(1, NDEV):
            acc = acc + g_ref[j]
        gsum[...] = acc
        me = 4 * lax.axis_index("x") + 2 * lax.axis_index("y") + lax.axis_index("c")
        cols = pl.ds(pl.multiple_of(me * LANES, LANES), LANES)
        for i in range(nv_ + 2):
            if i < nv_:
                g = gsum[i:i + 1, :]
            else:
                r0, k = conv_rows[i - nv_]
                g = gsum[r0:r0 + k, cols]
            o = out_refs[4 * i:4 * i + 4]
            o[0][...] = g
            o[1][...], o[2][...], o[3][...] = _adam_math(w_refs[i][...], m_refs[i][...], v_refs[i][...], g)

    ws, ms, vs = list(vec_w) + list(conv_w), list(vec_m) + list(conv_m), list(vec_v) + list(conv_v)
    out_shape = [jax.ShapeDtypeStruct(w.shape, F32) for w in ws for _ in range(4)]
    return pl.pallas_call(
        body, name="adamw_small", out_shape=out_shape,
        scratch_shapes=[pltpu.VMEM(small_g.shape[1:], F32)])(small_g, *ws, *ms, *vs)


def _pad_rows(a, rows):
    return jnp.pad(a, ((0, rows - a.shape[0]), (0, 0)))


def _local_step(p, xb, mixed, post, x, target, win_t, wup, wup_t, wdown, woa, wob, wo, wa, wb, vecs):
    yapre, conva, xhatb, rstdb, u3 = mixed
    ya, yb, merged, xhat1, rstd1, x1b = post
    r, h = _mlp_up(x1b, wup)
    dz2b, st2 = _mlp_down_loss(h, wdown, xhat1, target, vecs)

    by_owner = lambda g16: g16.reshape(NDEV, D // NDEV, D)
    dhpre = _mlp_down_bwd(dz2b, wdown, r)
    g_wdown = _tn_matmul_tiles_outer(h, dz2b, NDEV, 512, D, True, "grad_w_down")
    (dz1b, st1), land_wdown = _mlp_up_bwd(dhpre, wup_t, dz2b, xhat1, rstd1, vecs, _push(exch=[g_wdown[1]]))
    g_wup = _tn_matmul_tiles_outer(x1b, dhpre, NDEV, D, 512, False, "grad_w_up")
    (dya, dyb, dg, dba, dconva, du1, stb), land_wup = _merge_bwd(
        dz1b, p, ya, yb, conva, xhatb, rstdb, woa, wob, wo, vecs, _push(exch=[g_wup[1]]))
    g_wo = _tn_matmul(merged, dz1b, 1, D, D, False, False, "grad_w_o")
    g_woa = _tn_matmul(yapre, dya, 1, D, D, False, False, "grad_w_out_a")
    g_wob = _tn_matmul(u3, dyb, 1, D, D, False, False, "grad_w_out_b")
    (dp, gw), land_sq = _conv_bwd(dconva, du1, p, dba, dg, wa, wb,
                                  _push(exch=[by_owner(g_woa[1]), by_owner(g_wob[1]), by_owner(g_wo[1])]))
    g_win, land_win = _grad_w_in_send(xb, dp, dp.shape[1] // NDEV)

    small = jnp.concatenate([stb[2:3], stb[0:2], st1[0:2], st2[0:3], gw], axis=0)
    (grad_x,), land_small = _inproj_bwd(dp, win_t, dz1b, _push(gath=[small]))
    grads = (g_win, g_wup[0], g_wdown[0], g_woa[0], g_wob[0], g_wo[0])
    return grad_x, grads, small, land_wdown + land_wup + land_sq + [land_win] + land_small


def kernel(x, w_in, conv_a_w, w_out_a, conv_b_w, conv_b_bias, ln_b_gamma, ln_b_beta, w_out_b, w_o, ln1_gamma, ln1_beta, w_up, w_down, ln2_gamma, ln2_beta, loss_target, m_w_in, m_conv_a_w, m_w_out_a, m_conv_b_w, m_conv_b_bias, m_ln_b_gamma, m_ln_b_beta, m_w_out_b, m_w_o, m_ln1_gamma, m_ln1_beta, m_w_up, m_w_down, m_ln2_gamma, m_ln2_beta, v_w_in, v_conv_a_w, v_w_out_a, v_conv_b_w, v_conv_b_bias, v_ln_b_gamma, v_ln_b_beta, v_w_out_b, v_w_o, v_ln1_gamma, v_ln1_beta, v_w_up, v_w_down, v_ln2_gamma, v_ln2_beta):
    T = x.shape[1]
    me = 4 * lax.axis_index("x") + 2 * lax.axis_index("y") + lax.axis_index("c")

    conv_shard = jnp.concatenate([_pad_rows(conv_a_w, 8), _pad_rows(conv_b_w, 32)], axis=0)
    p, xb, win_g, conv_g = _inproj_gather(x[0], w_in.astype(BF), conv_shard)
    conv_full = jnp.transpose(conv_g, (1, 0, 2)).reshape(40, D)
    vecs = jnp.stack([conv_b_bias, ln_b_gamma, ln_b_beta, ln1_gamma, ln1_beta, ln2_gamma, ln2_beta,
                      jnp.zeros_like(ln2_beta)])
    whole = lambda g: jnp.transpose(g, (1, 0, 2)).reshape(D, -1)
    whole_t = lambda g: jnp.transpose(g, (0, 2, 1)).reshape(-1, D)
    mixed, (woa_g, wob_g, wo_g, wup_g, wdown_g) = _mixer_fwd(
        p, conv_full[0:8], conv_full[8:40], vecs,
        [w_out_a.astype(BF), w_out_b.astype(BF), w_o.astype(BF), w_up.astype(BF), w_down.astype(BF)])
    woa, wob, wo = woa_g.reshape(D, D), wob_g.reshape(D, D), wo_g.reshape(D, D)
    post, _ = _post_mixer(mixed[0], mixed[4], p, x[0], woa, wob, wo, vecs, [])

    grad_x, grads, small, landing = _local_step(
        p, xb, mixed, post, x[0], loss_target[0], whole_t(win_g), whole(wup_g), whole_t(wup_g),
        wdown_g.reshape(NDEV * 512, D),
        woa, wob, wo, conv_full[0:8], conv_full[8:40], vecs)
    g_win, g_wup, g_wdown, g_woa, g_wob, g_wo = grads
    l_wdown, l_wup, l_woa, l_wob, l_wo, l_win, small_g = landing

    loss = lax.psum(0.5 / D * jnp.sum(small[7]), ("x", "y", "c"))

    me1 = me.astype(jnp.int32).reshape(1)
    by_owner = lambda g32: g32.reshape(NDEV, D // NDEV, D)
    r_win = _adamw(w_in, m_w_in, v_w_in, g_win, l_win, me1, "adamw_w_in", 256)
    r_wup = _adamw(w_up, m_w_up, v_w_up, g_wup, l_wup, me1, "adamw_w_up", 256)
    r_wdown = _adamw(w_down, m_w_down, v_w_down, g_wdown, l_wdown, me1, "adamw_w_down", 256)
    r_woa = _adamw(w_out_a, m_w_out_a, v_w_out_a, by_owner(g_woa), l_woa, me1, "adamw_w_out_a", 128)
    r_wob = _adamw(w_out_b, m_w_out_b, v_w_out_b, by_owner(g_wob), l_wob, me1, "adamw_w_out_b", 128)
    r_wo = _adamw(w_o, m_w_o, v_w_o, by_owner(g_wo), l_wo, me1, "adamw_w_o", 128)

    row = lambda vec: vec.reshape(1, D)
    small_out = _adamw_small(
        small_g,
        [row(a) for a in (conv_b_bias, ln_b_gamma, ln_b_beta, ln1_gamma, ln1_beta, ln2_gamma, ln2_beta)],
        [row(a) for a in (m_conv_b_bias, m_ln_b_gamma, m_ln_b_beta, m_ln1_gamma, m_ln1_beta, m_ln2_gamma, m_ln2_beta)],
        [row(a) for a in (v_conv_b_bias, v_ln_b_gamma, v_ln_b_beta, v_ln1_gamma, v_ln1_beta, v_ln2_gamma, v_ln2_beta)],
        [conv_a_w, conv_b_w], [m_conv_a_w, m_conv_b_w], [v_conv_a_w, v_conv_b_w])
    r_vec = [[small_out[4 * i + q].reshape(D) for q in range(4)] for i in range(7)]
    r_conva, r_convb = small_out[28:32], small_out[32:36]

    per_weight = []
    for q in range(4):
        per_weight.append([
            r_win[q], r_conva[q], r_woa[q], r_convb[q],
            r_vec[0][q], r_vec[1][q], r_vec[2][q], r_wob[q], r_wo[q], r_vec[3][q], r_vec[4][q],
            r_wup[q], r_wdown[q], r_vec[5][q], r_vec[6][q]])
    return (loss, grad_x[None], *per_weight[0], *per_weight[1], *per_weight[2], *per_weight[3])
```

```python
import functools

import jax
import jax.numpy as jnp
from jax import lax
from jax.experimental import pallas as pl
from jax.experimental.pallas import tpu as pltpu

F32 = jnp.float32
BF = jnp.bfloat16
D = 1024
NDEV = 8
ALPHA = 2.0 ** 0.25
LN_EPS = 1e-5
KA, KB = 3, 31
HA, HB = 16, 32
HN = 8
RC = 64
W_IN_EARLY = 4
LANES = 128
VMEM_LIMIT = 56 * 1024 * 1024
MESH = pl.DeviceIdType.MESH
ADAM_LR, ADAM_B1, ADAM_B2, ADAM_EPS, ADAM_WD, ADAM_STEP = 0.001, 0.9, 0.999, 1e-08, 0.01, 10

ANY_SPEC = pl.BlockSpec(memory_space=pl.ANY)
NT_DIMS = (((1,), (1,)), ((), ()))
TN_DIMS = (((0,), (0,)), ((), ()))


def _params(n_axes):
    return pltpu.CompilerParams(dimension_semantics=("arbitrary",) * n_axes, vmem_limit_bytes=VMEM_LIMIT)


def _sigmoid(v):
    return 0.5 * jnp.tanh(0.5 * v) + 0.5


def _ln_fwd(z):
    mu = jnp.mean(z, axis=-1, keepdims=True)
    zc = z - mu
    var = jnp.mean(zc * zc, axis=-1, keepdims=True)
    rstd = lax.rsqrt(var + LN_EPS)
    return zc * rstd, rstd


def _ln_bwd(dy, xhat, rstd, gamma):
    dxhat = dy * gamma
    m1 = jnp.mean(dxhat, axis=-1, keepdims=True)
    m2 = jnp.mean(dxhat * xhat, axis=-1, keepdims=True)
    return rstd * (dxhat - m1 - xhat * m2)


def _colsum(v):
    return jnp.sum(v, axis=0, keepdims=True)


class _TwoLevelGather:
    def __init__(self, ins, outs, send_sems, recv_sems, local_sems):
        self.ins, self.outs = ins, outs
        self.send_sems, self.recv_sems, self.local_sems = send_sems, recv_sems, local_sems
        x, y, c = lax.axis_index("x"), lax.axis_index("y"), lax.axis_index("c")
        self.me, self.sibling, self.c = (x, y, c), (x, y, 1 - c), c
        self.chips = [(1 - x, y), (x, 1 - y), (1 - x, 1 - y)]
        self.n = len(ins)

    @staticmethod
    def out_shape(arrs):
        return [jax.ShapeDtypeStruct((NDEV,) + a.shape, a.dtype) for a in arrs]

    @staticmethod
    def scratch(n):
        return [pltpu.SemaphoreType.DMA((n, 7)), pltpu.SemaphoreType.DMA((n, 7)), pltpu.SemaphoreType.DMA((n,))]

    def _copy(self, a, k, block, to, src=None):
        px, py, pc = block
        rows = self.outs[a].at[4 * px + 2 * py + pc]
        return pltpu.make_async_remote_copy(
            src_ref=rows if src is None else src, dst_ref=rows,
            send_sem=self.send_sems.at[a, k], recv_sem=self.recv_sems.at[a, k],
            device_id=to, device_id_type=MESH)

    def _mine(self, a):
        x, y, c = self.me
        return pltpu.make_async_copy(self.ins[a], self.outs[a].at[4 * x + 2 * y + c], self.local_sems.at[a])

    def _first(self, a):
        cps = [self._copy(a, 0, self.me, self.sibling, src=self.ins[a])]
        return cps + [self._copy(a, 1 + j, self.me, (*chip, self.c), src=self.ins[a]) for j, chip in enumerate(self.chips)]

    def _passed(self, a, j):
        return self._copy(a, 4 + j, (*self.chips[j], self.c), self.sibling)

    def start(self, diagonal=True):
        for a in range(self.n):
            self._mine(a).start()
        for a in range(self.n):
            for cp in self._first(a)[:4 if diagonal else 3]:
                cp.start()

    def start_diagonal(self):
        for a in range(self.n):
            self._first(a)[3].start()

    def wait_ici(self, j):
        for a in range(self.n):
            self._copy(a, 1 + j, (*self.chips[j], self.c), self.me).wait_recv()

    def pass_on(self, j):
        for a in range(self.n):
            self._passed(a, j).start()

    def wait_sibling(self):
        for a in range(self.n):
            self._copy(a, 0, self.sibling, self.me).wait_recv()

    def wait_passed(self, j):
        for a in range(self.n):
            self._copy(a, 4 + j, (*self.chips[j], 1 - self.c), self.me).wait_recv()

    def drain(self):
        for a in range(self.n):
            for cp in self._first(a) + [self._passed(a, j) for j in range(3)]:
                cp.wait_send()
            self._mine(a).wait()

    def forward(self):
        for j in range(3):
            self.wait_ici(j)
            self.pass_on(j)

    def finish(self):
        self.wait_sibling()
        for j in range(3):
            self.wait_passed(j)
        self.drain()


class _Push:
    def __init__(self, exch=(), gath=(), ks=tuple(range(1, NDEV))):
        self.exch, self.gath, self.ks = list(exch), list(gath), tuple(ks)
        self.n = len(self.exch) + len(self.gath)

    def operands(self):
        return self.exch + self.gath

    def out_shape(self):
        return ([jax.ShapeDtypeStruct((len(self.ks),) + a.shape[1:], a.dtype) for a in self.exch]
                + [jax.ShapeDtypeStruct((NDEV,) + a.shape, a.dtype) for a in self.gath])

    def scratch(self):
        return [pltpu.SemaphoreType.DMA((self.n, 7)), pltpu.SemaphoreType.DMA((self.n, 7)),
                pltpu.SemaphoreType.DMA((max(len(self.gath), 1),))]

    def copies(self, ins, outs, send_sems, recv_sems, local_sems):
        x, y, c = lax.axis_index("x"), lax.axis_index("y"), lax.axis_index("c")
        me = 4 * x + 2 * y + c
        ne = len(self.exch)
        remote = []
        for k in range(1, NDEV):
            px = 1 - x if k & 4 else x
            py = 1 - y if k & 2 else y
            pc = 1 - c if k & 1 else c
            for a in range(self.n):
                if a < ne and k not in self.ks:
                    continue
                src = ins[a].at[4 * px + 2 * py + pc] if a < ne else ins[a]
                dst = outs[a].at[self.ks.index(k)] if a < ne else outs[a].at[me]
                remote.append(pltpu.make_async_remote_copy(
                    src_ref=src, dst_ref=dst, send_sem=send_sems.at[a, k - 1], recv_sem=recv_sems.at[a, k - 1],
                    device_id=(px, py, pc), device_id_type=MESH))
        local = [pltpu.make_async_copy(ins[a], outs[a].at[me], local_sems.at[a - ne]) for a in range(ne, self.n)]
        return remote, local


def _push(exch=(), gath=(), ks=tuple(range(1, NDEV))):
    return _Push(exch, gath, ks)


def _pallas(body, *, name, grid, in_specs, out_specs, out_shape, args, scratch_shapes=(), push=None):
    ni, no, ns = len(in_specs), len(out_specs), len(scratch_shapes)
    if push is None:
        outs = pl.pallas_call(
            body, name=name, grid=grid, in_specs=in_specs, out_specs=out_specs, out_shape=out_shape,
            scratch_shapes=list(scratch_shapes), compiler_params=_params(len(grid)))(*args)
        return list(outs), []
    npush = push.n

    def wrapped(*refs):
        ins, pins = refs[:ni], refs[ni:ni + npush]
        outs, pouts = refs[ni + npush:ni + npush + no], refs[ni + npush + no:ni + 2 * npush + no]
        scr, sems = refs[ni + 2 * npush + no:ni + 2 * npush + no + ns], refs[ni + 2 * npush + no + ns:]
        first = functools.reduce(jnp.logical_and, [pl.program_id(d) == 0 for d in range(len(grid))])
        last = functools.reduce(jnp.logical_and, [pl.program_id(d) == grid[d] - 1 for d in range(len(grid))])
        remote, local = push.copies(pins, pouts, *sems)

        @pl.when(first)
        def _():
            for cp in local + remote:
                cp.start()

        body(*ins, *outs, *scr)

        @pl.when(last)
        def _():
            for cp in remote + local:
                cp.wait()

    outs = pl.pallas_call(
        wrapped, name=name, grid=grid,
        in_specs=list(in_specs) + [ANY_SPEC] * npush, out_specs=list(out_specs) + [ANY_SPEC] * npush,
        out_shape=list(out_shape) + push.out_shape(), scratch_shapes=list(scratch_shapes) + push.scratch(),
        compiler_params=_params(len(grid)))(*args, *push.operands())
    return list(outs[:no]), list(outs[no:])


def _with_gather(body, late, *, name, nsteps, in_specs, out_specs, out_shape, args, scratch_shapes=()):
    ni, no, ns, n = len(in_specs), len(out_specs), len(scratch_shapes), len(late)
    pass_step = (7 * nsteps) // 8
    if not late:
        outs = pl.pallas_call(
            body, name=name, grid=(nsteps,), in_specs=in_specs, out_specs=out_specs, out_shape=out_shape,
            scratch_shapes=list(scratch_shapes), compiler_params=_params(1))(*args)
        return list(outs), []

    def wrapped(*refs):
        ins, outs = refs[:ni], refs[ni + n:ni + n + no]
        scr = refs[ni + 2 * n + no:ni + 2 * n + no + ns]
        gather = _TwoLevelGather(refs[ni:ni + n], refs[ni + n + no:ni + 2 * n + no], *refs[ni + 2 * n + no + ns:])
        step = pl.program_id(0)
        pl.when(step == 0)(gather.start)
        pl.when(step == pass_step)(gather.forward)
        body(*ins, *outs, *scr)
        pl.when(step == nsteps - 1)(gather.finish)

    outs = pl.pallas_call(
        wrapped, name=name, grid=(nsteps,),
        in_specs=list(in_specs) + [ANY_SPEC] * n, out_specs=list(out_specs) + [ANY_SPEC] * n,
        out_shape=list(out_shape) + _TwoLevelGather.out_shape(late),
        scratch_shapes=list(scratch_shapes) + _TwoLevelGather.scratch(n),
        compiler_params=_params(1))(*args, *late)
    return list(outs[:no]), list(outs[no:])


def _inproj_gather(x, w_shard, conv_shard, tm=1024):
    T = x.shape[0]
    ni = T // tm
    bw = w_shard.shape[1]
    cx, cy, cc = lax.axis_index("x"), lax.axis_index("y"), lax.axis_index("c")
    blk = lambda px, py, pc: 4 * px + 2 * py + pc
    order = [blk(cx, cy, cc), blk(cx, cy, 1 - cc)]
    for chip in [(1 - cx, cy), (cx, 1 - cy), (1 - cx, 1 - cy)]:
        order += [blk(*chip, cc), blk(*chip, 1 - cc)]
    order = jnp.stack(order).astype(jnp.int32)

    def body(order_ref, x_ref, w_ref, conv_ref, p_ref, xb_ref, wing_ref, convg_ref, xbs, wbuf, wsem, *sems):
        gather = _TwoLevelGather([w_ref, conv_ref], [wing_ref, convg_ref], *sems)
        j, i = pl.program_id(0), pl.program_id(1)

        def load(src):
            cp = pltpu.make_async_copy(src, wbuf, wsem)
            cp.start()
            cp.wait()

        def arrival(jj):
            if jj == 0:
                gather.start(diagonal=False)
                load(w_ref)
                return
            if jj == 1:
                gather.wait_sibling()
            elif jj % 2 == 0:
                if jj == 2:
                    gather.start_diagonal()
                gather.wait_ici(jj // 2 - 1)
                gather.pass_on(jj // 2 - 1)
            else:
                gather.wait_passed(jj // 2 - 1)
            load(wing_ref.at[order_ref[jj]])

        for jj in range(NDEV):
            pl.when(jnp.logical_and(j == jj, i == 0))(functools.partial(arrival, jj))

        @pl.when(j == 0)
        def _():
            xb = x_ref[...].astype(BF)
            xbs[i] = xb
            xb_ref[...] = xb

        p_ref[...] = jnp.dot(xbs[i], wbuf[...], preferred_element_type=F32).astype(BF)

        @pl.when(jnp.logical_and(j == NDEV - 1, i == ni - 1))
        def _():
            gather.drain()

    rows_once = lambda j, i, o: (jnp.where(j == 0, i, ni - 1), 0)
    grid_spec = pltpu.PrefetchScalarGridSpec(
        num_scalar_prefetch=1, grid=(NDEV, ni),
        in_specs=[pl.BlockSpec((tm, D), rows_once), ANY_SPEC, ANY_SPEC],
        out_specs=[pl.BlockSpec((tm, bw), lambda j, i, o: (i, o[j])), pl.BlockSpec((tm, D), rows_once),
                   ANY_SPEC, ANY_SPEC],
        scratch_shapes=[pltpu.VMEM((ni, tm, D), BF), pltpu.VMEM((D, bw), BF), pltpu.SemaphoreType.DMA(())]
        + _TwoLevelGather.scratch(2))
    p, xb, win_g, conv_g = pl.pallas_call(
        body, name="inproj_gather", grid_spec=grid_spec,
        out_shape=[jax.ShapeDtypeStruct((T, NDEV * bw), BF), jax.ShapeDtypeStruct((T, D), BF)]
        + _TwoLevelGather.out_shape([w_shard, conv_shard]),
        compiler_params=_params(2))(order, x, w_shard, conv_shard)
    return p, xb, win_g, conv_g


def _mixer_fwd(p, wa, wb, vecs, late, tt=256):
    T = p.shape[0]
    nt = T // tt

    def body(ba, ca, va, vb, gb, ca_p, va_p, vb_p, gb_p, wa_ref, wb_ref, vec_ref,
             yapre_ref, conva_ref, xhat_ref, rstd_ref, u3_ref, cabuf, u0buf, u1buf, shu):
        first = pl.program_id(0) == 0
        f = lambda ref: ref[...].astype(F32)
        cabuf[0:HA, :] = jnp.where(first, 0.0, f(ca_p) * f(va_p))
        cabuf[HA:HA + tt, :] = f(ca) * f(va)
        u0buf[0:HB, :] = jnp.where(first, 0.0, f(vb_p) * _sigmoid(f(gb_p)))
        u0buf[HB:HB + tt, :] = f(vb) * _sigmoid(f(gb))

        def lane_body(cidx, carry):
            ls = pl.ds(pl.multiple_of(cidx * LANES, LANES), LANES)
            _shifted_copies(shu, u0buf, ls, tt + HB - 8)
            for r in range(tt // RC):
                acc = jnp.zeros((RC, LANES), F32)
                for k in range(KA):
                    acc = acc + wa_ref[k:k + 1, ls] * cabuf[pl.ds(HA - (KA - 1) + k + r * RC, RC), ls]
                conva_ref[pl.ds(r * RC, RC), ls] = acc
                acc = jnp.zeros((RC, LANES), F32)
                for k in range(KB):
                    acc = acc + wb_ref[k:k + 1, ls] * _tap(shu, u0buf, ls, HB - (KB - 1) + k + r * RC, RC)
                u1buf[pl.ds(r * RC, RC), ls] = acc
            return carry

        lax.fori_loop(0, D // LANES, lane_body, 0)
        yapre_ref[...] = (f(ba) * conva_ref[...]).astype(BF)
        xhat, rstd = _ln_fwd(u1buf[...] + vec_ref[0:1, :])
        xhat_ref[...] = xhat
        rstd_ref[...] = rstd
        u2 = xhat * vec_ref[1:2, :] + vec_ref[2:3, :]
        u3_ref[...] = (u2 * _sigmoid(u2)).astype(BF)

    full = lambda r: pl.BlockSpec((r, D), lambda i: (0, 0))
    tok = pl.BlockSpec((tt, D), lambda i: (i, 0))
    return _with_gather(
        body, late, name="mixer_fwd", nsteps=nt,
        in_specs=[_seg(tt, 0), _seg(tt, 1), _seg(tt, 2), _seg(tt, 3), _seg(tt, 4),
                  _prev(tt, HA, 1), _prev(tt, HA, 2), _prev(tt, HB, 3), _prev(tt, HB, 4),
                  full(8), full(32), full(8)],
        out_specs=[tok, tok, tok, pl.BlockSpec((tt, 1), lambda i: (i, 0)), tok],
        out_shape=[jax.ShapeDtypeStruct((T, D), BF), jax.ShapeDtypeStruct((T, D), F32),
                   jax.ShapeDtypeStruct((T, D), F32), jax.ShapeDtypeStruct((T, 1), F32),
                   jax.ShapeDtypeStruct((T, D), BF)],
        scratch_shapes=[pltpu.VMEM((HA + tt, D), F32), pltpu.VMEM((HB + tt, D), F32), pltpu.VMEM((tt, D), F32),
                        pltpu.VMEM((8, HB + tt, LANES), F32)],
        args=(p, p, p, p, p, p, p, p, p, wa, wb, vecs))


def _seg(tt, s):
    return pl.BlockSpec((tt, D), lambda i: (i, s))


def _prev(tt, h, s):
    return pl.BlockSpec((h, D), lambda i: (jnp.maximum(i * (tt // h) - 1, 0), s))


def _shifted_copies(shbuf, src, ls, n):
    for s in range(1, 8):
        shbuf[s, 0:n, :] = src[pl.ds(s, n), ls]


def _tap(shbuf, src, ls, off, rows):
    s, q = off % 8, off // 8
    if s == 0:
        return src[pl.ds(off, rows), ls]
    return shbuf[s, pl.ds(8 * q, rows), :]


def _post_mixer(yapre, u3, p, x, woa, wob, wo, vecs, late, tm=512):
    T = x.shape[0]

    def body(yapre_ref, u3_ref, ga_ref, gb_ref, x_ref, woa_ref, wob_ref, wo_ref, vec_ref,
             ya_ref, yb_ref, merged_ref, xhat_ref, rstd_ref, x1b_ref):
        ya = jnp.dot(yapre_ref[...], woa_ref[...], preferred_element_type=F32)
        yb = jnp.dot(u3_ref[...], wob_ref[...], preferred_element_type=F32)
        ya_ref[...] = ya.astype(BF)
        yb_ref[...] = yb.astype(BF)
        merged = (_sigmoid(ga_ref[...].astype(F32)) * ya + _sigmoid(gb_ref[...].astype(F32)) * yb).astype(BF)
        merged_ref[...] = merged
        mix = jnp.dot(merged, wo_ref[...], preferred_element_type=F32)
        xhat, rstd = _ln_fwd(ALPHA * x_ref[...] + mix)
        xhat_ref[...] = xhat
        rstd_ref[...] = rstd
        x1b_ref[...] = (xhat * vec_ref[3:4, :] + vec_ref[4:5, :]).astype(BF)

    tok = pl.BlockSpec((tm, D), lambda i: (i, 0))
    wfull = _resident((D, D))
    one = pl.BlockSpec((tm, 1), lambda i: (i, 0))
    return _with_gather(
        body, late, name="post_mixer", nsteps=T // tm,
        in_specs=[tok, tok, _seg(tm, 5), _seg(tm, 6), tok, wfull, wfull, wfull, pl.BlockSpec((8, D), lambda i: (0, 0))],
        out_specs=[tok, tok, tok, tok, one, tok],
        out_shape=[jax.ShapeDtypeStruct((T, D), BF), jax.ShapeDtypeStruct((T, D), BF),
                   jax.ShapeDtypeStruct((T, D), BF), jax.ShapeDtypeStruct((T, D), F32),
                   jax.ShapeDtypeStruct((T, 1), F32), jax.ShapeDtypeStruct((T, D), BF)],
        args=(yapre, u3, p, p, x, woa, wob, wo, vecs))


def _mlp_up(x1b, wup, tm=512, tn=2048):
    T = x1b.shape[0]
    dff = wup.shape[1]

    def body(x_ref, w_ref, r_ref, h_ref):
        r = jnp.maximum(jnp.dot(x_ref[...], w_ref[...], preferred_element_type=F32), 0.0)
        r_ref[...] = r.astype(BF)
        h_ref[...] = (r * r).astype(BF)

    out = pl.BlockSpec((tm, tn), lambda j, i: (i, j))
    return pl.pallas_call(
        body, name="mlp_up", grid=(dff // tn, T // tm),
        in_specs=[pl.BlockSpec((tm, D), lambda j, i: (i, 0)), pl.BlockSpec((D, tn), lambda j, i: (0, j))],
        out_specs=[out, out],
        out_shape=[jax.ShapeDtypeStruct((T, dff), BF), jax.ShapeDtypeStruct((T, dff), BF)],
        compiler_params=_params(2))(x1b, wup)


def _resident(shape):
    return pl.BlockSpec(shape, lambda *_: (0,) * len(shape), pipeline_mode=pl.Buffered(1))


def _mlp_down_loss(h, wdown, xhat1, target, vecs, tm=512):
    T, dff = h.shape

    def body(h_ref, w_ref, xhat1_ref, tgt_ref, vec_ref, dz2b_ref, st_ref):
        @pl.when(pl.program_id(0) == 0)
        def _():
            st_ref[...] = jnp.zeros_like(st_ref)

        ff = jnp.dot(h_ref[...], w_ref[...], preferred_element_type=F32)
        x1 = xhat1_ref[...] * vec_ref[3:4, :] + vec_ref[4:5, :]
        xhat2, rstd2 = _ln_fwd(ALPHA * x1 + ff)
        g2 = vec_ref[5:6, :]
        diff = xhat2 * g2 + vec_ref[6:7, :] - tgt_ref[...]
        dx2 = diff * (1.0 / D)
        st_ref[0:1, :] += _colsum(dx2 * xhat2)
        st_ref[1:2, :] += _colsum(dx2)
        st_ref[2:3, :] += _colsum(diff * diff)
        dz2b_ref[...] = _ln_bwd(dx2, xhat2, rstd2, g2).astype(BF)

    tok = pl.BlockSpec((tm, D), lambda i: (i, 0))
    vec = pl.BlockSpec((8, D), lambda i: (0, 0))
    return pl.pallas_call(
        body, name="mlp_down_loss", grid=(T // tm,),
        in_specs=[pl.BlockSpec((tm, dff), lambda i: (i, 0)), _resident((dff, D)), tok, tok, vec],
        out_specs=[tok, vec],
        out_shape=[jax.ShapeDtypeStruct((T, D), BF), jax.ShapeDtypeStruct((8, D), F32)],
        compiler_params=_params(1))(h, wdown, xhat1, target, vecs)


def _mlp_down_bwd(dz2b, wdown, r, tm=512, tk=2048):
    T, dff = r.shape

    def body(dz_ref, w_ref, r_ref, o_ref):
        dh = lax.dot_general(dz_ref[...], w_ref[...], NT_DIMS, preferred_element_type=F32)
        o_ref[...] = (dh * (2.0 * r_ref[...].astype(F32))).astype(BF)

    blk = pl.BlockSpec((tm, tk), lambda j, i: (i, j))
    return pl.pallas_call(
        body, name="mlp_down_bwd", grid=(dff // tk, T // tm),
        in_specs=[pl.BlockSpec((tm, D), lambda j, i: (i, 0)), pl.BlockSpec((tk, D), lambda j, i: (j, 0)), blk],
        out_specs=blk,
        out_shape=jax.ShapeDtypeStruct((T, dff), BF),
        compiler_params=_params(2))(dz2b, wdown, r)


def _tn_matmul(a, b, nblk, a_bw, b_bw, a_blocked, b_blocked, name, tt=2048):
    T = a.shape[0]
    nt = T // tt

    def body(a_ref, b_ref, o32_ref, o16_ref):
        t = pl.program_id(1)

        @pl.when(t == 0)
        def _():
            o32_ref[...] = jnp.zeros_like(o32_ref)

        o32_ref[0] += lax.dot_general(a_ref[...], b_ref[...], TN_DIMS, preferred_element_type=F32)

        @pl.when(t == nt - 1)
        def _():
            o16_ref[...] = o32_ref[...].astype(BF)

    a_spec = pl.BlockSpec((tt, a_bw), (lambda j, t: (t, j)) if a_blocked else (lambda j, t: (t, 0)))
    b_spec = pl.BlockSpec((tt, b_bw), (lambda j, t: (t, j)) if b_blocked else (lambda j, t: (t, 0)))
    out = pl.BlockSpec((1, a_bw, b_bw), lambda j, t: (j, 0, 0))
    return pl.pallas_call(
        body, name=name, grid=(nblk, nt),
        in_specs=[a_spec, b_spec], out_specs=[out, out],
        out_shape=[jax.ShapeDtypeStruct((nblk, a_bw, b_bw), F32), jax.ShapeDtypeStruct((nblk, a_bw, b_bw), BF)],
        compiler_params=_params(2))(a, b)


def _grad_w_in_send(xb, dp, bw, tt=2048):
    T = xb.shape[0]
    nt = T // tt
    flip = lambda v, bit: 1 - v if bit else v

    def peer(k):
        return (flip(lax.axis_index("x"), k & 4), flip(lax.axis_index("y"), k & 2), flip(lax.axis_index("c"), k & 1))

    block_of = lambda dev: 4 * dev[0] + 2 * dev[1] + dev[2]
    order = jnp.stack([block_of(peer(NDEV - 1 - q)) for q in range(NDEV)]).astype(jnp.int32)

    def body(order_ref, a_ref, b_ref, o32_ref, o16_ref, land_ref, stage, send_sems, recv_sems):
        q, t = pl.program_id(0), pl.program_id(1)

        def copy(qq):
            return pltpu.make_async_remote_copy(
                src_ref=stage.at[qq], dst_ref=land_ref.at[qq], send_sem=send_sems.at[qq], recv_sem=recv_sems.at[qq],
                device_id=peer(NDEV - 1 - qq), device_id_type=MESH)

        @pl.when(t == 0)
        def _():
            o32_ref[...] = jnp.zeros_like(o32_ref)

        o32_ref[0] += lax.dot_general(a_ref[...], b_ref[...], TN_DIMS, preferred_element_type=F32)

        @pl.when(t == nt - 1)
        def _():
            o16_ref[...] = o32_ref[...].astype(BF)

        def send(qq):
            stage[qq] = o32_ref[0].astype(BF)
            copy(qq).start()

        for qq in range(W_IN_EARLY):
            pl.when(jnp.logical_and(q == qq, t == nt - 1))(functools.partial(send, qq))

        @pl.when(jnp.logical_and(q == NDEV - 1, t == nt - 1))
        def _():
            for qq in range(W_IN_EARLY):
                copy(qq).wait()

    blk = pl.BlockSpec((1, D, bw), lambda q, t, o: (o[q], 0, 0))
    grid_spec = pltpu.PrefetchScalarGridSpec(
        num_scalar_prefetch=1, grid=(NDEV, nt),
        in_specs=[pl.BlockSpec((tt, D), lambda q, t, o: (t, 0)), pl.BlockSpec((tt, bw), lambda q, t, o: (t, o[q]))],
        out_specs=[blk, blk, ANY_SPEC],
        scratch_shapes=[pltpu.VMEM((W_IN_EARLY, D, bw), BF), pltpu.SemaphoreType.DMA((W_IN_EARLY,)),
                        pltpu.SemaphoreType.DMA((W_IN_EARLY,))])
    return pl.pallas_call(
        body, name="grad_w_in", grid_spec=grid_spec,
        out_shape=[jax.ShapeDtypeStruct((NDEV, D, bw), F32), jax.ShapeDtypeStruct((NDEV, D, bw), BF),
                   jax.ShapeDtypeStruct((W_IN_EARLY, D, bw), BF)],
        compiler_params=_params(2))(order, xb, dp)


def _tn_matmul_tiles_outer(a, b, nblk, a_bw, b_bw, a_blocked, name, tt=2048):
    T = a.shape[0]
    nt = T // tt

    def body(a_ref, b_ref, o32_ref, o16_ref, acc):
        t, j = pl.program_id(0), pl.program_id(1)
        prod = lax.dot_general(a_ref[...], b_ref[...], TN_DIMS, preferred_element_type=F32)

        @pl.when(t == 0)
        def _():
            acc[j] = prod

        @pl.when(t > 0)
        def _():
            acc[j] += prod

        @pl.when(t == nt - 1)
        def _():
            o32_ref[0] = acc[j]
            o16_ref[0] = acc[j].astype(BF)

    a_spec = pl.BlockSpec((tt, a_bw), (lambda t, j: (t, j)) if a_blocked else (lambda t, j: (t, 0)))
    b_spec = pl.BlockSpec((tt, b_bw), (lambda t, j: (t, 0)) if a_blocked else (lambda t, j: (t, j)))
    out = pl.BlockSpec((1, a_bw, b_bw), lambda t, j: (jnp.where(t == nt - 1, j, 0), 0, 0))
    return pl.pallas_call(
        body, name=name, grid=(nt, nblk),
        in_specs=[a_spec, b_spec], out_specs=[out, out],
        out_shape=[jax.ShapeDtypeStruct((nblk, a_bw, b_bw), F32), jax.ShapeDtypeStruct((nblk, a_bw, b_bw), BF)],
        scratch_shapes=[pltpu.VMEM((nblk, a_bw, b_bw), F32)],
        compiler_params=_params(2))(a, b)


def _mlp_up_bwd(dhpre, wup_t, dz2, xhat1, rstd1, vecs, push, tm=512):
    T, dff = dhpre.shape

    def body(dh_ref, w_ref, dz2_ref, xhat_ref, rstd_ref, vec_ref, dz1b_ref, st_ref):
        @pl.when(pl.program_id(0) == 0)
        def _():
            st_ref[...] = jnp.zeros_like(st_ref)

        dx1 = jnp.dot(dh_ref[...], w_ref[...], preferred_element_type=F32) + ALPHA * dz2_ref[...].astype(F32)
        xhat = xhat_ref[...]
        st_ref[0:1, :] += _colsum(dx1 * xhat)
        st_ref[1:2, :] += _colsum(dx1)
        dz1b_ref[...] = _ln_bwd(dx1, xhat, rstd_ref[...], vec_ref[3:4, :]).astype(BF)

    tok = pl.BlockSpec((tm, D), lambda i: (i, 0))
    vec = pl.BlockSpec((8, D), lambda i: (0, 0))
    return _pallas(
        body, name="mlp_up_bwd", grid=(T // tm,),
        in_specs=[pl.BlockSpec((tm, dff), lambda i: (i, 0)), _resident((dff, D)),
                  tok, tok, pl.BlockSpec((tm, 1), lambda i: (i, 0)), vec],
        out_specs=[tok, vec],
        out_shape=[jax.ShapeDtypeStruct((T, D), BF), jax.ShapeDtypeStruct((8, D), F32)],
        args=(dhpre, wup_t, dz2, xhat1, rstd1, vecs), push=push)


def _merge_bwd(dz1, p, ya, yb, conva, xhatb, rstdb, woa, wob, wo, vecs, push, tm=256):
    T = dz1.shape[0]

    def body(dz1_ref, ga_ref, gb_ref, ba_ref, ya_ref, yb_ref, conva_ref, xhat_ref, rstd_ref,
             woa_ref, wob_ref, wo_ref, vec_ref,
             dya_ref, dyb_ref, dg_ref, dba_ref, dconva_ref, du1_ref, st_ref):
        @pl.when(pl.program_id(0) == 0)
        def _():
            st_ref[...] = jnp.zeros_like(st_ref)

        dmerged = lax.dot_general(dz1_ref[...], wo_ref[...], NT_DIMS, preferred_element_type=F32)
        sa, sb = _sigmoid(ga_ref[...].astype(F32)), _sigmoid(gb_ref[...].astype(F32))
        dya = (dmerged * sa).astype(BF)
        dyb = (dmerged * sb).astype(BF)
        dya_ref[...] = dya
        dyb_ref[...] = dyb
        dg_ref[:, 0:D] = (dmerged * ya_ref[...].astype(F32) * (sa * (1.0 - sa))).astype(BF)
        dg_ref[:, D:2 * D] = (dmerged * yb_ref[...].astype(F32) * (sb * (1.0 - sb))).astype(BF)

        dyapre = lax.dot_general(dya, woa_ref[...], NT_DIMS, preferred_element_type=F32)
        dba_ref[...] = (dyapre * conva_ref[...]).astype(BF)
        dconva_ref[...] = dyapre * ba_ref[...].astype(F32)

        du3 = lax.dot_general(dyb, wob_ref[...], NT_DIMS, preferred_element_type=F32)
        xhat = xhat_ref[...]
        gamma = vec_ref[1:2, :]
        u2 = xhat * gamma + vec_ref[2:3, :]
        s = _sigmoid(u2)
        du2 = du3 * (s * (1.0 + u2 * (1.0 - s)))
        st_ref[0:1, :] += _colsum(du2 * xhat)
        st_ref[1:2, :] += _colsum(du2)
        du1 = _ln_bwd(du2, xhat, rstd_ref[...], gamma)
        st_ref[2:3, :] += _colsum(du1)
        du1_ref[...] = du1

    tok = pl.BlockSpec((tm, D), lambda i: (i, 0))
    wfull = pl.BlockSpec((D, D), lambda i: (0, 0))
    vec = pl.BlockSpec((8, D), lambda i: (0, 0))
    return _pallas(
        body, name="merge_bwd", grid=(T // tm,),
        in_specs=[tok, _seg(tm, 5), _seg(tm, 6), _seg(tm, 0), tok, tok, tok, tok, pl.BlockSpec((tm, 1), lambda i: (i, 0)),
                  wfull, wfull, wfull, vec],
        out_specs=[tok, tok, pl.BlockSpec((tm, 2 * D), lambda i: (i, 0)), tok, tok, tok, vec],
        out_shape=[jax.ShapeDtypeStruct((T, D), BF), jax.ShapeDtypeStruct((T, D), BF),
                   jax.ShapeDtypeStruct((T, 2 * D), BF), jax.ShapeDtypeStruct((T, D), BF),
                   jax.ShapeDtypeStruct((T, D), F32), jax.ShapeDtypeStruct((T, D), F32),
                   jax.ShapeDtypeStruct((8, D), F32)],
        args=(dz1, p, p, p, ya, yb, conva, xhatb, rstdb, woa, wob, wo, vecs), push=push)


def _rows8(v):
    out = v[0:8]
    for q in range(1, RC // 8):
        out = out + v[8 * q:8 * q + 8]
    return out


def _conv_bwd(dconva, du1, p, dba, dg, wa, wb, push, tt=256):
    T = p.shape[0]
    nsteps = T // tt

    def body(dca_ref, dca_n, du1_ref, du1_n, ca, va, vb, gb, dba_ref, dg_ref, wa_ref, wb_ref,
             dp_ref, gw_ref, cabuf, u0buf, dcabuf, du1buf, dcain, du0, gwa, gwb, shd):
        i = pl.program_id(0)
        first, last = i == 0, i == nsteps - 1

        @pl.when(first)
        def _():
            gwa[...] = jnp.zeros_like(gwa)
            gwb[...] = jnp.zeros_like(gwb)

        f = lambda ref: ref[...].astype(F32)
        cav, vav, vbv = f(ca), f(va), f(vb)
        cabuf[...] = cav * vav
        sg = _sigmoid(f(gb))
        u0buf[...] = vbv * sg
        dcabuf[0:tt, :] = dca_ref[...]
        dcabuf[tt:tt + HN, :] = jnp.where(last, 0.0, dca_n[...])
        du1buf[0:tt, :] = du1_ref[...]
        du1buf[tt:tt + HB, :] = jnp.where(last, 0.0, du1_n[...])

        def lane_body(cidx, carry):
            ls = pl.ds(pl.multiple_of(cidx * LANES, LANES), LANES)
            _shifted_copies(shd, du1buf, ls, tt + HB - 8)
            for r in range(tt // RC):
                rows = pl.ds(r * RC, RC)
                cin = cabuf[rows, ls]
                acc = jnp.zeros((RC, LANES), F32)
                for k in range(KA):
                    dout = dcabuf[pl.ds(r * RC + KA - 1 - k, RC), ls]
                    acc = acc + wa_ref[k:k + 1, ls] * dout
                    gwa[8 * k:8 * k + 8, ls] += _rows8(cin * dout)
                dcain[rows, ls] = acc
                uin = u0buf[rows, ls]
                acc = jnp.zeros((RC, LANES), F32)
                for k in range(KB):
                    dout = _tap(shd, du1buf, ls, r * RC + KB - 1 - k, RC)
                    acc = acc + wb_ref[k:k + 1, ls] * dout
                    gwb[8 * k:8 * k + 8, ls] += _rows8(uin * dout)
                du0[rows, ls] = acc
            return carry

        lax.fori_loop(0, D // LANES, lane_body, 0)
        dca_in = dcain[...]
        du0v = du0[...]
        dp_ref[:, 0:D] = dba_ref[...]
        dp_ref[:, D:2 * D] = (dca_in * vav).astype(BF)
        dp_ref[:, 2 * D:3 * D] = (dca_in * cav).astype(BF)
        dp_ref[:, 3 * D:4 * D] = (du0v * sg).astype(BF)
        dp_ref[:, 4 * D:5 * D] = (du0v * vbv * (sg * (1.0 - sg))).astype(BF)
        dp_ref[:, 5 * D:7 * D] = dg_ref[...]

        @pl.when(last)
        def _():
            gw_ref[...] = jnp.zeros_like(gw_ref)
            for k in range(KA):
                gw_ref[k:k + 1, :] = _colsum(gwa[8 * k:8 * k + 8, :])
            for k in range(KB):
                gw_ref[8 + k:9 + k, :] = _colsum(gwb[8 * k:8 * k + 8, :])

    full = lambda r: pl.BlockSpec((r, D), lambda i: (0, 0))
    tok = pl.BlockSpec((tt, D), lambda i: (i, 0))
    nxt = lambda h: pl.BlockSpec((h, D), lambda i: (jnp.minimum((i + 1) * (tt // h), T // h - 1), 0))
    return _pallas(
        body, name="conv_bwd", grid=(nsteps,),
        in_specs=[tok, nxt(HN), tok, nxt(HB),
                  _seg(tt, 1), _seg(tt, 2), _seg(tt, 3), _seg(tt, 4),
                  tok, pl.BlockSpec((tt, 2 * D), lambda i: (i, 0)), full(8), full(32)],
        out_specs=[pl.BlockSpec((tt, 7 * D), lambda i: (i, 0)), full(40)],
        out_shape=[jax.ShapeDtypeStruct((T, 7 * D), BF), jax.ShapeDtypeStruct((40, D), F32)],
        scratch_shapes=[pltpu.VMEM((tt, D), F32), pltpu.VMEM((tt, D), F32),
                        pltpu.VMEM((tt + HN, D), F32), pltpu.VMEM((tt + HB, D), F32),
                        pltpu.VMEM((tt, D), F32), pltpu.VMEM((tt, D), F32),
                        pltpu.VMEM((8 * KA, D), F32), pltpu.VMEM((8 * KB, D), F32),
                        pltpu.VMEM((8, HB + tt, LANES), F32)],
        args=(dconva, dconva, du1, du1, p, p, p, p, dba, dg, wa, wb), push=push)


def _inproj_bwd(dp, win_t, dz1, push, tm=512):
    T, cols = dp.shape

    def body(dp_ref, w_ref, dz1_ref, o_ref):
        o_ref[...] = ALPHA * dz1_ref[...].astype(F32) + jnp.dot(dp_ref[...], w_ref[...], preferred_element_type=F32)

    tok = pl.BlockSpec((tm, D), lambda i: (i, 0))
    return _pallas(
        body, name="inproj_bwd", grid=(T // tm,),
        in_specs=[pl.BlockSpec((tm, cols), lambda i: (i, 0)), _resident((cols, D)), tok],
        out_specs=[tok],
        out_shape=[jax.ShapeDtypeStruct((T, D), F32)],
        args=(dp, win_t, dz1), push=push)


def _adam_math(w, m, v, g):
    nm = ADAM_B1 * m + (1.0 - ADAM_B1) * g
    nv = ADAM_B2 * v + (1.0 - ADAM_B2) * (g * g)
    m_hat = nm / (1.0 - ADAM_B1 ** ADAM_STEP)
    v_hat = nv / (1.0 - ADAM_B2 ** ADAM_STEP)
    return -ADAM_LR * (m_hat / (jnp.sqrt(v_hat) + ADAM_EPS) + ADAM_WD * w), nm, nv


def _adamw(w, m, v, g32, landings, me, name, rb):
    R, C = w.shape
    nland = len(landings)

    def body(me_ref, w_ref, m_ref, v_ref, own_ref, *refs):
        g_ref, d_ref, nm_ref, nv_ref = refs[nland:]
        g = own_ref[0]
        for l_ref in refs[:nland]:
            for k in range(l_ref.shape[0]):
                g = g + l_ref[k].astype(F32)
        g_ref[...] = g
        d_ref[...], nm_ref[...], nv_ref[...] = _adam_math(w_ref[...], m_ref[...], v_ref[...], g)

    blk = pl.BlockSpec((rb, C), lambda i, me_ref: (i, 0))
    grid_spec = pltpu.PrefetchScalarGridSpec(
        num_scalar_prefetch=1, grid=(R // rb,),
        in_specs=[blk, blk, blk, pl.BlockSpec((1, rb, C), lambda i, me_ref: (me_ref[0], i, 0))]
        + [pl.BlockSpec((l.shape[0], rb, C), lambda i, me_ref: (0, i, 0)) for l in landings],
        out_specs=[blk] * 4)
    return pl.pallas_call(
        body, name=name, grid_spec=grid_spec, out_shape=[jax.ShapeDtypeStruct((R, C), F32)] * 4,
        compiler_params=_params(1))(me, w, m, v, g32, *landings)


def _adamw_small(small_g, vec_w, vec_m, vec_v, conv_w, conv_m, conv_v):
    nv_ = len(vec_w)
    conv_rows = [(8, KA), (16, KB)]

    def body(*refs):
        g_ref = refs[0]
        w_refs, m_refs, v_refs = refs[1:10], refs[10:19], refs[19:28]
        out_refs, gsum = refs[28:64], refs[64]
        acc = g_ref[0]
        for j in range(1, NDEV):
            acc = acc + g_ref[j]
        gsum[...] = acc
        me = 4 * lax.axis_index("x") + 2 * lax.axis_index("y") + lax.axis_index("c")
        cols = pl.ds(pl.multiple_of(me * LANES, LANES), LANES)
        for i in range(nv_ + 2):
            if i < nv_:
                g = gsum[i:i + 1, :]
            else:
                r0, k = conv_rows[i - nv_]
                g = gsum[r0:r0 + k, cols]
            o = out_refs[4 * i:4 * i + 4]
            o[0][...] = g
            o[1][...], o[2][...], o[3][...] = _adam_math(w_refs[i][...], m_refs[i][...], v_refs[i][...], g)

    ws, ms, vs = list(vec_w) + list(conv_w), list(vec_m) + list(conv_m), list(vec_v) + list(conv_v)
    out_shape = [jax.ShapeDtypeStruct(w.shape, F32) for w in ws for _ in range(4)]
    return pl.pallas_call(
        body, name="adamw_small", out_shape=out_shape,
        scratch_shapes=[pltpu.VMEM(small_g.shape[1:], F32)])(small_g, *ws, *ms, *vs)


def _pad_rows(a, rows):
    return jnp.pad(a, ((0, rows - a.shape[0]), (0, 0)))


def _local_step(p, xb, mixed, post, x, target, win_t, wup, wup_t, wdown, woa, wob, wo, wa, wb, vecs):
    yapre, conva, xhatb, rstdb, u3 = mixed
    ya, yb, merged, xhat1, rstd1, x1b = post
    r, h = _mlp_up(x1b, wup)
    dz2b, st2 = _mlp_down_loss(h, wdown, xhat1, target, vecs)

    by_owner = lambda g16: g16.reshape(NDEV, D // NDEV, D)
    dhpre = _mlp_down_bwd(dz2b, wdown, r)
    g_wdown = _tn_matmul_tiles_outer(h, dz2b, NDEV, 512, D, True, "grad_w_down")
    (dz1b, st1), land_wdown = _mlp_up_bwd(dhpre, wup_t, dz2b, xhat1, rstd1, vecs, _push(exch=[g_wdown[1]]))
    g_wup = _tn_matmul_tiles_outer(x1b, dhpre, NDEV, D, 512, False, "grad_w_up")
    (dya, dyb, dg, dba, dconva, du1, stb), land_wup = _merge_bwd(
        dz1b, p, ya, yb, conva, xhatb, rstdb, woa, wob, wo, vecs, _push(exch=[g_wup[1]]))
    g_wo = _tn_matmul(merged, dz1b, 1, D, D, False, False, "grad_w_o")
    g_woa = _tn_matmul(yapre, dya, 1, D, D, False, False, "grad_w_out_a")
    g_wob = _tn_matmul(u3, dyb, 1, D, D, False, False, "grad_w_out_b")
    (dp, gw), land_sq = _conv_bwd(dconva, du1, p, dba, dg, wa, wb,
                                  _push(exch=[by_owner(g_woa[1]), by_owner(g_wob[1]), by_owner(g_wo[1])]))
    g_win, g_win16, land_win = _grad_w_in_send(xb, dp, dp.shape[1] // NDEV)

    small = jnp.concatenate([stb[2:3], stb[0:2], st1[0:2], st2[0:3], gw], axis=0)
    late_ks = tuple(range(1, NDEV - W_IN_EARLY))
    (grad_x,), land_last = _inproj_bwd(dp, win_t, dz1b, _push(exch=[g_win16], gath=[small], ks=late_ks))
    grads = (g_win, g_wup[0], g_wdown[0], g_woa[0], g_wob[0], g_wo[0])
    return grad_x, grads, small, land_wdown + land_wup + land_sq + [land_win] + land_last


def kernel(x, w_in, conv_a_w, w_out_a, conv_b_w, conv_b_bias, ln_b_gamma, ln_b_beta, w_out_b, w_o, ln1_gamma, ln1_beta, w_up, w_down, ln2_gamma, ln2_beta, loss_target, m_w_in, m_conv_a_w, m_w_out_a, m_conv_b_w, m_conv_b_bias, m_ln_b_gamma, m_ln_b_beta, m_w_out_b, m_w_o, m_ln1_gamma, m_ln1_beta, m_w_up, m_w_down, m_ln2_gamma, m_ln2_beta, v_w_in, v_conv_a_w, v_w_out_a, v_conv_b_w, v_conv_b_bias, v_ln_b_gamma, v_ln_b_beta, v_w_out_b, v_w_o, v_ln1_gamma, v_ln1_beta, v_w_up, v_w_down, v_ln2_gamma, v_ln2_beta):
    T = x.shape[1]
    me = 4 * lax.axis_index("x") + 2 * lax.axis_index("y") + lax.axis_index("c")

    conv_shard = jnp.concatenate([_pad_rows(conv_a_w, 8), _pad_rows(conv_b_w, 32)], axis=0)
    p, xb, win_g, conv_g = _inproj_gather(x[0], w_in.astype(BF), conv_shard)
    conv_full = jnp.transpose(conv_g, (1, 0, 2)).reshape(40, D)
    vecs = jnp.stack([conv_b_bias, ln_b_gamma, ln_b_beta, ln1_gamma, ln1_beta, ln2_gamma, ln2_beta,
                      jnp.zeros_like(ln2_beta)])
    whole = lambda g: jnp.transpose(g, (1, 0, 2)).reshape(D, -1)
    whole_t = lambda g: jnp.transpose(g, (0, 2, 1)).reshape(-1, D)
    mixed, (woa_g, wob_g, wo_g, wup_g, wdown_g) = _mixer_fwd(
        p, conv_full[0:8], conv_full[8:40], vecs,
        [w_out_a.astype(BF), w_out_b.astype(BF), w_o.astype(BF), w_up.astype(BF), w_down.astype(BF)])
    woa, wob, wo = woa_g.reshape(D, D), wob_g.reshape(D, D), wo_g.reshape(D, D)
    post, _ = _post_mixer(mixed[0], mixed[4], p, x[0], woa, wob, wo, vecs, [])

    grad_x, grads, small, landing = _local_step(
        p, xb, mixed, post, x[0], loss_target[0], whole_t(win_g), whole(wup_g), whole_t(wup_g),
        wdown_g.reshape(NDEV * 512, D),
        woa, wob, wo, conv_full[0:8], conv_full[8:40], vecs)
    g_win, g_wup, g_wdown, g_woa, g_wob, g_wo = grads
    l_wdown, l_wup, l_woa, l_wob, l_wo, l_win_early, l_win_late, small_g = landing

    loss = lax.psum(0.5 / D * jnp.sum(small[7]), ("x", "y", "c"))

    me1 = me.astype(jnp.int32).reshape(1)
    by_owner = lambda g32: g32.reshape(NDEV, D // NDEV, D)
    r_win = _adamw(w_in, m_w_in, v_w_in, g_win, [l_win_early, l_win_late], me1, "adamw_w_in", 256)
    r_wup = _adamw(w_up, m_w_up, v_w_up, g_wup, [l_wup], me1, "adamw_w_up", 256)
    r_wdown = _adamw(w_down, m_w_down, v_w_down, g_wdown, [l_wdown], me1, "adamw_w_down", 256)
    r_woa = _adamw(w_out_a, m_w_out_a, v_w_out_a, by_owner(g_woa), [l_woa], me1, "adamw_w_out_a", 128)
    r_wob = _adamw(w_out_b, m_w_out_b, v_w_out_b, by_owner(g_wob), [l_wob], me1, "adamw_w_out_b", 128)
    r_wo = _adamw(w_o, m_w_o, v_w_o, by_owner(g_wo), [l_wo], me1, "adamw_w_o", 128)

    row = lambda vec: vec.reshape(1, D)
    small_out = _adamw_small(
        small_g,
        [row(a) for a in (conv_b_bias, ln_b_gamma, ln_b_beta, ln1_gamma, ln1_beta, ln2_gamma, ln2_beta)],
        [row(a) for a in (m_conv_b_bias, m_ln_b_gamma, m_ln_b_beta, m_ln1_gamma, m_ln1_beta, m_ln2_gamma, m_ln2_beta)],
        [row(a) for a in (v_conv_b_bias, v_ln_b_gamma, v_ln_b_beta, v_ln1_gamma, v_ln1_beta, v_ln2_gamma, v_ln2_beta)],
        [conv_a_w, conv_b_w], [m_conv_a_w, m_conv_b_w], [v_conv_a_w, v_conv_b_w])
    r_vec = [[small_out[4 * i + q].reshape(D) for q in range(4)] for i in range(7)]
    r_conva, r_convb = small_out[28:32], small_out[32:36]

    per_weight = []
    for q in range(4):
        per_weight.append([
            r_win[q], r_conva[q], r_woa[q], r_convb[q],
            r_vec[0][q], r_vec[1][q], r_vec[2][q], r_wob[q], r_wo[q], r_vec[3][q], r_vec[4][q],
            r_wup[q], r_wdown[q], r_vec[5][q], r_vec[6][q]])
    return (loss, grad_x[None], *per_weight[0], *per_weight[1], *per_weight[2], *per_weight[3])
```

```python
import functools

import jax
import jax.numpy as jnp
from jax import lax
from jax.experimental import pallas as pl
from jax.experimental.pallas import tpu as pltpu

F32 = jnp.float32
BF = jnp.bfloat16
D = 1024
NDEV = 8
ALPHA = 2.0 ** 0.25
LN_EPS = 1e-5
KA, KB = 3, 31
HA, HB = 16, 32
HN = 8
RC = 64
W_IN_EARLY = 4
LANES = 128
VMEM_LIMIT = 56 * 1024 * 1024
MESH = pl.DeviceIdType.MESH
ADAM_LR, ADAM_B1, ADAM_B2, ADAM_EPS, ADAM_WD, ADAM_STEP = 0.001, 0.9, 0.999, 1e-08, 0.01, 10

ANY_SPEC = pl.BlockSpec(memory_space=pl.ANY)
NT_DIMS = (((1,), (1,)), ((), ()))
TN_DIMS = (((0,), (0,)), ((), ()))


def _params(n_axes):
    return pltpu.CompilerParams(dimension_semantics=("arbitrary",) * n_axes, vmem_limit_bytes=VMEM_LIMIT)


def _sigmoid(v):
    return 0.5 * jnp.tanh(0.5 * v) + 0.5


def _ln_fwd(z):
    mu = jnp.mean(z, axis=-1, keepdims=True)
    zc = z - mu
    var = jnp.mean(zc * zc, axis=-1, keepdims=True)
    rstd = lax.rsqrt(var + LN_EPS)
    return zc * rstd, rstd


def _ln_bwd(dy, xhat, rstd, gamma):
    dxhat = dy * gamma
    m1 = jnp.mean(dxhat, axis=-1, keepdims=True)
    m2 = jnp.mean(dxhat * xhat, axis=-1, keepdims=True)
    return rstd * (dxhat - m1 - xhat * m2)


def _colsum(v):
    return jnp.sum(v, axis=0, keepdims=True)


class _TwoLevelGather:
    def __init__(self, ins, outs, send_sems, recv_sems, local_sems):
        self.ins, self.outs = ins, outs
        self.send_sems, self.recv_sems, self.local_sems = send_sems, recv_sems, local_sems
        x, y, c = lax.axis_index("x"), lax.axis_index("y"), lax.axis_index("c")
        self.me, self.sibling, self.c = (x, y, c), (x, y, 1 - c), c
        self.chips = [(1 - x, y), (x, 1 - y), (1 - x, 1 - y)]
        self.n = len(ins)

    @staticmethod
    def out_shape(arrs):
        return [jax.ShapeDtypeStruct((NDEV,) + a.shape, a.dtype) for a in arrs]

    @staticmethod
    def scratch(n):
        return [pltpu.SemaphoreType.DMA((n, 7)), pltpu.SemaphoreType.DMA((n, 7)), pltpu.SemaphoreType.DMA((n,))]

    def _copy(self, a, k, block, to, src=None):
        px, py, pc = block
        rows = self.outs[a].at[4 * px + 2 * py + pc]
        return pltpu.make_async_remote_copy(
            src_ref=rows if src is None else src, dst_ref=rows,
            send_sem=self.send_sems.at[a, k], recv_sem=self.recv_sems.at[a, k],
            device_id=to, device_id_type=MESH)

    def _mine(self, a):
        x, y, c = self.me
        return pltpu.make_async_copy(self.ins[a], self.outs[a].at[4 * x + 2 * y + c], self.local_sems.at[a])

    def _first(self, a):
        cps = [self._copy(a, 0, self.me, self.sibling, src=self.ins[a])]
        return cps + [self._copy(a, 1 + j, self.me, (*chip, self.c), src=self.ins[a]) for j, chip in enumerate(self.chips)]

    def _passed(self, a, j):
        return self._copy(a, 4 + j, (*self.chips[j], self.c), self.sibling)

    def start(self, diagonal=True):
        for a in range(self.n):
            self._mine(a).start()
        for a in range(self.n):
            for cp in self._first(a)[:4 if diagonal else 3]:
                cp.start()

    def start_diagonal(self):
        for a in range(self.n):
            self._first(a)[3].start()

    def wait_ici(self, j):
        for a in range(self.n):
            self._copy(a, 1 + j, (*self.chips[j], self.c), self.me).wait_recv()

    def pass_on(self, j):
        for a in range(self.n):
            self._passed(a, j).start()

    def wait_sibling(self):
        for a in range(self.n):
            self._copy(a, 0, self.sibling, self.me).wait_recv()

    def wait_passed(self, j):
        for a in range(self.n):
            self._copy(a, 4 + j, (*self.chips[j], 1 - self.c), self.me).wait_recv()

    def drain(self):
        for a in range(self.n):
            for cp in self._first(a) + [self._passed(a, j) for j in range(3)]:
                cp.wait_send()
            self._mine(a).wait()

    def forward(self):
        for j in range(3):
            self.wait_ici(j)
            self.pass_on(j)

    def finish(self):
        self.wait_sibling()
        for j in range(3):
            self.wait_passed(j)
        self.drain()


class _Push:
    def __init__(self, exch=(), gath=(), ks=tuple(range(1, NDEV))):
        self.exch, self.gath, self.ks = list(exch), list(gath), tuple(ks)
        self.n = len(self.exch) + len(self.gath)

    def operands(self):
        return self.exch + self.gath

    def out_shape(self):
        return ([jax.ShapeDtypeStruct((len(self.ks),) + a.shape[1:], a.dtype) for a in self.exch]
                + [jax.ShapeDtypeStruct((NDEV,) + a.shape, a.dtype) for a in self.gath])

    def scratch(self):
        return [pltpu.SemaphoreType.DMA((self.n, 7)), pltpu.SemaphoreType.DMA((self.n, 7)),
                pltpu.SemaphoreType.DMA((max(len(self.gath), 1),))]

    def copies(self, ins, outs, send_sems, recv_sems, local_sems):
        x, y, c = lax.axis_index("x"), lax.axis_index("y"), lax.axis_index("c")
        me = 4 * x + 2 * y + c
        ne = len(self.exch)
        remote = []
        for k in range(1, NDEV):
            px = 1 - x if k & 4 else x
            py = 1 - y if k & 2 else y
            pc = 1 - c if k & 1 else c
            for a in range(self.n):
                if a < ne and k not in self.ks:
                    continue
                src = ins[a].at[4 * px + 2 * py + pc] if a < ne else ins[a]
                dst = outs[a].at[self.ks.index(k)] if a < ne else outs[a].at[me]
                remote.append(pltpu.make_async_remote_copy(
                    src_ref=src, dst_ref=dst, send_sem=send_sems.at[a, k - 1], recv_sem=recv_sems.at[a, k - 1],
                    device_id=(px, py, pc), device_id_type=MESH))
        local = [pltpu.make_async_copy(ins[a], outs[a].at[me], local_sems.at[a - ne]) for a in range(ne, self.n)]
        return remote, local


def _push(exch=(), gath=(), ks=tuple(range(1, NDEV))):
    return _Push(exch, gath, ks)


def _pallas(body, *, name, grid, in_specs, out_specs, out_shape, args, scratch_shapes=(), push=None):
    ni, no, ns = len(in_specs), len(out_specs), len(scratch_shapes)
    if push is None:
        outs = pl.pallas_call(
            body, name=name, grid=grid, in_specs=in_specs, out_specs=out_specs, out_shape=out_shape,
            scratch_shapes=list(scratch_shapes), compiler_params=_params(len(grid)))(*args)
        return list(outs), []
    npush = push.n

    def wrapped(*refs):
        ins, pins = refs[:ni], refs[ni:ni + npush]
        outs, pouts = refs[ni + npush:ni + npush + no], refs[ni + npush + no:ni + 2 * npush + no]
        scr, sems = refs[ni + 2 * npush + no:ni + 2 * npush + no + ns], refs[ni + 2 * npush + no + ns:]
        first = functools.reduce(jnp.logical_and, [pl.program_id(d) == 0 for d in range(len(grid))])
        last = functools.reduce(jnp.logical_and, [pl.program_id(d) == grid[d] - 1 for d in range(len(grid))])
        remote, local = push.copies(pins, pouts, *sems)

        @pl.when(first)
        def _():
            for cp in local + remote:
                cp.start()

        body(*ins, *outs, *scr)

        @pl.when(last)
        def _():
            for cp in remote + local:
                cp.wait()

    outs = pl.pallas_call(
        wrapped, name=name, grid=grid,
        in_specs=list(in_specs) + [ANY_SPEC] * npush, out_specs=list(out_specs) + [ANY_SPEC] * npush,
        out_shape=list(out_shape) + push.out_shape(), scratch_shapes=list(scratch_shapes) + push.scratch(),
        compiler_params=_params(len(grid)))(*args, *push.operands())
    return list(outs[:no]), list(outs[no:])


def _with_gather(body, late, *, name, nsteps, in_specs, out_specs, out_shape, args, scratch_shapes=()):
    ni, no, ns, n = len(in_specs), len(out_specs), len(scratch_shapes), len(late)
    pass_step = (7 * nsteps) // 8
    if not late:
        outs = pl.pallas_call(
            body, name=name, grid=(nsteps,), in_specs=in_specs, out_specs=out_specs, out_shape=out_shape,
            scratch_shapes=list(scratch_shapes), compiler_params=_params(1))(*args)
        return list(outs), []

    def wrapped(*refs):
        ins, outs = refs[:ni], refs[ni + n:ni + n + no]
        scr = refs[ni + 2 * n + no:ni + 2 * n + no + ns]
        gather = _TwoLevelGather(refs[ni:ni + n], refs[ni + n + no:ni + 2 * n + no], *refs[ni + 2 * n + no + ns:])
        step = pl.program_id(0)
        pl.when(step == 0)(gather.start)
        pl.when(step == pass_step)(gather.forward)
        body(*ins, *outs, *scr)
        pl.when(step == nsteps - 1)(gather.finish)

    outs = pl.pallas_call(
        wrapped, name=name, grid=(nsteps,),
        in_specs=list(in_specs) + [ANY_SPEC] * n, out_specs=list(out_specs) + [ANY_SPEC] * n,
        out_shape=list(out_shape) + _TwoLevelGather.out_shape(late),
        scratch_shapes=list(scratch_shapes) + _TwoLevelGather.scratch(n),
        compiler_params=_params(1))(*args, *late)
    return list(outs[:no]), list(outs[no:])


def _inproj_gather(x, w_shard, conv_shard, tm=1024):
    T = x.shape[0]
    ni = T // tm
    bw = w_shard.shape[1]
    cx, cy, cc = lax.axis_index("x"), lax.axis_index("y"), lax.axis_index("c")
    blk = lambda px, py, pc: 4 * px + 2 * py + pc
    order = [blk(cx, cy, cc), blk(cx, cy, 1 - cc)]
    for chip in [(1 - cx, cy), (cx, 1 - cy), (1 - cx, 1 - cy)]:
        order += [blk(*chip, cc), blk(*chip, 1 - cc)]
    order = jnp.stack(order).astype(jnp.int32)

    def body(order_ref, x_ref, w_ref, conv_ref, p_ref, xb_ref, wing_ref, convg_ref, xbs, wbuf, wsem, *sems):
        gather = _TwoLevelGather([w_ref, conv_ref], [wing_ref, convg_ref], *sems)
        j, i = pl.program_id(0), pl.program_id(1)

        def load(src):
            cp = pltpu.make_async_copy(src, wbuf, wsem)
            cp.start()
            cp.wait()

        def arrival(jj):
            if jj == 0:
                gather.start(diagonal=False)
                load(w_ref)
                return
            if jj == 1:
                gather.wait_sibling()
            elif jj % 2 == 0:
                if jj == 2:
                    gather.start_diagonal()
                gather.wait_ici(jj // 2 - 1)
                gather.pass_on(jj // 2 - 1)
            else:
                gather.wait_passed(jj // 2 - 1)
            load(wing_ref.at[order_ref[jj]])

        for jj in range(NDEV):
            pl.when(jnp.logical_and(j == jj, i == 0))(functools.partial(arrival, jj))

        @pl.when(j == 0)
        def _():
            xb = x_ref[...].astype(BF)
            xbs[i] = xb
            xb_ref[...] = xb

        p_ref[...] = jnp.dot(xbs[i], wbuf[...], preferred_element_type=F32).astype(BF)

        @pl.when(jnp.logical_and(j == NDEV - 1, i == ni - 1))
        def _():
            gather.drain()

    rows_once = lambda j, i, o: (jnp.where(j == 0, i, ni - 1), 0)
    grid_spec = pltpu.PrefetchScalarGridSpec(
        num_scalar_prefetch=1, grid=(NDEV, ni),
        in_specs=[pl.BlockSpec((tm, D), rows_once), ANY_SPEC, ANY_SPEC],
        out_specs=[pl.BlockSpec((tm, bw), lambda j, i, o: (i, o[j])), pl.BlockSpec((tm, D), rows_once),
                   ANY_SPEC, ANY_SPEC],
        scratch_shapes=[pltpu.VMEM((ni, tm, D), BF), pltpu.VMEM((D, bw), BF), pltpu.SemaphoreType.DMA(())]
        + _TwoLevelGather.scratch(2))
    p, xb, win_g, conv_g = pl.pallas_call(
        body, name="inproj_gather", grid_spec=grid_spec,
        out_shape=[jax.ShapeDtypeStruct((T, NDEV * bw), BF), jax.ShapeDtypeStruct((T, D), BF)]
        + _TwoLevelGather.out_shape([w_shard, conv_shard]),
        compiler_params=_params(2))(order, x, w_shard, conv_shard)
    return p, xb, win_g, conv_g


def _mixer_fwd(p, wa, wb, vecs, late, tt=256):
    T = p.shape[0]
    nt = T // tt

    def body(ba, ca, va, vb, gb, ca_p, va_p, vb_p, gb_p, wa_ref, wb_ref, vec_ref,
             yapre_ref, conva_ref, xhat_ref, rstd_ref, u3_ref, cabuf, u0buf, u1buf, shu):
        first = pl.program_id(0) == 0
        f = lambda ref: ref[...].astype(F32)
        cabuf[0:HA, :] = jnp.where(first, 0.0, f(ca_p) * f(va_p))
        cabuf[HA:HA + tt, :] = f(ca) * f(va)
        u0buf[0:HB, :] = jnp.where(first, 0.0, f(vb_p) * _sigmoid(f(gb_p)))
        u0buf[HB:HB + tt, :] = f(vb) * _sigmoid(f(gb))

        def lane_body(cidx, carry):
            ls = pl.ds(pl.multiple_of(cidx * LANES, LANES), LANES)
            _shifted_copies(shu, u0buf, ls, tt + HB - 8)
            for r in range(tt // RC):
                acc = jnp.zeros((RC, LANES), F32)
                for k in range(KA):
                    acc = acc + wa_ref[k:k + 1, ls] * cabuf[pl.ds(HA - (KA - 1) + k + r * RC, RC), ls]
                conva_ref[pl.ds(r * RC, RC), ls] = acc
                acc = jnp.zeros((RC, LANES), F32)
                for k in range(KB):
                    acc = acc + wb_ref[k:k + 1, ls] * _tap(shu, u0buf, ls, HB - (KB - 1) + k + r * RC, RC)
                u1buf[pl.ds(r * RC, RC), ls] = acc
            return carry

        lax.fori_loop(0, D // LANES, lane_body, 0)
        yapre_ref[...] = (f(ba) * conva_ref[...]).astype(BF)
        xhat, rstd = _ln_fwd(u1buf[...] + vec_ref[0:1, :])
        xhat_ref[...] = xhat
        rstd_ref[...] = rstd
        u2 = xhat * vec_ref[1:2, :] + vec_ref[2:3, :]
        u3_ref[...] = (u2 * _sigmoid(u2)).astype(BF)

    full = lambda r: pl.BlockSpec((r, D), lambda i: (0, 0))
    tok = pl.BlockSpec((tt, D), lambda i: (i, 0))
    return _with_gather(
        body, late, name="mixer_fwd", nsteps=nt,
        in_specs=[_seg(tt, 0), _seg(tt, 1), _seg(tt, 2), _seg(tt, 3), _seg(tt, 4),
                  _prev(tt, HA, 1), _prev(tt, HA, 2), _prev(tt, HB, 3), _prev(tt, HB, 4),
                  full(8), full(32), full(8)],
        out_specs=[tok, tok, tok, pl.BlockSpec((tt, 1), lambda i: (i, 0)), tok],
        out_shape=[jax.ShapeDtypeStruct((T, D), BF), jax.ShapeDtypeStruct((T, D), F32),
                   jax.ShapeDtypeStruct((T, D), F32), jax.ShapeDtypeStruct((T, 1), F32),
                   jax.ShapeDtypeStruct((T, D), BF)],
        scratch_shapes=[pltpu.VMEM((HA + tt, D), F32), pltpu.VMEM((HB + tt, D), F32), pltpu.VMEM((tt, D), F32),
                        pltpu.VMEM((8, HB + tt, LANES), F32)],
        args=(p, p, p, p, p, p, p, p, p, wa, wb, vecs))


def _seg(tt, s):
    return pl.BlockSpec((tt, D), lambda i: (i, s))


def _prev(tt, h, s):
    return pl.BlockSpec((h, D), lambda i: (jnp.maximum(i * (tt // h) - 1, 0), s))


def _shifted_copies(shbuf, src, ls, n):
    for s in range(1, 8):
        shbuf[s, 0:n, :] = src[pl.ds(s, n), ls]


def _tap(shbuf, src, ls, off, rows):
    s, q = off % 8, off // 8
    if s == 0:
        return src[pl.ds(off, rows), ls]
    return shbuf[s, pl.ds(8 * q, rows), :]


def _post_mixer(yapre, u3, p, x, woa, wob, wo, vecs, late, tm=512):
    T = x.shape[0]

    def body(yapre_ref, u3_ref, ga_ref, gb_ref, x_ref, woa_ref, wob_ref, wo_ref, vec_ref,
             ya_ref, yb_ref, merged_ref, xhat_ref, rstd_ref, x1b_ref):
        ya = jnp.dot(yapre_ref[...], woa_ref[...], preferred_element_type=F32)
        yb = jnp.dot(u3_ref[...], wob_ref[...], preferred_element_type=F32)
        ya_ref[...] = ya.astype(BF)
        yb_ref[...] = yb.astype(BF)
        merged = (_sigmoid(ga_ref[...].astype(F32)) * ya + _sigmoid(gb_ref[...].astype(F32)) * yb).astype(BF)
        merged_ref[...] = merged
        mix = jnp.dot(merged, wo_ref[...], preferred_element_type=F32)
        xhat, rstd = _ln_fwd(ALPHA * x_ref[...] + mix)
        xhat_ref[...] = xhat
        rstd_ref[...] = rstd
        x1b_ref[...] = (xhat * vec_ref[3:4, :] + vec_ref[4:5, :]).astype(BF)

    tok = pl.BlockSpec((tm, D), lambda i: (i, 0))
    wfull = _resident((D, D))
    one = pl.BlockSpec((tm, 1), lambda i: (i, 0))
    return _with_gather(
        body, late, name="post_mixer", nsteps=T // tm,
        in_specs=[tok, tok, _seg(tm, 5), _seg(tm, 6), tok, wfull, wfull, wfull, pl.BlockSpec((8, D), lambda i: (0, 0))],
        out_specs=[tok, tok, tok, tok, one, tok],
        out_shape=[jax.ShapeDtypeStruct((T, D), BF), jax.ShapeDtypeStruct((T, D), BF),
                   jax.ShapeDtypeStruct((T, D), BF), jax.ShapeDtypeStruct((T, D), F32),
                   jax.ShapeDtypeStruct((T, 1), F32), jax.ShapeDtypeStruct((T, D), BF)],
        args=(yapre, u3, p, p, x, woa, wob, wo, vecs))


def _mlp_up(x1b, wup, tm=512, tn=2048):
    T = x1b.shape[0]
    dff = wup.shape[1]

    def body(x_ref, w_ref, r_ref, h_ref):
        r = jnp.maximum(jnp.dot(x_ref[...], w_ref[...], preferred_element_type=F32), 0.0)
        r_ref[...] = r.astype(BF)
        h_ref[...] = (r * r).astype(BF)

    out = pl.BlockSpec((tm, tn), lambda j, i: (i, j))
    return pl.pallas_call(
        body, name="mlp_up", grid=(dff // tn, T // tm),
        in_specs=[pl.BlockSpec((tm, D), lambda j, i: (i, 0)), pl.BlockSpec((D, tn), lambda j, i: (0, j))],
        out_specs=[out, out],
        out_shape=[jax.ShapeDtypeStruct((T, dff), BF), jax.ShapeDtypeStruct((T, dff), BF)],
        compiler_params=_params(2))(x1b, wup)


def _resident(shape):
    return pl.BlockSpec(shape, lambda *_: (0,) * len(shape), pipeline_mode=pl.Buffered(1))


def _mlp_down_loss(h, wdown, xhat1, target, vecs, tm=512):
    T, dff = h.shape

    def body(h_ref, w_ref, xhat1_ref, tgt_ref, vec_ref, dz2b_ref, st_ref):
        @pl.when(pl.program_id(0) == 0)
        def _():
            st_ref[...] = jnp.zeros_like(st_ref)

        ff = jnp.dot(h_ref[...], w_ref[...], preferred_element_type=F32)
        x1 = xhat1_ref[...] * vec_ref[3:4, :] + vec_ref[4:5, :]
        xhat2, rstd2 = _ln_fwd(ALPHA * x1 + ff)
        g2 = vec_ref[5:6, :]
        diff = xhat2 * g2 + vec_ref[6:7, :] - tgt_ref[...]
        dx2 = diff * (1.0 / D)
        st_ref[0:1, :] += _colsum(dx2 * xhat2)
        st_ref[1:2, :] += _colsum(dx2)
        st_ref[2:3, :] += _colsum(diff * diff)
        dz2b_ref[...] = _ln_bwd(dx2, xhat2, rstd2, g2).astype(BF)

    tok = pl.BlockSpec((tm, D), lambda i: (i, 0))
    vec = pl.BlockSpec((8, D), lambda i: (0, 0))
    return pl.pallas_call(
        body, name="mlp_down_loss", grid=(T // tm,),
        in_specs=[pl.BlockSpec((tm, dff), lambda i: (i, 0)), _resident((dff, D)), tok, tok, vec],
        out_specs=[tok, vec],
        out_shape=[jax.ShapeDtypeStruct((T, D), BF), jax.ShapeDtypeStruct((8, D), F32)],
        compiler_params=_params(1))(h, wdown, xhat1, target, vecs)


def _mlp_down_bwd(dz2b, wdown, r, tm=512, tk=2048):
    T, dff = r.shape

    def body(dz_ref, w_ref, r_ref, o_ref):
        dh = lax.dot_general(dz_ref[...], w_ref[...], NT_DIMS, preferred_element_type=F32)
        o_ref[...] = (dh * (2.0 * r_ref[...].astype(F32))).astype(BF)

    blk = pl.BlockSpec((tm, tk), lambda j, i: (i, j))
    return pl.pallas_call(
        body, name="mlp_down_bwd", grid=(dff // tk, T // tm),
        in_specs=[pl.BlockSpec((tm, D), lambda j, i: (i, 0)), pl.BlockSpec((tk, D), lambda j, i: (j, 0)), blk],
        out_specs=blk,
        out_shape=jax.ShapeDtypeStruct((T, dff), BF),
        compiler_params=_params(2))(dz2b, wdown, r)


def _tn_matmul(a, b, nblk, a_bw, b_bw, a_blocked, b_blocked, name, tt=2048):
    T = a.shape[0]
    nt = T // tt

    def body(a_ref, b_ref, o32_ref, o16_ref):
        t = pl.program_id(1)

        @pl.when(t == 0)
        def _():
            o32_ref[...] = jnp.zeros_like(o32_ref)

        o32_ref[0] += lax.dot_general(a_ref[...], b_ref[...], TN_DIMS, preferred_element_type=F32)

        @pl.when(t == nt - 1)
        def _():
            o16_ref[...] = o32_ref[...].astype(BF)

    a_spec = pl.BlockSpec((tt, a_bw), (lambda j, t: (t, j)) if a_blocked else (lambda j, t: (t, 0)))
    b_spec = pl.BlockSpec((tt, b_bw), (lambda j, t: (t, j)) if b_blocked else (lambda j, t: (t, 0)))
    out = pl.BlockSpec((1, a_bw, b_bw), lambda j, t: (j, 0, 0))
    return pl.pallas_call(
        body, name=name, grid=(nblk, nt),
        in_specs=[a_spec, b_spec], out_specs=[out, out],
        out_shape=[jax.ShapeDtypeStruct((nblk, a_bw, b_bw), F32), jax.ShapeDtypeStruct((nblk, a_bw, b_bw), BF)],
        compiler_params=_params(2))(a, b)


def _grad_w_in_send(xb, dp, bw, tt=2048):
    T = xb.shape[0]
    nt = T // tt
    flip = lambda v, bit: 1 - v if bit else v

    def peer(k):
        return (flip(lax.axis_index("x"), k & 4), flip(lax.axis_index("y"), k & 2), flip(lax.axis_index("c"), k & 1))

    block_of = lambda dev: 4 * dev[0] + 2 * dev[1] + dev[2]
    order = jnp.stack([block_of(peer(NDEV - 1 - q)) for q in range(NDEV)]).astype(jnp.int32)

    def body(order_ref, a_ref, b_ref, o32_ref, o16_ref, land_ref, stage, send_sems, recv_sems):
        q, t = pl.program_id(0), pl.program_id(1)

        def copy(qq):
            return pltpu.make_async_remote_copy(
                src_ref=stage.at[qq], dst_ref=land_ref.at[qq], send_sem=send_sems.at[qq], recv_sem=recv_sems.at[qq],
                device_id=peer(NDEV - 1 - qq), device_id_type=MESH)

        @pl.when(t == 0)
        def _():
            o32_ref[...] = jnp.zeros_like(o32_ref)

        o32_ref[0] += lax.dot_general(a_ref[...], b_ref[...], TN_DIMS, preferred_element_type=F32)

        @pl.when(t == nt - 1)
        def _():
            o16_ref[...] = o32_ref[...].astype(BF)

        def send(qq):
            stage[qq] = o32_ref[0].astype(BF)
            copy(qq).start()

        for qq in range(W_IN_EARLY):
            pl.when(jnp.logical_and(q == qq, t == nt - 1))(functools.partial(send, qq))

        @pl.when(jnp.logical_and(q == NDEV - 1, t == nt - 1))
        def _():
            for qq in range(W_IN_EARLY):
                copy(qq).wait()

    blk = pl.BlockSpec((1, D, bw), lambda q, t, o: (o[q], 0, 0))
    grid_spec = pltpu.PrefetchScalarGridSpec(
        num_scalar_prefetch=1, grid=(NDEV, nt),
        in_specs=[pl.BlockSpec((tt, D), lambda q, t, o: (t, 0)), pl.BlockSpec((tt, bw), lambda q, t, o: (t, o[q]))],
        out_specs=[blk, blk, ANY_SPEC],
        scratch_shapes=[pltpu.VMEM((W_IN_EARLY, D, bw), BF), pltpu.SemaphoreType.DMA((W_IN_EARLY,)),
                        pltpu.SemaphoreType.DMA((W_IN_EARLY,))])
    return pl.pallas_call(
        body, name="grad_w_in", grid_spec=grid_spec,
        out_shape=[jax.ShapeDtypeStruct((NDEV, D, bw), F32), jax.ShapeDtypeStruct((NDEV, D, bw), BF),
                   jax.ShapeDtypeStruct((W_IN_EARLY, D, bw), BF)],
        compiler_params=_params(2))(order, xb, dp)


def _tn_matmul_tiles_outer(a, b, nblk, a_bw, b_bw, a_blocked, name, tt=2048):
    T = a.shape[0]
    nt = T // tt

    def body(a_ref, b_ref, o32_ref, o16_ref, acc):
        t, j = pl.program_id(0), pl.program_id(1)
        prod = lax.dot_general(a_ref[...], b_ref[...], TN_DIMS, preferred_element_type=F32)

        @pl.when(t == 0)
        def _():
            acc[j] = prod

        @pl.when(t > 0)
        def _():
            acc[j] += prod

        @pl.when(t == nt - 1)
        def _():
            o32_ref[0] = acc[j]
            o16_ref[0] = acc[j].astype(BF)

    a_spec = pl.BlockSpec((tt, a_bw), (lambda t, j: (t, j)) if a_blocked else (lambda t, j: (t, 0)))
    b_spec = pl.BlockSpec((tt, b_bw), (lambda t, j: (t, 0)) if a_blocked else (lambda t, j: (t, j)))
    out = pl.BlockSpec((1, a_bw, b_bw), lambda t, j: (jnp.where(t == nt - 1, j, 0), 0, 0))
    return pl.pallas_call(
        body, name=name, grid=(nt, nblk),
        in_specs=[a_spec, b_spec], out_specs=[out, out],
        out_shape=[jax.ShapeDtypeStruct((nblk, a_bw, b_bw), F32), jax.ShapeDtypeStruct((nblk, a_bw, b_bw), BF)],
        scratch_shapes=[pltpu.VMEM((nblk, a_bw, b_bw), F32)],
        compiler_params=_params(2))(a, b)


def _mlp_up_bwd(dhpre, wup_t, dz2, xhat1, rstd1, vecs, push, tm=512):
    T, dff = dhpre.shape

    def body(dh_ref, w_ref, dz2_ref, xhat_ref, rstd_ref, vec_ref, dz1b_ref, st_ref):
        @pl.when(pl.program_id(0) == 0)
        def _():
            st_ref[...] = jnp.zeros_like(st_ref)

        dx1 = jnp.dot(dh_ref[...], w_ref[...], preferred_element_type=F32) + ALPHA * dz2_ref[...].astype(F32)
        xhat = xhat_ref[...]
        st_ref[0:1, :] += _colsum(dx1 * xhat)
        st_ref[1:2, :] += _colsum(dx1)
        dz1b_ref[...] = _ln_bwd(dx1, xhat, rstd_ref[...], vec_ref[3:4, :]).astype(BF)

    tok = pl.BlockSpec((tm, D), lambda i: (i, 0))
    vec = pl.BlockSpec((8, D), lambda i: (0, 0))
    return _pallas(
        body, name="mlp_up_bwd", grid=(T // tm,),
        in_specs=[pl.BlockSpec((tm, dff), lambda i: (i, 0)), _resident((dff, D)),
                  tok, tok, pl.BlockSpec((tm, 1), lambda i: (i, 0)), vec],
        out_specs=[tok, vec],
        out_shape=[jax.ShapeDtypeStruct((T, D), BF), jax.ShapeDtypeStruct((8, D), F32)],
        args=(dhpre, wup_t, dz2, xhat1, rstd1, vecs), push=push)


def _merge_bwd(dz1, p, ya, yb, conva, xhatb, rstdb, woa, wob, wo, vecs, push, tm=256):
    T = dz1.shape[0]

    def body(dz1_ref, ga_ref, gb_ref, ba_ref, ya_ref, yb_ref, conva_ref, xhat_ref, rstd_ref,
             woa_ref, wob_ref, wo_ref, vec_ref,
             dya_ref, dyb_ref, dg_ref, dba_ref, dconva_ref, du1_ref, st_ref):
        @pl.when(pl.program_id(0) == 0)
        def _():
            st_ref[...] = jnp.zeros_like(st_ref)

        dmerged = lax.dot_general(dz1_ref[...], wo_ref[...], NT_DIMS, preferred_element_type=F32)
        sa, sb = _sigmoid(ga_ref[...].astype(F32)), _sigmoid(gb_ref[...].astype(F32))
        dya = (dmerged * sa).astype(BF)
        dyb = (dmerged * sb).astype(BF)
        dya_ref[...] = dya
        dyb_ref[...] = dyb
        dg_ref[:, 0:D] = (dmerged * ya_ref[...].astype(F32) * (sa * (1.0 - sa))).astype(BF)
        dg_ref[:, D:2 * D] = (dmerged * yb_ref[...].astype(F32) * (sb * (1.0 - sb))).astype(BF)

        dyapre = lax.dot_general(dya, woa_ref[...], NT_DIMS, preferred_element_type=F32)
        dba_ref[...] = (dyapre * conva_ref[...]).astype(BF)
        dconva_ref[...] = dyapre * ba_ref[...].astype(F32)

        du3 = lax.dot_general(dyb, wob_ref[...], NT_DIMS, preferred_element_type=F32)
        xhat = xhat_ref[...]
        gamma = vec_ref[1:2, :]
        u2 = xhat * gamma + vec_ref[2:3, :]
        s = _sigmoid(u2)
        du2 = du3 * (s * (1.0 + u2 * (1.0 - s)))
        st_ref[0:1, :] += _colsum(du2 * xhat)
        st_ref[1:2, :] += _colsum(du2)
        du1 = _ln_bwd(du2, xhat, rstd_ref[...], gamma)
        st_ref[2:3, :] += _colsum(du1)
        du1_ref[...] = du1

    tok = pl.BlockSpec((tm, D), lambda i: (i, 0))
    wfull = pl.BlockSpec((D, D), lambda i: (0, 0))
    vec = pl.BlockSpec((8, D), lambda i: (0, 0))
    return _pallas(
        body, name="merge_bwd", grid=(T // tm,),
        in_specs=[tok, _seg(tm, 5), _seg(tm, 6), _seg(tm, 0), tok, tok, tok, tok, pl.BlockSpec((tm, 1), lambda i: (i, 0)),
                  wfull, wfull, wfull, vec],
        out_specs=[tok, tok, pl.BlockSpec((tm, 2 * D), lambda i: (i, 0)), tok, tok, tok, vec],
        out_shape=[jax.ShapeDtypeStruct((T, D), BF), jax.ShapeDtypeStruct((T, D), BF),
                   jax.ShapeDtypeStruct((T, 2 * D), BF), jax.ShapeDtypeStruct((T, D), BF),
                   jax.ShapeDtypeStruct((T, D), F32), jax.ShapeDtypeStruct((T, D), F32),
                   jax.ShapeDtypeStruct((8, D), F32)],
        args=(dz1, p, p, p, ya, yb, conva, xhatb, rstdb, woa, wob, wo, vecs), push=push)


def _rows8(v):
    out = v[0:8]
    for q in range(1, RC // 8):
        out = out + v[8 * q:8 * q + 8]
    return out


def _conv_bwd(dconva, du1, p, dba, dg, wa, wb, push, tt=256):
    T = p.shape[0]
    nsteps = T // tt

    def body(dca_ref, dca_n, du1_ref, du1_n, ca, va, vb, gb, dba_ref, dg_ref, wa_ref, wb_ref,
             dp_ref, gw_ref, cabuf, u0buf, dcabuf, du1buf, dcain, du0, gwa, gwb, shd):
        i = pl.program_id(0)
        first, last = i == 0, i == nsteps - 1

        @pl.when(first)
        def _():
            gwa[...] = jnp.zeros_like(gwa)
            gwb[...] = jnp.zeros_like(gwb)

        f = lambda ref: ref[...].astype(F32)
        cav, vav, vbv = f(ca), f(va), f(vb)
        cabuf[...] = cav * vav
        sg = _sigmoid(f(gb))
        u0buf[...] = vbv * sg
        dcabuf[0:tt, :] = dca_ref[...]
        dcabuf[tt:tt + HN, :] = jnp.where(last, 0.0, dca_n[...])
        du1buf[0:tt, :] = du1_ref[...]
        du1buf[tt:tt + HB, :] = jnp.where(last, 0.0, du1_n[...])

        def lane_body(cidx, carry):
            ls = pl.ds(pl.multiple_of(cidx * LANES, LANES), LANES)
            _shifted_copies(shd, du1buf, ls, tt + HB - 8)
            for r in range(tt // RC):
                rows = pl.ds(r * RC, RC)
                cin = cabuf[rows, ls]
                acc = jnp.zeros((RC, LANES), F32)
                for k in range(KA):
                    dout = dcabuf[pl.ds(r * RC + KA - 1 - k, RC), ls]
                    acc = acc + wa_ref[k:k + 1, ls] * dout
                    gwa[8 * k:8 * k + 8, ls] += _rows8(cin * dout)
                dcain[rows, ls] = acc
                uin = u0buf[rows, ls]
                acc = jnp.zeros((RC, LANES), F32)
                for k in range(KB):
                    dout = _tap(shd, du1buf, ls, r * RC + KB - 1 - k, RC)
                    acc = acc + wb_ref[k:k + 1, ls] * dout
                    gwb[8 * k:8 * k + 8, ls] += _rows8(uin * dout)
                du0[rows, ls] = acc
            return carry

        lax.fori_loop(0, D // LANES, lane_body, 0)
        dca_in = dcain[...]
        du0v = du0[...]
        dp_ref[:, 0:D] = dba_ref[...]
        dp_ref[:, D:2 * D] = (dca_in * vav).astype(BF)
        dp_ref[:, 2 * D:3 * D] = (dca_in * cav).astype(BF)
        dp_ref[:, 3 * D:4 * D] = (du0v * sg).astype(BF)
        dp_ref[:, 4 * D:5 * D] = (du0v * vbv * (sg * (1.0 - sg))).astype(BF)
        dp_ref[:, 5 * D:7 * D] = dg_ref[...]

        @pl.when(last)
        def _():
            gw_ref[...] = jnp.zeros_like(gw_ref)
            for k in range(KA):
                gw_ref[k:k + 1, :] = _colsum(gwa[8 * k:8 * k + 8, :])
            for k in range(KB):
                gw_ref[8 + k:9 + k, :] = _colsum(gwb[8 * k:8 * k + 8, :])

    full = lambda r: pl.BlockSpec((r, D), lambda i: (0, 0))
    tok = pl.BlockSpec((tt, D), lambda i: (i, 0))
    nxt = lambda h: pl.BlockSpec((h, D), lambda i: (jnp.minimum((i + 1) * (tt // h), T // h - 1), 0))
    return _pallas(
        body, name="conv_bwd", grid=(nsteps,),
        in_specs=[tok, nxt(HN), tok, nxt(HB),
                  _seg(tt, 1), _seg(tt, 2), _seg(tt, 3), _seg(tt, 4),
                  tok, pl.BlockSpec((tt, 2 * D), lambda i: (i, 0)), full(8), full(32)],
        out_specs=[pl.BlockSpec((tt, 7 * D), lambda i: (i, 0)), full(40)],
        out_shape=[jax.ShapeDtypeStruct((T, 7 * D), BF), jax.ShapeDtypeStruct((40, D), F32)],
        scratch_shapes=[pltpu.VMEM((tt, D), F32), pltpu.VMEM((tt, D), F32),
                        pltpu.VMEM((tt + HN, D), F32), pltpu.VMEM((tt + HB, D), F32),
                        pltpu.VMEM((tt, D), F32), pltpu.VMEM((tt, D), F32),
                        pltpu.VMEM((8 * KA, D), F32), pltpu.VMEM((8 * KB, D), F32),
                        pltpu.VMEM((8, HB + tt, LANES), F32)],
        args=(dconva, dconva, du1, du1, p, p, p, p, dba, dg, wa, wb), push=push)


def _inproj_bwd(dp, win_t, dz1, push, tm=512):
    T, cols = dp.shape

    def body(dp_ref, w_ref, dz1_ref, o_ref):
        o_ref[...] = ALPHA * dz1_ref[...].astype(F32) + jnp.dot(dp_ref[...], w_ref[...], preferred_element_type=F32)

    tok = pl.BlockSpec((tm, D), lambda i: (i, 0))
    return _pallas(
        body, name="inproj_bwd", grid=(T // tm,),
        in_specs=[pl.BlockSpec((tm, cols), lambda i: (i, 0)), _resident((cols, D)), tok],
        out_specs=[tok],
        out_shape=[jax.ShapeDtypeStruct((T, D), F32)],
        args=(dp, win_t, dz1), push=push)


def _adam_math(w, m, v, g):
    nm = ADAM_B1 * m + (1.0 - ADAM_B1) * g
    nv = ADAM_B2 * v + (1.0 - ADAM_B2) * (g * g)
    m_hat = nm / (1.0 - ADAM_B1 ** ADAM_STEP)
    v_hat = nv / (1.0 - ADAM_B2 ** ADAM_STEP)
    return -ADAM_LR * (m_hat / (jnp.sqrt(v_hat) + ADAM_EPS) + ADAM_WD * w), nm, nv


def _adamw(w, m, v, g32, landings, me, name, rb):
    R, C = w.shape
    nland = len(landings)

    def body(me_ref, w_ref, m_ref, v_ref, own_ref, *refs):
        g_ref, d_ref, nm_ref, nv_ref = refs[nland:]
        g = own_ref[0]
        for l_ref in refs[:nland]:
            for k in range(l_ref.shape[0]):
                g = g + l_ref[k].astype(F32)
        g_ref[...] = g
        d_ref[...], nm_ref[...], nv_ref[...] = _adam_math(w_ref[...], m_ref[...], v_ref[...], g)

    blk = pl.BlockSpec((rb, C), lambda i, me_ref: (i, 0))
    grid_spec = pltpu.PrefetchScalarGridSpec(
        num_scalar_prefetch=1, grid=(R // rb,),
        in_specs=[blk, blk, blk, pl.BlockSpec((1, rb, C), lambda i, me_ref: (me_ref[0], i, 0))]
        + [pl.BlockSpec((l.shape[0], rb, C), lambda i, me_ref: (0, i, 0)) for l in landings],
        out_specs=[blk] * 4)
    return pl.pallas_call(
        body, name=name, grid_spec=grid_spec, out_shape=[jax.ShapeDtypeStruct((R, C), F32)] * 4,
        compiler_params=_params(1))(me, w, m, v, g32, *landings)


def _adamw_small(small_g, vec_w, vec_m, vec_v, conv_w, conv_m, conv_v):
    nv_ = len(vec_w)
    conv_rows = [(8, KA), (16, KB)]

    def body(*refs):
        g_ref = refs[0]
        w_refs, m_refs, v_refs = refs[1:10], refs[10:19], refs[19:28]
        out_refs, gsum = refs[28:64], refs[64]
        acc = g_ref[0]
        for j in range(1, NDEV):
            acc = acc + g_ref[j]
        gsum[...] = acc
        me = 4 * lax.axis_index("x") + 2 * lax.axis_index("y") + lax.axis_index("c")
        cols = pl.ds(pl.multiple_of(me * LANES, LANES), LANES)
        for i in range(nv_ + 2):
            if i < nv_:
                g = gsum[i:i + 1, :]
            else:
                r0, k = conv_rows[i - nv_]
                g = gsum[r0:r0 + k, cols]
            o = out_refs[4 * i:4 * i + 4]
            o[0][...] = g
            o[1][...], o[2][...], o[3][...] = _adam_math(w_refs[i][...], m_refs[i][...], v_refs[i][...], g)

    ws, ms, vs = list(vec_w) + list(conv_w), list(vec_m) + list(conv_m), list(vec_v) + list(conv_v)
    out_shape = [jax.ShapeDtypeStruct(w.shape, F32) for w in ws for _ in range(4)]
    return pl.pallas_call(
        body, name="adamw_small", out_shape=out_shape,
        scratch_shapes=[pltpu.VMEM(small_g.shape[1:], F32)])(small_g, *ws, *ms, *vs)


def _pad_rows(a, rows):
    return jnp.pad(a, ((0, rows - a.shape[0]), (0, 0)))


def _local_step(p, xb, mixed, post, x, target, win_t, wup, wup_t, wdown, woa, wob, wo, wa, wb, vecs):
    yapre, conva, xhatb, rstdb, u3 = mixed
    ya, yb, merged, xhat1, rstd1, x1b = post
    r, h = _mlp_up(x1b, wup)
    dz2b, st2 = _mlp_down_loss(h, wdown, xhat1, target, vecs)

    by_owner = lambda g16: g16.reshape(NDEV, D // NDEV, D)
    dhpre = _mlp_down_bwd(dz2b, wdown, r)
    g_wdown = _tn_matmul_tiles_outer(h, dz2b, NDEV, 512, D, True, "grad_w_down")
    (dz1b, st1), land_wdown = _mlp_up_bwd(dhpre, wup_t, dz2b, xhat1, rstd1, vecs, _push(exch=[g_wdown[1]]))
    g_wup = _tn_matmul_tiles_outer(x1b, dhpre, NDEV, D, 512, False, "grad_w_up")
    (dya, dyb, dg, dba, dconva, du1, stb), _ = _merge_bwd(
        dz1b, p, ya, yb, conva, xhatb, rstdb, woa, wob, wo, vecs, None)
    g_wo = _tn_matmul(merged, dz1b, 1, D, D, False, False, "grad_w_o")
    g_woa = _tn_matmul(yapre, dya, 1, D, D, False, False, "grad_w_out_a")
    g_wob = _tn_matmul(u3, dyb, 1, D, D, False, False, "grad_w_out_b")
    (dp, gw), land_conv = _conv_bwd(
        dconva, du1, p, dba, dg, wa, wb,
        _push(exch=[g_wup[1], by_owner(g_woa[1]), by_owner(g_wob[1]), by_owner(g_wo[1])]))
    g_win, g_win16, land_win = _grad_w_in_send(xb, dp, dp.shape[1] // NDEV)

    small = jnp.concatenate([stb[2:3], stb[0:2], st1[0:2], st2[0:3], gw], axis=0)
    late_ks = tuple(range(1, NDEV - W_IN_EARLY))
    (grad_x,), land_last = _inproj_bwd(dp, win_t, dz1b, _push(exch=[g_win16], gath=[small], ks=late_ks))
    grads = (g_win, g_wup[0], g_wdown[0], g_woa[0], g_wob[0], g_wo[0])
    return grad_x, grads, small, land_wdown + land_conv + [land_win] + land_last


def kernel(x, w_in, conv_a_w, w_out_a, conv_b_w, conv_b_bias, ln_b_gamma, ln_b_beta, w_out_b, w_o, ln1_gamma, ln1_beta, w_up, w_down, ln2_gamma, ln2_beta, loss_target, m_w_in, m_conv_a_w, m_w_out_a, m_conv_b_w, m_conv_b_bias, m_ln_b_gamma, m_ln_b_beta, m_w_out_b, m_w_o, m_ln1_gamma, m_ln1_beta, m_w_up, m_w_down, m_ln2_gamma, m_ln2_beta, v_w_in, v_conv_a_w, v_w_out_a, v_conv_b_w, v_conv_b_bias, v_ln_b_gamma, v_ln_b_beta, v_w_out_b, v_w_o, v_ln1_gamma, v_ln1_beta, v_w_up, v_w_down, v_ln2_gamma, v_ln2_beta):
    T = x.shape[1]
    me = 4 * lax.axis_index("x") + 2 * lax.axis_index("y") + lax.axis_index("c")

    conv_shard = jnp.concatenate([_pad_rows(conv_a_w, 8), _pad_rows(conv_b_w, 32)], axis=0)
    p, xb, win_g, conv_g = _inproj_gather(x[0], w_in.astype(BF), conv_shard)
    conv_full = jnp.transpose(conv_g, (1, 0, 2)).reshape(40, D)
    vecs = jnp.stack([conv_b_bias, ln_b_gamma, ln_b_beta, ln1_gamma, ln1_beta, ln2_gamma, ln2_beta,
                      jnp.zeros_like(ln2_beta)])
    whole = lambda g: jnp.transpose(g, (1, 0, 2)).reshape(D, -1)
    whole_t = lambda g: jnp.transpose(g, (0, 2, 1)).reshape(-1, D)
    mixed, (woa_g, wob_g, wo_g, wup_g, wdown_g) = _mixer_fwd(
        p, conv_full[0:8], conv_full[8:40], vecs,
        [w_out_a.astype(BF), w_out_b.astype(BF), w_o.astype(BF), w_up.astype(BF), w_down.astype(BF)])
    woa, wob, wo = woa_g.reshape(D, D), wob_g.reshape(D, D), wo_g.reshape(D, D)
    post, _ = _post_mixer(mixed[0], mixed[4], p, x[0], woa, wob, wo, vecs, [])

    grad_x, grads, small, landing = _local_step(
        p, xb, mixed, post, x[0], loss_target[0], whole_t(win_g), whole(wup_g), whole_t(wup_g),
        wdown_g.reshape(NDEV * 512, D),
        woa, wob, wo, conv_full[0:8], conv_full[8:40], vecs)
    g_win, g_wup, g_wdown, g_woa, g_wob, g_wo = grads
    l_wdown, l_wup, l_woa, l_wob, l_wo, l_win_early, l_win_late, small_g = landing

    loss = lax.psum(0.5 / D * jnp.sum(small[7]), ("x", "y", "c"))

    me1 = me.astype(jnp.int32).reshape(1)
    by_owner = lambda g32: g32.reshape(NDEV, D // NDEV, D)
    r_win = _adamw(w_in, m_w_in, v_w_in, g_win, [l_win_early, l_win_late], me1, "adamw_w_in", 256)
    r_wup = _adamw(w_up, m_w_up, v_w_up, g_wup, [l_wup], me1, "adamw_w_up", 256)
    r_wdown = _adamw(w_down, m_w_down, v_w_down, g_wdown, [l_wdown], me1, "adamw_w_down", 256)
    r_woa = _adamw(w_out_a, m_w_out_a, v_w_out_a, by_owner(g_woa), [l_woa], me1, "adamw_w_out_a", 128)
    r_wob = _adamw(w_out_b, m_w_out_b, v_w_out_b, by_owner(g_wob), [l_wob], me1, "adamw_w_out_b", 128)
    r_wo = _adamw(w_o, m_w_o, v_w_o, by_owner(g_wo), [l_wo], me1, "adamw_w_o", 128)

    row = lambda vec: vec.reshape(1, D)
    small_out = _adamw_small(
        small_g,
        [row(a) for a in (conv_b_bias, ln_b_gamma, ln_b_beta, ln1_gamma, ln1_beta, ln2_gamma, ln2_beta)],
        [row(a) for a in (m_conv_b_bias, m_ln_b_gamma, m_ln_b_beta, m_ln1_gamma, m_ln1_beta, m_ln2_gamma, m_ln2_beta)],
        [row(a) for a in (v_conv_b_bias, v_ln_b_gamma, v_ln_b_beta, v_ln1_gamma, v_ln1_beta, v_ln2_gamma, v_ln2_beta)],
        [conv_a_w, conv_b_w], [m_conv_a_w, m_conv_b_w], [v_conv_a_w, v_conv_b_w])
    r_vec = [[small_out[4 * i + q].reshape(D) for q in range(4)] for i in range(7)]
    r_conva, r_convb = small_out[28:32], small_out[32:36]

    per_weight = []
    for q in range(4):
        per_weight.append([
            r_win[q], r_conva[q], r_woa[q], r_convb[q],
            r_vec[0][q], r_vec[1][q], r_vec[2][q], r_wob[q], r_wo[q], r_vec[3][q], r_vec[4][q],
            r_wup[q], r_wdown[q], r_vec[5][q], r_vec[6][q]])
    return (loss, grad_x[None], *per_weight[0], *per_weight[1], *per_weight[2], *per_weight[3])
```

```python
import functools

import jax
import jax.numpy as jnp
from jax import lax
from jax.experimental import pallas as pl
from jax.experimental.pallas import tpu as pltpu

F32 = jnp.float32
BF = jnp.bfloat16
D = 1024
NDEV = 8
ALPHA = 2.0 ** 0.25
LN_EPS = 1e-5
KA, KB = 3, 31
HA, HB = 16, 32
HN = 8
RC = 64
W_IN_EARLY = 4
LANES = 128
VMEM_LIMIT = 56 * 1024 * 1024
MESH = pl.DeviceIdType.MESH
ADAM_LR, ADAM_B1, ADAM_B2, ADAM_EPS, ADAM_WD, ADAM_STEP = 0.001, 0.9, 0.999, 1e-08, 0.01, 10

ANY_SPEC = pl.BlockSpec(memory_space=pl.ANY)
NT_DIMS = (((1,), (1,)), ((), ()))
TN_DIMS = (((0,), (0,)), ((), ()))


def _params(n_axes):
    return pltpu.CompilerParams(dimension_semantics=("arbitrary",) * n_axes, vmem_limit_bytes=VMEM_LIMIT)


def _sigmoid(v):
    return 0.5 * jnp.tanh(0.5 * v) + 0.5


def _ln_fwd(z):
    mu = jnp.mean(z, axis=-1, keepdims=True)
    zc = z - mu
    var = jnp.mean(zc * zc, axis=-1, keepdims=True)
    rstd = lax.rsqrt(var + LN_EPS)
    return zc * rstd, rstd


def _ln_bwd(dy, xhat, rstd, gamma):
    dxhat = dy * gamma
    m1 = jnp.mean(dxhat, axis=-1, keepdims=True)
    m2 = jnp.mean(dxhat * xhat, axis=-1, keepdims=True)
    return rstd * (dxhat - m1 - xhat * m2)


def _colsum(v):
    return jnp.sum(v, axis=0, keepdims=True)


class _TwoLevelGather:
    def __init__(self, ins, outs, send_sems, recv_sems, local_sems):
        self.ins, self.outs = ins, outs
        self.send_sems, self.recv_sems, self.local_sems = send_sems, recv_sems, local_sems
        x, y, c = lax.axis_index("x"), lax.axis_index("y"), lax.axis_index("c")
        self.me, self.sibling, self.c = (x, y, c), (x, y, 1 - c), c
        self.chips = [(1 - x, y), (x, 1 - y), (1 - x, 1 - y)]
        self.n = len(ins)

    @staticmethod
    def out_shape(arrs):
        return [jax.ShapeDtypeStruct((NDEV,) + a.shape, a.dtype) for a in arrs]

    @staticmethod
    def scratch(n):
        return [pltpu.SemaphoreType.DMA((n, 7)), pltpu.SemaphoreType.DMA((n, 7)), pltpu.SemaphoreType.DMA((n,))]

    def _copy(self, a, k, block, to, src=None):
        px, py, pc = block
        rows = self.outs[a].at[4 * px + 2 * py + pc]
        return pltpu.make_async_remote_copy(
            src_ref=rows if src is None else src, dst_ref=rows,
            send_sem=self.send_sems.at[a, k], recv_sem=self.recv_sems.at[a, k],
            device_id=to, device_id_type=MESH)

    def _mine(self, a):
        x, y, c = self.me
        return pltpu.make_async_copy(self.ins[a], self.outs[a].at[4 * x + 2 * y + c], self.local_sems.at[a])

    def _first(self, a):
        cps = [self._copy(a, 0, self.me, self.sibling, src=self.ins[a])]
        return cps + [self._copy(a, 1 + j, self.me, (*chip, self.c), src=self.ins[a]) for j, chip in enumerate(self.chips)]

    def _passed(self, a, j):
        return self._copy(a, 4 + j, (*self.chips[j], self.c), self.sibling)

    def start(self, diagonal=True):
        for a in range(self.n):
            self._mine(a).start()
        for a in range(self.n):
            for cp in self._first(a)[:4 if diagonal else 3]:
                cp.start()

    def start_diagonal(self):
        for a in range(self.n):
            self._first(a)[3].start()

    def wait_ici(self, j):
        for a in range(self.n):
            self._copy(a, 1 + j, (*self.chips[j], self.c), self.me).wait_recv()

    def pass_on(self, j):
        for a in range(self.n):
            self._passed(a, j).start()

    def wait_sibling(self):
        for a in range(self.n):
            self._copy(a, 0, self.sibling, self.me).wait_recv()

    def wait_passed(self, j):
        for a in range(self.n):
            self._copy(a, 4 + j, (*self.chips[j], 1 - self.c), self.me).wait_recv()

    def drain(self):
        for a in range(self.n):
            for cp in self._first(a) + [self._passed(a, j) for j in range(3)]:
                cp.wait_send()
            self._mine(a).wait()

    def forward(self):
        for j in range(3):
            self.wait_ici(j)
            self.pass_on(j)

    def finish(self):
        self.wait_sibling()
        for j in range(3):
            self.wait_passed(j)
        self.drain()


class _Push:
    def __init__(self, exch=(), gath=(), ks=tuple(range(1, NDEV))):
        self.exch, self.gath, self.ks = list(exch), list(gath), tuple(ks)
        self.n = len(self.exch) + len(self.gath)

    def operands(self):
        return self.exch + self.gath

    def out_shape(self):
        return ([jax.ShapeDtypeStruct((len(self.ks),) + a.shape[1:], a.dtype) for a in self.exch]
                + [jax.ShapeDtypeStruct((NDEV,) + a.shape, a.dtype) for a in self.gath])

    def scratch(self):
        return [pltpu.SemaphoreType.DMA((self.n, 7)), pltpu.SemaphoreType.DMA((self.n, 7)),
                pltpu.SemaphoreType.DMA((max(len(self.gath), 1),))]

    def copies(self, ins, outs, send_sems, recv_sems, local_sems):
        x, y, c = lax.axis_index("x"), lax.axis_index("y"), lax.axis_index("c")
        me = 4 * x + 2 * y + c
        ne = len(self.exch)
        remote = []
        for k in range(1, NDEV):
            px = 1 - x if k & 4 else x
            py = 1 - y if k & 2 else y
            pc = 1 - c if k & 1 else c
            for a in range(self.n):
                if a < ne and k not in self.ks:
                    continue
                src = ins[a].at[4 * px + 2 * py + pc] if a < ne else ins[a]
                dst = outs[a].at[self.ks.index(k)] if a < ne else outs[a].at[me]
                remote.append(pltpu.make_async_remote_copy(
                    src_ref=src, dst_ref=dst, send_sem=send_sems.at[a, k - 1], recv_sem=recv_sems.at[a, k - 1],
                    device_id=(px, py, pc), device_id_type=MESH))
        local = [pltpu.make_async_copy(ins[a], outs[a].at[me], local_sems.at[a - ne]) for a in range(ne, self.n)]
        return remote, local


def _push(exch=(), gath=(), ks=tuple(range(1, NDEV))):
    return _Push(exch, gath, ks)


def _pallas(body, *, name, grid, in_specs, out_specs, out_shape, args, scratch_shapes=(), push=None):
    ni, no, ns = len(in_specs), len(out_specs), len(scratch_shapes)
    if push is None:
        outs = pl.pallas_call(
            body, name=name, grid=grid, in_specs=in_specs, out_specs=out_specs, out_shape=out_shape,
            scratch_shapes=list(scratch_shapes), compiler_params=_params(len(grid)))(*args)
        return list(outs), []
    npush = push.n

    def wrapped(*refs):
        ins, pins = refs[:ni], refs[ni:ni + npush]
        outs, pouts = refs[ni + npush:ni + npush + no], refs[ni + npush + no:ni + 2 * npush + no]
        scr, sems = refs[ni + 2 * npush + no:ni + 2 * npush + no + ns], refs[ni + 2 * npush + no + ns:]
        first = functools.reduce(jnp.logical_and, [pl.program_id(d) == 0 for d in range(len(grid))])
        last = functools.reduce(jnp.logical_and, [pl.program_id(d) == grid[d] - 1 for d in range(len(grid))])
        remote, local = push.copies(pins, pouts, *sems)

        @pl.when(first)
        def _():
            for cp in local + remote:
                cp.start()

        body(*ins, *outs, *scr)

        @pl.when(last)
        def _():
            for cp in remote + local:
                cp.wait()

    outs = pl.pallas_call(
        wrapped, name=name, grid=grid,
        in_specs=list(in_specs) + [ANY_SPEC] * npush, out_specs=list(out_specs) + [ANY_SPEC] * npush,
        out_shape=list(out_shape) + push.out_shape(), scratch_shapes=list(scratch_shapes) + push.scratch(),
        compiler_params=_params(len(grid)))(*args, *push.operands())
    return list(outs[:no]), list(outs[no:])


def _with_gather(body, late, *, name, nsteps, in_specs, out_specs, out_shape, args, scratch_shapes=()):
    ni, no, ns, n = len(in_specs), len(out_specs), len(scratch_shapes), len(late)
    pass_step = (7 * nsteps) // 8
    if not late:
        outs = pl.pallas_call(
            body, name=name, grid=(nsteps,), in_specs=in_specs, out_specs=out_specs, out_shape=out_shape,
            scratch_shapes=list(scratch_shapes), compiler_params=_params(1))(*args)
        return list(outs), []

    def wrapped(*refs):
        ins, outs = refs[:ni], refs[ni + n:ni + n + no]
        scr = refs[ni + 2 * n + no:ni + 2 * n + no + ns]
        gather = _TwoLevelGather(refs[ni:ni + n], refs[ni + n + no:ni + 2 * n + no], *refs[ni + 2 * n + no + ns:])
        step = pl.program_id(0)
        pl.when(step == 0)(gather.start)
        pl.when(step == pass_step)(gather.forward)
        body(*ins, *outs, *scr)
        pl.when(step == nsteps - 1)(gather.finish)

    outs = pl.pallas_call(
        wrapped, name=name, grid=(nsteps,),
        in_specs=list(in_specs) + [ANY_SPEC] * n, out_specs=list(out_specs) + [ANY_SPEC] * n,
        out_shape=list(out_shape) + _TwoLevelGather.out_shape(late),
        scratch_shapes=list(scratch_shapes) + _TwoLevelGather.scratch(n),
        compiler_params=_params(1))(*args, *late)
    return list(outs[:no]), list(outs[no:])


def _inproj_gather(x, w_shard, conv_shard, tm=2048):
    T = x.shape[0]
    ni = T // tm
    bw = w_shard.shape[1]
    cx, cy, cc = lax.axis_index("x"), lax.axis_index("y"), lax.axis_index("c")
    blk = lambda px, py, pc: 4 * px + 2 * py + pc
    order = [blk(cx, cy, cc), blk(cx, cy, 1 - cc)]
    for chip in [(1 - cx, cy), (cx, 1 - cy), (1 - cx, 1 - cy)]:
        order += [blk(*chip, cc), blk(*chip, 1 - cc)]
    order = jnp.stack(order).astype(jnp.int32)

    def body(order_ref, x_ref, w_ref, conv_ref, p_ref, xb_ref, wing_ref, convg_ref, xbs, wbuf, wsem, *sems):
        gather = _TwoLevelGather([w_ref, conv_ref], [wing_ref, convg_ref], *sems)
        j, i = pl.program_id(0), pl.program_id(1)

        def load(src):
            cp = pltpu.make_async_copy(src, wbuf, wsem)
            cp.start()
            cp.wait()

        def arrival(jj):
            if jj == 0:
                gather.start(diagonal=False)
                load(w_ref)
                return
            if jj == 1:
                gather.wait_sibling()
            elif jj % 2 == 0:
                if jj == 2:
                    gather.start_diagonal()
                gather.wait_ici(jj // 2 - 1)
                gather.pass_on(jj // 2 - 1)
            else:
                gather.wait_passed(jj // 2 - 1)
            load(wing_ref.at[order_ref[jj]])

        for jj in range(NDEV):
            pl.when(jnp.logical_and(j == jj, i == 0))(functools.partial(arrival, jj))

        @pl.when(j == 0)
        def _():
            xb = x_ref[...].astype(BF)
            xbs[i] = xb
            xb_ref[...] = xb

        p_ref[...] = jnp.dot(xbs[i], wbuf[...], preferred_element_type=F32).astype(BF)

        @pl.when(jnp.logical_and(j == NDEV - 1, i == ni - 1))
        def _():
            gather.drain()

    rows_once = lambda j, i, o: (jnp.where(j == 0, i, ni - 1), 0)
    grid_spec = pltpu.PrefetchScalarGridSpec(
        num_scalar_prefetch=1, grid=(NDEV, ni),
        in_specs=[pl.BlockSpec((tm, D), rows_once), ANY_SPEC, ANY_SPEC],
        out_specs=[pl.BlockSpec((tm, bw), lambda j, i, o: (i, o[j])), pl.BlockSpec((tm, D), rows_once),
                   ANY_SPEC, ANY_SPEC],
        scratch_shapes=[pltpu.VMEM((ni, tm, D), BF), pltpu.VMEM((D, bw), BF), pltpu.SemaphoreType.DMA(())]
        + _TwoLevelGather.scratch(2))
    p, xb, win_g, conv_g = pl.pallas_call(
        body, name="inproj_gather", grid_spec=grid_spec,
        out_shape=[jax.ShapeDtypeStruct((T, NDEV * bw), BF), jax.ShapeDtypeStruct((T, D), BF)]
        + _TwoLevelGather.out_shape([w_shard, conv_shard]),
        compiler_params=_params(2))(order, x, w_shard, conv_shard)
    return p, xb, win_g, conv_g


def _mixer_fwd(p, wa, wb, vecs, late, tt=256):
    T = p.shape[0]
    nt = T // tt

    def body(ba, ca, va, vb, gb, ca_p, va_p, vb_p, gb_p, wa_ref, wb_ref, vec_ref,
             yapre_ref, conva_ref, xhat_ref, rstd_ref, u3_ref, cabuf, u0buf, u1buf, shu):
        first = pl.program_id(0) == 0
        f = lambda ref: ref[...].astype(F32)
        cabuf[0:HA, :] = jnp.where(first, 0.0, f(ca_p) * f(va_p))
        cabuf[HA:HA + tt, :] = f(ca) * f(va)
        u0buf[0:HB, :] = jnp.where(first, 0.0, f(vb_p) * _sigmoid(f(gb_p)))
        u0buf[HB:HB + tt, :] = f(vb) * _sigmoid(f(gb))

        def lane_body(cidx, carry):
            ls = pl.ds(pl.multiple_of(cidx * LANES, LANES), LANES)
            _shifted_copies(shu, u0buf, ls, tt + HB - 8)
            for r in range(tt // RC):
                acc = jnp.zeros((RC, LANES), F32)
                for k in range(KA):
                    acc = acc + wa_ref[k:k + 1, ls] * cabuf[pl.ds(HA - (KA - 1) + k + r * RC, RC), ls]
                conva_ref[pl.ds(r * RC, RC), ls] = acc
                acc = jnp.zeros((RC, LANES), F32)
                for k in range(KB):
                    acc = acc + wb_ref[k:k + 1, ls] * _tap(shu, u0buf, ls, HB - (KB - 1) + k + r * RC, RC)
                u1buf[pl.ds(r * RC, RC), ls] = acc
            return carry

        lax.fori_loop(0, D // LANES, lane_body, 0)
        yapre_ref[...] = (f(ba) * conva_ref[...]).astype(BF)
        xhat, rstd = _ln_fwd(u1buf[...] + vec_ref[0:1, :])
        xhat_ref[...] = xhat
        rstd_ref[...] = rstd
        u2 = xhat * vec_ref[1:2, :] + vec_ref[2:3, :]
        u3_ref[...] = (u2 * _sigmoid(u2)).astype(BF)

    full = lambda r: pl.BlockSpec((r, D), lambda i: (0, 0))
    tok = pl.BlockSpec((tt, D), lambda i: (i, 0))
    return _with_gather(
        body, late, name="mixer_fwd", nsteps=nt,
        in_specs=[_seg(tt, 0), _seg(tt, 1), _seg(tt, 2), _seg(tt, 3), _seg(tt, 4),
                  _prev(tt, HA, 1), _prev(tt, HA, 2), _prev(tt, HB, 3), _prev(tt, HB, 4),
                  full(8), full(32), full(8)],
        out_specs=[tok, tok, tok, pl.BlockSpec((tt, 1), lambda i: (i, 0)), tok],
        out_shape=[jax.ShapeDtypeStruct((T, D), BF), jax.ShapeDtypeStruct((T, D), F32),
                   jax.ShapeDtypeStruct((T, D), F32), jax.ShapeDtypeStruct((T, 1), F32),
                   jax.ShapeDtypeStruct((T, D), BF)],
        scratch_shapes=[pltpu.VMEM((HA + tt, D), F32), pltpu.VMEM((HB + tt, D), F32), pltpu.VMEM((tt, D), F32),
                        pltpu.VMEM((8, HB + tt, LANES), F32)],
        args=(p, p, p, p, p, p, p, p, p, wa, wb, vecs))


def _seg(tt, s):
    return pl.BlockSpec((tt, D), lambda i: (i, s))


def _prev(tt, h, s):
    return pl.BlockSpec((h, D), lambda i: (jnp.maximum(i * (tt // h) - 1, 0), s))


def _shifted_copies(shbuf, src, ls, n):
    for s in range(1, 8):
        shbuf[s, 0:n, :] = src[pl.ds(s, n), ls]


def _tap(shbuf, src, ls, off, rows):
    s, q = off % 8, off // 8
    if s == 0:
        return src[pl.ds(off, rows), ls]
    return shbuf[s, pl.ds(8 * q, rows), :]


def _post_mixer(yapre, u3, p, x, woa, wob, wo, vecs, late, tm=512):
    T = x.shape[0]

    def body(yapre_ref, u3_ref, ga_ref, gb_ref, x_ref, woa_ref, wob_ref, wo_ref, vec_ref,
             ya_ref, yb_ref, merged_ref, xhat_ref, rstd_ref, x1b_ref):
        ya = jnp.dot(yapre_ref[...], woa_ref[...], preferred_element_type=F32)
        yb = jnp.dot(u3_ref[...], wob_ref[...], preferred_element_type=F32)
        ya_ref[...] = ya.astype(BF)
        yb_ref[...] = yb.astype(BF)
        merged = (_sigmoid(ga_ref[...].astype(F32)) * ya + _sigmoid(gb_ref[...].astype(F32)) * yb).astype(BF)
        merged_ref[...] = merged
        mix = jnp.dot(merged, wo_ref[...], preferred_element_type=F32)
        xhat, rstd = _ln_fwd(ALPHA * x_ref[...] + mix)
        xhat_ref[...] = xhat
        rstd_ref[...] = rstd
        x1b_ref[...] = (xhat * vec_ref[3:4, :] + vec_ref[4:5, :]).astype(BF)

    tok = pl.BlockSpec((tm, D), lambda i: (i, 0))
    wfull = _resident((D, D))
    one = pl.BlockSpec((tm, 1), lambda i: (i, 0))
    return _with_gather(
        body, late, name="post_mixer", nsteps=T // tm,
        in_specs=[tok, tok, _seg(tm, 5), _seg(tm, 6), tok, wfull, wfull, wfull, pl.BlockSpec((8, D), lambda i: (0, 0))],
        out_specs=[tok, tok, tok, tok, one, tok],
        out_shape=[jax.ShapeDtypeStruct((T, D), BF), jax.ShapeDtypeStruct((T, D), BF),
                   jax.ShapeDtypeStruct((T, D), BF), jax.ShapeDtypeStruct((T, D), F32),
                   jax.ShapeDtypeStruct((T, 1), F32), jax.ShapeDtypeStruct((T, D), BF)],
        args=(yapre, u3, p, p, x, woa, wob, wo, vecs))


def _mlp_up(x1b, wup, tm=512, tn=2048):
    T = x1b.shape[0]
    dff = wup.shape[1]

    def body(x_ref, w_ref, r_ref, h_ref):
        r = jnp.maximum(jnp.dot(x_ref[...], w_ref[...], preferred_element_type=F32), 0.0)
        r_ref[...] = r.astype(BF)
        h_ref[...] = (r * r).astype(BF)

    out = pl.BlockSpec((tm, tn), lambda j, i: (i, j))
    return pl.pallas_call(
        body, name="mlp_up", grid=(dff // tn, T // tm),
        in_specs=[pl.BlockSpec((tm, D), lambda j, i: (i, 0)), pl.BlockSpec((D, tn), lambda j, i: (0, j))],
        out_specs=[out, out],
        out_shape=[jax.ShapeDtypeStruct((T, dff), BF), jax.ShapeDtypeStruct((T, dff), BF)],
        compiler_params=_params(2))(x1b, wup)


def _resident(shape):
    return pl.BlockSpec(shape, lambda *_: (0,) * len(shape), pipeline_mode=pl.Buffered(1))


def _mlp_down_loss(h, wdown, xhat1, target, vecs, tm=512):
    T, dff = h.shape

    def body(h_ref, w_ref, xhat1_ref, tgt_ref, vec_ref, dz2b_ref, st_ref):
        @pl.when(pl.program_id(0) == 0)
        def _():
            st_ref[...] = jnp.zeros_like(st_ref)

        ff = jnp.dot(h_ref[...], w_ref[...], preferred_element_type=F32)
        x1 = xhat1_ref[...] * vec_ref[3:4, :] + vec_ref[4:5, :]
        xhat2, rstd2 = _ln_fwd(ALPHA * x1 + ff)
        g2 = vec_ref[5:6, :]
        diff = xhat2 * g2 + vec_ref[6:7, :] - tgt_ref[...]
        dx2 = diff * (1.0 / D)
        st_ref[0:1, :] += _colsum(dx2 * xhat2)
        st_ref[1:2, :] += _colsum(dx2)
        st_ref[2:3, :] += _colsum(diff * diff)
        dz2b_ref[...] = _ln_bwd(dx2, xhat2, rstd2, g2).astype(BF)

    tok = pl.BlockSpec((tm, D), lambda i: (i, 0))
    vec = pl.BlockSpec((8, D), lambda i: (0, 0))
    return pl.pallas_call(
        body, name="mlp_down_loss", grid=(T // tm,),
        in_specs=[pl.BlockSpec((tm, dff), lambda i: (i, 0)), _resident((dff, D)), tok, tok, vec],
        out_specs=[tok, vec],
        out_shape=[jax.ShapeDtypeStruct((T, D), BF), jax.ShapeDtypeStruct((8, D), F32)],
        compiler_params=_params(1))(h, wdown, xhat1, target, vecs)


def _mlp_down_bwd(dz2b, wdown, r, tm=512, tk=2048):
    T, dff = r.shape

    def body(dz_ref, w_ref, r_ref, o_ref):
        dh = lax.dot_general(dz_ref[...], w_ref[...], NT_DIMS, preferred_element_type=F32)
        o_ref[...] = (dh * (2.0 * r_ref[...].astype(F32))).astype(BF)

    blk = pl.BlockSpec((tm, tk), lambda j, i: (i, j))
    return pl.pallas_call(
        body, name="mlp_down_bwd", grid=(dff // tk, T // tm),
        in_specs=[pl.BlockSpec((tm, D), lambda j, i: (i, 0)), pl.BlockSpec((tk, D), lambda j, i: (j, 0)), blk],
        out_specs=blk,
        out_shape=jax.ShapeDtypeStruct((T, dff), BF),
        compiler_params=_params(2))(dz2b, wdown, r)


def _tn_matmul(a, b, nblk, a_bw, b_bw, a_blocked, b_blocked, name, tt=2048):
    T = a.shape[0]
    nt = T // tt

    def body(a_ref, b_ref, o32_ref, o16_ref):
        t = pl.program_id(1)

        @pl.when(t == 0)
        def _():
            o32_ref[...] = jnp.zeros_like(o32_ref)

        o32_ref[0] += lax.dot_general(a_ref[...], b_ref[...], TN_DIMS, preferred_element_type=F32)

        @pl.when(t == nt - 1)
        def _():
            o16_ref[...] = o32_ref[...].astype(BF)

    a_spec = pl.BlockSpec((tt, a_bw), (lambda j, t: (t, j)) if a_blocked else (lambda j, t: (t, 0)))
    b_spec = pl.BlockSpec((tt, b_bw), (lambda j, t: (t, j)) if b_blocked else (lambda j, t: (t, 0)))
    out = pl.BlockSpec((1, a_bw, b_bw), lambda j, t: (j, 0, 0))
    return pl.pallas_call(
        body, name=name, grid=(nblk, nt),
        in_specs=[a_spec, b_spec], out_specs=[out, out],
        out_shape=[jax.ShapeDtypeStruct((nblk, a_bw, b_bw), F32), jax.ShapeDtypeStruct((nblk, a_bw, b_bw), BF)],
        compiler_params=_params(2))(a, b)


def _grad_w_in_send(xb, dp, bw, tt=2048):
    T = xb.shape[0]
    nt = T // tt
    flip = lambda v, bit: 1 - v if bit else v

    def peer(k):
        return (flip(lax.axis_index("x"), k & 4), flip(lax.axis_index("y"), k & 2), flip(lax.axis_index("c"), k & 1))

    block_of = lambda dev: 4 * dev[0] + 2 * dev[1] + dev[2]
    order = jnp.stack([block_of(peer(NDEV - 1 - q)) for q in range(NDEV)]).astype(jnp.int32)

    def body(order_ref, a_ref, b_ref, o32_ref, o16_ref, land_ref, stage, send_sems, recv_sems):
        q, t = pl.program_id(0), pl.program_id(1)

        def copy(qq):
            return pltpu.make_async_remote_copy(
                src_ref=stage.at[qq], dst_ref=land_ref.at[qq], send_sem=send_sems.at[qq], recv_sem=recv_sems.at[qq],
                device_id=peer(NDEV - 1 - qq), device_id_type=MESH)

        @pl.when(t == 0)
        def _():
            o32_ref[...] = jnp.zeros_like(o32_ref)

        o32_ref[0] += lax.dot_general(a_ref[...], b_ref[...], TN_DIMS, preferred_element_type=F32)

        @pl.when(t == nt - 1)
        def _():
            o16_ref[...] = o32_ref[...].astype(BF)

        def send(qq):
            stage[qq] = o32_ref[0].astype(BF)
            copy(qq).start()

        for qq in range(W_IN_EARLY):
            pl.when(jnp.logical_and(q == qq, t == nt - 1))(functools.partial(send, qq))

        @pl.when(jnp.logical_and(q == NDEV - 1, t == nt - 1))
        def _():
            for qq in range(W_IN_EARLY):
                copy(qq).wait()

    blk = pl.BlockSpec((1, D, bw), lambda q, t, o: (o[q], 0, 0))
    grid_spec = pltpu.PrefetchScalarGridSpec(
        num_scalar_prefetch=1, grid=(NDEV, nt),
        in_specs=[pl.BlockSpec((tt, D), lambda q, t, o: (t, 0)), pl.BlockSpec((tt, bw), lambda q, t, o: (t, o[q]))],
        out_specs=[blk, blk, ANY_SPEC],
        scratch_shapes=[pltpu.VMEM((W_IN_EARLY, D, bw), BF), pltpu.SemaphoreType.DMA((W_IN_EARLY,)),
                        pltpu.SemaphoreType.DMA((W_IN_EARLY,))])
    return pl.pallas_call(
        body, name="grad_w_in", grid_spec=grid_spec,
        out_shape=[jax.ShapeDtypeStruct((NDEV, D, bw), F32), jax.ShapeDtypeStruct((NDEV, D, bw), BF),
                   jax.ShapeDtypeStruct((W_IN_EARLY, D, bw), BF)],
        compiler_params=_params(2))(order, xb, dp)


def _tn_matmul_tiles_outer(a, b, nblk, a_bw, b_bw, a_blocked, name, tt=2048):
    T = a.shape[0]
    nt = T // tt

    def body(a_ref, b_ref, o32_ref, o16_ref, acc):
        t, j = pl.program_id(0), pl.program_id(1)
        prod = lax.dot_general(a_ref[...], b_ref[...], TN_DIMS, preferred_element_type=F32)

        @pl.when(t == 0)
        def _():
            acc[j] = prod

        @pl.when(t > 0)
        def _():
            acc[j] += prod

        @pl.when(t == nt - 1)
        def _():
            o32_ref[0] = acc[j]
            o16_ref[0] = acc[j].astype(BF)

    a_spec = pl.BlockSpec((tt, a_bw), (lambda t, j: (t, j)) if a_blocked else (lambda t, j: (t, 0)))
    b_spec = pl.BlockSpec((tt, b_bw), (lambda t, j: (t, 0)) if a_blocked else (lambda t, j: (t, j)))
    out = pl.BlockSpec((1, a_bw, b_bw), lambda t, j: (jnp.where(t == nt - 1, j, 0), 0, 0))
    return pl.pallas_call(
        body, name=name, grid=(nt, nblk),
        in_specs=[a_spec, b_spec], out_specs=[out, out],
        out_shape=[jax.ShapeDtypeStruct((nblk, a_bw, b_bw), F32), jax.ShapeDtypeStruct((nblk, a_bw, b_bw), BF)],
        scratch_shapes=[pltpu.VMEM((nblk, a_bw, b_bw), F32)],
        compiler_params=_params(2))(a, b)


def _mlp_up_bwd(dhpre, wup_t, dz2, xhat1, rstd1, vecs, push, tm=512):
    T, dff = dhpre.shape

    def body(dh_ref, w_ref, dz2_ref, xhat_ref, rstd_ref, vec_ref, dz1b_ref, st_ref):
        @pl.when(pl.program_id(0) == 0)
        def _():
            st_ref[...] = jnp.zeros_like(st_ref)

        dx1 = jnp.dot(dh_ref[...], w_ref[...], preferred_element_type=F32) + ALPHA * dz2_ref[...].astype(F32)
        xhat = xhat_ref[...]
        st_ref[0:1, :] += _colsum(dx1 * xhat)
        st_ref[1:2, :] += _colsum(dx1)
        dz1b_ref[...] = _ln_bwd(dx1, xhat, rstd_ref[...], vec_ref[3:4, :]).astype(BF)

    tok = pl.BlockSpec((tm, D), lambda i: (i, 0))
    vec = pl.BlockSpec((8, D), lambda i: (0, 0))
    return _pallas(
        body, name="mlp_up_bwd", grid=(T // tm,),
        in_specs=[pl.BlockSpec((tm, dff), lambda i: (i, 0)), _resident((dff, D)),
                  tok, tok, pl.BlockSpec((tm, 1), lambda i: (i, 0)), vec],
        out_specs=[tok, vec],
        out_shape=[jax.ShapeDtypeStruct((T, D), BF), jax.ShapeDtypeStruct((8, D), F32)],
        args=(dhpre, wup_t, dz2, xhat1, rstd1, vecs), push=push)


def _merge_bwd(dz1, p, ya, yb, conva, xhatb, rstdb, woa, wob, wo, vecs, push, tm=256):
    T = dz1.shape[0]

    def body(dz1_ref, ga_ref, gb_ref, ba_ref, ya_ref, yb_ref, conva_ref, xhat_ref, rstd_ref,
             woa_ref, wob_ref, wo_ref, vec_ref,
             dya_ref, dyb_ref, dg_ref, dba_ref, dconva_ref, du1_ref, st_ref):
        @pl.when(pl.program_id(0) == 0)
        def _():
            st_ref[...] = jnp.zeros_like(st_ref)

        dmerged = lax.dot_general(dz1_ref[...], wo_ref[...], NT_DIMS, preferred_element_type=F32)
        sa, sb = _sigmoid(ga_ref[...].astype(F32)), _sigmoid(gb_ref[...].astype(F32))
        dya = (dmerged * sa).astype(BF)
        dyb = (dmerged * sb).astype(BF)
        dya_ref[...] = dya
        dyb_ref[...] = dyb
        dg_ref[:, 0:D] = (dmerged * ya_ref[...].astype(F32) * (sa * (1.0 - sa))).astype(BF)
        dg_ref[:, D:2 * D] = (dmerged * yb_ref[...].astype(F32) * (sb * (1.0 - sb))).astype(BF)

        dyapre = lax.dot_general(dya, woa_ref[...], NT_DIMS, preferred_element_type=F32)
        dba_ref[...] = (dyapre * conva_ref[...]).astype(BF)
        dconva_ref[...] = dyapre * ba_ref[...].astype(F32)

        du3 = lax.dot_general(dyb, wob_ref[...], NT_DIMS, preferred_element_type=F32)
        xhat = xhat_ref[...]
        gamma = vec_ref[1:2, :]
        u2 = xhat * gamma + vec_ref[2:3, :]
        s = _sigmoid(u2)
        du2 = du3 * (s * (1.0 + u2 * (1.0 - s)))
        st_ref[0:1, :] += _colsum(du2 * xhat)
        st_ref[1:2, :] += _colsum(du2)
        du1 = _ln_bwd(du2, xhat, rstd_ref[...], gamma)
        st_ref[2:3, :] += _colsum(du1)
        du1_ref[...] = du1

    tok = pl.BlockSpec((tm, D), lambda i: (i, 0))
    wfull = pl.BlockSpec((D, D), lambda i: (0, 0))
    vec = pl.BlockSpec((8, D), lambda i: (0, 0))
    return _pallas(
        body, name="merge_bwd", grid=(T // tm,),
        in_specs=[tok, _seg(tm, 5), _seg(tm, 6), _seg(tm, 0), tok, tok, tok, tok, pl.BlockSpec((tm, 1), lambda i: (i, 0)),
                  wfull, wfull, wfull, vec],
        out_specs=[tok, tok, pl.BlockSpec((tm, 2 * D), lambda i: (i, 0)), tok, tok, tok, vec],
        out_shape=[jax.ShapeDtypeStruct((T, D), BF), jax.ShapeDtypeStruct((T, D), BF),
                   jax.ShapeDtypeStruct((T, 2 * D), BF), jax.ShapeDtypeStruct((T, D), BF),
                   jax.ShapeDtypeStruct((T, D), F32), jax.ShapeDtypeStruct((T, D), F32),
                   jax.ShapeDtypeStruct((8, D), F32)],
        args=(dz1, p, p, p, ya, yb, conva, xhatb, rstdb, woa, wob, wo, vecs), push=push)


def _rows8(v):
    out = v[0:8]
    for q in range(1, RC // 8):
        out = out + v[8 * q:8 * q + 8]
    return out


def _conv_bwd(dconva, du1, p, dba, dg, wa, wb, push, tt=256):
    T = p.shape[0]
    nsteps = T // tt

    def body(dca_ref, dca_n, du1_ref, du1_n, ca, va, vb, gb, dba_ref, dg_ref, wa_ref, wb_ref,
             dp_ref, gw_ref, cabuf, u0buf, dcabuf, du1buf, dcain, du0, gwa, gwb, shd):
        i = pl.program_id(0)
        first, last = i == 0, i == nsteps - 1

        @pl.when(first)
        def _():
            gwa[...] = jnp.zeros_like(gwa)
            gwb[...] = jnp.zeros_like(gwb)

        f = lambda ref: ref[...].astype(F32)
        cav, vav, vbv = f(ca), f(va), f(vb)
        cabuf[...] = cav * vav
        sg = _sigmoid(f(gb))
        u0buf[...] = vbv * sg
        dcabuf[0:tt, :] = dca_ref[...]
        dcabuf[tt:tt + HN, :] = jnp.where(last, 0.0, dca_n[...])
        du1buf[0:tt, :] = du1_ref[...]
        du1buf[tt:tt + HB, :] = jnp.where(last, 0.0, du1_n[...])

        def lane_body(cidx, carry):
            ls = pl.ds(pl.multiple_of(cidx * LANES, LANES), LANES)
            _shifted_copies(shd, du1buf, ls, tt + HB - 8)
            for r in range(tt // RC):
                rows = pl.ds(r * RC, RC)
                cin = cabuf[rows, ls]
                acc = jnp.zeros((RC, LANES), F32)
                for k in range(KA):
                    dout = dcabuf[pl.ds(r * RC + KA - 1 - k, RC), ls]
                    acc = acc + wa_ref[k:k + 1, ls] * dout
                    gwa[8 * k:8 * k + 8, ls] += _rows8(cin * dout)
                dcain[rows, ls] = acc
                uin = u0buf[rows, ls]
                acc = jnp.zeros((RC, LANES), F32)
                for k in range(KB):
                    dout = _tap(shd, du1buf, ls, r * RC + KB - 1 - k, RC)
                    acc = acc + wb_ref[k:k + 1, ls] * dout
                    gwb[8 * k:8 * k + 8, ls] += _rows8(uin * dout)
                du0[rows, ls] = acc
            return carry

        lax.fori_loop(0, D // LANES, lane_body, 0)
        dca_in = dcain[...]
        du0v = du0[...]
        dp_ref[:, 0:D] = dba_ref[...]
        dp_ref[:, D:2 * D] = (dca_in * vav).astype(BF)
        dp_ref[:, 2 * D:3 * D] = (dca_in * cav).astype(BF)
        dp_ref[:, 3 * D:4 * D] = (du0v * sg).astype(BF)
        dp_ref[:, 4 * D:5 * D] = (du0v * vbv * (sg * (1.0 - sg))).astype(BF)
        dp_ref[:, 5 * D:7 * D] = dg_ref[...]

        @pl.when(last)
        def _():
            gw_ref[...] = jnp.zeros_like(gw_ref)
            for k in range(KA):
                gw_ref[k:k + 1, :] = _colsum(gwa[8 * k:8 * k + 8, :])
            for k in range(KB):
                gw_ref[8 + k:9 + k, :] = _colsum(gwb[8 * k:8 * k + 8, :])

    full = lambda r: pl.BlockSpec((r, D), lambda i: (0, 0))
    tok = pl.BlockSpec((tt, D), lambda i: (i, 0))
    nxt = lambda h: pl.BlockSpec((h, D), lambda i: (jnp.minimum((i + 1) * (tt // h), T // h - 1), 0))
    return _pallas(
        body, name="conv_bwd", grid=(nsteps,),
        in_specs=[tok, nxt(HN), tok, nxt(HB),
                  _seg(tt, 1), _seg(tt, 2), _seg(tt, 3), _seg(tt, 4),
                  tok, pl.BlockSpec((tt, 2 * D), lambda i: (i, 0)), full(8), full(32)],
        out_specs=[pl.BlockSpec((tt, 7 * D), lambda i: (i, 0)), full(40)],
        out_shape=[jax.ShapeDtypeStruct((T, 7 * D), BF), jax.ShapeDtypeStruct((40, D), F32)],
        scratch_shapes=[pltpu.VMEM((tt, D), F32), pltpu.VMEM((tt, D), F32),
                        pltpu.VMEM((tt + HN, D), F32), pltpu.VMEM((tt + HB, D), F32),
                        pltpu.VMEM((tt, D), F32), pltpu.VMEM((tt, D), F32),
                        pltpu.VMEM((8 * KA, D), F32), pltpu.VMEM((8 * KB, D), F32),
                        pltpu.VMEM((8, HB + tt, LANES), F32)],
        args=(dconva, dconva, du1, du1, p, p, p, p, dba, dg, wa, wb), push=push)


def _inproj_bwd(dp, win_t, dz1, push, tm=512):
    T, cols = dp.shape

    def body(dp_ref, w_ref, dz1_ref, o_ref):
        o_ref[...] = ALPHA * dz1_ref[...].astype(F32) + jnp.dot(dp_ref[...], w_ref[...], preferred_element_type=F32)

    tok = pl.BlockSpec((tm, D), lambda i: (i, 0))
    return _pallas(
        body, name="inproj_bwd", grid=(T // tm,),
        in_specs=[pl.BlockSpec((tm, cols), lambda i: (i, 0)), _resident((cols, D)), tok],
        out_specs=[tok],
        out_shape=[jax.ShapeDtypeStruct((T, D), F32)],
        args=(dp, win_t, dz1), push=push)


def _adam_math(w, m, v, g):
    nm = ADAM_B1 * m + (1.0 - ADAM_B1) * g
    nv = ADAM_B2 * v + (1.0 - ADAM_B2) * (g * g)
    m_hat = nm / (1.0 - ADAM_B1 ** ADAM_STEP)
    v_hat = nv / (1.0 - ADAM_B2 ** ADAM_STEP)
    return -ADAM_LR * (m_hat / (jnp.sqrt(v_hat) + ADAM_EPS) + ADAM_WD * w), nm, nv


def _adamw(w, m, v, g32, landings, me, name, rb):
    R, C = w.shape
    nland = len(landings)

    def body(me_ref, w_ref, m_ref, v_ref, own_ref, *refs):
        g_ref, d_ref, nm_ref, nv_ref = refs[nland:]
        g = own_ref[0]
        for l_ref in refs[:nland]:
            for k in range(l_ref.shape[0]):
                g = g + l_ref[k].astype(F32)
        g_ref[...] = g
        d_ref[...], nm_ref[...], nv_ref[...] = _adam_math(w_ref[...], m_ref[...], v_ref[...], g)

    blk = pl.BlockSpec((rb, C), lambda i, me_ref: (i, 0))
    grid_spec = pltpu.PrefetchScalarGridSpec(
        num_scalar_prefetch=1, grid=(R // rb,),
        in_specs=[blk, blk, blk, pl.BlockSpec((1, rb, C), lambda i, me_ref: (me_ref[0], i, 0))]
        + [pl.BlockSpec((l.shape[0], rb, C), lambda i, me_ref: (0, i, 0)) for l in landings],
        out_specs=[blk] * 4)
    return pl.pallas_call(
        body, name=name, grid_spec=grid_spec, out_shape=[jax.ShapeDtypeStruct((R, C), F32)] * 4,
        compiler_params=_params(1))(me, w, m, v, g32, *landings)


def _adamw_small(small_g, vec_w, vec_m, vec_v, conv_w, conv_m, conv_v):
    nv_ = len(vec_w)
    conv_rows = [(8, KA), (16, KB)]

    def body(*refs):
        g_ref = refs[0]
        w_refs, m_refs, v_refs = refs[1:10], refs[10:19], refs[19:28]
        out_refs, gsum = refs[28:64], refs[64]
        acc = g_ref[0]
        for j in range(1, NDEV):
            acc = acc + g_ref[j]
        gsum[...] = acc
        me = 4 * lax.axis_index("x") + 2 * lax.axis_index("y") + lax.axis_index("c")
        cols = pl.ds(pl.multiple_of(me * LANES, LANES), LANES)
        for i in range(nv_ + 2):
            if i < nv_:
                g = gsum[i:i + 1, :]
            else:
                r0, k = conv_rows[i - nv_]
                g = gsum[r0:r0 + k, cols]
            o = out_refs[4 * i:4 * i + 4]
            o[0][...] = g
            o[1][...], o[2][...], o[3][...] = _adam_math(w_refs[i][...], m_refs[i][...], v_refs[i][...], g)

    ws, ms, vs = list(vec_w) + list(conv_w), list(vec_m) + list(conv_m), list(vec_v) + list(conv_v)
    out_shape = [jax.ShapeDtypeStruct(w.shape, F32) for w in ws for _ in range(4)]
    return pl.pallas_call(
        body, name="adamw_small", out_shape=out_shape,
        scratch_shapes=[pltpu.VMEM(small_g.shape[1:], F32)])(small_g, *ws, *ms, *vs)


def _pad_rows(a, rows):
    return jnp.pad(a, ((0, rows - a.shape[0]), (0, 0)))


def _local_step(p, xb, mixed, post, x, target, win_t, wup, wup_t, wdown, woa, wob, wo, wa, wb, vecs):
    yapre, conva, xhatb, rstdb, u3 = mixed
    ya, yb, merged, xhat1, rstd1, x1b = post
    r, h = _mlp_up(x1b, wup)
    dz2b, st2 = _mlp_down_loss(h, wdown, xhat1, target, vecs)

    by_owner = lambda g16: g16.reshape(NDEV, D // NDEV, D)
    dhpre = _mlp_down_bwd(dz2b, wdown, r)
    g_wdown = _tn_matmul_tiles_outer(h, dz2b, NDEV, 512, D, True, "grad_w_down")
    (dz1b, st1), land_wdown = _mlp_up_bwd(dhpre, wup_t, dz2b, xhat1, rstd1, vecs, _push(exch=[g_wdown[1]]))
    g_wup = _tn_matmul_tiles_outer(x1b, dhpre, NDEV, D, 512, False, "grad_w_up")
    (dya, dyb, dg, dba, dconva, du1, stb), _ = _merge_bwd(
        dz1b, p, ya, yb, conva, xhatb, rstdb, woa, wob, wo, vecs, None)
    g_wo = _tn_matmul(merged, dz1b, 1, D, D, False, False, "grad_w_o")
    g_woa = _tn_matmul(yapre, dya, 1, D, D, False, False, "grad_w_out_a")
    g_wob = _tn_matmul(u3, dyb, 1, D, D, False, False, "grad_w_out_b")
    (dp, gw), land_conv = _conv_bwd(
        dconva, du1, p, dba, dg, wa, wb,
        _push(exch=[g_wup[1], by_owner(g_woa[1]), by_owner(g_wob[1]), by_owner(g_wo[1])]))
    g_win, g_win16, land_win = _grad_w_in_send(xb, dp, dp.shape[1] // NDEV)

    small = jnp.concatenate([stb[2:3], stb[0:2], st1[0:2], st2[0:3], gw], axis=0)
    late_ks = tuple(range(1, NDEV - W_IN_EARLY))
    (grad_x,), land_last = _inproj_bwd(dp, win_t, dz1b, _push(exch=[g_win16], gath=[small], ks=late_ks))
    grads = (g_win, g_wup[0], g_wdown[0], g_woa[0], g_wob[0], g_wo[0])
    return grad_x, grads, small, land_wdown + land_conv + [land_win] + land_last


def kernel(x, w_in, conv_a_w, w_out_a, conv_b_w, conv_b_bias, ln_b_gamma, ln_b_beta, w_out_b, w_o, ln1_gamma, ln1_beta, w_up, w_down, ln2_gamma, ln2_beta, loss_target, m_w_in, m_conv_a_w, m_w_out_a, m_conv_b_w, m_conv_b_bias, m_ln_b_gamma, m_ln_b_beta, m_w_out_b, m_w_o, m_ln1_gamma, m_ln1_beta, m_w_up, m_w_down, m_ln2_gamma, m_ln2_beta, v_w_in, v_conv_a_w, v_w_out_a, v_conv_b_w, v_conv_b_bias, v_ln_b_gamma, v_ln_b_beta, v_w_out_b, v_w_o, v_ln1_gamma, v_ln1_beta, v_w_up, v_w_down, v_ln2_gamma, v_ln2_beta):
    T = x.shape[1]
    me = 4 * lax.axis_index("x") + 2 * lax.axis_index("y") + lax.axis_index("c")

    conv_shard = jnp.concatenate([_pad_rows(conv_a_w, 8), _pad_rows(conv_b_w, 32)], axis=0)
    p, xb, win_g, conv_g = _inproj_gather(x[0], w_in.astype(BF), conv_shard)
    conv_full = jnp.transpose(conv_g, (1, 0, 2)).reshape(40, D)
    vecs = jnp.stack([conv_b_bias, ln_b_gamma, ln_b_beta, ln1_gamma, ln1_beta, ln2_gamma, ln2_beta,
                      jnp.zeros_like(ln2_beta)])
    whole = lambda g: jnp.transpose(g, (1, 0, 2)).reshape(D, -1)
    whole_t = lambda g: jnp.transpose(g, (0, 2, 1)).reshape(-1, D)
    mixed, (woa_g, wob_g, wo_g, wup_g, wdown_g) = _mixer_fwd(
        p, conv_full[0:8], conv_full[8:40], vecs,
        [w_out_a.astype(BF), w_out_b.astype(BF), w_o.astype(BF), w_up.astype(BF), w_down.astype(BF)])
    woa, wob, wo = woa_g.reshape(D, D), wob_g.reshape(D, D), wo_g.reshape(D, D)
    post, _ = _post_mixer(mixed[0], mixed[4], p, x[0], woa, wob, wo, vecs, [])

    grad_x, grads, small, landing = _local_step(
        p, xb, mixed, post, x[0], loss_target[0], whole_t(win_g), whole(wup_g), whole_t(wup_g),
        wdown_g.reshape(NDEV * 512, D),
        woa, wob, wo, conv_full[0:8], conv_full[8:40], vecs)
    g_win, g_wup, g_wdown, g_woa, g_wob, g_wo = grads
    l_wdown, l_wup, l_woa, l_wob, l_wo, l_win_early, l_win_late, small_g = landing

    loss = lax.psum(0.5 / D * jnp.sum(small[7]), ("x", "y", "c"))

    me1 = me.astype(jnp.int32).reshape(1)
    by_owner = lambda g32: g32.reshape(NDEV, D // NDEV, D)
    r_win = _adamw(w_in, m_w_in, v_w_in, g_win, [l_win_early, l_win_late], me1, "adamw_w_in", 256)
    r_wup = _adamw(w_up, m_w_up, v_w_up, g_wup, [l_wup], me1, "adamw_w_up", 256)
    r_wdown = _adamw(w_down, m_w_down, v_w_down, g_wdown, [l_wdown], me1, "adamw_w_down", 256)
    r_woa = _adamw(w_out_a, m_w_out_a, v_w_out_a, by_owner(g_woa), [l_woa], me1, "adamw_w_out_a", 128)
    r_wob = _adamw(w_out_b, m_w_out_b, v_w_out_b, by_owner(g_wob), [l_wob], me1, "adamw_w_out_b", 128)
    r_wo = _adamw(w_o, m_w_o, v_w_o, by_owner(g_wo), [l_wo], me1, "adamw_w_o", 128)

    row = lambda vec: vec.reshape(1, D)
    small_out = _adamw_small(
        small_g,
        [row(a) for a in (conv_b_bias, ln_b_gamma, ln_b_beta, ln1_gamma, ln1_beta, ln2_gamma, ln2_beta)],
        [row(a) for a in (m_conv_b_bias, m_ln_b_gamma, m_ln_b_beta, m_ln1_gamma, m_ln1_beta, m_ln2_gamma, m_ln2_beta)],
        [row(a) for a in (v_conv_b_bias, v_ln_b_gamma, v_ln_b_beta, v_ln1_gamma, v_ln1_beta, v_ln2_gamma, v_ln2_beta)],
        [conv_a_w, conv_b_w], [m_conv_a_w, m_conv_b_w], [v_conv_a_w, v_conv_b_w])
    r_vec = [[small_out[4 * i + q].reshape(D) for q in range(4)] for i in range(7)]
    r_conva, r_convb = small_out[28:32], small_out[32:36]

    per_weight = []
    for q in range(4):
        per_weight.append([
            r_win[q], r_conva[q], r_woa[q], r_convb[q],
            r_vec[0][q], r_vec[1][q], r_vec[2][q], r_wob[q], r_wo[q], r_vec[3][q], r_vec[4][q],
            r_wup[q], r_wdown[q], r_vec[5][q], r_vec[6][q]])
    return (loss, grad_x[None], *per_weight[0], *per_weight[1], *per_weight[2], *per_weight[3])
```

```python
import functools

import jax
import jax.numpy as jnp
from jax import lax
from jax.experimental import pallas as pl
from jax.experimental.pallas import tpu as pltpu

F32 = jnp.float32
BF = jnp.bfloat16
D = 1024
NDEV = 8
ALPHA = 2.0 ** 0.25
LN_EPS = 1e-5
KA, KB = 3, 31
HA, HB = 16, 32
HN = 8
RC = 64
W_IN_EARLY = 4
LANES = 128
VMEM_LIMIT = 56 * 1024 * 1024
MESH = pl.DeviceIdType.MESH
ADAM_LR, ADAM_B1, ADAM_B2, ADAM_EPS, ADAM_WD, ADAM_STEP = 0.001, 0.9, 0.999, 1e-08, 0.01, 10

ANY_SPEC = pl.BlockSpec(memory_space=pl.ANY)
NT_DIMS = (((1,), (1,)), ((), ()))
TN_DIMS = (((0,), (0,)), ((), ()))


def _params(n_axes):
    return pltpu.CompilerParams(dimension_semantics=("arbitrary",) * n_axes, vmem_limit_bytes=VMEM_LIMIT)


def _sigmoid(v):
    return 0.5 * jnp.tanh(0.5 * v) + 0.5


def _ln_fwd(z):
    mu = jnp.mean(z, axis=-1, keepdims=True)
    zc = z - mu
    var = jnp.mean(zc * zc, axis=-1, keepdims=True)
    rstd = lax.rsqrt(var + LN_EPS)
    return zc * rstd, rstd


def _ln_bwd(dy, xhat, rstd, gamma):
    dxhat = dy * gamma
    m1 = jnp.mean(dxhat, axis=-1, keepdims=True)
    m2 = jnp.mean(dxhat * xhat, axis=-1, keepdims=True)
    return rstd * (dxhat - m1 - xhat * m2)


def _colsum(v):
    return jnp.sum(v, axis=0, keepdims=True)


class _TwoLevelGather:
    def __init__(self, ins, outs, send_sems, recv_sems, local_sems):
        self.ins, self.outs = ins, outs
        self.send_sems, self.recv_sems, self.local_sems = send_sems, recv_sems, local_sems
        x, y, c = lax.axis_index("x"), lax.axis_index("y"), lax.axis_index("c")
        self.me, self.sibling, self.c = (x, y, c), (x, y, 1 - c), c
        self.chips = [(1 - x, y), (x, 1 - y), (1 - x, 1 - y)]
        self.n = len(ins)

    @staticmethod
    def out_shape(arrs):
        return [jax.ShapeDtypeStruct((NDEV,) + a.shape, a.dtype) for a in arrs]

    @staticmethod
    def scratch(n):
        return [pltpu.SemaphoreType.DMA((n, 7)), pltpu.SemaphoreType.DMA((n, 7)), pltpu.SemaphoreType.DMA((n,))]

    def _copy(self, a, k, block, to, src=None):
        px, py, pc = block
        rows = self.outs[a].at[4 * px + 2 * py + pc]
        return pltpu.make_async_remote_copy(
            src_ref=rows if src is None else src, dst_ref=rows,
            send_sem=self.send_sems.at[a, k], recv_sem=self.recv_sems.at[a, k],
            device_id=to, device_id_type=MESH)

    def _mine(self, a):
        x, y, c = self.me
        return pltpu.make_async_copy(self.ins[a], self.outs[a].at[4 * x + 2 * y + c], self.local_sems.at[a])

    def _first(self, a):
        cps = [self._copy(a, 0, self.me, self.sibling, src=self.ins[a])]
        return cps + [self._copy(a, 1 + j, self.me, (*chip, self.c), src=self.ins[a]) for j, chip in enumerate(self.chips)]

    def _passed(self, a, j):
        return self._copy(a, 4 + j, (*self.chips[j], self.c), self.sibling)

    def start(self, diagonal=True):
        for a in range(self.n):
            self._mine(a).start()
        for a in range(self.n):
            for cp in self._first(a)[:4 if diagonal else 3]:
                cp.start()

    def start_diagonal(self):
        for a in range(self.n):
            self._first(a)[3].start()

    def wait_ici(self, j):
        for a in range(self.n):
            self._copy(a, 1 + j, (*self.chips[j], self.c), self.me).wait_recv()

    def pass_on(self, j):
        for a in range(self.n):
            self._passed(a, j).start()

    def wait_sibling(self):
        for a in range(self.n):
            self._copy(a, 0, self.sibling, self.me).wait_recv()

    def wait_passed(self, j):
        for a in range(self.n):
            self._copy(a, 4 + j, (*self.chips[j], 1 - self.c), self.me).wait_recv()

    def drain(self):
        for a in range(self.n):
            for cp in self._first(a) + [self._passed(a, j) for j in range(3)]:
                cp.wait_send()
            self._mine(a).wait()

    def forward(self):
        for j in range(3):
            self.wait_ici(j)
            self.pass_on(j)

    def finish(self):
        self.wait_sibling()
        for j in range(3):
            self.wait_passed(j)
        self.drain()


class _Push:
    def __init__(self, exch=(), gath=(), ks=tuple(range(1, NDEV))):
        self.exch, self.gath, self.ks = list(exch), list(gath), tuple(ks)
        self.n = len(self.exch) + len(self.gath)

    def operands(self):
        return self.exch + self.gath

    def out_shape(self):
        return ([jax.ShapeDtypeStruct((len(self.ks),) + a.shape[1:], a.dtype) for a in self.exch]
                + [jax.ShapeDtypeStruct((NDEV,) + a.shape, a.dtype) for a in self.gath])

    def scratch(self):
        return [pltpu.SemaphoreType.DMA((self.n, 7)), pltpu.SemaphoreType.DMA((self.n, 7)),
                pltpu.SemaphoreType.DMA((max(len(self.gath), 1),))]

    def copies(self, ins, outs, send_sems, recv_sems, local_sems):
        x, y, c = lax.axis_index("x"), lax.axis_index("y"), lax.axis_index("c")
        me = 4 * x + 2 * y + c
        ne = len(self.exch)
        remote = []
        for k in range(1, NDEV):
            px = 1 - x if k & 4 else x
            py = 1 - y if k & 2 else y
            pc = 1 - c if k & 1 else c
            for a in range(self.n):
                if a < ne and k not in self.ks:
                    continue
                src = ins[a].at[4 * px + 2 * py + pc] if a < ne else ins[a]
                dst = outs[a].at[self.ks.index(k)] if a < ne else outs[a].at[me]
                remote.append(pltpu.make_async_remote_copy(
                    src_ref=src, dst_ref=dst, send_sem=send_sems.at[a, k - 1], recv_sem=recv_sems.at[a, k - 1],
                    device_id=(px, py, pc), device_id_type=MESH))
        local = [pltpu.make_async_copy(ins[a], outs[a].at[me], local_sems.at[a - ne]) for a in range(ne, self.n)]
        return remote, local


def _push(exch=(), gath=(), ks=tuple(range(1, NDEV))):
    return _Push(exch, gath, ks)


def _pallas(body, *, name, grid, in_specs, out_specs, out_shape, args, scratch_shapes=(), push=None):
    ni, no, ns = len(in_specs), len(out_specs), len(scratch_shapes)
    if push is None:
        outs = pl.pallas_call(
            body, name=name, grid=grid, in_specs=in_specs, out_specs=out_specs, out_shape=out_shape,
            scratch_shapes=list(scratch_shapes), compiler_params=_params(len(grid)))(*args)
        return list(outs), []
    npush = push.n

    def wrapped(*refs):
        ins, pins = refs[:ni], refs[ni:ni + npush]
        outs, pouts = refs[ni + npush:ni + npush + no], refs[ni + npush + no:ni + 2 * npush + no]
        scr, sems = refs[ni + 2 * npush + no:ni + 2 * npush + no + ns], refs[ni + 2 * npush + no + ns:]
        first = functools.reduce(jnp.logical_and, [pl.program_id(d) == 0 for d in range(len(grid))])
        last = functools.reduce(jnp.logical_and, [pl.program_id(d) == grid[d] - 1 for d in range(len(grid))])
        remote, local = push.copies(pins, pouts, *sems)

        @pl.when(first)
        def _():
            for cp in local + remote:
                cp.start()

        body(*ins, *outs, *scr)

        @pl.when(last)
        def _():
            for cp in remote + local:
                cp.wait()

    outs = pl.pallas_call(
        wrapped, name=name, grid=grid,
        in_specs=list(in_specs) + [ANY_SPEC] * npush, out_specs=list(out_specs) + [ANY_SPEC] * npush,
        out_shape=list(out_shape) + push.out_shape(), scratch_shapes=list(scratch_shapes) + push.scratch(),
        compiler_params=_params(len(grid)))(*args, *push.operands())
    return list(outs[:no]), list(outs[no:])


def _with_gather(body, late, *, name, nsteps, in_specs, out_specs, out_shape, args, scratch_shapes=()):
    ni, no, ns, n = len(in_specs), len(out_specs), len(scratch_shapes), len(late)
    pass_step = (7 * nsteps) // 8
    if not late:
        outs = pl.pallas_call(
            body, name=name, grid=(nsteps,), in_specs=in_specs, out_specs=out_specs, out_shape=out_shape,
            scratch_shapes=list(scratch_shapes), compiler_params=_params(1))(*args)
        return list(outs), []

    def wrapped(*refs):
        ins, outs = refs[:ni], refs[ni + n:ni + n + no]
        scr = refs[ni + 2 * n + no:ni + 2 * n + no + ns]
        gather = _TwoLevelGather(refs[ni:ni + n], refs[ni + n + no:ni + 2 * n + no], *refs[ni + 2 * n + no + ns:])
        step = pl.program_id(0)
        pl.when(step == 0)(gather.start)
        pl.when(step == pass_step)(gather.forward)
        body(*ins, *outs, *scr)
        pl.when(step == nsteps - 1)(gather.finish)

    outs = pl.pallas_call(
        wrapped, name=name, grid=(nsteps,),
        in_specs=list(in_specs) + [ANY_SPEC] * n, out_specs=list(out_specs) + [ANY_SPEC] * n,
        out_shape=list(out_shape) + _TwoLevelGather.out_shape(late),
        scratch_shapes=list(scratch_shapes) + _TwoLevelGather.scratch(n),
        compiler_params=_params(1))(*args, *late)
    return list(outs[:no]), list(outs[no:])


def _inproj_gather(x, w_shard, conv_shard, tm=2048):
    T = x.shape[0]
    ni = T // tm
    bw = w_shard.shape[1]
    cx, cy, cc = lax.axis_index("x"), lax.axis_index("y"), lax.axis_index("c")
    blk = lambda px, py, pc: 4 * px + 2 * py + pc
    order = [blk(cx, cy, cc), blk(cx, cy, 1 - cc)]
    for chip in [(1 - cx, cy), (cx, 1 - cy), (1 - cx, 1 - cy)]:
        order += [blk(*chip, cc), blk(*chip, 1 - cc)]
    order = jnp.stack(order).astype(jnp.int32)

    def body(order_ref, x_ref, w_ref, conv_ref, p_ref, xb_ref, wing_ref, convg_ref, xbs, wbuf, wsem, *sems):
        gather = _TwoLevelGather([w_ref, conv_ref], [wing_ref, convg_ref], *sems)
        j, i = pl.program_id(0), pl.program_id(1)

        def load(src):
            cp = pltpu.make_async_copy(src, wbuf, wsem)
            cp.start()
            cp.wait()

        def arrival(jj):
            if jj == 0:
                gather.start(diagonal=False)
                load(w_ref)
                return
            if jj == 1:
                gather.wait_sibling()
            elif jj % 2 == 0:
                if jj == 2:
                    gather.start_diagonal()
                gather.wait_ici(jj // 2 - 1)
                gather.pass_on(jj // 2 - 1)
            else:
                gather.wait_passed(jj // 2 - 1)
            load(wing_ref.at[order_ref[jj]])

        for jj in range(NDEV):
            pl.when(jnp.logical_and(j == jj, i == 0))(functools.partial(arrival, jj))

        @pl.when(j == 0)
        def _():
            xb = x_ref[...].astype(BF)
            xbs[i] = xb
            xb_ref[...] = xb

        p_ref[...] = jnp.dot(xbs[i], wbuf[...], preferred_element_type=F32).astype(BF)

        @pl.when(jnp.logical_and(j == NDEV - 1, i == ni - 1))
        def _():
            gather.drain()

    rows_once = lambda j, i, o: (jnp.where(j == 0, i, ni - 1), 0)
    grid_spec = pltpu.PrefetchScalarGridSpec(
        num_scalar_prefetch=1, grid=(NDEV, ni),
        in_specs=[pl.BlockSpec((tm, D), rows_once), ANY_SPEC, ANY_SPEC],
        out_specs=[pl.BlockSpec((tm, bw), lambda j, i, o: (i, o[j])), pl.BlockSpec((tm, D), rows_once),
                   ANY_SPEC, ANY_SPEC],
        scratch_shapes=[pltpu.VMEM((ni, tm, D), BF), pltpu.VMEM((D, bw), BF), pltpu.SemaphoreType.DMA(())]
        + _TwoLevelGather.scratch(2))
    p, xb, win_g, conv_g = pl.pallas_call(
        body, name="inproj_gather", grid_spec=grid_spec,
        out_shape=[jax.ShapeDtypeStruct((T, NDEV * bw), BF), jax.ShapeDtypeStruct((T, D), BF)]
        + _TwoLevelGather.out_shape([w_shard, conv_shard]),
        compiler_params=_params(2))(order, x, w_shard, conv_shard)
    return p, xb, win_g, conv_g


def _mixer_fwd(p, wa, wb, vecs, late, tt=256):
    T = p.shape[0]
    nt = T // tt

    def body(ba, ca, va, vb, gb, ca_p, va_p, vb_p, gb_p, wa_ref, wb_ref, vec_ref,
             yapre_ref, conva_ref, xhat_ref, rstd_ref, u3_ref, cabuf, u0buf, u1buf, shu):
        first = pl.program_id(0) == 0
        f = lambda ref: ref[...].astype(F32)
        cabuf[0:HA, :] = jnp.where(first, 0.0, f(ca_p) * f(va_p))
        cabuf[HA:HA + tt, :] = f(ca) * f(va)
        u0buf[0:HB, :] = jnp.where(first, 0.0, f(vb_p) * _sigmoid(f(gb_p)))
        u0buf[HB:HB + tt, :] = f(vb) * _sigmoid(f(gb))

        def lane_body(cidx, carry):
            ls = pl.ds(pl.multiple_of(cidx * LANES, LANES), LANES)
            _shifted_copies(shu, u0buf, ls, tt + HB - 8)
            for r in range(tt // RC):
                acc = jnp.zeros((RC, LANES), F32)
                for k in range(KA):
                    acc = acc + wa_ref[k:k + 1, ls] * cabuf[pl.ds(HA - (KA - 1) + k + r * RC, RC), ls]
                conva_ref[pl.ds(r * RC, RC), ls] = acc
                acc = jnp.zeros((RC, LANES), F32)
                for k in range(KB):
                    acc = acc + wb_ref[k:k + 1, ls] * _tap(shu, u0buf, ls, HB - (KB - 1) + k + r * RC, RC)
                u1buf[pl.ds(r * RC, RC), ls] = acc
            return carry

        lax.fori_loop(0, D // LANES, lane_body, 0)
        yapre_ref[...] = (f(ba) * conva_ref[...]).astype(BF)
        xhat, rstd = _ln_fwd(u1buf[...] + vec_ref[0:1, :])
        xhat_ref[...] = xhat
        rstd_ref[...] = rstd
        u2 = xhat * vec_ref[1:2, :] + vec_ref[2:3, :]
        u3_ref[...] = (u2 * _sigmoid(u2)).astype(BF)

    full = lambda r: pl.BlockSpec((r, D), lambda i: (0, 0))
    tok = pl.BlockSpec((tt, D), lambda i: (i, 0))
    return _with_gather(
        body, late, name="mixer_fwd", nsteps=nt,
        in_specs=[_seg(tt, 0), _seg(tt, 1), _seg(tt, 2), _seg(tt, 3), _seg(tt, 4),
                  _prev(tt, HA, 1), _prev(tt, HA, 2), _prev(tt, HB, 3), _prev(tt, HB, 4),
                  full(8), full(32), full(8)],
        out_specs=[tok, tok, tok, pl.BlockSpec((tt, 1), lambda i: (i, 0)), tok],
        out_shape=[jax.ShapeDtypeStruct((T, D), BF), jax.ShapeDtypeStruct((T, D), F32),
                   jax.ShapeDtypeStruct((T, D), F32), jax.ShapeDtypeStruct((T, 1), F32),
                   jax.ShapeDtypeStruct((T, D), BF)],
        scratch_shapes=[pltpu.VMEM((HA + tt, D), F32), pltpu.VMEM((HB + tt, D), F32), pltpu.VMEM((tt, D), F32),
                        pltpu.VMEM((8, HB + tt, LANES), F32)],
        args=(p, p, p, p, p, p, p, p, p, wa, wb, vecs))


def _seg(tt, s):
    return pl.BlockSpec((tt, D), lambda i: (i, s))


def _prev(tt, h, s):
    return pl.BlockSpec((h, D), lambda i: (jnp.maximum(i * (tt // h) - 1, 0), s))


def _shifted_copies(shbuf, src, ls, n):
    for s in range(1, 8):
        shbuf[s, 0:n, :] = src[pl.ds(s, n), ls]


def _tap(shbuf, src, ls, off, rows):
    s, q = off % 8, off // 8
    if s == 0:
        return src[pl.ds(off, rows), ls]
    return shbuf[s, pl.ds(8 * q, rows), :]


def _post_mixer(yapre, u3, p, x, woa, wob, wo, vecs, late, tm=512):
    T = x.shape[0]

    def body(yapre_ref, u3_ref, ga_ref, gb_ref, x_ref, woa_ref, wob_ref, wo_ref, vec_ref,
             ya_ref, yb_ref, merged_ref, xhat_ref, rstd_ref, x1b_ref):
        ya = jnp.dot(yapre_ref[...], woa_ref[...], preferred_element_type=F32)
        yb = jnp.dot(u3_ref[...], wob_ref[...], preferred_element_type=F32)
        ya_ref[...] = ya.astype(BF)
        yb_ref[...] = yb.astype(BF)
        merged = (_sigmoid(ga_ref[...].astype(F32)) * ya + _sigmoid(gb_ref[...].astype(F32)) * yb).astype(BF)
        merged_ref[...] = merged
        mix = jnp.dot(merged, wo_ref[...], preferred_element_type=F32)
        xhat, rstd = _ln_fwd(ALPHA * x_ref[...] + mix)
        xhat_ref[...] = xhat
        rstd_ref[...] = rstd
        x1b_ref[...] = (xhat * vec_ref[3:4, :] + vec_ref[4:5, :]).astype(BF)

    tok = pl.BlockSpec((tm, D), lambda i: (i, 0))
    wfull = _resident((D, D))
    one = pl.BlockSpec((tm, 1), lambda i: (i, 0))
    return _with_gather(
        body, late, name="post_mixer", nsteps=T // tm,
        in_specs=[tok, tok, _seg(tm, 5), _seg(tm, 6), tok, wfull, wfull, wfull, pl.BlockSpec((8, D), lambda i: (0, 0))],
        out_specs=[tok, tok, tok, tok, one, tok],
        out_shape=[jax.ShapeDtypeStruct((T, D), BF), jax.ShapeDtypeStruct((T, D), BF),
                   jax.ShapeDtypeStruct((T, D), BF), jax.ShapeDtypeStruct((T, D), F32),
                   jax.ShapeDtypeStruct((T, 1), F32), jax.ShapeDtypeStruct((T, D), BF)],
        args=(yapre, u3, p, p, x, woa, wob, wo, vecs))


def _mlp_up(x1b, wup, tm=512, tn=2048):
    T = x1b.shape[0]
    dff = wup.shape[1]

    def body(x_ref, w_ref, r_ref, h_ref):
        r = jnp.maximum(jnp.dot(x_ref[...], w_ref[...], preferred_element_type=F32), 0.0)
        r_ref[...] = r.astype(BF)
        h_ref[...] = (r * r).astype(BF)

    out = pl.BlockSpec((tm, tn), lambda j, i: (i, j))
    return pl.pallas_call(
        body, name="mlp_up", grid=(dff // tn, T // tm),
        in_specs=[pl.BlockSpec((tm, D), lambda j, i: (i, 0)), pl.BlockSpec((D, tn), lambda j, i: (0, j))],
        out_specs=[out, out],
        out_shape=[jax.ShapeDtypeStruct((T, dff), BF), jax.ShapeDtypeStruct((T, dff), BF)],
        compiler_params=_params(2))(x1b, wup)


def _resident(shape):
    return pl.BlockSpec(shape, lambda *_: (0,) * len(shape), pipeline_mode=pl.Buffered(1))


def _mlp_down_loss(h, wdown, xhat1, target, vecs, tm=512):
    T, dff = h.shape

    def body(h_ref, w_ref, xhat1_ref, tgt_ref, vec_ref, dz2b_ref, st_ref):
        @pl.when(pl.program_id(0) == 0)
        def _():
            st_ref[...] = jnp.zeros_like(st_ref)

        ff = jnp.dot(h_ref[...], w_ref[...], preferred_element_type=F32)
        x1 = xhat1_ref[...] * vec_ref[3:4, :] + vec_ref[4:5, :]
        xhat2, rstd2 = _ln_fwd(ALPHA * x1 + ff)
        g2 = vec_ref[5:6, :]
        diff = xhat2 * g2 + vec_ref[6:7, :] - tgt_ref[...]
        dx2 = diff * (1.0 / D)
        st_ref[0:1, :] += _colsum(dx2 * xhat2)
        st_ref[1:2, :] += _colsum(dx2)
        st_ref[2:3, :] += _colsum(diff * diff)
        dz2b_ref[...] = _ln_bwd(dx2, xhat2, rstd2, g2).astype(BF)

    tok = pl.BlockSpec((tm, D), lambda i: (i, 0))
    vec = pl.BlockSpec((8, D), lambda i: (0, 0))
    return pl.pallas_call(
        body, name="mlp_down_loss", grid=(T // tm,),
        in_specs=[pl.BlockSpec((tm, dff), lambda i: (i, 0)), _resident((dff, D)), tok, tok, vec],
        out_specs=[tok, vec],
        out_shape=[jax.ShapeDtypeStruct((T, D), BF), jax.ShapeDtypeStruct((8, D), F32)],
        compiler_params=_params(1))(h, wdown, xhat1, target, vecs)


def _mlp_down_bwd(dz2b, wdown, r, tm=512, tk=2048):
    T, dff = r.shape

    def body(dz_ref, w_ref, r_ref, o_ref):
        dh = lax.dot_general(dz_ref[...], w_ref[...], NT_DIMS, preferred_element_type=F32)
        o_ref[...] = (dh * (2.0 * r_ref[...].astype(F32))).astype(BF)

    blk = pl.BlockSpec((tm, tk), lambda j, i: (i, j))
    return pl.pallas_call(
        body, name="mlp_down_bwd", grid=(dff // tk, T // tm),
        in_specs=[pl.BlockSpec((tm, D), lambda j, i: (i, 0)), pl.BlockSpec((tk, D), lambda j, i: (j, 0)), blk],
        out_specs=blk,
        out_shape=jax.ShapeDtypeStruct((T, dff), BF),
        compiler_params=_params(2))(dz2b, wdown, r)


def _tn_matmul(a, b, nblk, a_bw, b_bw, a_blocked, b_blocked, name, tt=2048):
    T = a.shape[0]
    nt = T // tt

    def body(a_ref, b_ref, o32_ref, o16_ref):
        t = pl.program_id(1)

        @pl.when(t == 0)
        def _():
            o32_ref[...] = jnp.zeros_like(o32_ref)

        o32_ref[0] += lax.dot_general(a_ref[...], b_ref[...], TN_DIMS, preferred_element_type=F32)

        @pl.when(t == nt - 1)
        def _():
            o16_ref[...] = o32_ref[...].astype(BF)

    a_spec = pl.BlockSpec((tt, a_bw), (lambda j, t: (t, j)) if a_blocked else (lambda j, t: (t, 0)))
    b_spec = pl.BlockSpec((tt, b_bw), (lambda j, t: (t, j)) if b_blocked else (lambda j, t: (t, 0)))
    out = pl.BlockSpec((1, a_bw, b_bw), lambda j, t: (j, 0, 0))
    return pl.pallas_call(
        body, name=name, grid=(nblk, nt),
        in_specs=[a_spec, b_spec], out_specs=[out, out],
        out_shape=[jax.ShapeDtypeStruct((nblk, a_bw, b_bw), F32), jax.ShapeDtypeStruct((nblk, a_bw, b_bw), BF)],
        compiler_params=_params(2))(a, b)


def _grad_w_in_send(xb, dp, bw, tt=2048):
    T = xb.shape[0]
    nt = T // tt
    flip = lambda v, bit: 1 - v if bit else v

    def peer(k):
        return (flip(lax.axis_index("x"), k & 4), flip(lax.axis_index("y"), k & 2), flip(lax.axis_index("c"), k & 1))

    block_of = lambda dev: 4 * dev[0] + 2 * dev[1] + dev[2]
    order = jnp.stack([block_of(peer(NDEV - 1 - q)) for q in range(NDEV)]).astype(jnp.int32)

    def body(order_ref, a_ref, b_ref, o32_ref, o16_ref, land_ref, stage, send_sems, recv_sems):
        q, t = pl.program_id(0), pl.program_id(1)

        def copy(qq):
            return pltpu.make_async_remote_copy(
                src_ref=stage.at[qq], dst_ref=land_ref.at[qq], send_sem=send_sems.at[qq], recv_sem=recv_sems.at[qq],
                device_id=peer(NDEV - 1 - qq), device_id_type=MESH)

        @pl.when(t == 0)
        def _():
            o32_ref[...] = jnp.zeros_like(o32_ref)

        o32_ref[0] += lax.dot_general(a_ref[...], b_ref[...], TN_DIMS, preferred_element_type=F32)

        @pl.when(t == nt - 1)
        def _():
            o16_ref[...] = o32_ref[...].astype(BF)

        def send(qq):
            stage[qq] = o32_ref[0].astype(BF)
            copy(qq).start()

        for qq in range(W_IN_EARLY):
            pl.when(jnp.logical_and(q == qq, t == nt - 1))(functools.partial(send, qq))

        @pl.when(jnp.logical_and(q == NDEV - 1, t == nt - 1))
        def _():
            for qq in range(W_IN_EARLY):
                copy(qq).wait()

    blk = pl.BlockSpec((1, D, bw), lambda q, t, o: (o[q], 0, 0))
    grid_spec = pltpu.PrefetchScalarGridSpec(
        num_scalar_prefetch=1, grid=(NDEV, nt),
        in_specs=[pl.BlockSpec((tt, D), lambda q, t, o: (t, 0)), pl.BlockSpec((tt, bw), lambda q, t, o: (t, o[q]))],
        out_specs=[blk, blk, ANY_SPEC],
        scratch_shapes=[pltpu.VMEM((W_IN_EARLY, D, bw), BF), pltpu.SemaphoreType.DMA((W_IN_EARLY,)),
                        pltpu.SemaphoreType.DMA((W_IN_EARLY,))])
    return pl.pallas_call(
        body, name="grad_w_in", grid_spec=grid_spec,
        out_shape=[jax.ShapeDtypeStruct((NDEV, D, bw), F32), jax.ShapeDtypeStruct((NDEV, D, bw), BF),
                   jax.ShapeDtypeStruct((W_IN_EARLY, D, bw), BF)],
        compiler_params=_params(2))(order, xb, dp)


def _tn_matmul_tiles_outer(a, b, nblk, a_bw, b_bw, a_blocked, name, tt=4096):
    T = a.shape[0]
    nt = T // tt

    def body(a_ref, b_ref, o32_ref, o16_ref, acc):
        t, j = pl.program_id(0), pl.program_id(1)
        prod = lax.dot_general(a_ref[...], b_ref[...], TN_DIMS, preferred_element_type=F32)

        @pl.when(t == 0)
        def _():
            acc[j] = prod

        @pl.when(t > 0)
        def _():
            acc[j] += prod

        @pl.when(t == nt - 1)
        def _():
            o32_ref[0] = acc[j]
            o16_ref[0] = acc[j].astype(BF)

    a_spec = pl.BlockSpec((tt, a_bw), (lambda t, j: (t, j)) if a_blocked else (lambda t, j: (t, 0)))
    b_spec = pl.BlockSpec((tt, b_bw), (lambda t, j: (t, 0)) if a_blocked else (lambda t, j: (t, j)))
    out = pl.BlockSpec((1, a_bw, b_bw), lambda t, j: (jnp.where(t == nt - 1, j, 0), 0, 0))
    return pl.pallas_call(
        body, name=name, grid=(nt, nblk),
        in_specs=[a_spec, b_spec], out_specs=[out, out],
        out_shape=[jax.ShapeDtypeStruct((nblk, a_bw, b_bw), F32), jax.ShapeDtypeStruct((nblk, a_bw, b_bw), BF)],
        scratch_shapes=[pltpu.VMEM((nblk, a_bw, b_bw), F32)],
        compiler_params=_params(2))(a, b)


def _mlp_up_bwd(dhpre, wup_t, dz2, xhat1, rstd1, vecs, push, tm=512):
    T, dff = dhpre.shape

    def body(dh_ref, w_ref, dz2_ref, xhat_ref, rstd_ref, vec_ref, dz1b_ref, st_ref):
        @pl.when(pl.program_id(0) == 0)
        def _():
            st_ref[...] = jnp.zeros_like(st_ref)

        dx1 = jnp.dot(dh_ref[...], w_ref[...], preferred_element_type=F32) + ALPHA * dz2_ref[...].astype(F32)
        xhat = xhat_ref[...]
        st_ref[0:1, :] += _colsum(dx1 * xhat)
        st_ref[1:2, :] += _colsum(dx1)
        dz1b_ref[...] = _ln_bwd(dx1, xhat, rstd_ref[...], vec_ref[3:4, :]).astype(BF)

    tok = pl.BlockSpec((tm, D), lambda i: (i, 0))
    vec = pl.BlockSpec((8, D), lambda i: (0, 0))
    return _pallas(
        body, name="mlp_up_bwd", grid=(T // tm,),
        in_specs=[pl.BlockSpec((tm, dff), lambda i: (i, 0)), _resident((dff, D)),
                  tok, tok, pl.BlockSpec((tm, 1), lambda i: (i, 0)), vec],
        out_specs=[tok, vec],
        out_shape=[jax.ShapeDtypeStruct((T, D), BF), jax.ShapeDtypeStruct((8, D), F32)],
        args=(dhpre, wup_t, dz2, xhat1, rstd1, vecs), push=push)


def _merge_bwd(dz1, p, ya, yb, conva, xhatb, rstdb, woa, wob, wo, vecs, push, tm=256):
    T = dz1.shape[0]

    def body(dz1_ref, ga_ref, gb_ref, ba_ref, ya_ref, yb_ref, conva_ref, xhat_ref, rstd_ref,
             woa_ref, wob_ref, wo_ref, vec_ref,
             dya_ref, dyb_ref, dg_ref, dba_ref, dconva_ref, du1_ref, st_ref):
        @pl.when(pl.program_id(0) == 0)
        def _():
            st_ref[...] = jnp.zeros_like(st_ref)

        dmerged = lax.dot_general(dz1_ref[...], wo_ref[...], NT_DIMS, preferred_element_type=F32)
        sa, sb = _sigmoid(ga_ref[...].astype(F32)), _sigmoid(gb_ref[...].astype(F32))
        dya = (dmerged * sa).astype(BF)
        dyb = (dmerged * sb).astype(BF)
        dya_ref[...] = dya
        dyb_ref[...] = dyb
        dg_ref[:, 0:D] = (dmerged * ya_ref[...].astype(F32) * (sa * (1.0 - sa))).astype(BF)
        dg_ref[:, D:2 * D] = (dmerged * yb_ref[...].astype(F32) * (sb * (1.0 - sb))).astype(BF)

        dyapre = lax.dot_general(dya, woa_ref[...], NT_DIMS, preferred_element_type=F32)
        dba_ref[...] = (dyapre * conva_ref[...]).astype(BF)
        dconva_ref[...] = dyapre * ba_ref[...].astype(F32)

        du3 = lax.dot_general(dyb, wob_ref[...], NT_DIMS, preferred_element_type=F32)
        xhat = xhat_ref[...]
        gamma = vec_ref[1:2, :]
        u2 = xhat * gamma + vec_ref[2:3, :]
        s = _sigmoid(u2)
        du2 = du3 * (s * (1.0 + u2 * (1.0 - s)))
        st_ref[0:1, :] += _colsum(du2 * xhat)
        st_ref[1:2, :] += _colsum(du2)
        du1 = _ln_bwd(du2, xhat, rstd_ref[...], gamma)
        st_ref[2:3, :] += _colsum(du1)
        du1_ref[...] = du1

    tok = pl.BlockSpec((tm, D), lambda i: (i, 0))
    wfull = pl.BlockSpec((D, D), lambda i: (0, 0))
    vec = pl.BlockSpec((8, D), lambda i: (0, 0))
    return _pallas(
        body, name="merge_bwd", grid=(T // tm,),
        in_specs=[tok, _seg(tm, 5), _seg(tm, 6), _seg(tm, 0), tok, tok, tok, tok, pl.BlockSpec((tm, 1), lambda i: (i, 0)),
                  wfull, wfull, wfull, vec],
        out_specs=[tok, tok, pl.BlockSpec((tm, 2 * D), lambda i: (i, 0)), tok, tok, tok, vec],
        out_shape=[jax.ShapeDtypeStruct((T, D), BF), jax.ShapeDtypeStruct((T, D), BF),
                   jax.ShapeDtypeStruct((T, 2 * D), BF), jax.ShapeDtypeStruct((T, D), BF),
                   jax.ShapeDtypeStruct((T, D), F32), jax.ShapeDtypeStruct((T, D), F32),
                   jax.ShapeDtypeStruct((8, D), F32)],
        args=(dz1, p, p, p, ya, yb, conva, xhatb, rstdb, woa, wob, wo, vecs), push=push)


def _rows8(v):
    out = v[0:8]
    for q in range(1, RC // 8):
        out = out + v[8 * q:8 * q + 8]
    return out


def _conv_bwd(dconva, du1, p, dba, dg, wa, wb, push, tt=256):
    T = p.shape[0]
    nsteps = T // tt

    def body(dca_ref, dca_n, du1_ref, du1_n, ca, va, vb, gb, dba_ref, dg_ref, wa_ref, wb_ref,
             dp_ref, gw_ref, cabuf, u0buf, dcabuf, du1buf, dcain, du0, gwa, gwb, shd):
        i = pl.program_id(0)
        first, last = i == 0, i == nsteps - 1

        @pl.when(first)
        def _():
            gwa[...] = jnp.zeros_like(gwa)
            gwb[...] = jnp.zeros_like(gwb)

        f = lambda ref: ref[...].astype(F32)
        cav, vav, vbv = f(ca), f(va), f(vb)
        cabuf[...] = cav * vav
        sg = _sigmoid(f(gb))
        u0buf[...] = vbv * sg
        dcabuf[0:tt, :] = dca_ref[...]
        dcabuf[tt:tt + HN, :] = jnp.where(last, 0.0, dca_n[...])
        du1buf[0:tt, :] = du1_ref[...]
        du1buf[tt:tt + HB, :] = jnp.where(last, 0.0, du1_n[...])

        def lane_body(cidx, carry):
            ls = pl.ds(pl.multiple_of(cidx * LANES, LANES), LANES)
            _shifted_copies(shd, du1buf, ls, tt + HB - 8)
            for r in range(tt // RC):
                rows = pl.ds(r * RC, RC)
                cin = cabuf[rows, ls]
                acc = jnp.zeros((RC, LANES), F32)
                for k in range(KA):
                    dout = dcabuf[pl.ds(r * RC + KA - 1 - k, RC), ls]
                    acc = acc + wa_ref[k:k + 1, ls] * dout
                    gwa[8 * k:8 * k + 8, ls] += _rows8(cin * dout)
                dcain[rows, ls] = acc
                uin = u0buf[rows, ls]
                acc = jnp.zeros((RC, LANES), F32)
                for k in range(KB):
                    dout = _tap(shd, du1buf, ls, r * RC + KB - 1 - k, RC)
                    acc = acc + wb_ref[k:k + 1, ls] * dout
                    gwb[8 * k:8 * k + 8, ls] += _rows8(uin * dout)
                du0[rows, ls] = acc
            return carry

        lax.fori_loop(0, D // LANES, lane_body, 0)
        dca_in = dcain[...]
        du0v = du0[...]
        dp_ref[:, 0:D] = dba_ref[...]
        dp_ref[:, D:2 * D] = (dca_in * vav).astype(BF)
        dp_ref[:, 2 * D:3 * D] = (dca_in * cav).astype(BF)
        dp_ref[:, 3 * D:4 * D] = (du0v * sg).astype(BF)
        dp_ref[:, 4 * D:5 * D] = (du0v * vbv * (sg * (1.0 - sg))).astype(BF)
        dp_ref[:, 5 * D:7 * D] = dg_ref[...]

        @pl.when(last)
        def _():
            gw_ref[...] = jnp.zeros_like(gw_ref)
            for k in range(KA):
                gw_ref[k:k + 1, :] = _colsum(gwa[8 * k:8 * k + 8, :])
            for k in range(KB):
                gw_ref[8 + k:9 + k, :] = _colsum(gwb[8 * k:8 * k + 8, :])

    full = lambda r: pl.BlockSpec((r, D), lambda i: (0, 0))
    tok = pl.BlockSpec((tt, D), lambda i: (i, 0))
    nxt = lambda h: pl.BlockSpec((h, D), lambda i: (jnp.minimum((i + 1) * (tt // h), T // h - 1), 0))
    return _pallas(
        body, name="conv_bwd", grid=(nsteps,),
        in_specs=[tok, nxt(HN), tok, nxt(HB),
                  _seg(tt, 1), _seg(tt, 2), _seg(tt, 3), _seg(tt, 4),
                  tok, pl.BlockSpec((tt, 2 * D), lambda i: (i, 0)), full(8), full(32)],
        out_specs=[pl.BlockSpec((tt, 7 * D), lambda i: (i, 0)), full(40)],
        out_shape=[jax.ShapeDtypeStruct((T, 7 * D), BF), jax.ShapeDtypeStruct((40, D), F32)],
        scratch_shapes=[pltpu.VMEM((tt, D), F32), pltpu.VMEM((tt, D), F32),
                        pltpu.VMEM((tt + HN, D), F32), pltpu.VMEM((tt + HB, D), F32),
                        pltpu.VMEM((tt, D), F32), pltpu.VMEM((tt, D), F32),
                        pltpu.VMEM((8 * KA, D), F32), pltpu.VMEM((8 * KB, D), F32),
                        pltpu.VMEM((8, HB + tt, LANES), F32)],
        args=(dconva, dconva, du1, du1, p, p, p, p, dba, dg, wa, wb), push=push)


def _inproj_bwd(dp, win_t, dz1, push, tm=512):
    T, cols = dp.shape

    def body(dp_ref, w_ref, dz1_ref, o_ref):
        o_ref[...] = ALPHA * dz1_ref[...].astype(F32) + jnp.dot(dp_ref[...], w_ref[...], preferred_element_type=F32)

    tok = pl.BlockSpec((tm, D), lambda i: (i, 0))
    return _pallas(
        body, name="inproj_bwd", grid=(T // tm,),
        in_specs=[pl.BlockSpec((tm, cols), lambda i: (i, 0)), _resident((cols, D)), tok],
        out_specs=[tok],
        out_shape=[jax.ShapeDtypeStruct((T, D), F32)],
        args=(dp, win_t, dz1), push=push)


def _adam_math(w, m, v, g):
    nm = ADAM_B1 * m + (1.0 - ADAM_B1) * g
    nv = ADAM_B2 * v + (1.0 - ADAM_B2) * (g * g)
    m_hat = nm / (1.0 - ADAM_B1 ** ADAM_STEP)
    v_hat = nv / (1.0 - ADAM_B2 ** ADAM_STEP)
    return -ADAM_LR * (m_hat / (jnp.sqrt(v_hat) + ADAM_EPS) + ADAM_WD * w), nm, nv


def _adamw(w, m, v, g32, landings, me, name, rb):
    R, C = w.shape
    nland = len(landings)

    def body(me_ref, w_ref, m_ref, v_ref, own_ref, *refs):
        g_ref, d_ref, nm_ref, nv_ref = refs[nland:]
        g = own_ref[0]
        for l_ref in refs[:nland]:
            for k in range(l_ref.shape[0]):
                g = g + l_ref[k].astype(F32)
        g_ref[...] = g
        d_ref[...], nm_ref[...], nv_ref[...] = _adam_math(w_ref[...], m_ref[...], v_ref[...], g)

    blk = pl.BlockSpec((rb, C), lambda i, me_ref: (i, 0))
    grid_spec = pltpu.PrefetchScalarGridSpec(
        num_scalar_prefetch=1, grid=(R // rb,),
        in_specs=[blk, blk, blk, pl.BlockSpec((1, rb, C), lambda i, me_ref: (me_ref[0], i, 0))]
        + [pl.BlockSpec((l.shape[0], rb, C), lambda i, me_ref: (0, i, 0)) for l in landings],
        out_specs=[blk] * 4)
    return pl.pallas_call(
        body, name=name, grid_spec=grid_spec, out_shape=[jax.ShapeDtypeStruct((R, C), F32)] * 4,
        compiler_params=_params(1))(me, w, m, v, g32, *landings)


def _adamw_small(small_g, vec_w, vec_m, vec_v, conv_w, conv_m, conv_v):
    nv_ = len(vec_w)
    conv_rows = [(8, KA), (16, KB)]

    def body(*refs):
        g_ref = refs[0]
        w_refs, m_refs, v_refs = refs[1:10], refs[10:19], refs[19:28]
        out_refs, gsum = refs[28:64], refs[64]
        acc = g_ref[0]
        for j in range(1, NDEV):
            acc = acc + g_ref[j]
        gsum[...] = acc
        me = 4 * lax.axis_index("x") + 2 * lax.axis_index("y") + lax.axis_index("c")
        cols = pl.ds(pl.multiple_of(me * LANES, LANES), LANES)
        for i in range(nv_ + 2):
            if i < nv_:
                g = gsum[i:i + 1, :]
            else:
                r0, k = conv_rows[i - nv_]
                g = gsum[r0:r0 + k, cols]
            o = out_refs[4 * i:4 * i + 4]
            o[0][...] = g
            o[1][...], o[2][...], o[3][...] = _adam_math(w_refs[i][...], m_refs[i][...], v_refs[i][...], g)

    ws, ms, vs = list(vec_w) + list(conv_w), list(vec_m) + list(conv_m), list(vec_v) + list(conv_v)
    out_shape = [jax.ShapeDtypeStruct(w.shape, F32) for w in ws for _ in range(4)]
    return pl.pallas_call(
        body, name="adamw_small", out_shape=out_shape,
        scratch_shapes=[pltpu.VMEM(small_g.shape[1:], F32)])(small_g, *ws, *ms, *vs)


def _pad_rows(a, rows):
    return jnp.pad(a, ((0, rows - a.shape[0]), (0, 0)))


def _local_step(p, xb, mixed, post, x, target, win_t, wup, wup_t, wdown, woa, wob, wo, wa, wb, vecs):
    yapre, conva, xhatb, rstdb, u3 = mixed
    ya, yb, merged, xhat1, rstd1, x1b = post
    r, h = _mlp_up(x1b, wup)
    dz2b, st2 = _mlp_down_loss(h, wdown, xhat1, target, vecs)

    by_owner = lambda g16: g16.reshape(NDEV, D // NDEV, D)
    dhpre = _mlp_down_bwd(dz2b, wdown, r)
    g_wdown = _tn_matmul_tiles_outer(h, dz2b, NDEV, 512, D, True, "grad_w_down")
    (dz1b, st1), land_wdown = _mlp_up_bwd(dhpre, wup_t, dz2b, xhat1, rstd1, vecs, _push(exch=[g_wdown[1]]))
    g_wup = _tn_matmul_tiles_outer(x1b, dhpre, NDEV, D, 512, False, "grad_w_up")
    (dya, dyb, dg, dba, dconva, du1, stb), _ = _merge_bwd(
        dz1b, p, ya, yb, conva, xhatb, rstdb, woa, wob, wo, vecs, None)
    g_wo = _tn_matmul(merged, dz1b, 1, D, D, False, False, "grad_w_o")
    g_woa = _tn_matmul(yapre, dya, 1, D, D, False, False, "grad_w_out_a")
    g_wob = _tn_matmul(u3, dyb, 1, D, D, False, False, "grad_w_out_b")
    (dp, gw), land_conv = _conv_bwd(
        dconva, du1, p, dba, dg, wa, wb,
        _push(exch=[g_wup[1], by_owner(g_woa[1]), by_owner(g_wob[1]), by_owner(g_wo[1])]))
    g_win, g_win16, land_win = _grad_w_in_send(xb, dp, dp.shape[1] // NDEV)

    small = jnp.concatenate([stb[2:3], stb[0:2], st1[0:2], st2[0:3], gw], axis=0)
    late_ks = tuple(range(1, NDEV - W_IN_EARLY))
    (grad_x,), land_last = _inproj_bwd(dp, win_t, dz1b, _push(exch=[g_win16], gath=[small], ks=late_ks))
    grads = (g_win, g_wup[0], g_wdown[0], g_woa[0], g_wob[0], g_wo[0])
    return grad_x, grads, small, land_wdown + land_conv + [land_win] + land_last


def kernel(x, w_in, conv_a_w, w_out_a, conv_b_w, conv_b_bias, ln_b_gamma, ln_b_beta, w_out_b, w_o, ln1_gamma, ln1_beta, w_up, w_down, ln2_gamma, ln2_beta, loss_target, m_w_in, m_conv_a_w, m_w_out_a, m_conv_b_w, m_conv_b_bias, m_ln_b_gamma, m_ln_b_beta, m_w_out_b, m_w_o, m_ln1_gamma, m_ln1_beta, m_w_up, m_w_down, m_ln2_gamma, m_ln2_beta, v_w_in, v_conv_a_w, v_w_out_a, v_conv_b_w, v_conv_b_bias, v_ln_b_gamma, v_ln_b_beta, v_w_out_b, v_w_o, v_ln1_gamma, v_ln1_beta, v_w_up, v_w_down, v_ln2_gamma, v_ln2_beta):
    T = x.shape[1]
    me = 4 * lax.axis_index("x") + 2 * lax.axis_index("y") + lax.axis_index("c")

    conv_shard = jnp.concatenate([_pad_rows(conv_a_w, 8), _pad_rows(conv_b_w, 32)], axis=0)
    p, xb, win_g, conv_g = _inproj_gather(x[0], w_in.astype(BF), conv_shard)
    conv_full = jnp.transpose(conv_g, (1, 0, 2)).reshape(40, D)
    vecs = jnp.stack([conv_b_bias, ln_b_gamma, ln_b_beta, ln1_gamma, ln1_beta, ln2_gamma, ln2_beta,
                      jnp.zeros_like(ln2_beta)])
    whole = lambda g: jnp.transpose(g, (1, 0, 2)).reshape(D, -1)
    whole_t = lambda g: jnp.transpose(g, (0, 2, 1)).reshape(-1, D)
    mixed, (woa_g, wob_g, wo_g, wup_g, wdown_g) = _mixer_fwd(
        p, conv_full[0:8], conv_full[8:40], vecs,
        [w_out_a.astype(BF), w_out_b.astype(BF), w_o.astype(BF), w_up.astype(BF), w_down.astype(BF)])
    woa, wob, wo = woa_g.reshape(D, D), wob_g.reshape(D, D), wo_g.reshape(D, D)
    post, _ = _post_mixer(mixed[0], mixed[4], p, x[0], woa, wob, wo, vecs, [])

    grad_x, grads, small, landing = _local_step(
        p, xb, mixed, post, x[0], loss_target[0], whole_t(win_g), whole(wup_g), whole_t(wup_g),
        wdown_g.reshape(NDEV * 512, D),
        woa, wob, wo, conv_full[0:8], conv_full[8:40], vecs)
    g_win, g_wup, g_wdown, g_woa, g_wob, g_wo = grads
    l_wdown, l_wup, l_woa, l_wob, l_wo, l_win_early, l_win_late, small_g = landing

    loss = lax.psum(0.5 / D * jnp.sum(small[7]), ("x", "y", "c"))

    me1 = me.astype(jnp.int32).reshape(1)
    by_owner = lambda g32: g32.reshape(NDEV, D // NDEV, D)
    r_win = _adamw(w_in, m_w_in, v_w_in, g_win, [l_win_early, l_win_late], me1, "adamw_w_in", 256)
    r_wup = _adamw(w_up, m_w_up, v_w_up, g_wup, [l_wup], me1, "adamw_w_up", 256)
    r_wdown = _adamw(w_down, m_w_down, v_w_down, g_wdown, [l_wdown], me1, "adamw_w_down", 256)
    r_woa = _adamw(w_out_a, m_w_out_a, v_w_out_a, by_owner(g_woa), [l_woa], me1, "adamw_w_out_a", 128)
    r_wob = _adamw(w_out_b, m_w_out_b, v_w_out_b, by_owner(g_wob), [l_wob], me1, "adamw_w_out_b", 128)
    r_wo = _adamw(w_o, m_w_o, v_w_o, by_owner(g_wo), [l_wo], me1, "adamw_w_o", 128)

    row = lambda vec: vec.reshape(1, D)
    small_out = _adamw_small(
        small_g,
        [row(a) for a in (conv_b_bias, ln_b_gamma, ln_b_beta, ln1_gamma, ln1_beta, ln2_gamma, ln2_beta)],
        [row(a) for a in (m_conv_b_bias, m_ln_b_gamma, m_ln_b_beta, m_ln1_gamma, m_ln1_beta, m_ln2_gamma, m_ln2_beta)],
        [row(a) for a in (v_conv_b_bias, v_ln_b_gamma, v_ln_b_beta, v_ln1_gamma, v_ln1_beta, v_ln2_gamma, v_ln2_beta)],
        [conv_a_w, conv_b_w], [m_conv_a_w, m_conv_b_w], [v_conv_a_w, v_conv_b_w])
    r_vec = [[small_out[4 * i + q].reshape(D) for q in range(4)] for i in range(7)]
    r_conva, r_convb = small_out[28:32], small_out[32:36]

    per_weight = []
    for q in range(4):
        per_weight.append([
            r_win[q], r_conva[q], r_woa[q], r_convb[q],
            r_vec[0][q], r_vec[1][q], r_vec[2][q], r_wob[q], r_wo[q], r_vec[3][q], r_vec[4][q],
            r_wup[q], r_wdown[q], r_vec[5][q], r_vec[6][q]])
    return (loss, grad_x[None], *per_weight[0], *per_weight[1], *per_weight[2], *per_weight[3])
```

```python
import functools

import jax
import jax.numpy as jnp
from jax import lax
from jax.experimental import pallas as pl
from jax.experimental.pallas import tpu as pltpu

F32 = jnp.float32
BF = jnp.bfloat16
D = 1024
NDEV = 8
ALPHA = 2.0 ** 0.25
LN_EPS = 1e-5
KA, KB = 3, 31
HA, HB = 16, 32
HN = 8
RC = 64
W_IN_EARLY = 4
LANES = 128
VMEM_LIMIT = 56 * 1024 * 1024
MESH = pl.DeviceIdType.MESH
ADAM_LR, ADAM_B1, ADAM_B2, ADAM_EPS, ADAM_WD, ADAM_STEP = 0.001, 0.9, 0.999, 1e-08, 0.01, 10

ANY_SPEC = pl.BlockSpec(memory_space=pl.ANY)
NT_DIMS = (((1,), (1,)), ((), ()))
TN_DIMS = (((0,), (0,)), ((), ()))


def _params(n_axes):
    return pltpu.CompilerParams(dimension_semantics=("arbitrary",) * n_axes, vmem_limit_bytes=VMEM_LIMIT)


def _sigmoid(v):
    return 0.5 * jnp.tanh(0.5 * v) + 0.5


def _ln_fwd(z):
    mu = jnp.mean(z, axis=-1, keepdims=True)
    zc = z - mu
    var = jnp.mean(zc * zc, axis=-1, keepdims=True)
    rstd = lax.rsqrt(var + LN_EPS)
    return zc * rstd, rstd


def _ln_bwd(dy, xhat, rstd, gamma):
    dxhat = dy * gamma
    m1 = jnp.mean(dxhat, axis=-1, keepdims=True)
    m2 = jnp.mean(dxhat * xhat, axis=-1, keepdims=True)
    return rstd * (dxhat - m1 - xhat * m2)


def _colsum(v):
    return jnp.sum(v, axis=0, keepdims=True)


class _TwoLevelGather:
    def __init__(self, ins, outs, send_sems, recv_sems, local_sems):
        self.ins, self.outs = ins, outs
        self.send_sems, self.recv_sems, self.local_sems = send_sems, recv_sems, local_sems
        x, y, c = lax.axis_index("x"), lax.axis_index("y"), lax.axis_index("c")
        self.me, self.sibling, self.c = (x, y, c), (x, y, 1 - c), c
        self.chips = [(1 - x, y), (x, 1 - y), (1 - x, 1 - y)]
        self.n = len(ins)

    @staticmethod
    def out_shape(arrs):
        return [jax.ShapeDtypeStruct((NDEV,) + a.shape, a.dtype) for a in arrs]

    @staticmethod
    def scratch(n):
        return [pltpu.SemaphoreType.DMA((n, 7)), pltpu.SemaphoreType.DMA((n, 7)), pltpu.SemaphoreType.DMA((n,))]

    def _copy(self, a, k, block, to, src=None):
        px, py, pc = block
        rows = self.outs[a].at[4 * px + 2 * py + pc]
        return pltpu.make_async_remote_copy(
            src_ref=rows if src is None else src, dst_ref=rows,
            send_sem=self.send_sems.at[a, k], recv_sem=self.recv_sems.at[a, k],
            device_id=to, device_id_type=MESH)

    def _mine(self, a):
        x, y, c = self.me
        return pltpu.make_async_copy(self.ins[a], self.outs[a].at[4 * x + 2 * y + c], self.local_sems.at[a])

    def _first(self, a):
        cps = [self._copy(a, 0, self.me, self.sibling, src=self.ins[a])]
        return cps + [self._copy(a, 1 + j, self.me, (*chip, self.c), src=self.ins[a]) for j, chip in enumerate(self.chips)]

    def _passed(self, a, j):
        return self._copy(a, 4 + j, (*self.chips[j], self.c), self.sibling)

    def start(self, diagonal=True):
        for a in range(self.n):
            self._mine(a).start()
        for a in range(self.n):
            for cp in self._first(a)[:4 if diagonal else 3]:
                cp.start()

    def start_diagonal(self):
        for a in range(self.n):
            self._first(a)[3].start()

    def wait_ici(self, j):
        for a in range(self.n):
            self._copy(a, 1 + j, (*self.chips[j], self.c), self.me).wait_recv()

    def pass_on(self, j):
        for a in range(self.n):
            self._passed(a, j).start()

    def wait_sibling(self):
        for a in range(self.n):
            self._copy(a, 0, self.sibling, self.me).wait_recv()

    def wait_passed(self, j):
        for a in range(self.n):
            self._copy(a, 4 + j, (*self.chips[j], 1 - self.c), self.me).wait_recv()

    def drain(self):
        for a in range(self.n):
            for cp in self._first(a) + [self._passed(a, j) for j in range(3)]:
                cp.wait_send()
            self._mine(a).wait()

    def forward(self):
        for j in range(3):
            self.wait_ici(j)
            self.pass_on(j)

    def finish(self):
        self.wait_sibling()
        for j in range(3):
            self.wait_passed(j)
        self.drain()


class _Push:
    def __init__(self, exch=(), gath=(), ks=tuple(range(1, NDEV))):
        self.exch, self.gath, self.ks = list(exch), list(gath), tuple(ks)
        self.n = len(self.exch) + len(self.gath)

    def operands(self):
        return self.exch + self.gath

    def out_shape(self):
        return ([jax.ShapeDtypeStruct((len(self.ks),) + a.shape[1:], a.dtype) for a in self.exch]
                + [jax.ShapeDtypeStruct((NDEV,) + a.shape, a.dtype) for a in self.gath])

    def scratch(self):
        return [pltpu.SemaphoreType.DMA((self.n, 7)), pltpu.SemaphoreType.DMA((self.n, 7)),
                pltpu.SemaphoreType.DMA((max(len(self.gath), 1),))]

    def copies(self, ins, outs, send_sems, recv_sems, local_sems):
        x, y, c = lax.axis_index("x"), lax.axis_index("y"), lax.axis_index("c")
        me = 4 * x + 2 * y + c
        ne = len(self.exch)
        remote = []
        for k in range(1, NDEV):
            px = 1 - x if k & 4 else x
            py = 1 - y if k & 2 else y
            pc = 1 - c if k & 1 else c
            for a in range(self.n):
                if a < ne and k not in self.ks:
                    continue
                src = ins[a].at[4 * px + 2 * py + pc] if a < ne else ins[a]
                dst = outs[a].at[self.ks.index(k)] if a < ne else outs[a].at[me]
                remote.append(pltpu.make_async_remote_copy(
                    src_ref=src, dst_ref=dst, send_sem=send_sems.at[a, k - 1], recv_sem=recv_sems.at[a, k - 1],
                    device_id=(px, py, pc), device_id_type=MESH))
        local = [pltpu.make_async_copy(ins[a], outs[a].at[me], local_sems.at[a - ne]) for a in range(ne, self.n)]
        return remote, local


def _push(exch=(), gath=(), ks=tuple(range(1, NDEV))):
    return _Push(exch, gath, ks)


def _pallas(body, *, name, grid, in_specs, out_specs, out_shape, args, scratch_shapes=(), push=None):
    ni, no, ns = len(in_specs), len(out_specs), len(scratch_shapes)
    if push is None:
        outs = pl.pallas_call(
            body, name=name, grid=grid, in_specs=in_specs, out_specs=out_specs, out_shape=out_shape,
            scratch_shapes=list(scratch_shapes), compiler_params=_params(len(grid)))(*args)
        return list(outs), []
    npush = push.n

    def wrapped(*refs):
        ins, pins = refs[:ni], refs[ni:ni + npush]
        outs, pouts = refs[ni + npush:ni + npush + no], refs[ni + npush + no:ni + 2 * npush + no]
        scr, sems = refs[ni + 2 * npush + no:ni + 2 * npush + no + ns], refs[ni + 2 * npush + no + ns:]
        first = functools.reduce(jnp.logical_and, [pl.program_id(d) == 0 for d in range(len(grid))])
        last = functools.reduce(jnp.logical_and, [pl.program_id(d) == grid[d] - 1 for d in range(len(grid))])
        remote, local = push.copies(pins, pouts, *sems)

        @pl.when(first)
        def _():
            for cp in local + remote:
                cp.start()

        body(*ins, *outs, *scr)

        @pl.when(last)
        def _():
            for cp in remote + local:
                cp.wait()

    outs = pl.pallas_call(
        wrapped, name=name, grid=grid,
        in_specs=list(in_specs) + [ANY_SPEC] * npush, out_specs=list(out_specs) + [ANY_SPEC] * npush,
        out_shape=list(out_shape) + push.out_shape(), scratch_shapes=list(scratch_shapes) + push.scratch(),
        compiler_params=_params(len(grid)))(*args, *push.operands())
    return list(outs[:no]), list(outs[no:])


def _with_gather(body, late, *, name, nsteps, in_specs, out_specs, out_shape, args, scratch_shapes=()):
    ni, no, ns, n = len(in_specs), len(out_specs), len(scratch_shapes), len(late)
    pass_step = (7 * nsteps) // 8
    if not late:
        outs = pl.pallas_call(
            body, name=name, grid=(nsteps,), in_specs=in_specs, out_specs=out_specs, out_shape=out_shape,
            scratch_shapes=list(scratch_shapes), compiler_params=_params(1))(*args)
        return list(outs), []

    def wrapped(*refs):
        ins, outs = refs[:ni], refs[ni + n:ni + n + no]
        scr = refs[ni + 2 * n + no:ni + 2 * n + no + ns]
        gather = _TwoLevelGather(refs[ni:ni + n], refs[ni + n + no:ni + 2 * n + no], *refs[ni + 2 * n + no + ns:])
        step = pl.program_id(0)
        pl.when(step == 0)(gather.start)
        pl.when(step == pass_step)(gather.forward)
        body(*ins, *outs, *scr)
        pl.when(step == nsteps - 1)(gather.finish)

    outs = pl.pallas_call(
        wrapped, name=name, grid=(nsteps,),
        in_specs=list(in_specs) + [ANY_SPEC] * n, out_specs=list(out_specs) + [ANY_SPEC] * n,
        out_shape=list(out_shape) + _TwoLevelGather.out_shape(late),
        scratch_shapes=list(scratch_shapes) + _TwoLevelGather.scratch(n),
        compiler_params=_params(1))(*args, *late)
    return list(outs[:no]), list(outs[no:])


def _inproj_gather(x, w_shard, conv_shard, tm=2048):
    T = x.shape[0]
    ni = T // tm
    bw = w_shard.shape[1]
    cx, cy, cc = lax.axis_index("x"), lax.axis_index("y"), lax.axis_index("c")
    blk = lambda px, py, pc: 4 * px + 2 * py + pc
    order = [blk(cx, cy, cc), blk(cx, cy, 1 - cc)]
    for chip in [(1 - cx, cy), (cx, 1 - cy), (1 - cx, 1 - cy)]:
        order += [blk(*chip, cc), blk(*chip, 1 - cc)]
    order = jnp.stack(order).astype(jnp.int32)

    def body(order_ref, x_ref, w_ref, conv_ref, p_ref, xb_ref, wing_ref, convg_ref, xbs, wbuf, wsem, *sems):
        gather = _TwoLevelGather([w_ref, conv_ref], [wing_ref, convg_ref], *sems)
        j, i = pl.program_id(0), pl.program_id(1)

        def load(src):
            cp = pltpu.make_async_copy(src, wbuf, wsem)
            cp.start()
            cp.wait()

        def arrival(jj):
            if jj == 0:
                gather.start(diagonal=False)
                load(w_ref)
                return
            if jj == 1:
                gather.wait_sibling()
            elif jj % 2 == 0:
                if jj == 2:
                    gather.start_diagonal()
                gather.wait_ici(jj // 2 - 1)
                gather.pass_on(jj // 2 - 1)
            else:
                gather.wait_passed(jj // 2 - 1)
            load(wing_ref.at[order_ref[jj]])

        for jj in range(NDEV):
            pl.when(jnp.logical_and(j == jj, i == 0))(functools.partial(arrival, jj))

        @pl.when(j == 0)
        def _():
            xb = x_ref[...].astype(BF)
            xbs[i] = xb
            xb_ref[...] = xb

        p_ref[...] = jnp.dot(xbs[i], wbuf[...], preferred_element_type=F32).astype(BF)

        @pl.when(jnp.logical_and(j == NDEV - 1, i == ni - 1))
        def _():
            gather.drain()

    rows_once = lambda j, i, o: (jnp.where(j == 0, i, ni - 1), 0)
    grid_spec = pltpu.PrefetchScalarGridSpec(
        num_scalar_prefetch=1, grid=(NDEV, ni),
        in_specs=[pl.BlockSpec((tm, D), rows_once), ANY_SPEC, ANY_SPEC],
        out_specs=[pl.BlockSpec((tm, bw), lambda j, i, o: (i, o[j])), pl.BlockSpec((tm, D), rows_once),
                   ANY_SPEC, ANY_SPEC],
        scratch_shapes=[pltpu.VMEM((ni, tm, D), BF), pltpu.VMEM((D, bw), BF), pltpu.SemaphoreType.DMA(())]
        + _TwoLevelGather.scratch(2))
    p, xb, win_g, conv_g = pl.pallas_call(
        body, name="inproj_gather", grid_spec=grid_spec,
        out_shape=[jax.ShapeDtypeStruct((T, NDEV * bw), BF), jax.ShapeDtypeStruct((T, D), BF)]
        + _TwoLevelGather.out_shape([w_shard, conv_shard]),
        compiler_params=_params(2))(order, x, w_shard, conv_shard)
    return p, xb, win_g, conv_g


def _mixer_fwd(p, wa, wb, vecs, late, tt=256):
    T = p.shape[0]
    nt = T // tt

    def body(ba, ca, va, vb, gb, ca_p, va_p, vb_p, gb_p, wa_ref, wb_ref, vec_ref,
             yapre_ref, conva_ref, xhat_ref, rstd_ref, u3_ref, cabuf, u0buf, u1buf, shu):
        first = pl.program_id(0) == 0
        f = lambda ref: ref[...].astype(F32)
        cabuf[0:HA, :] = jnp.where(first, 0.0, f(ca_p) * f(va_p))
        cabuf[HA:HA + tt, :] = f(ca) * f(va)
        u0buf[0:HB, :] = jnp.where(first, 0.0, f(vb_p) * _sigmoid(f(gb_p)))
        u0buf[HB:HB + tt, :] = f(vb) * _sigmoid(f(gb))

        def lane_body(cidx, carry):
            ls = pl.ds(pl.multiple_of(cidx * LANES, LANES), LANES)
            _shifted_copies(shu, u0buf, ls, tt + HB - 8)
            for r in range(tt // RC):
                acc = jnp.zeros((RC, LANES), F32)
                for k in range(KA):
                    acc = acc + wa_ref[k:k + 1, ls] * cabuf[pl.ds(HA - (KA - 1) + k + r * RC, RC), ls]
                conva_ref[pl.ds(r * RC, RC), ls] = acc
                acc = jnp.zeros((RC, LANES), F32)
                for k in range(KB):
                    acc = acc + wb_ref[k:k + 1, ls] * _tap(shu, u0buf, ls, HB - (KB - 1) + k + r * RC, RC)
                u1buf[pl.ds(r * RC, RC), ls] = acc
            return carry

        lax.fori_loop(0, D // LANES, lane_body, 0)
        yapre_ref[...] = (f(ba) * conva_ref[...]).astype(BF)
        xhat, rstd = _ln_fwd(u1buf[...] + vec_ref[0:1, :])
        xhat_ref[...] = xhat
        rstd_ref[...] = rstd
        u2 = xhat * vec_ref[1:2, :] + vec_ref[2:3, :]
        u3_ref[...] = (u2 * _sigmoid(u2)).astype(BF)

    full = lambda r: pl.BlockSpec((r, D), lambda i: (0, 0))
    tok = pl.BlockSpec((tt, D), lambda i: (i, 0))
    return _with_gather(
        body, late, name="mixer_fwd", nsteps=nt,
        in_specs=[_seg(tt, 0), _seg(tt, 1), _seg(tt, 2), _seg(tt, 3), _seg(tt, 4),
                  _prev(tt, HA, 1), _prev(tt, HA, 2), _prev(tt, HB, 3), _prev(tt, HB, 4),
                  full(8), full(32), full(8)],
        out_specs=[tok, tok, tok, pl.BlockSpec((tt, 1), lambda i: (i, 0)), tok],
        out_shape=[jax.ShapeDtypeStruct((T, D), BF), jax.ShapeDtypeStruct((T, D), F32),
                   jax.ShapeDtypeStruct((T, D), F32), jax.ShapeDtypeStruct((T, 1), F32),
                   jax.ShapeDtypeStruct((T, D), BF)],
        scratch_shapes=[pltpu.VMEM((HA + tt, D), F32), pltpu.VMEM((HB + tt, D), F32), pltpu.VMEM((tt, D), F32),
                        pltpu.VMEM((8, HB + tt, LANES), F32)],
        args=(p, p, p, p, p, p, p, p, p, wa, wb, vecs))


def _seg(tt, s):
    return pl.BlockSpec((tt, D), lambda i: (i, s))


def _prev(tt, h, s):
    return pl.BlockSpec((h, D), lambda i: (jnp.maximum(i * (tt // h) - 1, 0), s))


def _shifted_copies(shbuf, src, ls, n):
    for s in range(1, 8):
        shbuf[s, 0:n, :] = src[pl.ds(s, n), ls]


def _tap(shbuf, src, ls, off, rows):
    s, q = off % 8, off // 8
    if s == 0:
        return src[pl.ds(off, rows), ls]
    return shbuf[s, pl.ds(8 * q, rows), :]


def _post_mixer(yapre, u3, p, x, woa, wob, wo, vecs, late, tm=512):
    T = x.shape[0]

    def body(yapre_ref, u3_ref, ga_ref, gb_ref, x_ref, woa_ref, wob_ref, wo_ref, vec_ref,
             ya_ref, yb_ref, merged_ref, xhat_ref, rstd_ref, x1b_ref):
        ya = jnp.dot(yapre_ref[...], woa_ref[...], preferred_element_type=F32)
        yb = jnp.dot(u3_ref[...], wob_ref[...], preferred_element_type=F32)
        ya_ref[...] = ya.astype(BF)
        yb_ref[...] = yb.astype(BF)
        merged = (_sigmoid(ga_ref[...].astype(F32)) * ya + _sigmoid(gb_ref[...].astype(F32)) * yb).astype(BF)
        merged_ref[...] = merged
        mix = jnp.dot(merged, wo_ref[...], preferred_element_type=F32)
        xhat, rstd = _ln_fwd(ALPHA * x_ref[...] + mix)
        xhat_ref[...] = xhat
        rstd_ref[...] = rstd
        x1b_ref[...] = (xhat * vec_ref[3:4, :] + vec_ref[4:5, :]).astype(BF)

    tok = pl.BlockSpec((tm, D), lambda i: (i, 0))
    wfull = _resident((D, D))
    one = pl.BlockSpec((tm, 1), lambda i: (i, 0))
    return _with_gather(
        body, late, name="post_mixer", nsteps=T // tm,
        in_specs=[tok, tok, _seg(tm, 5), _seg(tm, 6), tok, wfull, wfull, wfull, pl.BlockSpec((8, D), lambda i: (0, 0))],
        out_specs=[tok, tok, tok, tok, one, tok],
        out_shape=[jax.ShapeDtypeStruct((T, D), BF), jax.ShapeDtypeStruct((T, D), BF),
                   jax.ShapeDtypeStruct((T, D), BF), jax.ShapeDtypeStruct((T, D), F32),
                   jax.ShapeDtypeStruct((T, 1), F32), jax.ShapeDtypeStruct((T, D), BF)],
        args=(yapre, u3, p, p, x, woa, wob, wo, vecs))


def _mlp_up(x1b, wup, tm=1024, tn=2048):
    T = x1b.shape[0]
    dff = wup.shape[1]

    def body(x_ref, w_ref, r_ref, h_ref):
        r = jnp.maximum(jnp.dot(x_ref[...], w_ref[...], preferred_element_type=F32), 0.0)
        r_ref[...] = r.astype(BF)
        h_ref[...] = (r * r).astype(BF)

    out = pl.BlockSpec((tm, tn), lambda j, i: (i, j))
    return pl.pallas_call(
        body, name="mlp_up", grid=(dff // tn, T // tm),
        in_specs=[pl.BlockSpec((tm, D), lambda j, i: (i, 0)), pl.BlockSpec((D, tn), lambda j, i: (0, j))],
        out_specs=[out, out],
        out_shape=[jax.ShapeDtypeStruct((T, dff), BF), jax.ShapeDtypeStruct((T, dff), BF)],
        compiler_params=_params(2))(x1b, wup)


def _resident(shape):
    return pl.BlockSpec(shape, lambda *_: (0,) * len(shape), pipeline_mode=pl.Buffered(1))


def _mlp_down_loss(h, wdown, xhat1, target, vecs, tm=512):
    T, dff = h.shape

    def body(h_ref, w_ref, xhat1_ref, tgt_ref, vec_ref, dz2b_ref, st_ref):
        @pl.when(pl.program_id(0) == 0)
        def _():
            st_ref[...] = jnp.zeros_like(st_ref)

        ff = jnp.dot(h_ref[...], w_ref[...], preferred_element_type=F32)
        x1 = xhat1_ref[...] * vec_ref[3:4, :] + vec_ref[4:5, :]
        xhat2, rstd2 = _ln_fwd(ALPHA * x1 + ff)
        g2 = vec_ref[5:6, :]
        diff = xhat2 * g2 + vec_ref[6:7, :] - tgt_ref[...]
        dx2 = diff * (1.0 / D)
        st_ref[0:1, :] += _colsum(dx2 * xhat2)
        st_ref[1:2, :] += _colsum(dx2)
        st_ref[2:3, :] += _colsum(diff * diff)
        dz2b_ref[...] = _ln_bwd(dx2, xhat2, rstd2, g2).astype(BF)

    tok = pl.BlockSpec((tm, D), lambda i: (i, 0))
    vec = pl.BlockSpec((8, D), lambda i: (0, 0))
    return pl.pallas_call(
        body, name="mlp_down_loss", grid=(T // tm,),
        in_specs=[pl.BlockSpec((tm, dff), lambda i: (i, 0)), _resident((dff, D)), tok, tok, vec],
        out_specs=[tok, vec],
        out_shape=[jax.ShapeDtypeStruct((T, D), BF), jax.ShapeDtypeStruct((8, D), F32)],
        compiler_params=_params(1))(h, wdown, xhat1, target, vecs)


def _mlp_down_bwd(dz2b, wdown, r, tm=1024, tk=2048):
    T, dff = r.shape

    def body(dz_ref, w_ref, r_ref, o_ref):
        dh = lax.dot_general(dz_ref[...], w_ref[...], NT_DIMS, preferred_element_type=F32)
        o_ref[...] = (dh * (2.0 * r_ref[...].astype(F32))).astype(BF)

    blk = pl.BlockSpec((tm, tk), lambda j, i: (i, j))
    return pl.pallas_call(
        body, name="mlp_down_bwd", grid=(dff // tk, T // tm),
        in_specs=[pl.BlockSpec((tm, D), lambda j, i: (i, 0)), pl.BlockSpec((tk, D), lambda j, i: (j, 0)), blk],
        out_specs=blk,
        out_shape=jax.ShapeDtypeStruct((T, dff), BF),
        compiler_params=_params(2))(dz2b, wdown, r)


def _tn_matmul(a, b, nblk, a_bw, b_bw, a_blocked, b_blocked, name, tt=2048):
    T = a.shape[0]
    nt = T // tt

    def body(a_ref, b_ref, o32_ref, o16_ref):
        t = pl.program_id(1)

        @pl.when(t == 0)
        def _():
            o32_ref[...] = jnp.zeros_like(o32_ref)

        o32_ref[0] += lax.dot_general(a_ref[...], b_ref[...], TN_DIMS, preferred_element_type=F32)

        @pl.when(t == nt - 1)
        def _():
            o16_ref[...] = o32_ref[...].astype(BF)

    a_spec = pl.BlockSpec((tt, a_bw), (lambda j, t: (t, j)) if a_blocked else (lambda j, t: (t, 0)))
    b_spec = pl.BlockSpec((tt, b_bw), (lambda j, t: (t, j)) if b_blocked else (lambda j, t: (t, 0)))
    out = pl.BlockSpec((1, a_bw, b_bw), lambda j, t: (j, 0, 0))
    return pl.pallas_call(
        body, name=name, grid=(nblk, nt),
        in_specs=[a_spec, b_spec], out_specs=[out, out],
        out_shape=[jax.ShapeDtypeStruct((nblk, a_bw, b_bw), F32), jax.ShapeDtypeStruct((nblk, a_bw, b_bw), BF)],
        compiler_params=_params(2))(a, b)


def _grad_w_in_send(xb, dp, bw, tt=2048):
    T = xb.shape[0]
    nt = T // tt
    flip = lambda v, bit: 1 - v if bit else v

    def peer(k):
        return (flip(lax.axis_index("x"), k & 4), flip(lax.axis_index("y"), k & 2), flip(lax.axis_index("c"), k & 1))

    block_of = lambda dev: 4 * dev[0] + 2 * dev[1] + dev[2]
    order = jnp.stack([block_of(peer(NDEV - 1 - q)) for q in range(NDEV)]).astype(jnp.int32)

    def body(order_ref, a_ref, b_ref, o32_ref, o16_ref, land_ref, stage, send_sems, recv_sems):
        q, t = pl.program_id(0), pl.program_id(1)

        def copy(qq):
            return pltpu.make_async_remote_copy(
                src_ref=stage.at[qq], dst_ref=land_ref.at[qq], send_sem=send_sems.at[qq], recv_sem=recv_sems.at[qq],
                device_id=peer(NDEV - 1 - qq), device_id_type=MESH)

        @pl.when(t == 0)
        def _():
            o32_ref[...] = jnp.zeros_like(o32_ref)

        o32_ref[0] += lax.dot_general(a_ref[...], b_ref[...], TN_DIMS, preferred_element_type=F32)

        @pl.when(t == nt - 1)
        def _():
            o16_ref[...] = o32_ref[...].astype(BF)

        def send(qq):
            stage[qq] = o32_ref[0].astype(BF)
            copy(qq).start()

        for qq in range(W_IN_EARLY):
            pl.when(jnp.logical_and(q == qq, t == nt - 1))(functools.partial(send, qq))

        @pl.when(jnp.logical_and(q == NDEV - 1, t == nt - 1))
        def _():
            for qq in range(W_IN_EARLY):
                copy(qq).wait()

    blk = pl.BlockSpec((1, D, bw), lambda q, t, o: (o[q], 0, 0))
    grid_spec = pltpu.PrefetchScalarGridSpec(
        num_scalar_prefetch=1, grid=(NDEV, nt),
        in_specs=[pl.BlockSpec((tt, D), lambda q, t, o: (t, 0)), pl.BlockSpec((tt, bw), lambda q, t, o: (t, o[q]))],
        out_specs=[blk, blk, ANY_SPEC],
        scratch_shapes=[pltpu.VMEM((W_IN_EARLY, D, bw), BF), pltpu.SemaphoreType.DMA((W_IN_EARLY,)),
                        pltpu.SemaphoreType.DMA((W_IN_EARLY,))])
    return pl.pallas_call(
        body, name="grad_w_in", grid_spec=grid_spec,
        out_shape=[jax.ShapeDtypeStruct((NDEV, D, bw), F32), jax.ShapeDtypeStruct((NDEV, D, bw), BF),
                   jax.ShapeDtypeStruct((W_IN_EARLY, D, bw), BF)],
        compiler_params=_params(2))(order, xb, dp)


def _tn_matmul_tiles_outer(a, b, nblk, a_bw, b_bw, a_blocked, name, tt=4096):
    T = a.shape[0]
    nt = T // tt

    def body(a_ref, b_ref, o32_ref, o16_ref, acc):
        t, j = pl.program_id(0), pl.program_id(1)
        prod = lax.dot_general(a_ref[...], b_ref[...], TN_DIMS, preferred_element_type=F32)

        @pl.when(t == 0)
        def _():
            acc[j] = prod

        @pl.when(t > 0)
        def _():
            acc[j] += prod

        @pl.when(t == nt - 1)
        def _():
            o32_ref[0] = acc[j]
            o16_ref[0] = acc[j].astype(BF)

    a_spec = pl.BlockSpec((tt, a_bw), (lambda t, j: (t, j)) if a_blocked else (lambda t, j: (t, 0)))
    b_spec = pl.BlockSpec((tt, b_bw), (lambda t, j: (t, 0)) if a_blocked else (lambda t, j: (t, j)))
    out = pl.BlockSpec((1, a_bw, b_bw), lambda t, j: (jnp.where(t == nt - 1, j, 0), 0, 0))
    return pl.pallas_call(
        body, name=name, grid=(nt, nblk),
        in_specs=[a_spec, b_spec], out_specs=[out, out],
        out_shape=[jax.ShapeDtypeStruct((nblk, a_bw, b_bw), F32), jax.ShapeDtypeStruct((nblk, a_bw, b_bw), BF)],
        scratch_shapes=[pltpu.VMEM((nblk, a_bw, b_bw), F32)],
        compiler_params=_params(2))(a, b)


def _mlp_up_bwd(dhpre, wup_t, dz2, xhat1, rstd1, vecs, push, tm=512):
    T, dff = dhpre.shape

    def body(dh_ref, w_ref, dz2_ref, xhat_ref, rstd_ref, vec_ref, dz1b_ref, st_ref):
        @pl.when(pl.program_id(0) == 0)
        def _():
            st_ref[...] = jnp.zeros_like(st_ref)

        dx1 = jnp.dot(dh_ref[...], w_ref[...], preferred_element_type=F32) + ALPHA * dz2_ref[...].astype(F32)
        xhat = xhat_ref[...]
        st_ref[0:1, :] += _colsum(dx1 * xhat)
        st_ref[1:2, :] += _colsum(dx1)
        dz1b_ref[...] = _ln_bwd(dx1, xhat, rstd_ref[...], vec_ref[3:4, :]).astype(BF)

    tok = pl.BlockSpec((tm, D), lambda i: (i, 0))
    vec = pl.BlockSpec((8, D), lambda i: (0, 0))
    return _pallas(
        body, name="mlp_up_bwd", grid=(T // tm,),
        in_specs=[pl.BlockSpec((tm, dff), lambda i: (i, 0)), _resident((dff, D)),
                  tok, tok, pl.BlockSpec((tm, 1), lambda i: (i, 0)), vec],
        out_specs=[tok, vec],
        out_shape=[jax.ShapeDtypeStruct((T, D), BF), jax.ShapeDtypeStruct((8, D), F32)],
        args=(dhpre, wup_t, dz2, xhat1, rstd1, vecs), push=push)


def _merge_bwd(dz1, p, ya, yb, conva, xhatb, rstdb, woa, wob, wo, vecs, push, tm=256):
    T = dz1.shape[0]

    def body(dz1_ref, ga_ref, gb_ref, ba_ref, ya_ref, yb_ref, conva_ref, xhat_ref, rstd_ref,
             woa_ref, wob_ref, wo_ref, vec_ref,
             dya_ref, dyb_ref, dg_ref, dba_ref, dconva_ref, du1_ref, st_ref):
        @pl.when(pl.program_id(0) == 0)
        def _():
            st_ref[...] = jnp.zeros_like(st_ref)

        dmerged = lax.dot_general(dz1_ref[...], wo_ref[...], NT_DIMS, preferred_element_type=F32)
        sa, sb = _sigmoid(ga_ref[...].astype(F32)), _sigmoid(gb_ref[...].astype(F32))
        dya = (dmerged * sa).astype(BF)
        dyb = (dmerged * sb).astype(BF)
        dya_ref[...] = dya
        dyb_ref[...] = dyb
        dg_ref[:, 0:D] = (dmerged * ya_ref[...].astype(F32) * (sa * (1.0 - sa))).astype(BF)
        dg_ref[:, D:2 * D] = (dmerged * yb_ref[...].astype(F32) * (sb * (1.0 - sb))).astype(BF)

        dyapre = lax.dot_general(dya, woa_ref[...], NT_DIMS, preferred_element_type=F32)
        dba_ref[...] = (dyapre * conva_ref[...]).astype(BF)
        dconva_ref[...] = dyapre * ba_ref[...].astype(F32)

        du3 = lax.dot_general(dyb, wob_ref[...], NT_DIMS, preferred_element_type=F32)
        xhat = xhat_ref[...]
        gamma = vec_ref[1:2, :]
        u2 = xhat * gamma + vec_ref[2:3, :]
        s = _sigmoid(u2)
        du2 = du3 * (s * (1.0 + u2 * (1.0 - s)))
        st_ref[0:1, :] += _colsum(du2 * xhat)
        st_ref[1:2, :] += _colsum(du2)
        du1 = _ln_bwd(du2, xhat, rstd_ref[...], gamma)
        st_ref[2:3, :] += _colsum(du1)
        du1_ref[...] = du1

    tok = pl.BlockSpec((tm, D), lambda i: (i, 0))
    wfull = pl.BlockSpec((D, D), lambda i: (0, 0))
    vec = pl.BlockSpec((8, D), lambda i: (0, 0))
    return _pallas(
        body, name="merge_bwd", grid=(T // tm,),
        in_specs=[tok, _seg(tm, 5), _seg(tm, 6), _seg(tm, 0), tok, tok, tok, tok, pl.BlockSpec((tm, 1), lambda i: (i, 0)),
                  wfull, wfull, wfull, vec],
        out_specs=[tok, tok, pl.BlockSpec((tm, 2 * D), lambda i: (i, 0)), tok, tok, tok, vec],
        out_shape=[jax.ShapeDtypeStruct((T, D), BF), jax.ShapeDtypeStruct((T, D), BF),
                   jax.ShapeDtypeStruct((T, 2 * D), BF), jax.ShapeDtypeStruct((T, D), BF),
                   jax.ShapeDtypeStruct((T, D), F32), jax.ShapeDtypeStruct((T, D), F32),
                   jax.ShapeDtypeStruct((8, D), F32)],
        args=(dz1, p, p, p, ya, yb, conva, xhatb, rstdb, woa, wob, wo, vecs), push=push)


def _rows8(v):
    out = v[0:8]
    for q in range(1, RC // 8):
        out = out + v[8 * q:8 * q + 8]
    return out


def _conv_bwd(dconva, du1, p, dba, dg, wa, wb, push, tt=256):
    T = p.shape[0]
    nsteps = T // tt

    def body(dca_ref, dca_n, du1_ref, du1_n, ca, va, vb, gb, dba_ref, dg_ref, wa_ref, wb_ref,
             dp_ref, gw_ref, cabuf, u0buf, dcabuf, du1buf, dcain, du0, gwa, gwb, shd):
        i = pl.program_id(0)
        first, last = i == 0, i == nsteps - 1

        @pl.when(first)
        def _():
            gwa[...] = jnp.zeros_like(gwa)
            gwb[...] = jnp.zeros_like(gwb)

        f = lambda ref: ref[...].astype(F32)
        cav, vav, vbv = f(ca), f(va), f(vb)
        cabuf[...] = cav * vav
        sg = _sigmoid(f(gb))
        u0buf[...] = vbv * sg
        dcabuf[0:tt, :] = dca_ref[...]
        dcabuf[tt:tt + HN, :] = jnp.where(last, 0.0, dca_n[...])
        du1buf[0:tt, :] = du1_ref[...]
        du1buf[tt:tt + HB, :] = jnp.where(last, 0.0, du1_n[...])

        def lane_body(cidx, carry):
            ls = pl.ds(pl.multiple_of(cidx * LANES, LANES), LANES)
            _shifted_copies(shd, du1buf, ls, tt + HB - 8)
            for r in range(tt // RC):
                rows = pl.ds(r * RC, RC)
                cin = cabuf[rows, ls]
                acc = jnp.zeros((RC, LANES), F32)
                for k in range(KA):
                    dout = dcabuf[pl.ds(r * RC + KA - 1 - k, RC), ls]
                    acc = acc + wa_ref[k:k + 1, ls] * dout
                    gwa[8 * k:8 * k + 8, ls] += _rows8(cin * dout)
                dcain[rows, ls] = acc
                uin = u0buf[rows, ls]
                acc = jnp.zeros((RC, LANES), F32)
                for k in range(KB):
                    dout = _tap(shd, du1buf, ls, r * RC + KB - 1 - k, RC)
                    acc = acc + wb_ref[k:k + 1, ls] * dout
                    gwb[8 * k:8 * k + 8, ls] += _rows8(uin * dout)
                du0[rows, ls] = acc
            return carry

        lax.fori_loop(0, D // LANES, lane_body, 0)
        dca_in = dcain[...]
        du0v = du0[...]
        dp_ref[:, 0:D] = dba_ref[...]
        dp_ref[:, D:2 * D] = (dca_in * vav).astype(BF)
        dp_ref[:, 2 * D:3 * D] = (dca_in * cav).astype(BF)
        dp_ref[:, 3 * D:4 * D] = (du0v * sg).astype(BF)
        dp_ref[:, 4 * D:5 * D] = (du0v * vbv * (sg * (1.0 - sg))).astype(BF)
        dp_ref[:, 5 * D:7 * D] = dg_ref[...]

        @pl.when(last)
        def _():
            gw_ref[...] = jnp.zeros_like(gw_ref)
            for k in range(KA):
                gw_ref[k:k + 1, :] = _colsum(gwa[8 * k:8 * k + 8, :])
            for k in range(KB):
                gw_ref[8 + k:9 + k, :] = _colsum(gwb[8 * k:8 * k + 8, :])

    full = lambda r: pl.BlockSpec((r, D), lambda i: (0, 0))
    tok = pl.BlockSpec((tt, D), lambda i: (i, 0))
    nxt = lambda h: pl.BlockSpec((h, D), lambda i: (jnp.minimum((i + 1) * (tt // h), T // h - 1), 0))
    return _pallas(
        body, name="conv_bwd", grid=(nsteps,),
        in_specs=[tok, nxt(HN), tok, nxt(HB),
                  _seg(tt, 1), _seg(tt, 2), _seg(tt, 3), _seg(tt, 4),
                  tok, pl.BlockSpec((tt, 2 * D), lambda i: (i, 0)), full(8), full(32)],
        out_specs=[pl.BlockSpec((tt, 7 * D), lambda i: (i, 0)), full(40)],
        out_shape=[jax.ShapeDtypeStruct((T, 7 * D), BF), jax.ShapeDtypeStruct((40, D), F32)],
        scratch_shapes=[pltpu.VMEM((tt, D), F32), pltpu.VMEM((tt, D), F32),
                        pltpu.VMEM((tt + HN, D), F32), pltpu.VMEM((tt + HB, D), F32),
                        pltpu.VMEM((tt, D), F32), pltpu.VMEM((tt, D), F32),
                        pltpu.VMEM((8 * KA, D), F32), pltpu.VMEM((8 * KB, D), F32),
                        pltpu.VMEM((8, HB + tt, LANES), F32)],
        args=(dconva, dconva, du1, du1, p, p, p, p, dba, dg, wa, wb), push=push)


def _inproj_bwd(dp, win_t, dz1, push, tm=512):
    T, cols = dp.shape

    def body(dp_ref, w_ref, dz1_ref, o_ref):
        o_ref[...] = ALPHA * dz1_ref[...].astype(F32) + jnp.dot(dp_ref[...], w_ref[...], preferred_element_type=F32)

    tok = pl.BlockSpec((tm, D), lambda i: (i, 0))
    return _pallas(
        body, name="inproj_bwd", grid=(T // tm,),
        in_specs=[pl.BlockSpec((tm, cols), lambda i: (i, 0)), _resident((cols, D)), tok],
        out_specs=[tok],
        out_shape=[jax.ShapeDtypeStruct((T, D), F32)],
        args=(dp, win_t, dz1), push=push)


def _adam_math(w, m, v, g):
    nm = ADAM_B1 * m + (1.0 - ADAM_B1) * g
    nv = ADAM_B2 * v + (1.0 - ADAM_B2) * (g * g)
    m_hat = nm / (1.0 - ADAM_B1 ** ADAM_STEP)
    v_hat = nv / (1.0 - ADAM_B2 ** ADAM_STEP)
    return -ADAM_LR * (m_hat / (jnp.sqrt(v_hat) + ADAM_EPS) + ADAM_WD * w), nm, nv


def _adamw(w, m, v, g32, landings, me, name, rb):
    R, C = w.shape
    nland = len(landings)

    def body(me_ref, w_ref, m_ref, v_ref, own_ref, *refs):
        g_ref, d_ref, nm_ref, nv_ref = refs[nland:]
        g = own_ref[0]
        for l_ref in refs[:nland]:
            for k in range(l_ref.shape[0]):
                g = g + l_ref[k].astype(F32)
        g_ref[...] = g
        d_ref[...], nm_ref[...], nv_ref[...] = _adam_math(w_ref[...], m_ref[...], v_ref[...], g)

    blk = pl.BlockSpec((rb, C), lambda i, me_ref: (i, 0))
    grid_spec = pltpu.PrefetchScalarGridSpec(
        num_scalar_prefetch=1, grid=(R // rb,),
        in_specs=[blk, blk, blk, pl.BlockSpec((1, rb, C), lambda i, me_ref: (me_ref[0], i, 0))]
        + [pl.BlockSpec((l.shape[0], rb, C), lambda i, me_ref: (0, i, 0)) for l in landings],
        out_specs=[blk] * 4)
    return pl.pallas_call(
        body, name=name, grid_spec=grid_spec, out_shape=[jax.ShapeDtypeStruct((R, C), F32)] * 4,
        compiler_params=_params(1))(me, w, m, v, g32, *landings)


def _adamw_small(small_g, vec_w, vec_m, vec_v, conv_w, conv_m, conv_v):
    nv_ = len(vec_w)
    conv_rows = [(8, KA), (16, KB)]

    def body(*refs):
        g_ref = refs[0]
        w_refs, m_refs, v_refs = refs[1:10], refs[10:19], refs[19:28]
        out_refs, gsum = refs[28:64], refs[64]
        acc = g_ref[0]
        for j in range(1, NDEV):
            acc = acc + g_ref[j]
        gsum[...] = acc
        me = 4 * lax.axis_index("x") + 2 * lax.axis_index("y") + lax.axis_index("c")
        cols = pl.ds(pl.multiple_of(me * LANES, LANES), LANES)
        for i in range(nv_ + 2):
            if i < nv_:
                g = gsum[i:i + 1, :]
            else:
                r0, k = conv_rows[i - nv_]
                g = gsum[r0:r0 + k, cols]
            o = out_refs[4 * i:4 * i + 4]
            o[0][...] = g
            o[1][...], o[2][...], o[3][...] = _adam_math(w_refs[i][...], m_refs[i][...], v_refs[i][...], g)

    ws, ms, vs = list(vec_w) + list(conv_w), list(vec_m) + list(conv_m), list(vec_v) + list(conv_v)
    out_shape = [jax.ShapeDtypeStruct(w.shape, F32) for w in ws for _ in range(4)]
    return pl.pallas_call(
        body, name="adamw_small", out_shape=out_shape,
        scratch_shapes=[pltpu.VMEM(small_g.shape[1:], F32)])(small_g, *ws, *ms, *vs)


def _pad_rows(a, rows):
    return jnp.pad(a, ((0, rows - a.shape[0]), (0, 0)))


def _local_step(p, xb, mixed, post, x, target, win_t, wup, wup_t, wdown, woa, wob, wo, wa, wb, vecs):
    yapre, conva, xhatb, rstdb, u3 = mixed
    ya, yb, merged, xhat1, rstd1, x1b = post
    r, h = _mlp_up(x1b, wup)
    dz2b, st2 = _mlp_down_loss(h, wdown, xhat1, target, vecs)

    by_owner = lambda g16: g16.reshape(NDEV, D // NDEV, D)
    dhpre = _mlp_down_bwd(dz2b, wdown, r)
    g_wdown = _tn_matmul_tiles_outer(h, dz2b, NDEV, 512, D, True, "grad_w_down")
    (dz1b, st1), land_wdown = _mlp_up_bwd(dhpre, wup_t, dz2b, xhat1, rstd1, vecs, _push(exch=[g_wdown[1]]))
    g_wup = _tn_matmul_tiles_outer(x1b, dhpre, NDEV, D, 512, False, "grad_w_up")
    (dya, dyb, dg, dba, dconva, du1, stb), _ = _merge_bwd(
        dz1b, p, ya, yb, conva, xhatb, rstdb, woa, wob, wo, vecs, None)
    g_wo = _tn_matmul(merged, dz1b, 1, D, D, False, False, "grad_w_o")
    g_woa = _tn_matmul(yapre, dya, 1, D, D, False, False, "grad_w_out_a")
    g_wob = _tn_matmul(u3, dyb, 1, D, D, False, False, "grad_w_out_b")
    (dp, gw), land_conv = _conv_bwd(
        dconva, du1, p, dba, dg, wa, wb,
        _push(exch=[g_wup[1], by_owner(g_woa[1]), by_owner(g_wob[1]), by_owner(g_wo[1])]))
    g_win, g_win16, land_win = _grad_w_in_send(xb, dp, dp.shape[1] // NDEV)

    small = jnp.concatenate([stb[2:3], stb[0:2], st1[0:2], st2[0:3], gw], axis=0)
    late_ks = tuple(range(1, NDEV - W_IN_EARLY))
    (grad_x,), land_last = _inproj_bwd(dp, win_t, dz1b, _push(exch=[g_win16], gath=[small], ks=late_ks))
    grads = (g_win, g_wup[0], g_wdown[0], g_woa[0], g_wob[0], g_wo[0])
    return grad_x, grads, small, land_wdown + land_conv + [land_win] + land_last


def kernel(x, w_in, conv_a_w, w_out_a, conv_b_w, conv_b_bias, ln_b_gamma, ln_b_beta, w_out_b, w_o, ln1_gamma, ln1_beta, w_up, w_down, ln2_gamma, ln2_beta, loss_target, m_w_in, m_conv_a_w, m_w_out_a, m_conv_b_w, m_conv_b_bias, m_ln_b_gamma, m_ln_b_beta, m_w_out_b, m_w_o, m_ln1_gamma, m_ln1_beta, m_w_up, m_w_down, m_ln2_gamma, m_ln2_beta, v_w_in, v_conv_a_w, v_w_out_a, v_conv_b_w, v_conv_b_bias, v_ln_b_gamma, v_ln_b_beta, v_w_out_b, v_w_o, v_ln1_gamma, v_ln1_beta, v_w_up, v_w_down, v_ln2_gamma, v_ln2_beta):
    T = x.shape[1]
    me = 4 * lax.axis_index("x") + 2 * lax.axis_index("y") + lax.axis_index("c")

    conv_shard = jnp.concatenate([_pad_rows(conv_a_w, 8), _pad_rows(conv_b_w, 32)], axis=0)
    p, xb, win_g, conv_g = _inproj_gather(x[0], w_in.astype(BF), conv_shard)
    conv_full = jnp.transpose(conv_g, (1, 0, 2)).reshape(40, D)
    vecs = jnp.stack([conv_b_bias, ln_b_gamma, ln_b_beta, ln1_gamma, ln1_beta, ln2_gamma, ln2_beta,
                      jnp.zeros_like(ln2_beta)])
    whole = lambda g: jnp.transpose(g, (1, 0, 2)).reshape(D, -1)
    whole_t = lambda g: jnp.transpose(g, (0, 2, 1)).reshape(-1, D)
    mixed, (woa_g, wob_g, wo_g, wup_g, wdown_g) = _mixer_fwd(
        p, conv_full[0:8], conv_full[8:40], vecs,
        [w_out_a.astype(BF), w_out_b.astype(BF), w_o.astype(BF), w_up.astype(BF), w_down.astype(BF)])
    woa, wob, wo = woa_g.reshape(D, D), wob_g.reshape(D, D), wo_g.reshape(D, D)
    post, _ = _post_mixer(mixed[0], mixed[4], p, x[0], woa, wob, wo, vecs, [])

    grad_x, grads, small, landing = _local_step(
        p, xb, mixed, post, x[0], loss_target[0], whole_t(win_g), whole(wup_g), whole_t(wup_g),
        wdown_g.reshape(NDEV * 512, D),
        woa, wob, wo, conv_full[0:8], conv_full[8:40], vecs)
    g_win, g_wup, g_wdown, g_woa, g_wob, g_wo = grads
    l_wdown, l_wup, l_woa, l_wob, l_wo, l_win_early, l_win_late, small_g = landing

    loss = lax.psum(0.5 / D * jnp.sum(small[7]), ("x", "y", "c"))

    me1 = me.astype(jnp.int32).reshape(1)
    by_owner = lambda g32: g32.reshape(NDEV, D // NDEV, D)
    r_win = _adamw(w_in, m_w_in, v_w_in, g_win, [l_win_early, l_win_late], me1, "adamw_w_in", 256)
    r_wup = _adamw(w_up, m_w_up, v_w_up, g_wup, [l_wup], me1, "adamw_w_up", 256)
    r_wdown = _adamw(w_down, m_w_down, v_w_down, g_wdown, [l_wdown], me1, "adamw_w_down", 256)
    r_woa = _adamw(w_out_a, m_w_out_a, v_w_out_a, by_owner(g_woa), [l_woa], me1, "adamw_w_out_a", 128)
    r_wob = _adamw(w_out_b, m_w_out_b, v_w_out_b, by_owner(g_wob), [l_wob], me1, "adamw_w_out_b", 128)
    r_wo = _adamw(w_o, m_w_o, v_w_o, by_owner(g_wo), [l_wo], me1, "adamw_w_o", 128)

    row = lambda vec: vec.reshape(1, D)
    small_out = _adamw_small(
        small_g,
        [row(a) for a in (conv_b_bias, ln_b_gamma, ln_b_beta, ln1_gamma, ln1_beta, ln2_gamma, ln2_beta)],
        [row(a) for a in (m_conv_b_bias, m_ln_b_gamma, m_ln_b_beta, m_ln1_gamma, m_ln1_beta, m_ln2_gamma, m_ln2_beta)],
        [row(a) for a in (v_conv_b_bias, v_ln_b_gamma, v_ln_b_beta, v_ln1_gamma, v_ln1_beta, v_ln2_gamma, v_ln2_beta)],
        [conv_a_w, conv_b_w], [m_conv_a_w, m_conv_b_w], [v_conv_a_w, v_conv_b_w])
    r_vec = [[small_out[4 * i + q].reshape(D) for q in range(4)] for i in range(7)]
    r_conva, r_convb = small_out[28:32], small_out[32:36]

    per_weight = []
    for q in range(4):
        per_weight.append([
            r_win[q], r_conva[q], r_woa[q], r_convb[q],
            r_vec[0][q], r_vec[1][q], r_vec[2][q], r_wob[q], r_wo[q], r_vec[3][q], r_vec[4][q],
            r_wup[q], r_wdown[q], r_vec[5][q], r_vec[6][q]])
    return (loss, grad_x[None], *per_weight[0], *per_weight[1], *per_weight[2], *per_weight[3])
```

```python
import functools

import jax
import jax.numpy as jnp
from jax import lax
from jax.experimental import pallas as pl
from jax.experimental.pallas import tpu as pltpu

F32 = jnp.float32
BF = jnp.bfloat16
D = 1024
NDEV = 8
ALPHA = 2.0 ** 0.25
LN_EPS = 1e-5
KA, KB = 3, 31
HA, HB = 16, 32
HN = 8
RC = 64
W_IN_EARLY = 4
LANES = 128
VMEM_LIMIT = 56 * 1024 * 1024
MESH = pl.DeviceIdType.MESH
ADAM_LR, ADAM_B1, ADAM_B2, ADAM_EPS, ADAM_WD, ADAM_STEP = 0.001, 0.9, 0.999, 1e-08, 0.01, 10

ANY_SPEC = pl.BlockSpec(memory_space=pl.ANY)
NT_DIMS = (((1,), (1,)), ((), ()))
TN_DIMS = (((0,), (0,)), ((), ()))


def _params(n_axes):
    return pltpu.CompilerParams(dimension_semantics=("arbitrary",) * n_axes, vmem_limit_bytes=VMEM_LIMIT)


def _sigmoid(v):
    return 0.5 * jnp.tanh(0.5 * v) + 0.5


def _ln_fwd(z):
    mu = jnp.mean(z, axis=-1, keepdims=True)
    zc = z - mu
    var = jnp.mean(zc * zc, axis=-1, keepdims=True)
    rstd = lax.rsqrt(var + LN_EPS)
    return zc * rstd, rstd


def _ln_bwd(dy, xhat, rstd, gamma):
    dxhat = dy * gamma
    m1 = jnp.mean(dxhat, axis=-1, keepdims=True)
    m2 = jnp.mean(dxhat * xhat, axis=-1, keepdims=True)
    return rstd * (dxhat - m1 - xhat * m2)


def _colsum(v):
    return jnp.sum(v, axis=0, keepdims=True)


class _TwoLevelGather:
    def __init__(self, ins, outs, send_sems, recv_sems, local_sems):
        self.ins, self.outs = ins, outs
        self.send_sems, self.recv_sems, self.local_sems = send_sems, recv_sems, local_sems
        x, y, c = lax.axis_index("x"), lax.axis_index("y"), lax.axis_index("c")
        self.me, self.sibling, self.c = (x, y, c), (x, y, 1 - c), c
        self.chips = [(1 - x, y), (x, 1 - y), (1 - x, 1 - y)]
        self.n = len(ins)

    @staticmethod
    def out_shape(arrs):
        return [jax.ShapeDtypeStruct((NDEV,) + a.shape, a.dtype) for a in arrs]

    @staticmethod
    def scratch(n):
        return [pltpu.SemaphoreType.DMA((n, 7)), pltpu.SemaphoreType.DMA((n, 7)), pltpu.SemaphoreType.DMA((n,))]

    def _copy(self, a, k, block, to, src=None):
        px, py, pc = block
        rows = self.outs[a].at[4 * px + 2 * py + pc]
        return pltpu.make_async_remote_copy(
            src_ref=rows if src is None else src, dst_ref=rows,
            send_sem=self.send_sems.at[a, k], recv_sem=self.recv_sems.at[a, k],
            device_id=to, device_id_type=MESH)

    def _mine(self, a):
        x, y, c = self.me
        return pltpu.make_async_copy(self.ins[a], self.outs[a].at[4 * x + 2 * y + c], self.local_sems.at[a])

    def _first(self, a):
        cps = [self._copy(a, 0, self.me, self.sibling, src=self.ins[a])]
        return cps + [self._copy(a, 1 + j, self.me, (*chip, self.c), src=self.ins[a]) for j, chip in enumerate(self.chips)]

    def _passed(self, a, j):
        return self._copy(a, 4 + j, (*self.chips[j], self.c), self.sibling)

    def start(self, diagonal=True):
        for a in range(self.n):
            self._mine(a).start()
        for a in range(self.n):
            for cp in self._first(a)[:4 if diagonal else 3]:
                cp.start()

    def start_diagonal(self):
        for a in range(self.n):
            self._first(a)[3].start()

    def wait_ici(self, j):
        for a in range(self.n):
            self._copy(a, 1 + j, (*self.chips[j], self.c), self.me).wait_recv()

    def pass_on(self, j):
        for a in range(self.n):
            self._passed(a, j).start()

    def wait_sibling(self):
        for a in range(self.n):
            self._copy(a, 0, self.sibling, self.me).wait_recv()

    def wait_passed(self, j):
        for a in range(self.n):
            self._copy(a, 4 + j, (*self.chips[j], 1 - self.c), self.me).wait_recv()

    def drain(self):
        for a in range(self.n):
            for cp in self._first(a) + [self._passed(a, j) for j in range(3)]:
                cp.wait_send()
            self._mine(a).wait()

    def forward(self):
        for j in range(3):
            self.wait_ici(j)
            self.pass_on(j)

    def finish(self):
        self.wait_sibling()
        for j in range(3):
            self.wait_passed(j)
        self.drain()


class _Push:
    def __init__(self, exch=(), gath=(), ks=tuple(range(1, NDEV))):
        self.exch, self.gath, self.ks = list(exch), list(gath), tuple(ks)
        self.n = len(self.exch) + len(self.gath)

    def operands(self):
        return self.exch + self.gath

    def out_shape(self):
        return ([jax.ShapeDtypeStruct((len(self.ks),) + a.shape[1:], a.dtype) for a in self.exch]
                + [jax.ShapeDtypeStruct((NDEV,) + a.shape, a.dtype) for a in self.gath])

    def scratch(self):
        return [pltpu.SemaphoreType.DMA((self.n, 7)), pltpu.SemaphoreType.DMA((self.n, 7)),
                pltpu.SemaphoreType.DMA((max(len(self.gath), 1),))]

    def copies(self, ins, outs, send_sems, recv_sems, local_sems):
        x, y, c = lax.axis_index("x"), lax.axis_index("y"), lax.axis_index("c")
        me = 4 * x + 2 * y + c
        ne = len(self.exch)
        remote = []
        for k in range(1, NDEV):
            px = 1 - x if k & 4 else x
            py = 1 - y if k & 2 else y
            pc = 1 - c if k & 1 else c
            for a in range(self.n):
                if a < ne and k not in self.ks:
                    continue
                src = ins[a].at[4 * px + 2 * py + pc] if a < ne else ins[a]
                dst = outs[a].at[self.ks.index(k)] if a < ne else outs[a].at[me]
                remote.append(pltpu.make_async_remote_copy(
                    src_ref=src, dst_ref=dst, send_sem=send_sems.at[a, k - 1], recv_sem=recv_sems.at[a, k - 1],
                    device_id=(px, py, pc), device_id_type=MESH))
        local = [pltpu.make_async_copy(ins[a], outs[a].at[me], local_sems.at[a - ne]) for a in range(ne, self.n)]
        return remote, local


def _push(exch=(), gath=(), ks=tuple(range(1, NDEV))):
    return _Push(exch, gath, ks)


def _pallas(body, *, name, grid, in_specs, out_specs, out_shape, args, scratch_shapes=(), push=None):
    ni, no, ns = len(in_specs), len(out_specs), len(scratch_shapes)
    if push is None:
        outs = pl.pallas_call(
            body, name=name, grid=grid, in_specs=in_specs, out_specs=out_specs, out_shape=out_shape,
            scratch_shapes=list(scratch_shapes), compiler_params=_params(len(grid)))(*args)
        return list(outs), []
    npush = push.n

    def wrapped(*refs):
        ins, pins = refs[:ni], refs[ni:ni + npush]
        outs, pouts = refs[ni + npush:ni + npush + no], refs[ni + npush + no:ni + 2 * npush + no]
        scr, sems = refs[ni + 2 * npush + no:ni + 2 * npush + no + ns], refs[ni + 2 * npush + no + ns:]
        first = functools.reduce(jnp.logical_and, [pl.program_id(d) == 0 for d in range(len(grid))])
        last = functools.reduce(jnp.logical_and, [pl.program_id(d) == grid[d] - 1 for d in range(len(grid))])
        remote, local = push.copies(pins, pouts, *sems)

        @pl.when(first)
        def _():
            for cp in local + remote:
                cp.start()

        body(*ins, *outs, *scr)

        @pl.when(last)
        def _():
            for cp in remote + local:
                cp.wait()

    outs = pl.pallas_call(
        wrapped, name=name, grid=grid,
        in_specs=list(in_specs) + [ANY_SPEC] * npush, out_specs=list(out_specs) + [ANY_SPEC] * npush,
        out_shape=list(out_shape) + push.out_shape(), scratch_shapes=list(scratch_shapes) + push.scratch(),
        compiler_params=_params(len(grid)))(*args, *push.operands())
    return list(outs[:no]), list(outs[no:])


def _with_gather(body, late, *, name, nsteps, in_specs, out_specs, out_shape, args, scratch_shapes=()):
    ni, no, ns, n = len(in_specs), len(out_specs), len(scratch_shapes), len(late)
    pass_step = (7 * nsteps) // 8
    if not late:
        outs = pl.pallas_call(
            body, name=name, grid=(nsteps,), in_specs=in_specs, out_specs=out_specs, out_shape=out_shape,
            scratch_shapes=list(scratch_shapes), compiler_params=_params(1))(*args)
        return list(outs), []

    def wrapped(*refs):
        ins, outs = refs[:ni], refs[ni + n:ni + n + no]
        scr = refs[ni + 2 * n + no:ni + 2 * n + no + ns]
        gather = _TwoLevelGather(refs[ni:ni + n], refs[ni + n + no:ni + 2 * n + no], *refs[ni + 2 * n + no + ns:])
        step = pl.program_id(0)
        pl.when(step == 0)(gather.start)
        pl.when(step == pass_step)(gather.forward)
        body(*ins, *outs, *scr)
        pl.when(step == nsteps - 1)(gather.finish)

    outs = pl.pallas_call(
        wrapped, name=name, grid=(nsteps,),
        in_specs=list(in_specs) + [ANY_SPEC] * n, out_specs=list(out_specs) + [ANY_SPEC] * n,
        out_shape=list(out_shape) + _TwoLevelGather.out_shape(late),
        scratch_shapes=list(scratch_shapes) + _TwoLevelGather.scratch(n),
        compiler_params=_params(1))(*args, *late)
    return list(outs[:no]), list(outs[no:])


def _inproj_gather(x, w_shard, conv_shard, tm=2048):
    T = x.shape[0]
    ni = T // tm
    bw = w_shard.shape[1]
    cx, cy, cc = lax.axis_index("x"), lax.axis_index("y"), lax.axis_index("c")
    blk = lambda px, py, pc: 4 * px + 2 * py + pc
    order = [blk(cx, cy, cc), blk(cx, cy, 1 - cc)]
    for chip in [(1 - cx, cy), (cx, 1 - cy), (1 - cx, 1 - cy)]:
        order += [blk(*chip, cc), blk(*chip, 1 - cc)]
    order = jnp.stack(order).astype(jnp.int32)

    def body(order_ref, x_ref, w_ref, conv_ref, p_ref, xb_ref, wing_ref, convg_ref, xbs, wbuf, wsem, *sems):
        gather = _TwoLevelGather([w_ref, conv_ref], [wing_ref, convg_ref], *sems)
        j, i = pl.program_id(0), pl.program_id(1)

        def load(src):
            cp = pltpu.make_async_copy(src, wbuf, wsem)
            cp.start()
            cp.wait()

        def arrival(jj):
            if jj == 0:
                gather.start(diagonal=False)
                load(w_ref)
                return
            if jj == 1:
                gather.wait_sibling()
            elif jj % 2 == 0:
                if jj == 2:
                    gather.start_diagonal()
                gather.wait_ici(jj // 2 - 1)
                gather.pass_on(jj // 2 - 1)
            else:
                gather.wait_passed(jj // 2 - 1)
            load(wing_ref.at[order_ref[jj]])

        for jj in range(NDEV):
            pl.when(jnp.logical_and(j == jj, i == 0))(functools.partial(arrival, jj))

        @pl.when(j == 0)
        def _():
            xb = x_ref[...].astype(BF)
            xbs[i] = xb
            xb_ref[...] = xb

        p_ref[...] = jnp.dot(xbs[i], wbuf[...], preferred_element_type=F32).astype(BF)

        @pl.when(jnp.logical_and(j == NDEV - 1, i == ni - 1))
        def _():
            gather.drain()

    rows_once = lambda j, i, o: (jnp.where(j == 0, i, ni - 1), 0)
    grid_spec = pltpu.PrefetchScalarGridSpec(
        num_scalar_prefetch=1, grid=(NDEV, ni),
        in_specs=[pl.BlockSpec((tm, D), rows_once), ANY_SPEC, ANY_SPEC],
        out_specs=[pl.BlockSpec((tm, bw), lambda j, i, o: (i, o[j])), pl.BlockSpec((tm, D), rows_once),
                   ANY_SPEC, ANY_SPEC],
        scratch_shapes=[pltpu.VMEM((ni, tm, D), BF), pltpu.VMEM((D, bw), BF), pltpu.SemaphoreType.DMA(())]
        + _TwoLevelGather.scratch(2))
    p, xb, win_g, conv_g = pl.pallas_call(
        body, name="inproj_gather", grid_spec=grid_spec,
        out_shape=[jax.ShapeDtypeStruct((T, NDEV * bw), BF), jax.ShapeDtypeStruct((T, D), BF)]
        + _TwoLevelGather.out_shape([w_shard, conv_shard]),
        compiler_params=_params(2))(order, x, w_shard, conv_shard)
    return p, xb, win_g, conv_g


def _mixer_fwd(p, wa, wb, vecs, late, tt=256):
    T = p.shape[0]
    nt = T // tt

    def body(ba, ca, va, vb, gb, ca_p, va_p, vb_p, gb_p, wa_ref, wb_ref, vec_ref,
             yapre_ref, conva_ref, xhat_ref, rstd_ref, u3_ref, cabuf, u0buf, u1buf, shu):
        first = pl.program_id(0) == 0
        f = lambda ref: ref[...].astype(F32)
        cabuf[0:HA, :] = jnp.where(first, 0.0, f(ca_p) * f(va_p))
        cabuf[HA:HA + tt, :] = f(ca) * f(va)
        u0buf[0:HB, :] = jnp.where(first, 0.0, f(vb_p) * _sigmoid(f(gb_p)))
        u0buf[HB:HB + tt, :] = f(vb) * _sigmoid(f(gb))

        def lane_body(cidx, carry):
            ls = pl.ds(pl.multiple_of(cidx * LANES, LANES), LANES)
            _shifted_copies(shu, u0buf, ls, tt + HB - 8)
            for r in range(tt // RC):
                acc = jnp.zeros((RC, LANES), F32)
                for k in range(KA):
                    acc = acc + wa_ref[k:k + 1, ls] * cabuf[pl.ds(HA - (KA - 1) + k + r * RC, RC), ls]
                conva_ref[pl.ds(r * RC, RC), ls] = acc
                acc = jnp.zeros((RC, LANES), F32)
                for k in range(KB):
                    acc = acc + wb_ref[k:k + 1, ls] * _tap(shu, u0buf, ls, HB - (KB - 1) + k + r * RC, RC)
                u1buf[pl.ds(r * RC, RC), ls] = acc
            return carry

        lax.fori_loop(0, D // LANES, lane_body, 0)
        yapre_ref[...] = (f(ba) * conva_ref[...]).astype(BF)
        xhat, rstd = _ln_fwd(u1buf[...] + vec_ref[0:1, :])
        xhat_ref[...] = xhat
        rstd_ref[...] = rstd
        u2 = xhat * vec_ref[1:2, :] + vec_ref[2:3, :]
        u3_ref[...] = (u2 * _sigmoid(u2)).astype(BF)

    full = lambda r: pl.BlockSpec((r, D), lambda i: (0, 0))
    tok = pl.BlockSpec((tt, D), lambda i: (i, 0))
    return _with_gather(
        body, late, name="mixer_fwd", nsteps=nt,
        in_specs=[_seg(tt, 0), _seg(tt, 1), _seg(tt, 2), _seg(tt, 3), _seg(tt, 4),
                  _prev(tt, HA, 1), _prev(tt, HA, 2), _prev(tt, HB, 3), _prev(tt, HB, 4),
                  full(8), full(32), full(8)],
        out_specs=[tok, tok, tok, pl.BlockSpec((tt, 1), lambda i: (i, 0)), tok],
        out_shape=[jax.ShapeDtypeStruct((T, D), BF), jax.ShapeDtypeStruct((T, D), F32),
                   jax.ShapeDtypeStruct((T, D), F32), jax.ShapeDtypeStruct((T, 1), F32),
                   jax.ShapeDtypeStruct((T, D), BF)],
        scratch_shapes=[pltpu.VMEM((HA + tt, D), F32), pltpu.VMEM((HB + tt, D), F32), pltpu.VMEM((tt, D), F32),
                        pltpu.VMEM((8, HB + tt, LANES), F32)],
        args=(p, p, p, p, p, p, p, p, p, wa, wb, vecs))


def _seg(tt, s):
    return pl.BlockSpec((tt, D), lambda i: (i, s))


def _prev(tt, h, s):
    return pl.BlockSpec((h, D), lambda i: (jnp.maximum(i * (tt // h) - 1, 0), s))


def _shifted_copies(shbuf, src, ls, n):
    for s in range(1, 8):
        shbuf[s, 0:n, :] = src[pl.ds(s, n), ls]


def _tap(shbuf, src, ls, off, rows):
    s, q = off % 8, off // 8
    if s == 0:
        return src[pl.ds(off, rows), ls]
    return shbuf[s, pl.ds(8 * q, rows), :]


def _post_mixer(yapre, u3, p, x, woa, wob, wo, vecs, late, tm=512):
    T = x.shape[0]

    def body(yapre_ref, u3_ref, ga_ref, gb_ref, x_ref, woa_ref, wob_ref, wo_ref, vec_ref,
             ya_ref, yb_ref, merged_ref, xhat_ref, rstd_ref, x1b_ref):
        ya = jnp.dot(yapre_ref[...], woa_ref[...], preferred_element_type=F32)
        yb = jnp.dot(u3_ref[...], wob_ref[...], preferred_element_type=F32)
        ya_ref[...] = ya.astype(BF)
        yb_ref[...] = yb.astype(BF)
        merged = (_sigmoid(ga_ref[...].astype(F32)) * ya + _sigmoid(gb_ref[...].astype(F32)) * yb).astype(BF)
        merged_ref[...] = merged
        mix = jnp.dot(merged, wo_ref[...], preferred_element_type=F32)
        xhat, rstd = _ln_fwd(ALPHA * x_ref[...] + mix)
        xhat_ref[...] = xhat
        rstd_ref[...] = rstd
        x1b_ref[...] = (xhat * vec_ref[3:4, :] + vec_ref[4:5, :]).astype(BF)

    tok = pl.BlockSpec((tm, D), lambda i: (i, 0))
    wfull = _resident((D, D))
    one = pl.BlockSpec((tm, 1), lambda i: (i, 0))
    return _with_gather(
        body, late, name="post_mixer", nsteps=T // tm,
        in_specs=[tok, tok, _seg(tm, 5), _seg(tm, 6), tok, wfull, wfull, wfull, pl.BlockSpec((8, D), lambda i: (0, 0))],
        out_specs=[tok, tok, tok, tok, one, tok],
        out_shape=[jax.ShapeDtypeStruct((T, D), BF), jax.ShapeDtypeStruct((T, D), BF),
                   jax.ShapeDtypeStruct((T, D), BF), jax.ShapeDtypeStruct((T, D), F32),
                   jax.ShapeDtypeStruct((T, 1), F32), jax.ShapeDtypeStruct((T, D), BF)],
        args=(yapre, u3, p, p, x, woa, wob, wo, vecs))


def _mlp_up(x1b, wup, tm=1024, tn=2048):
    T = x1b.shape[0]
    dff = wup.shape[1]

    def body(x_ref, w_ref, r_ref, h_ref):
        r = jnp.maximum(jnp.dot(x_ref[...], w_ref[...], preferred_element_type=F32), 0.0)
        r_ref[...] = r.astype(BF)
        h_ref[...] = (r * r).astype(BF)

    out = pl.BlockSpec((tm, tn), lambda j, i: (i, j))
    return pl.pallas_call(
        body, name="mlp_up", grid=(dff // tn, T // tm),
        in_specs=[pl.BlockSpec((tm, D), lambda j, i: (i, 0)), pl.BlockSpec((D, tn), lambda j, i: (0, j))],
        out_specs=[out, out],
        out_shape=[jax.ShapeDtypeStruct((T, dff), BF), jax.ShapeDtypeStruct((T, dff), BF)],
        compiler_params=_params(2))(x1b, wup)


def _resident(shape):
    return pl.BlockSpec(shape, lambda *_: (0,) * len(shape), pipeline_mode=pl.Buffered(1))


def _mlp_down_loss(h, wdown, xhat1, target, vecs, tm=512):
    T, dff = h.shape

    def body(h_ref, w_ref, xhat1_ref, tgt_ref, vec_ref, dz2b_ref, st_ref):
        @pl.when(pl.program_id(0) == 0)
        def _():
            st_ref[...] = jnp.zeros_like(st_ref)

        ff = jnp.dot(h_ref[...], w_ref[...], preferred_element_type=F32)
        x1 = xhat1_ref[...] * vec_ref[3:4, :] + vec_ref[4:5, :]
        xhat2, rstd2 = _ln_fwd(ALPHA * x1 + ff)
        g2 = vec_ref[5:6, :]
        diff = xhat2 * g2 + vec_ref[6:7, :] - tgt_ref[...]
        dx2 = diff * (1.0 / D)
        st_ref[0:1, :] += _colsum(dx2 * xhat2)
        st_ref[1:2, :] += _colsum(dx2)
        st_ref[2:3, :] += _colsum(diff * diff)
        dz2b_ref[...] = _ln_bwd(dx2, xhat2, rstd2, g2).astype(BF)

    tok = pl.BlockSpec((tm, D), lambda i: (i, 0))
    vec = pl.BlockSpec((8, D), lambda i: (0, 0))
    return pl.pallas_call(
        body, name="mlp_down_loss", grid=(T // tm,),
        in_specs=[pl.BlockSpec((tm, dff), lambda i: (i, 0)), _resident((dff, D)), tok, tok, vec],
        out_specs=[tok, vec],
        out_shape=[jax.ShapeDtypeStruct((T, D), BF), jax.ShapeDtypeStruct((8, D), F32)],
        compiler_params=_params(1))(h, wdown, xhat1, target, vecs)


def _mlp_down_bwd(dz2b, wdown, r, tm=1024, tk=2048):
    T, dff = r.shape

    def body(dz_ref, w_ref, r_ref, o_ref):
        dh = lax.dot_general(dz_ref[...], w_ref[...], NT_DIMS, preferred_element_type=F32)
        o_ref[...] = (dh * (2.0 * r_ref[...].astype(F32))).astype(BF)

    blk = pl.BlockSpec((tm, tk), lambda j, i: (i, j))
    return pl.pallas_call(
        body, name="mlp_down_bwd", grid=(dff // tk, T // tm),
        in_specs=[pl.BlockSpec((tm, D), lambda j, i: (i, 0)), pl.BlockSpec((tk, D), lambda j, i: (j, 0)), blk],
        out_specs=blk,
        out_shape=jax.ShapeDtypeStruct((T, dff), BF),
        compiler_params=_params(2))(dz2b, wdown, r)


def _tn_matmul(a, b, nblk, a_bw, b_bw, a_blocked, b_blocked, name, tt=4096):
    T = a.shape[0]
    nt = T // tt

    def body(a_ref, b_ref, o32_ref, o16_ref):
        t = pl.program_id(1)

        @pl.when(t == 0)
        def _():
            o32_ref[...] = jnp.zeros_like(o32_ref)

        o32_ref[0] += lax.dot_general(a_ref[...], b_ref[...], TN_DIMS, preferred_element_type=F32)

        @pl.when(t == nt - 1)
        def _():
            o16_ref[...] = o32_ref[...].astype(BF)

    a_spec = pl.BlockSpec((tt, a_bw), (lambda j, t: (t, j)) if a_blocked else (lambda j, t: (t, 0)))
    b_spec = pl.BlockSpec((tt, b_bw), (lambda j, t: (t, j)) if b_blocked else (lambda j, t: (t, 0)))
    out = pl.BlockSpec((1, a_bw, b_bw), lambda j, t: (j, 0, 0))
    return pl.pallas_call(
        body, name=name, grid=(nblk, nt),
        in_specs=[a_spec, b_spec], out_specs=[out, out],
        out_shape=[jax.ShapeDtypeStruct((nblk, a_bw, b_bw), F32), jax.ShapeDtypeStruct((nblk, a_bw, b_bw), BF)],
        compiler_params=_params(2))(a, b)


def _grad_w_in_send(xb, dp, bw, tt=2048):
    T = xb.shape[0]
    nt = T // tt
    flip = lambda v, bit: 1 - v if bit else v

    def peer(k):
        return (flip(lax.axis_index("x"), k & 4), flip(lax.axis_index("y"), k & 2), flip(lax.axis_index("c"), k & 1))

    block_of = lambda dev: 4 * dev[0] + 2 * dev[1] + dev[2]
    order = jnp.stack([block_of(peer(NDEV - 1 - q)) for q in range(NDEV)]).astype(jnp.int32)

    def body(order_ref, a_ref, b_ref, o32_ref, o16_ref, land_ref, stage, send_sems, recv_sems):
        q, t = pl.program_id(0), pl.program_id(1)

        def copy(qq):
            return pltpu.make_async_remote_copy(
                src_ref=stage.at[qq], dst_ref=land_ref.at[qq], send_sem=send_sems.at[qq], recv_sem=recv_sems.at[qq],
                device_id=peer(NDEV - 1 - qq), device_id_type=MESH)

        @pl.when(t == 0)
        def _():
            o32_ref[...] = jnp.zeros_like(o32_ref)

        o32_ref[0] += lax.dot_general(a_ref[...], b_ref[...], TN_DIMS, preferred_element_type=F32)

        @pl.when(t == nt - 1)
        def _():
            o16_ref[...] = o32_ref[...].astype(BF)

        def send(qq):
            stage[qq] = o32_ref[0].astype(BF)
            copy(qq).start()

        for qq in range(W_IN_EARLY):
            pl.when(jnp.logical_and(q == qq, t == nt - 1))(functools.partial(send, qq))

        @pl.when(jnp.logical_and(q == NDEV - 1, t == nt - 1))
        def _():
            for qq in range(W_IN_EARLY):
                copy(qq).wait()

    blk = pl.BlockSpec((1, D, bw), lambda q, t, o: (o[q], 0, 0))
    grid_spec = pltpu.PrefetchScalarGridSpec(
        num_scalar_prefetch=1, grid=(NDEV, nt),
        in_specs=[pl.BlockSpec((tt, D), lambda q, t, o: (t, 0)), pl.BlockSpec((tt, bw), lambda q, t, o: (t, o[q]))],
        out_specs=[blk, blk, ANY_SPEC],
        scratch_shapes=[pltpu.VMEM((W_IN_EARLY, D, bw), BF), pltpu.SemaphoreType.DMA((W_IN_EARLY,)),
                        pltpu.SemaphoreType.DMA((W_IN_EARLY,))])
    return pl.pallas_call(
        body, name="grad_w_in", grid_spec=grid_spec,
        out_shape=[jax.ShapeDtypeStruct((NDEV, D, bw), F32), jax.ShapeDtypeStruct((NDEV, D, bw), BF),
                   jax.ShapeDtypeStruct((W_IN_EARLY, D, bw), BF)],
        compiler_params=_params(2))(order, xb, dp)


def _tn_matmul_tiles_outer(a, b, nblk, a_bw, b_bw, a_blocked, name, tt=4096):
    T = a.shape[0]
    nt = T // tt

    def body(a_ref, b_ref, o32_ref, o16_ref, acc):
        t, j = pl.program_id(0), pl.program_id(1)
        prod = lax.dot_general(a_ref[...], b_ref[...], TN_DIMS, preferred_element_type=F32)

        @pl.when(t == 0)
        def _():
            acc[j] = prod

        @pl.when(t > 0)
        def _():
            acc[j] += prod

        @pl.when(t == nt - 1)
        def _():
            o32_ref[0] = acc[j]
            o16_ref[0] = acc[j].astype(BF)

    a_spec = pl.BlockSpec((tt, a_bw), (lambda t, j: (t, j)) if a_blocked else (lambda t, j: (t, 0)))
    b_spec = pl.BlockSpec((tt, b_bw), (lambda t, j: (t, 0)) if a_blocked else (lambda t, j: (t, j)))
    out = pl.BlockSpec((1, a_bw, b_bw), lambda t, j: (jnp.where(t == nt - 1, j, 0), 0, 0))
    return pl.pallas_call(
        body, name=name, grid=(nt, nblk),
        in_specs=[a_spec, b_spec], out_specs=[out, out],
        out_shape=[jax.ShapeDtypeStruct((nblk, a_bw, b_bw), F32), jax.ShapeDtypeStruct((nblk, a_bw, b_bw), BF)],
        scratch_shapes=[pltpu.VMEM((nblk, a_bw, b_bw), F32)],
        compiler_params=_params(2))(a, b)


def _mlp_up_bwd(dhpre, wup_t, dz2, xhat1, rstd1, vecs, push, tm=512):
    T, dff = dhpre.shape

    def body(dh_ref, w_ref, dz2_ref, xhat_ref, rstd_ref, vec_ref, dz1b_ref, st_ref):
        @pl.when(pl.program_id(0) == 0)
        def _():
            st_ref[...] = jnp.zeros_like(st_ref)

        dx1 = jnp.dot(dh_ref[...], w_ref[...], preferred_element_type=F32) + ALPHA * dz2_ref[...].astype(F32)
        xhat = xhat_ref[...]
        st_ref[0:1, :] += _colsum(dx1 * xhat)
        st_ref[1:2, :] += _colsum(dx1)
        dz1b_ref[...] = _ln_bwd(dx1, xhat, rstd_ref[...], vec_ref[3:4, :]).astype(BF)

    tok = pl.BlockSpec((tm, D), lambda i: (i, 0))
    vec = pl.BlockSpec((8, D), lambda i: (0, 0))
    return _pallas(
        body, name="mlp_up_bwd", grid=(T // tm,),
        in_specs=[pl.BlockSpec((tm, dff), lambda i: (i, 0)), _resident((dff, D)),
                  tok, tok, pl.BlockSpec((tm, 1), lambda i: (i, 0)), vec],
        out_specs=[tok, vec],
        out_shape=[jax.ShapeDtypeStruct((T, D), BF), jax.ShapeDtypeStruct((8, D), F32)],
        args=(dhpre, wup_t, dz2, xhat1, rstd1, vecs), push=push)


def _merge_bwd(dz1, p, ya, yb, conva, xhatb, rstdb, woa, wob, wo, vecs, push, tm=256):
    T = dz1.shape[0]

    def body(dz1_ref, ga_ref, gb_ref, ba_ref, ya_ref, yb_ref, conva_ref, xhat_ref, rstd_ref,
             woa_ref, wob_ref, wo_ref, vec_ref,
             dya_ref, dyb_ref, dg_ref, dba_ref, dconva_ref, du1_ref, st_ref):
        @pl.when(pl.program_id(0) == 0)
        def _():
            st_ref[...] = jnp.zeros_like(st_ref)

        dmerged = lax.dot_general(dz1_ref[...], wo_ref[...], NT_DIMS, preferred_element_type=F32)
        sa, sb = _sigmoid(ga_ref[...].astype(F32)), _sigmoid(gb_ref[...].astype(F32))
        dya = (dmerged * sa).astype(BF)
        dyb = (dmerged * sb).astype(BF)
        dya_ref[...] = dya
        dyb_ref[...] = dyb
        dg_ref[:, 0:D] = (dmerged * ya_ref[...].astype(F32) * (sa * (1.0 - sa))).astype(BF)
        dg_ref[:, D:2 * D] = (dmerged * yb_ref[...].astype(F32) * (sb * (1.0 - sb))).astype(BF)

        dyapre = lax.dot_general(dya, woa_ref[...], NT_DIMS, preferred_element_type=F32)
        dba_ref[...] = (dyapre * conva_ref[...]).astype(BF)
        dconva_ref[...] = dyapre * ba_ref[...].astype(F32)

        du3 = lax.dot_general(dyb, wob_ref[...], NT_DIMS, preferred_element_type=F32)
        xhat = xhat_ref[...]
        gamma = vec_ref[1:2, :]
        u2 = xhat * gamma + vec_ref[2:3, :]
        s = _sigmoid(u2)
        du2 = du3 * (s * (1.0 + u2 * (1.0 - s)))
        st_ref[0:1, :] += _colsum(du2 * xhat)
        st_ref[1:2, :] += _colsum(du2)
        du1 = _ln_bwd(du2, xhat, rstd_ref[...], gamma)
        st_ref[2:3, :] += _colsum(du1)
        du1_ref[...] = du1

    tok = pl.BlockSpec((tm, D), lambda i: (i, 0))
    wfull = pl.BlockSpec((D, D), lambda i: (0, 0))
    vec = pl.BlockSpec((8, D), lambda i: (0, 0))
    return _pallas(
        body, name="merge_bwd", grid=(T // tm,),
        in_specs=[tok, _seg(tm, 5), _seg(tm, 6), _seg(tm, 0), tok, tok, tok, tok, pl.BlockSpec((tm, 1), lambda i: (i, 0)),
                  wfull, wfull, wfull, vec],
        out_specs=[tok, tok, pl.BlockSpec((tm, 2 * D), lambda i: (i, 0)), tok, tok, tok, vec],
        out_shape=[jax.ShapeDtypeStruct((T, D), BF), jax.ShapeDtypeStruct((T, D), BF),
                   jax.ShapeDtypeStruct((T, 2 * D), BF), jax.ShapeDtypeStruct((T, D), BF),
                   jax.ShapeDtypeStruct((T, D), F32), jax.ShapeDtypeStruct((T, D), F32),
                   jax.ShapeDtypeStruct((8, D), F32)],
        args=(dz1, p, p, p, ya, yb, conva, xhatb, rstdb, woa, wob, wo, vecs), push=push)


def _rows8(v):
    out = v[0:8]
    for q in range(1, RC // 8):
        out = out + v[8 * q:8 * q + 8]
    return out


def _conv_bwd(dconva, du1, p, dba, dg, wa, wb, push, tt=256):
    T = p.shape[0]
    nsteps = T // tt

    def body(dca_ref, dca_n, du1_ref, du1_n, ca, va, vb, gb, dba_ref, dg_ref, wa_ref, wb_ref,
             dp_ref, gw_ref, cabuf, u0buf, dcabuf, du1buf, dcain, du0, gwa, gwb, shd):
        i = pl.program_id(0)
        first, last = i == 0, i == nsteps - 1

        @pl.when(first)
        def _():
            gwa[...] = jnp.zeros_like(gwa)
            gwb[...] = jnp.zeros_like(gwb)

        f = lambda ref: ref[...].astype(F32)
        cav, vav, vbv = f(ca), f(va), f(vb)
        cabuf[...] = cav * vav
        sg = _sigmoid(f(gb))
        u0buf[...] = vbv * sg
        dcabuf[0:tt, :] = dca_ref[...]
        dcabuf[tt:tt + HN, :] = jnp.where(last, 0.0, dca_n[...])
        du1buf[0:tt, :] = du1_ref[...]
        du1buf[tt:tt + HB, :] = jnp.where(last, 0.0, du1_n[...])

        def lane_body(cidx, carry):
            ls = pl.ds(pl.multiple_of(cidx * LANES, LANES), LANES)
            _shifted_copies(shd, du1buf, ls, tt + HB - 8)
            for r in range(tt // RC):
                rows = pl.ds(r * RC, RC)
                cin = cabuf[rows, ls]
                acc = jnp.zeros((RC, LANES), F32)
                for k in range(KA):
                    dout = dcabuf[pl.ds(r * RC + KA - 1 - k, RC), ls]
                    acc = acc + wa_ref[k:k + 1, ls] * dout
                    gwa[8 * k:8 * k + 8, ls] += _rows8(cin * dout)
                dcain[rows, ls] = acc
                uin = u0buf[rows, ls]
                acc = jnp.zeros((RC, LANES), F32)
                for k in range(KB):
                    dout = _tap(shd, du1buf, ls, r * RC + KB - 1 - k, RC)
                    acc = acc + wb_ref[k:k + 1, ls] * dout
                    gwb[8 * k:8 * k + 8, ls] += _rows8(uin * dout)
                du0[rows, ls] = acc
            return carry

        lax.fori_loop(0, D // LANES, lane_body, 0)
        dca_in = dcain[...]
        du0v = du0[...]
        dp_ref[:, 0:D] = dba_ref[...]
        dp_ref[:, D:2 * D] = (dca_in * vav).astype(BF)
        dp_ref[:, 2 * D:3 * D] = (dca_in * cav).astype(BF)
        dp_ref[:, 3 * D:4 * D] = (du0v * sg).astype(BF)
        dp_ref[:, 4 * D:5 * D] = (du0v * vbv * (sg * (1.0 - sg))).astype(BF)
        dp_ref[:, 5 * D:7 * D] = dg_ref[...]

        @pl.when(last)
        def _():
            gw_ref[...] = jnp.zeros_like(gw_ref)
            for k in range(KA):
                gw_ref[k:k + 1, :] = _colsum(gwa[8 * k:8 * k + 8, :])
            for k in range(KB):
                gw_ref[8 + k:9 + k, :] = _colsum(gwb[8 * k:8 * k + 8, :])

    full = lambda r: pl.BlockSpec((r, D), lambda i: (0, 0))
    tok = pl.BlockSpec((tt, D), lambda i: (i, 0))
    nxt = lambda h: pl.BlockSpec((h, D), lambda i: (jnp.minimum((i + 1) * (tt // h), T // h - 1), 0))
    return _pallas(
        body, name="conv_bwd", grid=(nsteps,),
        in_specs=[tok, nxt(HN), tok, nxt(HB),
                  _seg(tt, 1), _seg(tt, 2), _seg(tt, 3), _seg(tt, 4),
                  tok, pl.BlockSpec((tt, 2 * D), lambda i: (i, 0)), full(8), full(32)],
        out_specs=[pl.BlockSpec((tt, 7 * D), lambda i: (i, 0)), full(40)],
        out_shape=[jax.ShapeDtypeStruct((T, 7 * D), BF), jax.ShapeDtypeStruct((40, D), F32)],
        scratch_shapes=[pltpu.VMEM((tt, D), F32), pltpu.VMEM((tt, D), F32),
                        pltpu.VMEM((tt + HN, D), F32), pltpu.VMEM((tt + HB, D), F32),
                        pltpu.VMEM((tt, D), F32), pltpu.VMEM((tt, D), F32),
                        pltpu.VMEM((8 * KA, D), F32), pltpu.VMEM((8 * KB, D), F32),
                        pltpu.VMEM((8, HB + tt, LANES), F32)],
        args=(dconva, dconva, du1, du1, p, p, p, p, dba, dg, wa, wb), push=push)


def _inproj_bwd(dp, win_t, dz1, push, tm=512):
    T, cols = dp.shape

    def body(dp_ref, w_ref, dz1_ref, o_ref):
        o_ref[...] = ALPHA * dz1_ref[...].astype(F32) + jnp.dot(dp_ref[...], w_ref[...], preferred_element_type=F32)

    tok = pl.BlockSpec((tm, D), lambda i: (i, 0))
    return _pallas(
        body, name="inproj_bwd", grid=(T // tm,),
        in_specs=[pl.BlockSpec((tm, cols), lambda i: (i, 0)), _resident((cols, D)), tok],
        out_specs=[tok],
        out_shape=[jax.ShapeDtypeStruct((T, D), F32)],
        args=(dp, win_t, dz1), push=push)


def _adam_math(w, m, v, g):
    nm = ADAM_B1 * m + (1.0 - ADAM_B1) * g
    nv = ADAM_B2 * v + (1.0 - ADAM_B2) * (g * g)
    m_hat = nm / (1.0 - ADAM_B1 ** ADAM_STEP)
    v_hat = nv / (1.0 - ADAM_B2 ** ADAM_STEP)
    return -ADAM_LR * (m_hat / (jnp.sqrt(v_hat) + ADAM_EPS) + ADAM_WD * w), nm, nv


def _adamw(w, m, v, g32, landings, me, name, rb):
    R, C = w.shape
    nland = len(landings)

    def body(me_ref, w_ref, m_ref, v_ref, own_ref, *refs):
        g_ref, d_ref, nm_ref, nv_ref = refs[nland:]
        g = own_ref[0]
        for l_ref in refs[:nland]:
            for k in range(l_ref.shape[0]):
                g = g + l_ref[k].astype(F32)
        g_ref[...] = g
        d_ref[...], nm_ref[...], nv_ref[...] = _adam_math(w_ref[...], m_ref[...], v_ref[...], g)

    blk = pl.BlockSpec((rb, C), lambda i, me_ref: (i, 0))
    grid_spec = pltpu.PrefetchScalarGridSpec(
        num_scalar_prefetch=1, grid=(R // rb,),
        in_specs=[blk, blk, blk, pl.BlockSpec((1, rb, C), lambda i, me_ref: (me_ref[0], i, 0))]
        + [pl.BlockSpec((l.shape[0], rb, C), lambda i, me_ref: (0, i, 0)) for l in landings],
        out_specs=[blk] * 4)
    return pl.pallas_call(
        body, name=name, grid_spec=grid_spec, out_shape=[jax.ShapeDtypeStruct((R, C), F32)] * 4,
        compiler_params=_params(1))(me, w, m, v, g32, *landings)


def _adamw_small(small_g, vec_w, vec_m, vec_v, conv_w, conv_m, conv_v):
    nv_ = len(vec_w)
    conv_rows = [(8, KA), (16, KB)]

    def body(*refs):
        g_ref = refs[0]
        w_refs, m_refs, v_refs = refs[1:10], refs[10:19], refs[19:28]
        out_refs, gsum = refs[28:64], refs[64]
        acc = g_ref[0]
        for j in range(1, NDEV):
            acc = acc + g_ref[j]
        gsum[...] = acc
        me = 4 * lax.axis_index("x") + 2 * lax.axis_index("y") + lax.axis_index("c")
        cols = pl.ds(pl.multiple_of(me * LANES, LANES), LANES)
        for i in range(nv_ + 2):
            if i < nv_:
                g = gsum[i:i + 1, :]
            else:
                r0, k = conv_rows[i - nv_]
                g = gsum[r0:r0 + k, cols]
            o = out_refs[4 * i:4 * i + 4]
            o[0][...] = g
            o[1][...], o[2][...], o[3][...] = _adam_math(w_refs[i][...], m_refs[i][...], v_refs[i][...], g)

    ws, ms, vs = list(vec_w) + list(conv_w), list(vec_m) + list(conv_m), list(vec_v) + list(conv_v)
    out_shape = [jax.ShapeDtypeStruct(w.shape, F32) for w in ws for _ in range(4)]
    return pl.pallas_call(
        body, name="adamw_small", out_shape=out_shape,
        scratch_shapes=[pltpu.VMEM(small_g.shape[1:], F32)])(small_g, *ws, *ms, *vs)


def _pad_rows(a, rows):
    return jnp.pad(a, ((0, rows - a.shape[0]), (0, 0)))


def _local_step(p, xb, mixed, post, x, target, win_t, wup, wup_t, wdown, woa, wob, wo, wa, wb, vecs):
    yapre, conva, xhatb, rstdb, u3 = mixed
    ya, yb, merged, xhat1, rstd1, x1b = post
    r, h = _mlp_up(x1b, wup)
    dz2b, st2 = _mlp_down_loss(h, wdown, xhat1, target, vecs)

    by_owner = lambda g16: g16.reshape(NDEV, D // NDEV, D)
    dhpre = _mlp_down_bwd(dz2b, wdown, r)
    g_wdown = _tn_matmul_tiles_outer(h, dz2b, NDEV, 512, D, True, "grad_w_down")
    (dz1b, st1), land_wdown = _mlp_up_bwd(dhpre, wup_t, dz2b, xhat1, rstd1, vecs, _push(exch=[g_wdown[1]]))
    g_wup = _tn_matmul_tiles_outer(x1b, dhpre, NDEV, D, 512, False, "grad_w_up")
    (dya, dyb, dg, dba, dconva, du1, stb), _ = _merge_bwd(
        dz1b, p, ya, yb, conva, xhatb, rstdb, woa, wob, wo, vecs, None)
    g_wo = _tn_matmul(merged, dz1b, 1, D, D, False, False, "grad_w_o")
    g_woa = _tn_matmul(yapre, dya, 1, D, D, False, False, "grad_w_out_a")
    g_wob = _tn_matmul(u3, dyb, 1, D, D, False, False, "grad_w_out_b")
    (dp, gw), land_conv = _conv_bwd(
        dconva, du1, p, dba, dg, wa, wb,
        _push(exch=[g_wup[1], by_owner(g_woa[1]), by_owner(g_wob[1]), by_owner(g_wo[1])]))
    g_win, g_win16, land_win = _grad_w_in_send(xb, dp, dp.shape[1] // NDEV)

    small = jnp.concatenate([stb[2:3], stb[0:2], st1[0:2], st2[0:3], gw], axis=0)
    late_ks = tuple(range(1, NDEV - W_IN_EARLY))
    (grad_x,), land_last = _inproj_bwd(dp, win_t, dz1b, _push(exch=[g_win16], gath=[small], ks=late_ks))
    grads = (g_win, g_wup[0], g_wdown[0], g_woa[0], g_wob[0], g_wo[0])
    return grad_x, grads, small, land_wdown + land_conv + [land_win] + land_last


def kernel(x, w_in, conv_a_w, w_out_a, conv_b_w, conv_b_bias, ln_b_gamma, ln_b_beta, w_out_b, w_o, ln1_gamma, ln1_beta, w_up, w_down, ln2_gamma, ln2_beta, loss_target, m_w_in, m_conv_a_w, m_w_out_a, m_conv_b_w, m_conv_b_bias, m_ln_b_gamma, m_ln_b_beta, m_w_out_b, m_w_o, m_ln1_gamma, m_ln1_beta, m_w_up, m_w_down, m_ln2_gamma, m_ln2_beta, v_w_in, v_conv_a_w, v_w_out_a, v_conv_b_w, v_conv_b_bias, v_ln_b_gamma, v_ln_b_beta, v_w_out_b, v_w_o, v_ln1_gamma, v_ln1_beta, v_w_up, v_w_down, v_ln2_gamma, v_ln2_beta):
    T = x.shape[1]
    me = 4 * lax.axis_index("x") + 2 * lax.axis_index("y") + lax.axis_index("c")

    conv_shard = jnp.concatenate([_pad_rows(conv_a_w, 8), _pad_rows(conv_b_w, 32)], axis=0)
    p, xb, win_g, conv_g = _inproj_gather(x[0], w_in.astype(BF), conv_shard)
    conv_full = jnp.transpose(conv_g, (1, 0, 2)).reshape(40, D)
    vecs = jnp.stack([conv_b_bias, ln_b_gamma, ln_b_beta, ln1_gamma, ln1_beta, ln2_gamma, ln2_beta,
                      jnp.zeros_like(ln2_beta)])
    whole = lambda g: jnp.transpose(g, (1, 0, 2)).reshape(D, -1)
    whole_t = lambda g: jnp.transpose(g, (0, 2, 1)).reshape(-1, D)
    mixed, (woa_g, wob_g, wo_g, wup_g, wdown_g) = _mixer_fwd(
        p, conv_full[0:8], conv_full[8:40], vecs,
        [w_out_a.astype(BF), w_out_b.astype(BF), w_o.astype(BF), w_up.astype(BF), w_down.astype(BF)])
    woa, wob, wo = woa_g.reshape(D, D), wob_g.reshape(D, D), wo_g.reshape(D, D)
    post, _ = _post_mixer(mixed[0], mixed[4], p, x[0], woa, wob, wo, vecs, [])

    grad_x, grads, small, landing = _local_step(
        p, xb, mixed, post, x[0], loss_target[0], whole_t(win_g), whole(wup_g), whole_t(wup_g),
        wdown_g.reshape(NDEV * 512, D),
        woa, wob, wo, conv_full[0:8], conv_full[8:40], vecs)
    g_win, g_wup, g_wdown, g_woa, g_wob, g_wo = grads
    l_wdown, l_wup, l_woa, l_wob, l_wo, l_win_early, l_win_late, small_g = landing

    loss = lax.psum(0.5 / D * jnp.sum(small[7]), ("x", "y", "c"))

    me1 = me.astype(jnp.int32).reshape(1)
    by_owner = lambda g32: g32.reshape(NDEV, D // NDEV, D)
    r_win = _adamw(w_in, m_w_in, v_w_in, g_win, [l_win_early, l_win_late], me1, "adamw_w_in", 256)
    r_wup = _adamw(w_up, m_w_up, v_w_up, g_wup, [l_wup], me1, "adamw_w_up", 256)
    r_wdown = _adamw(w_down, m_w_down, v_w_down, g_wdown, [l_wdown], me1, "adamw_w_down", 256)
    r_woa = _adamw(w_out_a, m_w_out_a, v_w_out_a, by_owner(g_woa), [l_woa], me1, "adamw_w_out_a", 128)
    r_wob = _adamw(w_out_b, m_w_out_b, v_w_out_b, by_owner(g_wob), [l_wob], me1, "adamw_w_out_b", 128)
    r_wo = _adamw(w_o, m_w_o, v_w_o, by_owner(g_wo), [l_wo], me1, "adamw_w_o", 128)

    row = lambda vec: vec.reshape(1, D)
    small_out = _adamw_small(
        small_g,
        [row(a) for a in (conv_b_bias, ln_b_gamma, ln_b_beta, ln1_gamma, ln1_beta, ln2_gamma, ln2_beta)],
        [row(a) for a in (m_conv_b_bias, m_ln_b_gamma, m_ln_b_beta, m_ln1_gamma, m_ln1_beta, m_ln2_gamma, m_ln2_beta)],
        [row(a) for a in (v_conv_b_bias, v_ln_b_gamma, v_ln_b_beta, v_ln1_gamma, v_ln1_beta, v_ln2_gamma, v_ln2_beta)],
        [conv_a_w, conv_b_w], [m_conv_a_w, m_conv_b_w], [v_conv_a_w, v_conv_b_w])
    r_vec = [[small_out[4 * i + q].reshape(D) for q in range(4)] for i in range(7)]
    r_conva, r_convb = small_out[28:32], small_out[32:36]

    per_weight = []
    for q in range(4):
        per_weight.append([
            r_win[q], r_conva[q], r_woa[q], r_convb[q],
            r_vec[0][q], r_vec[1][q], r_vec[2][q], r_wob[q], r_wo[q], r_vec[3][q], r_vec[4][q],
            r_wup[q], r_wdown[q], r_vec[5][q], r_vec[6][q]])
    return (loss, grad_x[None], *per_weight[0], *per_weight[1], *per_weight[2], *per_weight[3])
```
